```python
import math
import jax
import jax.numpy as jnp
from jax import lax
import numpy as np

D_MODEL = 1024
BATCH = 2
SEQ = 8192
DEPTH = 1

PLE_DIM = 256
EPS = 1e-6
SSM_GROUP = 16
SSM_STATE = 64
SSM_WIDTH = D_MODEL // 2
SSM_GROUPS = SSM_WIDTH // SSM_GROUP
HEAD_DIM = 64
DILATED_PATTERNS = ((128, 1), (512, 4), (2048, 16))
N_DIL = len(DILATED_PATTERNS)
HEADS_PER_GROUP = 4
N_ATTN_HEADS = N_DIL * HEADS_PER_GROUP
ATTN_WIDTH = N_ATTN_HEADS * HEAD_DIM
ATTN_OUT = HEADS_PER_GROUP * HEAD_DIM
ROT_DIM = HEAD_DIM // 4
ROPE_THETA = 500000.0
Q_BLOCK = 128
D_FF = 2816
CONV_WIDTH = 3
IN_WIDTH = SSM_WIDTH + 3 * ATTN_WIDTH + 2 * D_MODEL
NEG_BIG = -1e30

kernel_name = 'hybrid_s5_dilated_attn_block'


def rms_norm(x, g):
    xf = x.astype(jnp.float32)
    var = jnp.mean(xf * xf, axis=-1, keepdims=True)
    return (xf * lax.rsqrt(var + EPS) * g.astype(jnp.float32)).astype(x.dtype)


def _complex_linear_combine(e1, e2):
    a1r, a1i, b1r, b1i = e1
    a2r, a2i, b2r, b2i = e2
    ar = a2r * a1r - a2i * a1i
    ai = a2r * a1i + a2i * a1r
    br = a2r * b1r - a2i * b1i + b2r
    bi = a2r * b1i + a2i * b1r + b2i
    return (ar, ai, br, bi)


def s5_mixer(u, lam_re, lam_im, log_dt, b_re, b_im, c_re, c_im, d_skip, glu_w, glu_b):
    bsz, l = u.shape[0], u.shape[1]
    uf = u.astype(jnp.float32).reshape(bsz, l, SSM_GROUPS, SSM_GROUP)
    lr = lam_re.astype(jnp.float32)
    li = lam_im.astype(jnp.float32)
    dt = jnp.exp(log_dt.astype(jnp.float32))[:, None]
    mag = jnp.exp(lr * dt)
    ar = mag * jnp.cos(li * dt)
    ai = mag * jnp.sin(li * dt)
    den = lr * lr + li * li
    cr = ((ar - 1.0) * lr + ai * li) / den
    ci = (ai * lr - (ar - 1.0) * li) / den
    br_ = b_re.astype(jnp.float32)
    bi_ = b_im.astype(jnp.float32)
    bbar_re = cr[..., None] * br_ - ci[..., None] * bi_
    bbar_im = cr[..., None] * bi_ + ci[..., None] * br_
    bu_re = jnp.einsum('blgh,gph->blgp', uf, bbar_re)
    bu_im = jnp.einsum('blgh,gph->blgp', uf, bbar_im)
    a_re = jnp.broadcast_to(ar, bu_re.shape)
    a_im = jnp.broadcast_to(ai, bu_re.shape)
    _, _, s_re, s_im = lax.associative_scan(_complex_linear_combine, (a_re, a_im, bu_re, bu_im), axis=1)
    y = (jnp.einsum('blgp,ghp->blgh', s_re, c_re.astype(jnp.float32))
         - jnp.einsum('blgp,ghp->blgh', s_im, c_im.astype(jnp.float32))
         + d_skip.astype(jnp.float32) * uf)
    y = jax.nn.gelu(y.reshape(bsz, l, SSM_WIDTH))
    y = y * jax.nn.sigmoid(y @ glu_w.astype(jnp.float32) + glu_b.astype(jnp.float32))
    return y.astype(u.dtype)


def rotary(x, pos):
    half = ROT_DIM // 2
    freqs = ROPE_THETA ** (-jnp.arange(half, dtype=jnp.float32) * (2.0 / ROT_DIM))
    ang = pos[:, None] * freqs[None, :]
    cos = jnp.cos(ang)[None, :, None, :]
    sin = jnp.sin(ang)[None, :, None, :]
    xf = x.astype(jnp.float32)
    x1 = xf[..., :half]
    x2 = xf[..., half:ROT_DIM]
    out = jnp.concatenate([x1 * cos - x2 * sin, x2 * cos + x1 * sin, xf[..., ROT_DIM:]], axis=-1)
    return out.astype(x.dtype)


def dilated_attention(q, k, v):
    bsz, l = q.shape[0], q.shape[1]
    pos = jnp.arange(l, dtype=jnp.float32)
    q = rotary(q, pos)
    k = rotary(k, pos)
    qs = [q[:, :, g * HEADS_PER_GROUP:(g + 1) * HEADS_PER_GROUP] for g in range(N_DIL)]
    ks = [k[:, :, g * HEADS_PER_GROUP:(g + 1) * HEADS_PER_GROUP] for g in range(N_DIL)]
    vs = [v[:, :, g * HEADS_PER_GROUP:(g + 1) * HEADS_PER_GROUP] for g in range(N_DIL)]
    scale = HEAD_DIM ** -0.5

    def block(bi):
        start = bi * Q_BLOCK
        t = start + jnp.arange(Q_BLOCK, dtype=jnp.int32)
        ms, dens, nums = [], [], []
        for g, (window, dil) in enumerate(DILATED_PATTERNS):
            n_keys = window // dil + 1
            idx = t[:, None] - dil * jnp.arange(n_keys, dtype=jnp.int32)[None, :]
            valid = idx >= 0
            idx = jnp.maximum(idx, 0)
            qb = lax.dynamic_slice_in_dim(qs[g], start, Q_BLOCK, axis=1)
            kb = jnp.take(ks[g], idx, axis=1)
            vb = jnp.take(vs[g], idx, axis=1)
            s = jnp.einsum('bqhd,bqjhd->bqhj', qb, kb).astype(jnp.float32) * scale
            s = jnp.where(valid[None, :, None, :], s, NEG_BIG)
            m = jnp.max(s, axis=-1)
            e = jnp.exp(s - m[..., None])
            dens.append(jnp.sum(e, axis=-1))
            nums.append(jnp.einsum('bqhj,bqjhd->bqhd', e, vb.astype(jnp.float32)))
            ms.append(m)
        m_all = jnp.stack(ms)
        w = jnp.exp(m_all - jnp.max(m_all, axis=0))
        num = sum(w[g][..., None] * nums[g] for g in range(N_DIL))
        den = sum(w[g] * dens[g] for g in range(N_DIL))
        return (num / den[..., None]).astype(q.dtype)

    out = lax.map(block, jnp.arange(l // Q_BLOCK, dtype=jnp.int32))
    return out.transpose(1, 0, 2, 3, 4).reshape(bsz, l, ATTN_OUT)


def causal_depthwise_conv(x, w, bias):
    rhs = w.reshape(CONV_WIDTH, 1, w.shape[-1])
    y = lax.conv_general_dilated(x, rhs, window_strides=(1,), padding=((CONV_WIDTH - 1, 0),),
                                 dimension_numbers=('NWC', 'WIO', 'NWC'), feature_group_count=x.shape[-1])
    return y + bias


def setup_inputs(seed: int = 0) -> dict:
    key = jax.random.key(seed)
    ks = jax.random.split(key, 32)
    f32 = jnp.float32

    def nrm(k, shape, fan_in):
        return jax.random.normal(k, shape, f32) * (fan_in ** -0.5)

    def gain(k, shape):
        return 1.0 + 0.02 * jax.random.normal(k, shape, f32)

    lam_im = jnp.broadcast_to(math.pi * jnp.arange(SSM_STATE, dtype=f32), (DEPTH, SSM_GROUPS, SSM_STATE))
    return {
        'x': jax.random.normal(ks[0], (BATCH, SEQ, D_MODEL), f32),
        'p': jax.random.normal(ks[1], (DEPTH, BATCH, SEQ, PLE_DIM), f32),
        'mix_norm_g': gain(ks[2], (DEPTH, D_MODEL)),
        'w_in': nrm(ks[3], (DEPTH, D_MODEL, IN_WIDTH), D_MODEL),
        'gate_b': 0.02 * jax.random.normal(ks[4], (DEPTH, 2 * D_MODEL), f32),
        'ssm_lam_re': -0.5 + 0.01 * jax.random.normal(ks[5], (DEPTH, SSM_GROUPS, SSM_STATE), f32),
        'ssm_lam_im': lam_im + 0.01 * jax.random.normal(ks[6], (DEPTH, SSM_GROUPS, SSM_STATE), f32),
        'ssm_log_dt': jax.random.uniform(ks[7], (DEPTH, SSM_GROUPS), f32, minval=math.log(1e-3), maxval=math.log(1e-1)),
        'ssm_b_re': nrm(ks[8], (DEPTH, SSM_GROUPS, SSM_STATE, SSM_GROUP), 2 * SSM_GROUP),
        'ssm_b_im': nrm(ks[9], (DEPTH, SSM_GROUPS, SSM_STATE, SSM_GROUP), 2 * SSM_GROUP),
        'ssm_c_re': nrm(ks[10], (DEPTH, SSM_GROUPS, SSM_GROUP, SSM_STATE), 2 * SSM_STATE),
        'ssm_c_im': nrm(ks[11], (DEPTH, SSM_GROUPS, SSM_GROUP, SSM_STATE), 2 * SSM_STATE),
        'ssm_d': jax.random.normal(ks[12], (DEPTH, SSM_GROUPS, SSM_GROUP), f32),
        'ssm_glu_w': nrm(ks[13], (DEPTH, SSM_WIDTH, SSM_WIDTH), SSM_WIDTH),
        'ssm_glu_b': 0.02 * jax.random.normal(ks[14], (DEPTH, SSM_WIDTH), f32),
        'w_branch_a': nrm(ks[15], (DEPTH, SSM_WIDTH, D_MODEL), SSM_WIDTH),
        'w_branch_b': nrm(ks[16], (DEPTH, ATTN_OUT, D_MODEL), ATTN_OUT),
        'w_out': nrm(ks[17], (DEPTH, D_MODEL, D_MODEL), D_MODEL),
        'ffn_norm_g': gain(ks[18], (DEPTH, D_MODEL)),
        'ffn_w_gate': nrm(ks[19], (DEPTH, D_MODEL, D_FF), D_MODEL),
        'ffn_w_up': nrm(ks[20], (DEPTH, D_MODEL, D_FF), D_MODEL),
        'ffn_conv_w': nrm(ks[21], (DEPTH, CONV_WIDTH, D_FF), CONV_WIDTH),
        'ffn_conv_b': 0.02 * jax.random.normal(ks[22], (DEPTH, D_FF), f32),
        'ffn_w_down': nrm(ks[23], (DEPTH, D_FF, D_MODEL), D_FF),
        'ple_norm_g': gain(ks[24], (DEPTH, D_MODEL)),
        'ple_w_gate': nrm(ks[25], (DEPTH, D_MODEL, D_MODEL), D_MODEL),
        'ple_w_proj': nrm(ks[26], (DEPTH, PLE_DIM, D_MODEL), PLE_DIM),
        'final_norm_g': gain(ks[27], (D_MODEL,)),
    }


def reference(x, p, mix_norm_g, w_in, gate_b, ssm_lam_re, ssm_lam_im, ssm_log_dt, ssm_b_re, ssm_b_im,
              ssm_c_re, ssm_c_im, ssm_d, ssm_glu_w, ssm_glu_b, w_branch_a, w_branch_b, w_out,
              ffn_norm_g, ffn_w_gate, ffn_w_up, ffn_conv_w, ffn_conv_b, ffn_w_down,
              ple_norm_g, ple_w_gate, ple_w_proj, final_norm_g):
    bsz, l = x.shape[0], x.shape[1]
    o_q = SSM_WIDTH
    o_k = o_q + ATTN_WIDTH
    o_v = o_k + ATTN_WIDTH
    o_ga = o_v + ATTN_WIDTH
    o_gb = o_ga + D_MODEL
    h = x
    for i in range(DEPTH):
        u = rms_norm(h, mix_norm_g[i])
        z = u @ w_in[i]
        s_in = z[..., :o_q]
        q = z[..., o_q:o_k].reshape(bsz, l, N_ATTN_HEADS, HEAD_DIM)
        k = z[..., o_k:o_v].reshape(bsz, l, N_ATTN_HEADS, HEAD_DIM)
        v = z[..., o_v:o_ga].reshape(bsz, l, N_ATTN_HEADS, HEAD_DIM)
        g_a = jax.nn.sigmoid(z[..., o_ga:o_gb] + gate_b[i, :D_MODEL])
        g_b = jax.nn.sigmoid(z[..., o_gb:] + gate_b[i, D_MODEL:])
        y_a = s5_mixer(s_in, ssm_lam_re[i], ssm_lam_im[i], ssm_log_dt[i], ssm_b_re[i], ssm_b_im[i],
                       ssm_c_re[i], ssm_c_im[i], ssm_d[i], ssm_glu_w[i], ssm_glu_b[i]) @ w_branch_a[i]
        y_b = dilated_attention(q, k, v) @ w_branch_b[i]
        h = h + (g_a * y_a + g_b * y_b) @ w_out[i]
        u2 = rms_norm(h, ffn_norm_g[i])
        gate = causal_depthwise_conv(u2 @ ffn_w_gate[i], ffn_conv_w[i], ffn_conv_b[i])
        h = h + (jax.nn.gelu(gate) * (u2 @ ffn_w_up[i])) @ ffn_w_down[i]
        u3 = rms_norm(h, ple_norm_g[i])
        h = h + jax.nn.sigmoid(u3 @ ple_w_gate[i]) * (p[i] @ ple_w_proj[i])
    return rms_norm(h, final_norm_g)
```

```python
import functools
import math

import jax
import jax.numpy as jnp
from jax import lax
from jax.experimental import pallas as pl
from jax.experimental.pallas import tpu as pltpu

F32 = jnp.float32
BF16 = jnp.bfloat16

D_MODEL = 1024
EPS = 1e-6
PLE_DIM = 256
SSM_GROUP = 16
SSM_STATE = 64
SSM_WIDTH = 512
SSM_GROUPS = SSM_WIDTH // SSM_GROUP
HEAD_DIM = 64
DILATIONS = (1, 4, 16)
WINDOW_KEYS = 128
HEADS_PER_GROUP = 4
GROUP_WIDTH = HEADS_PER_GROUP * HEAD_DIM
ATTN_WIDTH = len(DILATIONS) * GROUP_WIDTH
ROT_DIM = HEAD_DIM // 4
ROPE_THETA = 500000.0
NEG_BIG = -1e30
D_FF = 2816
CONV_WIDTH = 3

LANES = 128
SUBLANES = 8
VMEM_LIMIT_BYTES = 56 * 1024 * 1024

TOKEN_TILE = 512
SSM_CHUNK = 16
SSM_CHUNKS_PER_TILE = LANES
SSM_TILE = SSM_CHUNK * SSM_CHUNKS_PER_TILE
SSM_CK = SSM_CHUNK * SSM_GROUP
ATTN_QB = 128
ATTN_STEP_ROWS = 512
FFN_CHUNK = 256


def _dot(a, b):
    return jnp.dot(a, b, preferred_element_type=F32)


def _rms(x, g):
    var = jnp.mean(x * x, axis=-1, keepdims=True)
    return x * lax.rsqrt(var + EPS) * g


def _const_spec(shape):
    nd = len(shape)
    return pl.BlockSpec(shape, lambda *_: (0,) * nd, pipeline_mode=pl.Buffered(1))


def _params(*sem):
    return pltpu.CompilerParams(dimension_semantics=sem, vmem_limit_bytes=VMEM_LIMIT_BYTES)


def _rope(z, cos, sin_lo, sin_hi):
    up = pltpu.roll(z, LANES - ROT_DIM // 2, 1)
    dn = pltpu.roll(z, ROT_DIM // 2, 1)
    return z * cos + up * sin_lo + dn * sin_hi


def _store_residue_major(stage_ref, out_ref, z, dil):
    rows = z.shape[0] // dil
    for j in range(z.shape[1] // LANES):
        sl = slice(j * LANES, (j + 1) * LANES)
        stage_ref[...] = z[:, sl]
        for r in range(dil):
            out_ref[r, :, sl] = stage_ref[pl.ds(r, rows, stride=dil), :].astype(out_ref.dtype)


def _in_proj_kernel(x_ref, g_ref, cos_ref, slo_ref, shi_ref, ws_ref, wq_ref, wk_ref, wv_ref,
                    wga_ref, wgb_ref, ba_ref, bb_ref,
                    s_ref, q0_ref, k0_ref, v0_ref, q1_ref, k1_ref, v1_ref, q2_ref, k2_ref, v2_ref,
                    ga_ref, gb_ref, stage_ref):
    u = _rms(x_ref[...], g_ref[...]).astype(BF16)
    s_ref[...] = _dot(u, ws_ref[...])
    cos, slo, shi = cos_ref[...], slo_ref[...], shi_ref[...]
    scale = HEAD_DIM ** -0.5

    def rope(z):
        return jnp.concatenate([_rope(z[:, j * LANES:(j + 1) * LANES], cos, slo, shi)
                                for j in range(z.shape[1] // LANES)], axis=1)

    q = rope(_dot(u, wq_ref[...])) * scale
    k = rope(_dot(u, wk_ref[...]))
    v = _dot(u, wv_ref[...])
    for z, refs in ((q, (q0_ref, q1_ref, q2_ref)), (k, (k0_ref, k1_ref, k2_ref)), (v, (v0_ref, v1_ref, v2_ref))):
        refs[0][...] = z[:, :GROUP_WIDTH].astype(BF16)
        for grp in (1, 2):
            _store_residue_major(stage_ref, refs[grp], z[:, grp * GROUP_WIDTH:(grp + 1) * GROUP_WIDTH],
                                 DILATIONS[grp])
    ga_ref[...] = jax.nn.sigmoid(_dot(u, wga_ref[...]) + ba_ref[...]).astype(BF16)
    gb_ref[...] = jax.nn.sigmoid(_dot(u, wgb_ref[...]) + bb_ref[...]).astype(BF16)


def _rope_tables(seq):
    half = ROT_DIM // 2
    pos = jnp.arange(seq, dtype=F32)
    freqs = ROPE_THETA ** (-jnp.arange(half, dtype=F32) * (2.0 / ROT_DIM))
    ang = pos[:, None] * freqs[None, :]
    cos, sin = jnp.cos(ang), jnp.sin(ang)
    ones = jnp.ones((seq, HEAD_DIM - ROT_DIM), F32)
    zeros = jnp.zeros((seq, HEAD_DIM - half), F32)
    cos_h = jnp.concatenate([cos, cos, ones], axis=1)
    slo_h = jnp.concatenate([-sin, zeros], axis=1)
    shi_h = jnp.concatenate([jnp.zeros((seq, half), F32), sin, ones * 0.0], axis=1)
    rep = LANES // HEAD_DIM
    return tuple(jnp.tile(t, (1, rep)) for t in (cos_h, slo_h, shi_h))


def _in_proj(x2, seq, norm_g, w_in, gate_b):
    n = x2.shape[0]
    tm = TOKEN_TILE
    tiles_per_seq = seq // tm
    o_q = SSM_WIDTH
    o_k = o_q + ATTN_WIDTH
    o_v = o_k + ATTN_WIDTH
    o_ga = o_v + ATTN_WIDTH
    o_gb = o_ga + D_MODEL
    wb = w_in.astype(BF16)
    ws, wq, wk, wv = wb[:, :o_q], wb[:, o_q:o_k], wb[:, o_k:o_v], wb[:, o_v:o_ga]
    wga, wgb = wb[:, o_ga:o_gb], wb[:, o_gb:]
    ba = gate_b[:D_MODEL].reshape(1, D_MODEL)
    bb = gate_b[D_MODEL:].reshape(1, D_MODEL)
    cos, slo, shi = _rope_tables(seq)

    bsz = n // seq
    row = lambda w: pl.BlockSpec((tm, w), lambda i: (i, 0))
    tab = pl.BlockSpec((tm, LANES), lambda i: (i % tiles_per_seq, 0))
    res = lambda d: pl.BlockSpec((None, d, tm // d, GROUP_WIDTH),
                                 lambda i: (i // tiles_per_seq, 0, i % tiles_per_seq, 0))
    res_shape = lambda d: jax.ShapeDtypeStruct((bsz, d, seq // d, GROUP_WIDTH), BF16)
    nat_shape = jax.ShapeDtypeStruct((n, GROUP_WIDTH), BF16)
    d1, d2 = DILATIONS[1], DILATIONS[2]
    outs = pl.pallas_call(
        _in_proj_kernel,
        grid=(n // tm,),
        in_specs=[row(D_MODEL), _const_spec((1, D_MODEL)), tab, tab, tab,
                  _const_spec((D_MODEL, SSM_WIDTH)), _const_spec((D_MODEL, ATTN_WIDTH)),
                  _const_spec((D_MODEL, ATTN_WIDTH)), _const_spec((D_MODEL, ATTN_WIDTH)),
                  _const_spec((D_MODEL, D_MODEL)), _const_spec((D_MODEL, D_MODEL)),
                  _const_spec((1, D_MODEL)), _const_spec((1, D_MODEL))],
        out_specs=[row(SSM_WIDTH)] + [row(GROUP_WIDTH)] * 3 + [res(d1)] * 3 + [res(d2)] * 3
                  + [row(D_MODEL), row(D_MODEL)],
        out_shape=[jax.ShapeDtypeStruct((n, SSM_WIDTH), F32)] + [nat_shape] * 3 + [res_shape(d1)] * 3
                  + [res_shape(d2)] * 3
                  + [jax.ShapeDtypeStruct((n, D_MODEL), BF16), jax.ShapeDtypeStruct((n, D_MODEL), BF16)],
        scratch_shapes=[pltpu.VMEM((tm, LANES), F32)],
        compiler_params=_params("parallel"),
        name="in_proj",
    )(x2, norm_g.reshape(1, D_MODEL), cos, slo, shi, ws, wq, wk, wv, wga, wgb, ba, bb)
    u, q0, k0, v0, q1, k1, v1, q2, k2, v2, ga, gb = outs
    nat4 = lambda a: a.reshape(bsz, 1, seq, GROUP_WIDTH)
    qkv = ((nat4(q0), nat4(k0), nat4(v0)), (q1, k1, v1), (q2, k2, v2))
    return u, qkv, ga, gb


def _cmul(ar, ai, br, bi):
    return ar * br - ai * bi, ar * bi + ai * br


def _discretise(lr, li, dt):
    mag = jnp.exp(lr * dt)
    ar = mag * jnp.cos(li * dt)
    ai = mag * jnp.sin(li * dt)
    den = lr * lr + li * li
    cr = ((ar - 1.0) * lr + ai * li) / den
    ci = (ai * lr - (ar - 1.0) * li) / den
    return ar, ai, cr, ci


def _ssm_prep_kernel(lr_ref, li_ref, ldt_ref, lrc_ref, lic_ref, brt_ref, bit_ref, cre_ref, cim_ref,
                     rtr_ref, rti_ref, ftr_ref, fti_ref, kt_ref, apr_ref, api_ref, a128r_ref, a128i_ref):
    dt = jnp.exp(ldt_ref[...])
    ar, ai, cr, ci = _discretise(lr_ref[...], li_ref[...], dt)
    brt, bit = brt_ref[...], bit_ref[...]
    bbr = cr * brt - ci * bit
    bbi = cr * bit + ci * brt
    cre, cim = cre_ref[...], cim_ref[...]
    pr, pi = jnp.ones_like(ar), jnp.zeros_like(ai)
    kt = None
    for j in range(SSM_CHUNK):
        rows = slice(j * SSM_GROUP, (j + 1) * SSM_GROUP)
        rr, ri = _cmul(pr, pi, bbr, bbi)
        rtr_ref[rows, :] = rr
        rti_ref[rows, :] = ri
        pr, pi = _cmul(pr, pi, ar, ai)
        ftr_ref[rows, :] = cre * pr - cim * pi
        fti_ref[rows, :] = -cre * pi - cim * pr
    dn = (((1,), (1,)), ((), ()))
    hi = lax.Precision.HIGHEST
    kt_ref[...] = (lax.dot_general(rtr_ref[...], cre, dn, precision=hi, preferred_element_type=F32)
                   - lax.dot_general(rti_ref[...], cim, dn, precision=hi, preferred_element_type=F32))

    acr, aci, _, _ = _discretise(lrc_ref[...], lic_ref[...], dt)
    for _ in range(4):
        acr, aci = _cmul(acr, aci, acr, aci)
    lane = lax.broadcasted_iota(jnp.int32, (SSM_STATE, LANES), 1)
    qr = jnp.ones((SSM_STATE, LANES), F32)
    qi = jnp.zeros((SSM_STATE, LANES), F32)
    s = 1
    while s < LANES:
        cr_, ci_ = _cmul(pltpu.roll(qr, s, 1), pltpu.roll(qi, s, 1), acr, aci)
        qr = jnp.where(lane >= s, cr_, qr)
        qi = jnp.where(lane >= s, ci_, qi)
        acr, aci = _cmul(acr, aci, acr, aci)
        s *= 2
    apr_ref[...] = qr
    api_ref[...] = qi
    a128r_ref[...] = jnp.broadcast_to(acr, (SSM_STATE, LANES))
    a128i_ref[...] = jnp.broadcast_to(aci, (SSM_STATE, LANES))


def _ssm_operators(lam_re, lam_im, log_dt, b_re, b_im, c_re, c_im):
    g, p, h = SSM_GROUPS, SSM_STATE, SSM_GROUP
    blk = lambda a, b: pl.BlockSpec((None, a, b), lambda i: (i, 0, 0))
    outs = pl.pallas_call(
        _ssm_prep_kernel,
        grid=(g,),
        in_specs=[blk(1, p), blk(1, p), blk(1, 1), blk(p, 1), blk(p, 1),
                  blk(h, p), blk(h, p), blk(h, p), blk(h, p)],
        out_specs=[blk(SSM_CK, p)] * 4 + [blk(SSM_CK, h)] + [blk(p, LANES)] * 4,
        out_shape=[jax.ShapeDtypeStruct((g, SSM_CK, p), F32)] * 4
                  + [jax.ShapeDtypeStruct((g, SSM_CK, h), F32)]
                  + [jax.ShapeDtypeStruct((g, p, LANES), F32)] * 4,
        compiler_params=_params("parallel"),
        name="ssm_prep",
    )(lam_re.reshape(g, 1, p), lam_im.reshape(g, 1, p), log_dt.reshape(g, 1, 1),
      lam_re.reshape(g, p, 1), lam_im.reshape(g, p, 1),
      b_re.transpose(0, 2, 1), b_im.transpose(0, 2, 1), c_re, c_im)
    rtr, rti, ftr, fti, kt, apr, api, a128r, a128i = outs
    t = SSM_CHUNK
    rt = jnp.concatenate([rtr, rti], axis=-1).reshape(g, t, h, 2 * p)
    e_t = jnp.flip(rt, axis=1).reshape(g, SSM_CK, 2 * p).transpose(0, 2, 1)
    f_t = jnp.concatenate([ftr, fti], axis=-1)
    k4 = kt.reshape(g, t, h, h)
    lag = jnp.arange(t)[:, None] - jnp.arange(t)[None, :]
    blocks = jnp.where((lag >= 0)[None, :, :, None, None], k4[:, jnp.maximum(lag, 0)], 0.0)
    m_t = blocks.transpose(0, 1, 4, 2, 3).reshape(g, SSM_CK, SSM_CK)
    w2 = jnp.concatenate([m_t, f_t], axis=-1).astype(BF16)
    return w2, e_t.astype(BF16), apr, api, a128r, a128i


def _ssm_scan_kernel(u_ref, d_ref, w2_ref, et_ref, apr_ref, api_ref, a128r_ref, a128i_ref,
                     y_ref, stage_ref, xs_ref, st_ref, yt_ref, carry_ref):
    g_n, t_n, h_n, p_n, c_n = SSM_GROUPS, SSM_CHUNK, SSM_GROUP, SSM_STATE, SSM_CHUNKS_PER_TILE
    n_slab = SSM_WIDTH // LANES
    g_slab = LANES // h_n

    @pl.when(pl.program_id(1) == 0)
    def _():
        carry_ref[...] = jnp.zeros_like(carry_ref)

    for j in range(n_slab):
        stage_ref[j] = u_ref[:, j * LANES:(j + 1) * LANES]
    for t in range(t_n):
        for j in range(n_slab):
            xt = stage_ref[j, pl.ds(t, c_n, stride=t_n), :]
            xs_ref[j * g_slab:(j + 1) * g_slab, t * h_n:(t + 1) * h_n, :] = (
                xt.T.reshape(g_slab, h_n, c_n).astype(BF16))

    def local_states(g, _):
        st_ref[g] = _dot(et_ref[g], xs_ref[g, :SSM_CK, :])
        return 0
    lax.fori_loop(0, g_n, local_states, 0)

    lane = lax.broadcasted_iota(jnp.int32, (p_n, c_n), 1)

    def shifted(z, s):
        return jnp.where(lane >= s, pltpu.roll(z, s, 1), 0.0)

    def chunk_scan(g, _):
        zr, zi = st_ref[g, :p_n, :], st_ref[g, p_n:, :]
        apr, api = apr_ref[g], api_ref[g]
        s = 1
        while s < c_n:
            ar = jnp.broadcast_to(apr[:, s:s + 1], (p_n, c_n))
            ai = jnp.broadcast_to(api[:, s:s + 1], (p_n, c_n))
            dr, di = _cmul(shifted(zr, s), shifted(zi, s), ar, ai)
            zr, zi = zr + dr, zi + di
            s *= 2
        cr, ci = carry_ref[g, :p_n, :], carry_ref[g, p_n:, :]
        hr, hi = _cmul(apr, api, cr, ci)
        xs_ref[g, SSM_CK:SSM_CK + p_n, :] = (shifted(zr, 1) + hr).astype(BF16)
        xs_ref[g, SSM_CK + p_n:, :] = (shifted(zi, 1) + hi).astype(BF16)
        nr, ni = _cmul(a128r_ref[g], a128i_ref[g], cr, ci)
        carry_ref[g, :p_n, :] = jnp.broadcast_to(zr[:, c_n - 1:c_n], (p_n, c_n)) + nr
        carry_ref[g, p_n:, :] = jnp.broadcast_to(zi[:, c_n - 1:c_n], (p_n, c_n)) + ni
        return 0
    lax.fori_loop(0, g_n, chunk_scan, 0)

    def outputs(g, _):
        yg = _dot(w2_ref[g], xs_ref[g])
        yt_ref[:, pl.ds(pl.multiple_of(g * h_n, h_n), h_n), :] = yg.reshape(t_n, h_n, c_n)
        return 0
    lax.fori_loop(0, g_n, outputs, 0)

    for t in range(t_n):
        for j in range(n_slab):
            sl = slice(j * LANES, (j + 1) * LANES)
            rows = pl.ds(t, c_n, stride=t_n)
            stage_ref[j, rows, :] = yt_ref[t, sl, :].T + d_ref[:, sl] * stage_ref[j, rows, :]
    for j in range(n_slab):
        y_ref[:, j * LANES:(j + 1) * LANES] = stage_ref[j]


def _ssm_scan(u3, d_skip, ops):
    bsz, seq, _ = u3.shape
    w2, e_t, apr, api, a128r, a128i = ops
    g, p = SSM_GROUPS, SSM_STATE
    kw = SSM_CK + 2 * p
    tile = pl.BlockSpec((None, SSM_TILE, SSM_WIDTH), lambda b, i: (b, i, 0))
    return pl.pallas_call(
        _ssm_scan_kernel,
        grid=(bsz, seq // SSM_TILE),
        in_specs=[tile, _const_spec((1, SSM_WIDTH)),
                  _const_spec((g, SSM_CK, kw)), _const_spec((g, 2 * p, SSM_CK)),
                  _const_spec((g, p, LANES)), _const_spec((g, p, LANES)),
                  _const_spec((g, p, LANES)), _const_spec((g, p, LANES))],
        out_specs=tile,
        out_shape=jax.ShapeDtypeStruct(u3.shape, F32),
        scratch_shapes=[pltpu.VMEM((SSM_WIDTH // LANES, SSM_TILE, LANES), F32),
                        pltpu.VMEM((g, kw, SSM_CHUNKS_PER_TILE), BF16),
                        pltpu.VMEM((g, 2 * p, SSM_CHUNKS_PER_TILE), F32),
                        pltpu.VMEM((SSM_CHUNK, SSM_WIDTH, SSM_CHUNKS_PER_TILE), F32),
                        pltpu.VMEM((g, 2 * p, LANES), F32)],
        compiler_params=_params("parallel", "arbitrary"),
        name="ssm_scan",
    )(u3, d_skip.reshape(1, SSM_WIDTH), w2, e_t, apr, api, a128r, a128i)


def _attn_kernel(q_ref, k_ref, v_ref, o_ref, lse_ref):
    step = pl.program_id(2)
    qb, nk = ATTN_QB, ATTN_QB + WINDOW_KEYS
    row = lax.broadcasted_iota(jnp.int32, (qb, nk), 0)
    col = lax.broadcasted_iota(jnp.int32, (qb, nk), 1)
    lane = lax.broadcasted_iota(jnp.int32, (qb, LANES), 1)
    first_head = lane < HEAD_DIM
    nt = (((1,), (1,)), ((), ()))
    for sb in range(o_ref.shape[0] // qb):
        rows = slice(sb * qb, (sb + 1) * qb)
        q_start = step * o_ref.shape[0] + sb * qb
        back = jnp.minimum(q_start, WINDOW_KEYS)
        k_start = pl.multiple_of(q_start - back, ATTN_QB)
        dist = row + back - col
        valid = (dist >= 0) & (dist <= WINDOW_KEYS)
        for pair in range(GROUP_WIDTH // LANES):
            cols = slice(pair * LANES, (pair + 1) * LANES)
            qp = q_ref[rows, cols]
            kp = k_ref[pl.ds(k_start, nk), cols]
            vp = v_ref[pl.ds(k_start, nk), cols]
            outs, lses = [], []
            for sel in (first_head, ~first_head):
                qm = jnp.where(sel, qp, jnp.zeros_like(qp))
                s = lax.dot_general(qm, kp, nt, preferred_element_type=F32)
                s = jnp.where(valid, s, NEG_BIG)
                m = jnp.max(s, axis=-1, keepdims=True)
                e = jnp.exp(s - m)
                den = jnp.sum(e, axis=-1, keepdims=True)
                outs.append(_dot(e.astype(BF16), vp) / den)
                lses.append(m + jnp.log(den))
            o_ref[rows, cols] = jnp.where(first_head, outs[0], outs[1]).astype(BF16)
            lse_ref[rows, cols] = jnp.where(first_head, lses[0], lses[1])


def _attn_group(q4, k4, v4):
    bsz, dil, lr, _ = q4.shape
    rows = min(ATTN_STEP_ROWS, lr)
    q_spec = pl.BlockSpec((None, None, rows, GROUP_WIDTH), lambda b, r, i: (b, r, i, 0))
    kv_spec = pl.BlockSpec((None, None, lr, GROUP_WIDTH), lambda b, r, i: (b, r, 0, 0))
    return pl.pallas_call(
        _attn_kernel,
        grid=(bsz, dil, lr // rows),
        in_specs=[q_spec, kv_spec, kv_spec],
        out_specs=[q_spec, q_spec],
        out_shape=[jax.ShapeDtypeStruct(q4.shape, BF16), jax.ShapeDtypeStruct(q4.shape, F32)],
        compiler_params=_params("parallel", "parallel", "arbitrary"),
        name=f"attn_d{dil}",
    )(q4, k4, v4)


def _load_token_major(stage_ref, in_ref):
    dil, rows, width = in_ref.shape
    slabs = []
    for j in range(width // LANES):
        for r in range(dil):
            stage_ref[pl.ds(r, rows, stride=dil), :] = in_ref[r, :, j * LANES:(j + 1) * LANES].astype(F32)
        slabs.append(stage_ref[...])
    return jnp.concatenate(slabs, axis=1)


def _merge_kernel(x_ref, ys_ref, gluw_ref, glub_ref, wa_ref,
                  o0_ref, o1_ref, o2_ref, l0_ref, l1_ref, l2_ref, wb_ref,
                  ga_ref, gb_ref, wout_ref, h_ref, stage_ref):
    y = jax.nn.gelu(ys_ref[...])
    y = y * jax.nn.sigmoid(_dot(y.astype(BF16), gluw_ref[...]) + glub_ref[...])
    ya = _dot(y.astype(BF16), wa_ref[...])

    o0, l0 = o0_ref[0].astype(F32), l0_ref[0]
    o1, l1 = _load_token_major(stage_ref, o1_ref), _load_token_major(stage_ref, l1_ref)
    o2, l2 = _load_token_major(stage_ref, o2_ref), _load_token_major(stage_ref, l2_ref)
    top = jnp.maximum(jnp.maximum(l0, l1), l2)
    w0, w1, w2 = jnp.exp(l0 - top), jnp.exp(l1 - top), jnp.exp(l2 - top)
    attn = (w0 * o0 + w1 * o1 + w2 * o2) / (w0 + w1 + w2)
    yb = _dot(attn.astype(BF16), wb_ref[...])

    mix = ga_ref[...].astype(F32) * ya + gb_ref[...].astype(F32) * yb
    h_ref[...] = x_ref[...] + _dot(mix.astype(BF16), wout_ref[...])


def _merge(x2, seq, ys, glu_w, glu_b, w_a, attn_outs, w_b, ga, gb, w_out):
    n = x2.shape[0]
    tm = TOKEN_TILE
    tiles_per_seq = seq // tm
    row = lambda w: pl.BlockSpec((tm, w), lambda i: (i, 0))
    res = lambda d: pl.BlockSpec((None, d, tm // d, GROUP_WIDTH),
                                 lambda i: (i // tiles_per_seq, 0, i % tiles_per_seq, 0))
    (o0, l0), (o1, l1), (o2, l2) = attn_outs
    d0, d1, d2 = DILATIONS
    return pl.pallas_call(
        _merge_kernel,
        grid=(n // tm,),
        in_specs=[row(D_MODEL), row(SSM_WIDTH),
                  _const_spec((SSM_WIDTH, SSM_WIDTH)), _const_spec((1, SSM_WIDTH)),
                  _const_spec((SSM_WIDTH, D_MODEL)),
                  res(d0), res(d1), res(d2), res(d0), res(d1), res(d2),
                  _const_spec((GROUP_WIDTH, D_MODEL)), row(D_MODEL), row(D_MODEL),
                  _const_spec((D_MODEL, D_MODEL))],
        out_specs=row(D_MODEL),
        out_shape=jax.ShapeDtypeStruct((n, D_MODEL), F32),
        scratch_shapes=[pltpu.VMEM((tm, LANES), F32)],
        compiler_params=_params("parallel"),
        name="merge",
    )(x2, ys, glu_w.astype(BF16), glu_b.reshape(1, SSM_WIDTH),
      w_a.astype(BF16), o0, o1, o2, l0, l1, l2, w_b.astype(BF16), ga, gb, w_out.astype(BF16))


def _ffn_kernel(h_ref, g2_ref, wg_ref, wu_ref, cw_ref, cb_ref, wd_ref, g3_ref, wpg_ref,
                p_ref, wpp_ref, gf_ref, out_ref, act_ref, carry_ref, *, tiles_per_seq):
    tm = h_ref.shape[0]

    @pl.when(pl.program_id(0) % tiles_per_seq == 0)
    def _():
        carry_ref[...] = jnp.zeros_like(carry_ref)

    h = h_ref[...]
    u2 = _rms(h, g2_ref[...]).astype(BF16)
    row = lax.broadcasted_iota(jnp.int32, (SUBLANES, FFN_CHUNK), 0)
    for c in range(D_FF // FFN_CHUNK):
        sl = slice(c * FFN_CHUNK, (c + 1) * FFN_CHUNK)
        gp = _dot(u2, wg_ref[:, sl])
        up = _dot(u2, wu_ref[:, sl])
        prev = carry_ref[:, sl]
        r1 = pltpu.roll(gp, 1, 0)
        r2 = pltpu.roll(gp, 2, 0)
        r1 = jnp.concatenate([jnp.where(row < 1, pltpu.roll(prev, 1, 0), r1[:SUBLANES]), r1[SUBLANES:]], axis=0)
        r2 = jnp.concatenate([jnp.where(row < 2, pltpu.roll(prev, 2, 0), r2[:SUBLANES]), r2[SUBLANES:]], axis=0)
        carry_ref[:, sl] = gp[tm - SUBLANES:, :]
        gate = cw_ref[0:1, sl] * r2 + cw_ref[1:2, sl] * r1 + cw_ref[2:3, sl] * gp + cb_ref[:, sl]
        act_ref[:, sl] = (jax.nn.gelu(gate) * up).astype(BF16)
    h = h + _dot(act_ref[...], wd_ref[...])
    u3 = _rms(h, g3_ref[...]).astype(BF16)
    h = h + jax.nn.sigmoid(_dot(u3, wpg_ref[...])) * _dot(p_ref[...].astype(BF16), wpp_ref[...])
    out_ref[...] = _rms(h, gf_ref[...])


def _ffn(h1, seq, p2, norm_g, w_gate, w_up, conv_w, conv_b, w_down, ple_g, ple_w_gate, ple_w_proj, final_g):
    n = h1.shape[0]
    tm = TOKEN_TILE
    row = lambda w: pl.BlockSpec((tm, w), lambda i: (i, 0))
    vec = lambda a: a.reshape(1, -1)
    return pl.pallas_call(
        functools.partial(_ffn_kernel, tiles_per_seq=seq // tm),
        grid=(n // tm,),
        in_specs=[row(D_MODEL), _const_spec((1, D_MODEL)), _const_spec((D_MODEL, D_FF)),
                  _const_spec((D_MODEL, D_FF)), _const_spec((CONV_WIDTH, D_FF)), _const_spec((1, D_FF)),
                  _const_spec((D_FF, D_MODEL)), _const_spec((1, D_MODEL)), _const_spec((D_MODEL, D_MODEL)),
                  row(PLE_DIM), _const_spec((PLE_DIM, D_MODEL)), _const_spec((1, D_MODEL))],
        out_specs=row(D_MODEL),
        out_shape=jax.ShapeDtypeStruct((n, D_MODEL), F32),
        scratch_shapes=[pltpu.VMEM((tm, D_FF), BF16), pltpu.VMEM((SUBLANES, D_FF), F32)],
        compiler_params=_params("arbitrary"),
        name="ffn",
    )(h1, vec(norm_g), w_gate.astype(BF16), w_up.astype(BF16), conv_w, vec(conv_b),
      w_down.astype(BF16), vec(ple_g), ple_w_gate.astype(BF16), p2, ple_w_proj.astype(BF16), vec(final_g))


def _layer(h2, bsz, seq, p2, mix_norm_g, w_in, gate_b, ssm_lam_re, ssm_lam_im, ssm_log_dt, ssm_b_re,
           ssm_b_im, ssm_c_re, ssm_c_im, ssm_d, ssm_glu_w, ssm_glu_b, w_branch_a, w_branch_b, w_out,
           ffn_norm_g, ffn_w_gate, ffn_w_up, ffn_conv_w, ffn_conv_b, ffn_w_down,
           ple_norm_g, ple_w_gate, ple_w_proj, out_norm_g):
    u, qkv, ga, gb = _in_proj(h2, seq, mix_norm_g, w_in, gate_b)
    ops = _ssm_operators(ssm_lam_re, ssm_lam_im, ssm_log_dt, ssm_b_re, ssm_b_im, ssm_c_re, ssm_c_im)
    ys = _ssm_scan(u.reshape(bsz, seq, SSM_WIDTH), ssm_d.reshape(-1), ops).reshape(bsz * seq, SSM_WIDTH)
    attn_outs = [_attn_group(*group) for group in qkv]
    h1 = _merge(h2, seq, ys, ssm_glu_w, ssm_glu_b, w_branch_a, attn_outs, w_branch_b, ga, gb, w_out)
    return _ffn(h1, seq, p2, ffn_norm_g, ffn_w_gate, ffn_w_up, ffn_conv_w, ffn_conv_b, ffn_w_down,
                ple_norm_g, ple_w_gate, ple_w_proj, out_norm_g)


def kernel(x, p, mix_norm_g, w_in, gate_b, ssm_lam_re, ssm_lam_im, ssm_log_dt, ssm_b_re, ssm_b_im, ssm_c_re, ssm_c_im, ssm_d, ssm_glu_w, ssm_glu_b, w_branch_a, w_branch_b, w_out, ffn_norm_g, ffn_w_gate, ffn_w_up, ffn_conv_w, ffn_conv_b, ffn_w_down, ple_norm_g, ple_w_gate, ple_w_proj, final_norm_g):
    bsz, seq, _ = x.shape
    depth = p.shape[0]
    assert depth == 1, "the final norm is fused into the layer's last kernel"
    h2 = x.reshape(bsz * seq, D_MODEL)
    out = _layer(h2, bsz, seq, p[0].reshape(bsz * seq, PLE_DIM), mix_norm_g[0], w_in[0], gate_b[0],
                 ssm_lam_re[0], ssm_lam_im[0], ssm_log_dt[0], ssm_b_re[0], ssm_b_im[0], ssm_c_re[0],
                 ssm_c_im[0], ssm_d[0], ssm_glu_w[0], ssm_glu_b[0], w_branch_a[0], w_branch_b[0],
                 w_out[0], ffn_norm_g[0], ffn_w_gate[0], ffn_w_up[0], ffn_conv_w[0], ffn_conv_b[0],
                 ffn_w_down[0], ple_norm_g[0], ple_w_gate[0], ple_w_proj[0], final_norm_g)
    return out.reshape(bsz, seq, D_MODEL)
```

```python
import functools

import jax
import jax.numpy as jnp
from jax import lax
from jax.experimental import pallas as pl
from jax.experimental.pallas import tpu as pltpu

F32 = jnp.float32
BF16 = jnp.bfloat16

D_MODEL = 1024
EPS = 1e-6
PLE_DIM = 256
SSM_GROUP = 16
SSM_STATE = 64
SSM_WIDTH = 512
SSM_GROUPS = SSM_WIDTH // SSM_GROUP
HEAD_DIM = 64
DILATIONS = (1, 4, 16)
WINDOW_KEYS = 128
HEADS_PER_GROUP = 4
GROUP_WIDTH = HEADS_PER_GROUP * HEAD_DIM
ATTN_WIDTH = len(DILATIONS) * GROUP_WIDTH
ROT_DIM = HEAD_DIM // 4
ROPE_THETA = 500000.0
NEG_BIG = -1e30
D_FF = 2816
CONV_WIDTH = 3
OFF_Q = SSM_WIDTH
OFF_K = OFF_Q + ATTN_WIDTH
OFF_V = OFF_K + ATTN_WIDTH
OFF_GA = OFF_V + ATTN_WIDTH
OFF_GB = OFF_GA + D_MODEL
IN_WIDTH = OFF_GB + D_MODEL

LANES = 128
SUBLANES = 8
VMEM_LIMIT_BYTES = 56 * 1024 * 1024

TOKEN_TILE = 512
SSM_CHUNK = 16
SSM_CHUNKS_PER_TILE = LANES
SSM_TILE = SSM_CHUNK * SSM_CHUNKS_PER_TILE
SSM_CK = SSM_CHUNK * SSM_GROUP
SSM_ROW = SSM_CHUNK * SSM_WIDTH
SSM_PAIRS = SSM_GROUPS // 2
SSM_LOG_STEPS = 7
ATTN_QB = 128
ATTN_STEP_ROWS = 512
FFN_CHUNK = 256


def _dot(a, b):
    return jnp.dot(a, b, preferred_element_type=F32)


def _rms(x, g):
    var = jnp.mean(x * x, axis=-1, keepdims=True)
    return x * lax.rsqrt(var + EPS) * g


def _const_spec(shape):
    nd = len(shape)
    return pl.BlockSpec(shape, lambda *_: (0,) * nd, pipeline_mode=pl.Buffered(1))


def _params(*sem):
    return pltpu.CompilerParams(dimension_semantics=sem, vmem_limit_bytes=VMEM_LIMIT_BYTES)


def _rope(z, cos, sin_lo, sin_hi):
    up = pltpu.roll(z, LANES - ROT_DIM // 2, 1)
    dn = pltpu.roll(z, ROT_DIM // 2, 1)
    return z * cos + up * sin_lo + dn * sin_hi


def _store_residue_major(stage_ref, out_ref, z, dil):
    rows = z.shape[0] // dil
    for j in range(z.shape[1] // LANES):
        sl = slice(j * LANES, (j + 1) * LANES)
        stage_ref[...] = z[:, sl]
        for r in range(dil):
            out_ref[r, :, sl] = stage_ref[pl.ds(r, rows, stride=dil), :].astype(out_ref.dtype)


def _store_chunk_rows(stage_ref, out_ref, z):
    rows, width = z.shape[0] // SSM_CHUNK, z.shape[1]
    for j in range(width // LANES):
        stage_ref[...] = z[:, j * LANES:(j + 1) * LANES]
        for t in range(SSM_CHUNK):
            col = t * width + j * LANES
            out_ref[:, col:col + LANES] = stage_ref[pl.ds(t, rows, stride=SSM_CHUNK), :]


def _load_chunk_rows(stage_ref, in_ref, width):
    rows = in_ref.shape[0]
    slabs = []
    for j in range(width // LANES):
        for t in range(SSM_CHUNK):
            col = t * width + j * LANES
            stage_ref[pl.ds(t, rows, stride=SSM_CHUNK), :] = in_ref[:, col:col + LANES]
        slabs.append(stage_ref[...])
    return jnp.concatenate(slabs, axis=1)


def _in_proj_kernel(x_ref, g_ref, cos_ref, slo_ref, shi_ref, w_ref, ba_ref, bb_ref,
                    s_ref, q0_ref, k0_ref, v0_ref, q1_ref, k1_ref, v1_ref, q2_ref, k2_ref, v2_ref,
                    ga_ref, gb_ref, stage_ref):
    u = _rms(x_ref[...], g_ref[...]).astype(BF16)
    _store_chunk_rows(stage_ref, s_ref, _dot(u, w_ref[:, :OFF_Q]))
    cos, slo, shi = cos_ref[...], slo_ref[...], shi_ref[...]
    scale = HEAD_DIM ** -0.5

    def rope(z):
        return jnp.concatenate([_rope(z[:, j * LANES:(j + 1) * LANES], cos, slo, shi)
                                for j in range(z.shape[1] // LANES)], axis=1)

    q = rope(_dot(u, w_ref[:, OFF_Q:OFF_K])) * scale
    k = rope(_dot(u, w_ref[:, OFF_K:OFF_V]))
    v = _dot(u, w_ref[:, OFF_V:OFF_GA])
    for z, refs in ((q, (q0_ref, q1_ref, q2_ref)), (k, (k0_ref, k1_ref, k2_ref)), (v, (v0_ref, v1_ref, v2_ref))):
        refs[0][...] = z[:, :GROUP_WIDTH].astype(BF16)
        for grp in (1, 2):
            _store_residue_major(stage_ref, refs[grp], z[:, grp * GROUP_WIDTH:(grp + 1) * GROUP_WIDTH],
                                 DILATIONS[grp])
    ga_ref[...] = jax.nn.sigmoid(_dot(u, w_ref[:, OFF_GA:OFF_GB]) + ba_ref[...]).astype(BF16)
    gb_ref[...] = jax.nn.sigmoid(_dot(u, w_ref[:, OFF_GB:]) + bb_ref[...]).astype(BF16)


def _rope_tables(seq):
    half = ROT_DIM // 2
    pos = jnp.arange(seq, dtype=F32)
    freqs = ROPE_THETA ** (-jnp.arange(half, dtype=F32) * (2.0 / ROT_DIM))
    ang = pos[:, None] * freqs[None, :]
    cos, sin = jnp.cos(ang), jnp.sin(ang)
    ones = jnp.ones((seq, HEAD_DIM - ROT_DIM), F32)
    zeros = jnp.zeros((seq, HEAD_DIM - half), F32)
    cos_h = jnp.concatenate([cos, cos, ones], axis=1)
    slo_h = jnp.concatenate([-sin, zeros], axis=1)
    shi_h = jnp.concatenate([jnp.zeros((seq, half), F32), sin, ones * 0.0], axis=1)
    rep = LANES // HEAD_DIM
    return tuple(jnp.tile(t, (1, rep)) for t in (cos_h, slo_h, shi_h))


def _in_proj(x2, seq, norm_g, w_in, gate_b):
    n = x2.shape[0]
    tm = TOKEN_TILE
    tiles_per_seq = seq // tm
    bsz = n // seq
    ba = gate_b[:D_MODEL].reshape(1, D_MODEL)
    bb = gate_b[D_MODEL:].reshape(1, D_MODEL)
    cos, slo, shi = _rope_tables(seq)

    row = lambda w: pl.BlockSpec((tm, w), lambda i: (i, 0))
    tab = pl.BlockSpec((tm, LANES), lambda i: (i % tiles_per_seq, 0))
    res = lambda d: pl.BlockSpec((None, d, tm // d, GROUP_WIDTH),
                                 lambda i: (i // tiles_per_seq, 0, i % tiles_per_seq, 0))
    res_shape = lambda d: jax.ShapeDtypeStruct((bsz, d, seq // d, GROUP_WIDTH), BF16)
    nat_shape = jax.ShapeDtypeStruct((n, GROUP_WIDTH), BF16)
    d1, d2 = DILATIONS[1], DILATIONS[2]
    outs = pl.pallas_call(
        _in_proj_kernel,
        grid=(n // tm,),
        in_specs=[row(D_MODEL), _const_spec((1, D_MODEL)), tab, tab, tab,
                  _const_spec((D_MODEL, IN_WIDTH)), _const_spec((1, D_MODEL)), _const_spec((1, D_MODEL))],
        out_specs=[pl.BlockSpec((tm // SSM_CHUNK, SSM_ROW), lambda i: (i, 0))]
                  + [row(GROUP_WIDTH)] * 3 + [res(d1)] * 3 + [res(d2)] * 3 + [row(D_MODEL), row(D_MODEL)],
        out_shape=[jax.ShapeDtypeStruct((n // SSM_CHUNK, SSM_ROW), F32)] + [nat_shape] * 3
                  + [res_shape(d1)] * 3 + [res_shape(d2)] * 3
                  + [jax.ShapeDtypeStruct((n, D_MODEL), BF16), jax.ShapeDtypeStruct((n, D_MODEL), BF16)],
        scratch_shapes=[pltpu.VMEM((tm, LANES), F32)],
        compiler_params=_params("parallel"),
        name="in_proj",
    )(x2, norm_g.reshape(1, D_MODEL), cos, slo, shi, w_in.astype(BF16), ba, bb)
    u, q0, k0, v0, q1, k1, v1, q2, k2, v2, ga, gb = outs
    nat4 = lambda a: a.reshape(bsz, 1, seq, GROUP_WIDTH)
    qkv = ((nat4(q0), nat4(k0), nat4(v0)), (q1, k1, v1), (q2, k2, v2))
    return u, qkv, ga, gb


def _cmul(ar, ai, br, bi):
    return ar * br - ai * bi, ar * bi + ai * br


def _discretise(lr, li, dt):
    mag = jnp.exp(lr * dt)
    ar = mag * jnp.cos(li * dt)
    ai = mag * jnp.sin(li * dt)
    den = lr * lr + li * li
    cr = ((ar - 1.0) * lr + ai * li) / den
    ci = (ai * lr - (ar - 1.0) * li) / den
    return ar, ai, cr, ci


def _ssm_prep_kernel(lr_ref, li_ref, ldt_ref, brt_ref, bit_ref, cre_ref, cim_ref, lrf_ref, lif_ref, ldtf_ref,
                     rtr_ref, rti_ref, ftr_ref, fti_ref, kt_ref, apsr_ref, apsi_ref, aptr_ref, apti_ref):
    ar, ai, cr, ci = _discretise(lr_ref[...], li_ref[...], jnp.exp(ldt_ref[...]))
    brt, bit = brt_ref[...], bit_ref[...]
    bbr = cr * brt - ci * bit
    bbi = cr * bit + ci * brt
    cre, cim = cre_ref[...], cim_ref[...]
    pr, pi = jnp.ones_like(ar), jnp.zeros_like(ai)
    for j in range(SSM_CHUNK):
        rtr_ref[j], rti_ref[j] = _cmul(pr, pi, bbr, bbi)
        pr, pi = _cmul(pr, pi, ar, ai)
        ftr_ref[j] = cre * pr - cim * pi
        fti_ref[j] = -cre * pi - cim * pr
    nt = (((1,), (1,)), ((), ()))
    hi = lax.Precision.HIGHEST
    for g in range(SSM_GROUPS):
        rows = slice(g * SSM_GROUP, (g + 1) * SSM_GROUP)
        rg_r = rtr_ref[:, rows, :].reshape(SSM_CK, SSM_STATE)
        rg_i = rti_ref[:, rows, :].reshape(SSM_CK, SSM_STATE)
        kt_ref[g] = (lax.dot_general(cre[rows], rg_r, nt, precision=hi, preferred_element_type=F32)
                     - lax.dot_general(cim[rows], rg_i, nt, precision=hi, preferred_element_type=F32))

    acr, aci, _, _ = _discretise(lrf_ref[...], lif_ref[...], jnp.exp(ldtf_ref[...]))
    for _ in range(4):
        acr, aci = _cmul(acr, aci, acr, aci)
    shape = (SUBLANES, acr.shape[1])
    row = lax.broadcasted_iota(jnp.int32, shape, 0)
    qr, qi = jnp.ones(shape, F32), jnp.zeros(shape, F32)
    for k in range(SSM_LOG_STEPS + 1):
        apsr_ref[k:k + 1, :] = acr
        apsi_ref[k:k + 1, :] = aci
        if (1 << k) < SUBLANES:
            nr, ni = _cmul(qr, qi, acr, aci)
            bit_set = (row & (1 << k)) != 0
            qr, qi = jnp.where(bit_set, nr, qr), jnp.where(bit_set, ni, qi)
        if (1 << k) == SUBLANES:
            a8r, a8i = acr, aci
        acr, aci = _cmul(acr, aci, acr, aci)
    br, bi = jnp.ones_like(a8r), jnp.zeros_like(a8i)
    for b in range(SSM_CHUNKS_PER_TILE // SUBLANES):
        rows = slice(b * SUBLANES, (b + 1) * SUBLANES)
        aptr_ref[rows, :], apti_ref[rows, :] = _cmul(qr, qi, br, bi)
        br, bi = _cmul(br, bi, a8r, a8i)


def _pair_lanes(re, im):
    z = jnp.zeros_like(re)
    even = (jnp.arange(re.shape[0]) % 2 == 0)[:, None, None]
    return jnp.where(even, jnp.concatenate([re, z, im, z], -1), jnp.concatenate([z, re, z, im], -1))


def _ssm_operators(lam_re, lam_im, log_dt, b_re, b_im, c_re, c_im):
    g, p, h, t = SSM_GROUPS, SSM_STATE, SSM_GROUP, SSM_CHUNK
    gh, gp = g * h, g * p
    rep = lambda a: jnp.repeat(a, h, axis=0)
    ldt2 = jnp.broadcast_to(log_dt[:, None], (g, p))
    full = lambda shape: pl.BlockSpec(shape, lambda: (0,) * len(shape))
    in_arrays = (rep(lam_re), rep(lam_im), rep(ldt2),
                 b_re.transpose(0, 2, 1).reshape(gh, p), b_im.transpose(0, 2, 1).reshape(gh, p),
                 c_re.reshape(gh, p), c_im.reshape(gh, p),
                 lam_re.reshape(1, gp), lam_im.reshape(1, gp), ldt2.reshape(1, gp))
    out_shapes = ([(t, gh, p)] * 4 + [(g, h, SSM_CK)] + [(SUBLANES, gp)] * 2 + [(SSM_CHUNKS_PER_TILE, gp)] * 2)
    rtr, rti, ftr, fti, kt, apsr, apsi, aptr, apti = pl.pallas_call(
        _ssm_prep_kernel,
        in_specs=[full(a.shape) for a in in_arrays],
        out_specs=[full(s) for s in out_shapes],
        out_shape=[jax.ShapeDtypeStruct(s, F32) for s in out_shapes],
        compiler_params=pltpu.CompilerParams(vmem_limit_bytes=VMEM_LIMIT_BYTES),
        name="ssm_prep",
    )(*in_arrays)
    by_group = lambda a: a.reshape(t, g, h, p).transpose(1, 0, 2, 3).reshape(g, SSM_CK, p)
    e2 = _pair_lanes(by_group(jnp.flip(rtr, 0)), by_group(jnp.flip(rti, 0)))
    e2 = e2.reshape(SSM_PAIRS, 2 * SSM_CK, 4 * p).astype(BF16)
    f_pad = _pair_lanes(by_group(ftr), by_group(fti))
    lag = jnp.arange(t)[:, None] - jnp.arange(t)[None, :]
    k4 = kt.reshape(g, h, t, h)
    blocks = jnp.where((lag >= 0)[None, None, :, :, None], k4[:, :, jnp.maximum(lag, 0)], 0.0)
    m_t = blocks.transpose(0, 2, 1, 3, 4).reshape(g, SSM_CK, SSM_CK)
    w2 = jnp.concatenate([m_t, f_pad], axis=-1).astype(BF16)
    by_pair = lambda a: a.reshape(a.shape[0], SSM_PAIRS, 2 * p).transpose(1, 0, 2)
    return w2, e2, by_pair(apsr), by_pair(apsi), by_pair(aptr), by_pair(apti)


def _shift_rows(z, s, row):
    if s % SUBLANES == 0:
        return jnp.concatenate([jnp.zeros((s, z.shape[1]), z.dtype), z[:-s]], axis=0)
    return jnp.where(row >= s, pltpu.roll(z, s, 0), 0.0)


def _ssm_scan_kernel(u_ref, d_ref, w2_ref, e2_ref, apsr_ref, apsi_ref, aptr_ref, apti_ref,
                     y_ref, xs_ref, sc_ref, yt_ref, carry_ref):
    t_n, h_n, c_n = SSM_CHUNK, SSM_GROUP, SSM_CHUNKS_PER_TILE
    n_slab = SSM_WIDTH // LANES
    pairs_per_slab = LANES // (2 * h_n)
    sw = 2 * SSM_STATE

    @pl.when(pl.program_id(1) == 0)
    def _():
        carry_ref[...] = jnp.zeros_like(carry_ref)

    for t in range(t_n):
        for j in range(n_slab):
            col = t * SSM_WIDTH + j * LANES
            blk = u_ref[:, col:col + LANES].T.astype(BF16)
            xs_ref[j * pairs_per_slab:(j + 1) * pairs_per_slab, :, t * h_n:(t + 1) * h_n, :] = (
                blk.reshape(pairs_per_slab, 2, h_n, c_n))

    row = lax.broadcasted_iota(jnp.int32, (c_n, sw), 0)
    nt = (((1,), (1,)), ((), ()))
    tn = (((0,), (0,)), ((), ()))

    def pair_body(pr, _):
        xp = xs_ref[pr].reshape(2 * SSM_CK, c_n)
        loc = lax.dot_general(xp, e2_ref[pr], tn, preferred_element_type=F32)
        zr, zi = loc[:, :sw], loc[:, sw:]
        for k in range(SSM_LOG_STEPS):
            s = 1 << k
            ar, ai = apsr_ref[pr, k:k + 1, :], apsi_ref[pr, k:k + 1, :]
            dr, di = _cmul(_shift_rows(zr, s, row), _shift_rows(zi, s, row), ar, ai)
            zr, zi = zr + dr, zi + di
        cr, ci = carry_ref[pr, 0:1, :], carry_ref[pr, 1:2, :]
        hr, hi = _cmul(aptr_ref[pr], apti_ref[pr], cr, ci)
        sc_ref[pr, :, :sw] = (_shift_rows(zr, 1, row) + hr).astype(BF16)
        sc_ref[pr, :, sw:] = (_shift_rows(zi, 1, row) + hi).astype(BF16)
        k_last = SSM_LOG_STEPS
        nr, ni = _cmul(apsr_ref[pr, k_last:k_last + 1, :], apsi_ref[pr, k_last:k_last + 1, :], cr, ci)
        carry_ref[pr, 0:1, :] = zr[c_n - 1:c_n, :] + nr
        carry_ref[pr, 1:2, :] = zi[c_n - 1:c_n, :] + ni
        for gi in range(2):
            g = pr * 2 + gi
            yg = (_dot(w2_ref[g, :, :SSM_CK], xs_ref[pr, gi])
                  + lax.dot_general(w2_ref[g, :, SSM_CK:], sc_ref[pr], nt, preferred_element_type=F32))
            yt_ref[:, pl.ds(pl.multiple_of(g * h_n, h_n), h_n), :] = yg.reshape(t_n, h_n, c_n)
        return 0
    lax.fori_loop(0, SSM_PAIRS, pair_body, 0, unroll=2)

    for t in range(t_n):
        for j in range(n_slab):
            sl = slice(j * LANES, (j + 1) * LANES)
            col = t * SSM_WIDTH + j * LANES
            y_ref[:, col:col + LANES] = yt_ref[t, sl, :].T + d_ref[:, sl] * u_ref[:, col:col + LANES]


def _ssm_scan(u_rows, bsz, d_skip, ops):
    w2, e2, apsr, apsi, aptr, apti = ops
    g, p, c_n = SSM_GROUPS, SSM_STATE, SSM_CHUNKS_PER_TILE
    tiles = u_rows.shape[0] // (bsz * c_n)
    tile = pl.BlockSpec((c_n, SSM_ROW), lambda b, i: (b * tiles + i, 0))
    return pl.pallas_call(
        _ssm_scan_kernel,
        grid=(bsz, tiles),
        in_specs=[tile, _const_spec((1, SSM_WIDTH)),
                  _const_spec(w2.shape), _const_spec(e2.shape),
                  _const_spec(apsr.shape), _const_spec(apsi.shape),
                  _const_spec(aptr.shape), _const_spec(apti.shape)],
        out_specs=tile,
        out_shape=jax.ShapeDtypeStruct(u_rows.shape, F32),
        scratch_shapes=[pltpu.VMEM((SSM_PAIRS, 2, SSM_CK, c_n), BF16),
                        pltpu.VMEM((SSM_PAIRS, c_n, 4 * p), BF16),
                        pltpu.VMEM((SSM_CHUNK, SSM_WIDTH, c_n), F32),
                        pltpu.VMEM((SSM_PAIRS, SUBLANES, 2 * p), F32)],
        compiler_params=_params("parallel", "arbitrary"),
        name="ssm_scan",
    )(u_rows, d_skip.reshape(1, SSM_WIDTH), w2, e2, apsr, apsi, aptr, apti)


def _attn_kernel(q_ref, k_ref, v_ref, o_ref, lse_ref):
    step = pl.program_id(2)
    qb, nk = ATTN_QB, ATTN_QB + WINDOW_KEYS
    row = lax.broadcasted_iota(jnp.int32, (qb, nk), 0)
    col = lax.broadcasted_iota(jnp.int32, (qb, nk), 1)
    lane = lax.broadcasted_iota(jnp.int32, (qb, LANES), 1)
    first_head = lane < HEAD_DIM
    nt = (((1,), (1,)), ((), ()))
    for sb in range(o_ref.shape[0] // qb):
        rows = slice(sb * qb, (sb + 1) * qb)
        q_start = step * o_ref.shape[0] + sb * qb
        back = jnp.minimum(q_start, WINDOW_KEYS)
        k_start = pl.multiple_of(q_start - back, ATTN_QB)
        dist = row + back - col
        valid = (dist >= 0) & (dist <= WINDOW_KEYS)
        for pair in range(GROUP_WIDTH // LANES):
            cols = slice(pair * LANES, (pair + 1) * LANES)
            qp = q_ref[rows, cols]
            kp = k_ref[pl.ds(k_start, nk), cols]
            vp = v_ref[pl.ds(k_start, nk), cols]
            outs, lses = [], []
            for sel in (first_head, ~first_head):
                qm = jnp.where(sel, qp, jnp.zeros_like(qp))
                s = lax.dot_general(qm, kp, nt, preferred_element_type=F32)
                s = jnp.where(valid, s, NEG_BIG)
                m = jnp.max(s, axis=-1, keepdims=True)
                e = jnp.exp(s - m)
                den = jnp.sum(e, axis=-1, keepdims=True)
                outs.append(_dot(e.astype(BF16), vp) / den)
                lses.append(m + jnp.log(den))
            o_ref[rows, cols] = jnp.where(first_head, outs[0], outs[1]).astype(BF16)
            lse_ref[rows, cols] = jnp.where(first_head, lses[0], lses[1])


def _attn_group(q4, k4, v4):
    bsz, dil, lr, _ = q4.shape
    rows = min(ATTN_STEP_ROWS, lr)
    q_spec = pl.BlockSpec((None, None, rows, GROUP_WIDTH), lambda b, r, i: (b, r, i, 0))
    kv_spec = pl.BlockSpec((None, None, lr, GROUP_WIDTH), lambda b, r, i: (b, r, 0, 0))
    return pl.pallas_call(
        _attn_kernel,
        grid=(bsz, dil, lr // rows),
        in_specs=[q_spec, kv_spec, kv_spec],
        out_specs=[q_spec, q_spec],
        out_shape=[jax.ShapeDtypeStruct(q4.shape, BF16), jax.ShapeDtypeStruct(q4.shape, F32)],
        compiler_params=_params("parallel", "parallel", "arbitrary"),
        name=f"attn_d{dil}",
    )(q4, k4, v4)


def _load_token_major(stage_ref, in_ref):
    dil, rows, width = in_ref.shape
    slabs = []
    for j in range(width // LANES):
        for r in range(dil):
            stage_ref[pl.ds(r, rows, stride=dil), :] = in_ref[r, :, j * LANES:(j + 1) * LANES].astype(F32)
        slabs.append(stage_ref[...])
    return jnp.concatenate(slabs, axis=1)


def _merge_kernel(x_ref, ys_ref, gluw_ref, glub_ref, wa_ref,
                  o0_ref, o1_ref, o2_ref, l0_ref, l1_ref, l2_ref, wb_ref,
                  ga_ref, gb_ref, wout_ref, h_ref, stage_ref):
    y = jax.nn.gelu(_load_chunk_rows(stage_ref, ys_ref, SSM_WIDTH))
    y = y * jax.nn.sigmoid(_dot(y.astype(BF16), gluw_ref[...]) + glub_ref[...])
    ya = _dot(y.astype(BF16), wa_ref[...])

    o0, l0 = o0_ref[0].astype(F32), l0_ref[0]
    o1, l1 = _load_token_major(stage_ref, o1_ref), _load_token_major(stage_ref, l1_ref)
    o2, l2 = _load_token_major(stage_ref, o2_ref), _load_token_major(stage_ref, l2_ref)
    top = jnp.maximum(jnp.maximum(l0, l1), l2)
    w0, w1, w2 = jnp.exp(l0 - top), jnp.exp(l1 - top), jnp.exp(l2 - top)
    attn = (w0 * o0 + w1 * o1 + w2 * o2) / (w0 + w1 + w2)
    yb = _dot(attn.astype(BF16), wb_ref[...])

    mix = ga_ref[...].astype(F32) * ya + gb_ref[...].astype(F32) * yb
    h_ref[...] = x_ref[...] + _dot(mix.astype(BF16), wout_ref[...])


def _merge(x2, seq, ys_rows, glu_w, glu_b, w_a, attn_outs, w_b, ga, gb, w_out):
    n = x2.shape[0]
    tm = TOKEN_TILE
    tiles_per_seq = seq // tm
    row = lambda w: pl.BlockSpec((tm, w), lambda i: (i, 0))
    res = lambda d: pl.BlockSpec((None, d, tm // d, GROUP_WIDTH),
                                 lambda i: (i // tiles_per_seq, 0, i % tiles_per_seq, 0))
    (o0, l0), (o1, l1), (o2, l2) = attn_outs
    d0, d1, d2 = DILATIONS
    return pl.pallas_call(
        _merge_kernel,
        grid=(n // tm,),
        in_specs=[row(D_MODEL), pl.BlockSpec((tm // SSM_CHUNK, SSM_ROW), lambda i: (i, 0)),
                  _const_spec((SSM_WIDTH, SSM_WIDTH)), _const_spec((1, SSM_WIDTH)),
                  _const_spec((SSM_WIDTH, D_MODEL)),
                  res(d0), res(d1), res(d2), res(d0), res(d1), res(d2),
                  _const_spec((GROUP_WIDTH, D_MODEL)), row(D_MODEL), row(D_MODEL),
                  _const_spec((D_MODEL, D_MODEL))],
        out_specs=row(D_MODEL),
        out_shape=jax.ShapeDtypeStruct((n, D_MODEL), F32),
        scratch_shapes=[pltpu.VMEM((tm, LANES), F32)],
        compiler_params=_params("parallel"),
        name="merge",
    )(x2, ys_rows, glu_w.astype(BF16), glu_b.reshape(1, SSM_WIDTH),
      w_a.astype(BF16), o0, o1, o2, l0, l1, l2, w_b.astype(BF16), ga, gb, w_out.astype(BF16))


def _ffn_kernel(h_ref, g2_ref, wg_ref, wu_ref, cw_ref, cb_ref, wd_ref, g3_ref, wpg_ref,
                p_ref, wpp_ref, gf_ref, out_ref, act_ref, carry_ref, *, tiles_per_seq):
    tm = h_ref.shape[0]

    @pl.when(pl.program_id(0) % tiles_per_seq == 0)
    def _():
        carry_ref[...] = jnp.zeros_like(carry_ref)

    h = h_ref[...]
    u2 = _rms(h, g2_ref[...]).astype(BF16)
    row = lax.broadcasted_iota(jnp.int32, (SUBLANES, FFN_CHUNK), 0)
    for c in range(D_FF // FFN_CHUNK):
        sl = slice(c * FFN_CHUNK, (c + 1) * FFN_CHUNK)
        gp = _dot(u2, wg_ref[:, sl])
        up = _dot(u2, wu_ref[:, sl])
        prev = carry_ref[:, sl]
        r1 = pltpu.roll(gp, 1, 0)
        r2 = pltpu.roll(gp, 2, 0)
        r1 = jnp.concatenate([jnp.where(row < 1, pltpu.roll(prev, 1, 0), r1[:SUBLANES]), r1[SUBLANES:]], axis=0)
        r2 = jnp.concatenate([jnp.where(row < 2, pltpu.roll(prev, 2, 0), r2[:SUBLANES]), r2[SUBLANES:]], axis=0)
        carry_ref[:, sl] = gp[tm - SUBLANES:, :]
        gate = cw_ref[0:1, sl] * r2 + cw_ref[1:2, sl] * r1 + cw_ref[2:3, sl] * gp + cb_ref[:, sl]
        act_ref[:, sl] = (jax.nn.gelu(gate) * up).astype(BF16)
    h = h + _dot(act_ref[...], wd_ref[...])
    u3 = _rms(h, g3_ref[...]).astype(BF16)
    h = h + jax.nn.sigmoid(_dot(u3, wpg_ref[...])) * _dot(p_ref[...].astype(BF16), wpp_ref[...])
    out_ref[...] = _rms(h, gf_ref[...])


def _ffn(h1, seq, p2, norm_g, w_gate, w_up, conv_w, conv_b, w_down, ple_g, ple_w_gate, ple_w_proj, final_g):
    n = h1.shape[0]
    tm = TOKEN_TILE
    row = lambda w: pl.BlockSpec((tm, w), lambda i: (i, 0))
    vec = lambda a: a.reshape(1, -1)
    return pl.pallas_call(
        functools.partial(_ffn_kernel, tiles_per_seq=seq // tm),
        grid=(n // tm,),
        in_specs=[row(D_MODEL), _const_spec((1, D_MODEL)), _const_spec((D_MODEL, D_FF)),
                  _const_spec((D_MODEL, D_FF)), _const_spec((CONV_WIDTH, D_FF)), _const_spec((1, D_FF)),
                  _const_spec((D_FF, D_MODEL)), _const_spec((1, D_MODEL)), _const_spec((D_MODEL, D_MODEL)),
                  row(PLE_DIM), _const_spec((PLE_DIM, D_MODEL)), _const_spec((1, D_MODEL))],
        out_specs=row(D_MODEL),
        out_shape=jax.ShapeDtypeStruct((n, D_MODEL), F32),
        scratch_shapes=[pltpu.VMEM((tm, D_FF), BF16), pltpu.VMEM((SUBLANES, D_FF), F32)],
        compiler_params=_params("arbitrary"),
        name="ffn",
    )(h1, vec(norm_g), w_gate.astype(BF16), w_up.astype(BF16), conv_w, vec(conv_b),
      w_down.astype(BF16), vec(ple_g), ple_w_gate.astype(BF16), p2, ple_w_proj.astype(BF16), vec(final_g))


def _layer(h2, bsz, seq, p2, mix_norm_g, w_in, gate_b, ssm_lam_re, ssm_lam_im, ssm_log_dt, ssm_b_re,
           ssm_b_im, ssm_c_re, ssm_c_im, ssm_d, ssm_glu_w, ssm_glu_b, w_branch_a, w_branch_b, w_out,
           ffn_norm_g, ffn_w_gate, ffn_w_up, ffn_conv_w, ffn_conv_b, ffn_w_down,
           ple_norm_g, ple_w_gate, ple_w_proj, out_norm_g):
    u_rows, qkv, ga, gb = _in_proj(h2, seq, mix_norm_g, w_in, gate_b)
    ops = _ssm_operators(ssm_lam_re, ssm_lam_im, ssm_log_dt, ssm_b_re, ssm_b_im, ssm_c_re, ssm_c_im)
    ys_rows = _ssm_scan(u_rows, bsz, ssm_d.reshape(-1), ops)
    attn_outs = [_attn_group(*group) for group in qkv]
    h1 = _merge(h2, seq, ys_rows, ssm_glu_w, ssm_glu_b, w_branch_a, attn_outs, w_branch_b, ga, gb, w_out)
    return _ffn(h1, seq, p2, ffn_norm_g, ffn_w_gate, ffn_w_up, ffn_conv_w, ffn_conv_b, ffn_w_down,
                ple_norm_g, ple_w_gate, ple_w_proj, out_norm_g)


def kernel(x, p, mix_norm_g, w_in, gate_b, ssm_lam_re, ssm_lam_im, ssm_log_dt, ssm_b_re, ssm_b_im, ssm_c_re, ssm_c_im, ssm_d, ssm_glu_w, ssm_glu_b, w_branch_a, w_branch_b, w_out, ffn_norm_g, ffn_w_gate, ffn_w_up, ffn_conv_w, ffn_conv_b, ffn_w_down, ple_norm_g, ple_w_gate, ple_w_proj, final_norm_g):
    bsz, seq, _ = x.shape
    depth = p.shape[0]
    assert depth == 1, "the final norm is fused into the layer's last kernel"
    h2 = x.reshape(bsz * seq, D_MODEL)
    out = _layer(h2, bsz, seq, p[0].reshape(bsz * seq, PLE_DIM), mix_norm_g[0], w_in[0], gate_b[0],
                 ssm_lam_re[0], ssm_lam_im[0], ssm_log_dt[0], ssm_b_re[0], ssm_b_im[0], ssm_c_re[0],
                 ssm_c_im[0], ssm_d[0], ssm_glu_w[0], ssm_glu_b[0], w_branch_a[0], w_branch_b[0],
                 w_out[0], ffn_norm_g[0], ffn_w_gate[0], ffn_w_up[0], ffn_conv_w[0], ffn_conv_b[0],
                 ffn_w_down[0], ple_norm_g[0], ple_w_gate[0], ple_w_proj[0], final_norm_g)
    return out.reshape(bsz, seq, D_MODEL)
```

```python
import functools

import jax
import jax.numpy as jnp
from jax import lax
from jax.experimental import pallas as pl
from jax.experimental.pallas import tpu as pltpu

F32 = jnp.float32
BF16 = jnp.bfloat16

D_MODEL = 1024
EPS = 1e-6
PLE_DIM = 256
SSM_GROUP = 16
SSM_STATE = 64
SSM_WIDTH = 512
SSM_GROUPS = SSM_WIDTH // SSM_GROUP
HEAD_DIM = 64
DILATIONS = (1, 4, 16)
WINDOW_KEYS = 128
HEADS_PER_GROUP = 4
GROUP_WIDTH = HEADS_PER_GROUP * HEAD_DIM
ATTN_WIDTH = len(DILATIONS) * GROUP_WIDTH
ROT_DIM = HEAD_DIM // 4
ROPE_THETA = 500000.0
NEG_BIG = -1e30
D_FF = 2816
CONV_WIDTH = 3
OFF_Q = SSM_WIDTH
OFF_K = OFF_Q + ATTN_WIDTH
OFF_V = OFF_K + ATTN_WIDTH
OFF_GA = OFF_V + ATTN_WIDTH
OFF_GB = OFF_GA + D_MODEL
IN_WIDTH = OFF_GB + D_MODEL

LANES = 128
SUBLANES = 8
VMEM_LIMIT_BYTES = 56 * 1024 * 1024

TOKEN_TILE = 512
SSM_CHUNK = 16
SSM_CHUNKS_PER_TILE = LANES
SSM_TILE = SSM_CHUNK * SSM_CHUNKS_PER_TILE
SSM_CK = SSM_CHUNK * SSM_GROUP
SSM_ROW = SSM_CHUNK * SSM_WIDTH
SSM_PAIRS = SSM_GROUPS // 2
SSM_LOG_STEPS = 7
ATTN_QB = 128
ATTN_STEP_ROWS = 512
FFN_CHUNK = 256


def _dot(a, b):
    return jnp.dot(a, b, preferred_element_type=F32)


def _rms(x, g):
    var = jnp.mean(x * x, axis=-1, keepdims=True)
    return x * lax.rsqrt(var + EPS) * g


def _const_spec(shape):
    nd = len(shape)
    return pl.BlockSpec(shape, lambda *_: (0,) * nd, pipeline_mode=pl.Buffered(1))


def _params(*sem):
    return pltpu.CompilerParams(dimension_semantics=sem, vmem_limit_bytes=VMEM_LIMIT_BYTES)


def _rope(z, cos, sin_lo, sin_hi):
    up = pltpu.roll(z, LANES - ROT_DIM // 2, 1)
    dn = pltpu.roll(z, ROT_DIM // 2, 1)
    return z * cos + up * sin_lo + dn * sin_hi


def _store_residue_major(stage_ref, out_ref, z, dil):
    rows = z.shape[0] // dil
    for j in range(z.shape[1] // LANES):
        sl = slice(j * LANES, (j + 1) * LANES)
        stage_ref[...] = z[:, sl]
        for r in range(dil):
            out_ref[r, :, sl] = stage_ref[pl.ds(r, rows, stride=dil), :].astype(out_ref.dtype)


def _store_chunk_rows(stage_ref, out_ref, z):
    rows, width = z.shape[0] // SSM_CHUNK, z.shape[1]
    for j in range(width // LANES):
        stage_ref[...] = z[:, j * LANES:(j + 1) * LANES]
        for t in range(SSM_CHUNK):
            col = t * width + j * LANES
            out_ref[:, col:col + LANES] = stage_ref[pl.ds(t, rows, stride=SSM_CHUNK), :]


def _load_chunk_rows(stage_ref, in_ref, width):
    rows = in_ref.shape[0]
    slabs = []
    for j in range(width // LANES):
        for t in range(SSM_CHUNK):
            col = t * width + j * LANES
            stage_ref[pl.ds(t, rows, stride=SSM_CHUNK), :] = in_ref[:, col:col + LANES]
        slabs.append(stage_ref[...])
    return jnp.concatenate(slabs, axis=1)


def _in_proj_kernel(x_ref, g_ref, ca_ref, sa_ref, cb_ref, sb_ref, sign_ref, w_ref, ba_ref, bb_ref,
                    s_ref, q0_ref, k0_ref, v0_ref, q1_ref, k1_ref, v1_ref, q2_ref, k2_ref, v2_ref,
                    ga_ref, gb_ref, stage_ref):
    u = _rms(x_ref[...], g_ref[...]).astype(BF16)
    _store_chunk_rows(stage_ref, s_ref, _dot(u, w_ref[:, :OFF_Q]))
    ca, sa, cb, sb = ca_ref[...], sa_ref[...], cb_ref[...], sb_ref[...]
    cos = ca * cb - sa * sb
    sin = sa * cb + ca * sb
    slo, shi = sin * sign_ref[0:1, :], sin * sign_ref[1:2, :]
    scale = HEAD_DIM ** -0.5

    def rope(z):
        return jnp.concatenate([_rope(z[:, j * LANES:(j + 1) * LANES], cos, slo, shi)
                                for j in range(z.shape[1] // LANES)], axis=1)

    q = rope(_dot(u, w_ref[:, OFF_Q:OFF_K])) * scale
    k = rope(_dot(u, w_ref[:, OFF_K:OFF_V]))
    v = _dot(u, w_ref[:, OFF_V:OFF_GA])
    for z, refs in ((q, (q0_ref, q1_ref, q2_ref)), (k, (k0_ref, k1_ref, k2_ref)), (v, (v0_ref, v1_ref, v2_ref))):
        refs[0][...] = z[:, :GROUP_WIDTH].astype(BF16)
        for grp in (1, 2):
            _store_residue_major(stage_ref, refs[grp], z[:, grp * GROUP_WIDTH:(grp + 1) * GROUP_WIDTH],
                                 DILATIONS[grp])
    ga_ref[...] = jax.nn.sigmoid(_dot(u, w_ref[:, OFF_GA:OFF_GB]) + ba_ref[...]).astype(BF16)
    gb_ref[...] = jax.nn.sigmoid(_dot(u, w_ref[:, OFF_GB:]) + bb_ref[...]).astype(BF16)


def _rope_tables(seq, tm):
    half = ROT_DIM // 2
    freqs = ROPE_THETA ** (-jnp.arange(half, dtype=F32) * (2.0 / ROT_DIM))
    head = jnp.concatenate([freqs, freqs, jnp.zeros((HEAD_DIM - ROT_DIM,), F32)])
    lane_freq = jnp.tile(head, LANES // HEAD_DIM)[None, :]
    base = jnp.arange(0, seq, tm, dtype=F32)[:, None] * lane_freq
    offs = jnp.arange(tm, dtype=F32)[:, None] * lane_freq
    in_head = jnp.arange(LANES) % HEAD_DIM
    sign = jnp.zeros((SUBLANES, LANES), F32)
    sign = sign.at[0].set(jnp.where(in_head < half, -1.0, 0.0))
    sign = sign.at[1].set(jnp.where((in_head >= half) & (in_head < ROT_DIM), 1.0, 0.0))
    n_tiles = seq // tm
    return (jnp.cos(base).reshape(n_tiles, 1, LANES), jnp.sin(base).reshape(n_tiles, 1, LANES),
            jnp.cos(offs), jnp.sin(offs), sign)


def _in_proj(x2, seq, norm_g, w_in, gate_b):
    n = x2.shape[0]
    tm = TOKEN_TILE
    tiles_per_seq = seq // tm
    bsz = n // seq
    ba = gate_b[:D_MODEL].reshape(1, D_MODEL)
    bb = gate_b[D_MODEL:].reshape(1, D_MODEL)
    cos_a, sin_a, cos_b, sin_b, sign = _rope_tables(seq, tm)

    row = lambda w: pl.BlockSpec((tm, w), lambda i: (i, 0))
    tile_tab = pl.BlockSpec((None, 1, LANES), lambda i: (i % tiles_per_seq, 0, 0))
    res = lambda d: pl.BlockSpec((None, d, tm // d, GROUP_WIDTH),
                                 lambda i: (i // tiles_per_seq, 0, i % tiles_per_seq, 0))
    res_shape = lambda d: jax.ShapeDtypeStruct((bsz, d, seq // d, GROUP_WIDTH), BF16)
    nat_shape = jax.ShapeDtypeStruct((n, GROUP_WIDTH), BF16)
    d1, d2 = DILATIONS[1], DILATIONS[2]
    outs = pl.pallas_call(
        _in_proj_kernel,
        grid=(n // tm,),
        in_specs=[row(D_MODEL), _const_spec((1, D_MODEL)), tile_tab, tile_tab,
                  _const_spec((tm, LANES)), _const_spec((tm, LANES)), _const_spec((SUBLANES, LANES)),
                  _const_spec((D_MODEL, IN_WIDTH)), _const_spec((1, D_MODEL)), _const_spec((1, D_MODEL))],
        out_specs=[pl.BlockSpec((tm // SSM_CHUNK, SSM_ROW), lambda i: (i, 0))]
                  + [row(GROUP_WIDTH)] * 3 + [res(d1)] * 3 + [res(d2)] * 3 + [row(D_MODEL), row(D_MODEL)],
        out_shape=[jax.ShapeDtypeStruct((n // SSM_CHUNK, SSM_ROW), F32)] + [nat_shape] * 3
                  + [res_shape(d1)] * 3 + [res_shape(d2)] * 3
                  + [jax.ShapeDtypeStruct((n, D_MODEL), BF16), jax.ShapeDtypeStruct((n, D_MODEL), BF16)],
        scratch_shapes=[pltpu.VMEM((tm, LANES), F32)],
        compiler_params=_params("parallel"),
        name="in_proj",
    )(x2, norm_g.reshape(1, D_MODEL), cos_a, sin_a, cos_b, sin_b, sign, w_in.astype(BF16), ba, bb)
    u, q0, k0, v0, q1, k1, v1, q2, k2, v2, ga, gb = outs
    nat4 = lambda a: a.reshape(bsz, 1, seq, GROUP_WIDTH)
    qkv = ((nat4(q0), nat4(k0), nat4(v0)), (q1, k1, v1), (q2, k2, v2))
    return u, qkv, ga, gb


def _cmul(ar, ai, br, bi):
    return ar * br - ai * bi, ar * bi + ai * br


def _discretise(lr, li, dt):
    mag = jnp.exp(lr * dt)
    ar = mag * jnp.cos(li * dt)
    ai = mag * jnp.sin(li * dt)
    den = lr * lr + li * li
    cr = ((ar - 1.0) * lr + ai * li) / den
    ci = (ai * lr - (ar - 1.0) * li) / den
    return ar, ai, cr, ci


def _ssm_prep_kernel(lr_ref, li_ref, ldt_ref, brt_ref, bit_ref, cre_ref, cim_ref, lrf_ref, lif_ref, ldtf_ref,
                     w2_ref, e2_ref, apsr_ref, apsi_ref, aptr_ref, apti_ref, er_ref, ei_ref):
    g_n, t_n, h_n, sw = SSM_GROUPS, SSM_CHUNK, SSM_GROUP, 2 * SSM_STATE
    ar, ai, cr, ci = _discretise(lr_ref[...], li_ref[...], jnp.exp(ldt_ref[...]))
    brt, bit = brt_ref[...], bit_ref[...]
    bbr = cr * brt - ci * bit
    bbi = cr * bit + ci * brt
    cre, cim = cre_ref[...], cim_ref[...]
    by_group = lambda a: a.reshape(g_n, h_n, sw)
    pr, pi = jnp.ones_like(ar), jnp.zeros_like(ai)
    for j in range(t_n):
        rows = slice((t_n - 1 - j) * h_n, (t_n - j) * h_n)
        rr, ri = _cmul(pr, pi, bbr, bbi)
        er_ref[:, rows, :] = by_group(rr)
        ei_ref[:, rows, :] = by_group(ri)
        e2_ref[:, rows, :sw] = by_group(rr).astype(BF16)
        e2_ref[:, rows, sw:] = by_group(ri).astype(BF16)
        pr, pi = _cmul(pr, pi, ar, ai)
        rows = slice(j * h_n, (j + 1) * h_n)
        w2_ref[:, rows, SSM_CK:SSM_CK + sw] = by_group(cre * pr - cim * pi).astype(BF16)
        w2_ref[:, rows, SSM_CK + sw:] = by_group(-cre * pi - cim * pr).astype(BF16)

    nt = (((1,), (1,)), ((), ()))
    hi = lax.Precision.HIGHEST

    def toeplitz(g, _):
        rows = pl.ds(pl.multiple_of(g * h_n, h_n), h_n)
        krev = (lax.dot_general(cre_ref[rows, :], er_ref[g], nt, precision=hi, preferred_element_type=F32)
                - lax.dot_general(cim_ref[rows, :], ei_ref[g], nt, precision=hi, preferred_element_type=F32))
        kext = jnp.concatenate([krev, jnp.zeros_like(krev)], axis=1)
        for t in range(t_n):
            off = (t_n - 1 - t) * h_n
            win = kext if off == 0 else pltpu.roll(kext, 2 * SSM_CK - off, 1)
            w2_ref[g, t * h_n:(t + 1) * h_n, :SSM_CK] = win[:, :SSM_CK].astype(BF16)
        return 0
    lax.fori_loop(0, g_n, toeplitz, 0)

    acr, aci, _, _ = _discretise(lrf_ref[...], lif_ref[...], jnp.exp(ldtf_ref[...]))
    for _ in range(4):
        acr, aci = _cmul(acr, aci, acr, aci)
    shape = (SUBLANES, acr.shape[1])
    row = lax.broadcasted_iota(jnp.int32, shape, 0)
    qr, qi = jnp.ones(shape, F32), jnp.zeros(shape, F32)
    for k in range(SSM_LOG_STEPS + 1):
        for p2 in range(SSM_PAIRS):
            apsr_ref[p2, k:k + 1, :] = acr[:, p2 * sw:(p2 + 1) * sw]
            apsi_ref[p2, k:k + 1, :] = aci[:, p2 * sw:(p2 + 1) * sw]
        if (1 << k) < SUBLANES:
            nr, ni = _cmul(qr, qi, acr, aci)
            bit_set = (row & (1 << k)) != 0
            qr, qi = jnp.where(bit_set, nr, qr), jnp.where(bit_set, ni, qi)
        if (1 << k) == SUBLANES:
            a8r, a8i = acr, aci
        acr, aci = _cmul(acr, aci, acr, aci)
    br, bi = jnp.ones_like(a8r), jnp.zeros_like(a8i)
    for b in range(SSM_CHUNKS_PER_TILE // SUBLANES):
        rows = slice(b * SUBLANES, (b + 1) * SUBLANES)
        blk_r, blk_i = _cmul(qr, qi, br, bi)
        for p2 in range(SSM_PAIRS):
            aptr_ref[p2, rows, :] = blk_r[:, p2 * sw:(p2 + 1) * sw]
            apti_ref[p2, rows, :] = blk_i[:, p2 * sw:(p2 + 1) * sw]
        br, bi = _cmul(br, bi, a8r, a8i)


def _pad_pair_lanes(a):
    z = jnp.zeros_like(a)
    even = (jnp.arange(a.shape[0]) % 2 == 0)[:, None, None]
    padded = jnp.where(even, jnp.concatenate([a, z], -1), jnp.concatenate([z, a], -1))
    return padded.reshape(a.shape[0] * a.shape[1], 2 * a.shape[2])


def _ssm_operators(lam_re, lam_im, log_dt, b_re, b_im, c_re, c_im):
    g, p, h = SSM_GROUPS, SSM_STATE, SSM_GROUP
    gp, sw = g * p, 2 * p
    rep = lambda a: jnp.repeat(jnp.tile(a, (1, 2)), h, axis=0)
    ldt2 = jnp.broadcast_to(log_dt[:, None], (g, p))
    full = lambda shape: pl.BlockSpec(shape, lambda: (0,) * len(shape))
    in_arrays = (rep(lam_re), rep(lam_im), rep(ldt2),
                 _pad_pair_lanes(b_re.transpose(0, 2, 1)), _pad_pair_lanes(b_im.transpose(0, 2, 1)),
                 _pad_pair_lanes(c_re), _pad_pair_lanes(c_im),
                 lam_re.reshape(1, gp), lam_im.reshape(1, gp), ldt2.reshape(1, gp))
    out_shapes = ([((g, SSM_CK, SSM_CK + 2 * sw), BF16), ((g, SSM_CK, 2 * sw), BF16)]
                  + [((SSM_PAIRS, SUBLANES, sw), F32)] * 2 + [((SSM_PAIRS, SSM_CHUNKS_PER_TILE, sw), F32)] * 2)
    w2, e2, apsr, apsi, aptr, apti = pl.pallas_call(
        _ssm_prep_kernel,
        in_specs=[full(a.shape) for a in in_arrays],
        out_specs=[full(s) for s, _ in out_shapes],
        out_shape=[jax.ShapeDtypeStruct(s, dt) for s, dt in out_shapes],
        scratch_shapes=[pltpu.VMEM((g, SSM_CK, sw), F32), pltpu.VMEM((g, SSM_CK, sw), F32)],
        compiler_params=pltpu.CompilerParams(vmem_limit_bytes=VMEM_LIMIT_BYTES),
        name="ssm_prep",
    )(*in_arrays)
    return w2, e2.reshape(SSM_PAIRS, 2 * SSM_CK, 2 * sw), apsr, apsi, aptr, apti


def _shift_rows(z, s, row):
    if s % SUBLANES == 0:
        return jnp.concatenate([jnp.zeros((s, z.shape[1]), z.dtype), z[:-s]], axis=0)
    return jnp.where(row >= s, pltpu.roll(z, s, 0), 0.0)


def _ssm_scan_kernel(u_ref, d_ref, w2_ref, e2_ref, apsr_ref, apsi_ref, aptr_ref, apti_ref,
                     y_ref, xs_ref, sc_ref, yt_ref, carry_ref, loc_ref, sin_ref):
    t_n, h_n, c_n = SSM_CHUNK, SSM_GROUP, SSM_CHUNKS_PER_TILE
    n_slab = SSM_WIDTH // LANES
    pairs_per_slab = LANES // (2 * h_n)
    sw = 2 * SSM_STATE
    blk_n = SUBLANES
    n_blk = c_n // blk_n
    log_blk = blk_n.bit_length() - 1

    @pl.when(pl.program_id(1) == 0)
    def _():
        carry_ref[...] = jnp.zeros_like(carry_ref)

    for t in range(t_n):
        for j in range(n_slab):
            col = t * SSM_WIDTH + j * LANES
            blk = u_ref[:, col:col + LANES].astype(BF16).T
            xs_ref[j * pairs_per_slab:(j + 1) * pairs_per_slab, :, t * h_n:(t + 1) * h_n, :] = (
                blk.reshape(pairs_per_slab, 2, h_n, c_n))

    row = lax.broadcasted_iota(jnp.int32, (n_blk, sw), 0)
    nt = (((1,), (1,)), ((), ()))
    tn = (((0,), (0,)), ((), ()))

    def local_states(pr, _):
        xp = xs_ref[pr].reshape(2 * SSM_CK, c_n)
        loc = lax.dot_general(xp, e2_ref[pr], tn, preferred_element_type=F32)
        loc_ref[pr, 0] = loc[:, :sw]
        loc_ref[pr, 1] = loc[:, sw:]
        return 0
    lax.fori_loop(0, SSM_PAIRS, local_states, 0, unroll=4)

    def chunk_scan(pr, _):
        slot = pr % 2
        power = lambda k: (apsr_ref[pr, k:k + 1, :], apsi_ref[pr, k:k + 1, :])
        zr, zi = [], []
        for lo in range(blk_n):
            rows = pl.ds(lo, n_blk, stride=blk_n)
            xr, xi = loc_ref[pr, 0, rows, :], loc_ref[pr, 1, rows, :]
            if lo:
                dr, di = _cmul(zr[-1], zi[-1], *power(0))
                xr, xi = xr + dr, xi + di
            zr.append(xr)
            zi.append(xi)
        er, ei = zr[-1], zi[-1]
        s = 1
        while s < n_blk:
            dr, di = _cmul(_shift_rows(er, s, row), _shift_rows(ei, s, row),
                           *power(log_blk + s.bit_length() - 1))
            er, ei = er + dr, ei + di
            s *= 2
        cr, ci = carry_ref[pr, 0:1, :], carry_ref[pr, 1:2, :]
        blk_rows = pl.ds(0, n_blk, stride=blk_n)
        hr, hi = _cmul(aptr_ref[pr, blk_rows, :], apti_ref[pr, blk_rows, :], cr, ci)
        br, bi = _shift_rows(er, 1, row) + hr, _shift_rows(ei, 1, row) + hi
        for lo in range(blk_n):
            sr, si = _cmul(br, bi, aptr_ref[pr, lo:lo + 1, :], apti_ref[pr, lo:lo + 1, :])
            if lo:
                sr, si = sr + zr[lo - 1], si + zi[lo - 1]
            rows = pl.ds(lo, n_blk, stride=blk_n)
            sin_ref[slot, 0, rows, :] = sr
            sin_ref[slot, 1, rows, :] = si
        sc_ref[pr, :, :sw] = sin_ref[slot, 0].astype(BF16)
        sc_ref[pr, :, sw:] = sin_ref[slot, 1].astype(BF16)
        nr, ni = _cmul(cr, ci, *power(SSM_LOG_STEPS))
        carry_ref[pr, 0:1, :] = er[n_blk - 1:n_blk, :] + nr
        carry_ref[pr, 1:2, :] = ei[n_blk - 1:n_blk, :] + ni
        return 0
    lax.fori_loop(0, SSM_PAIRS, chunk_scan, 0, unroll=2)

    def outputs(g, _):
        pr = g // 2
        yg = (_dot(w2_ref[g, :, :SSM_CK], xs_ref[pr, g % 2])
              + lax.dot_general(w2_ref[g, :, SSM_CK:], sc_ref[pr], nt, preferred_element_type=F32))
        yt_ref[:, pl.ds(pl.multiple_of(g * h_n, h_n), h_n), :] = yg.reshape(t_n, h_n, c_n)
        return 0
    lax.fori_loop(0, SSM_GROUPS, outputs, 0, unroll=4)

    for t in range(t_n):
        for j in range(n_slab):
            sl = slice(j * LANES, (j + 1) * LANES)
            col = t * SSM_WIDTH + j * LANES
            y_ref[:, col:col + LANES] = yt_ref[t, sl, :].T + d_ref[:, sl] * u_ref[:, col:col + LANES]


def _ssm_scan(u_rows, bsz, d_skip, ops):
    w2, e2, apsr, apsi, aptr, apti = ops
    g, p, c_n = SSM_GROUPS, SSM_STATE, SSM_CHUNKS_PER_TILE
    tiles = u_rows.shape[0] // (bsz * c_n)
    tile = pl.BlockSpec((c_n, SSM_ROW), lambda b, i: (b * tiles + i, 0))
    return pl.pallas_call(
        _ssm_scan_kernel,
        grid=(bsz, tiles),
        in_specs=[tile, _const_spec((1, SSM_WIDTH)),
                  _const_spec(w2.shape), _const_spec(e2.shape),
                  _const_spec(apsr.shape), _const_spec(apsi.shape),
                  _const_spec(aptr.shape), _const_spec(apti.shape)],
        out_specs=tile,
        out_shape=jax.ShapeDtypeStruct(u_rows.shape, F32),
        scratch_shapes=[pltpu.VMEM((SSM_PAIRS, 2, SSM_CK, c_n), BF16),
                        pltpu.VMEM((SSM_PAIRS, c_n, 4 * p), BF16),
                        pltpu.VMEM((SSM_CHUNK, SSM_WIDTH, c_n), F32),
                        pltpu.VMEM((SSM_PAIRS, SUBLANES, 2 * p), F32),
                        pltpu.VMEM((SSM_PAIRS, 2, c_n, 2 * p), F32),
                        pltpu.VMEM((2, 2, c_n, 2 * p), F32)],
        compiler_params=_params("parallel", "arbitrary"),
        name="ssm_scan",
    )(u_rows, d_skip.reshape(1, SSM_WIDTH), w2, e2, apsr, apsi, aptr, apti)


def _attn_kernel(q_ref, k_ref, v_ref, o_ref, lse_ref):
    step = pl.program_id(2)
    qb, nk = ATTN_QB, ATTN_QB + WINDOW_KEYS
    row = lax.broadcasted_iota(jnp.int32, (qb, nk), 0)
    col = lax.broadcasted_iota(jnp.int32, (qb, nk), 1)
    lane = lax.broadcasted_iota(jnp.int32, (qb, LANES), 1)
    first_head = lane < HEAD_DIM
    nt = (((1,), (1,)), ((), ()))
    for sb in range(o_ref.shape[0] // qb):
        rows = slice(sb * qb, (sb + 1) * qb)
        q_start = step * o_ref.shape[0] + sb * qb
        back = jnp.minimum(q_start, WINDOW_KEYS)
        k_start = pl.multiple_of(q_start - back, ATTN_QB)
        dist = row + back - col
        valid = (dist >= 0) & (dist <= WINDOW_KEYS)
        for pair in range(GROUP_WIDTH // LANES):
            cols = slice(pair * LANES, (pair + 1) * LANES)
            qp = q_ref[rows, cols]
            kp = k_ref[pl.ds(k_start, nk), cols]
            vp = v_ref[pl.ds(k_start, nk), cols]
            outs, lses = [], []
            for sel in (first_head, ~first_head):
                qm = jnp.where(sel, qp, jnp.zeros_like(qp))
                s = lax.dot_general(qm, kp, nt, preferred_element_type=F32)
                s = jnp.where(valid, s, NEG_BIG)
                m = jnp.max(s, axis=-1, keepdims=True)
                e = jnp.exp(s - m)
                den = jnp.sum(e, axis=-1, keepdims=True)
                outs.append(_dot(e.astype(BF16), vp) / den)
                lses.append(m + jnp.log(den))
            o_ref[rows, cols] = jnp.where(first_head, outs[0], outs[1]).astype(BF16)
            lse_ref[rows, cols] = jnp.where(first_head, lses[0], lses[1])


def _attn_group(q4, k4, v4):
    bsz, dil, lr, _ = q4.shape
    rows = min(ATTN_STEP_ROWS, lr)
    q_spec = pl.BlockSpec((None, None, rows, GROUP_WIDTH), lambda b, r, i: (b, r, i, 0))
    kv_spec = pl.BlockSpec((None, None, lr, GROUP_WIDTH), lambda b, r, i: (b, r, 0, 0))
    return pl.pallas_call(
        _attn_kernel,
        grid=(bsz, dil, lr // rows),
        in_specs=[q_spec, kv_spec, kv_spec],
        out_specs=[q_spec, q_spec],
        out_shape=[jax.ShapeDtypeStruct(q4.shape, BF16), jax.ShapeDtypeStruct(q4.shape, F32)],
        compiler_params=_params("parallel", "parallel", "arbitrary"),
        name=f"attn_d{dil}",
    )(q4, k4, v4)


def _load_token_major(stage_ref, in_ref):
    dil, rows, width = in_ref.shape
    slabs = []
    for j in range(width // LANES):
        for r in range(dil):
            stage_ref[pl.ds(r, rows, stride=dil), :] = in_ref[r, :, j * LANES:(j + 1) * LANES].astype(F32)
        slabs.append(stage_ref[...])
    return jnp.concatenate(slabs, axis=1)


def _merge_kernel(x_ref, ys_ref, gluw_ref, glub_ref, wa_ref,
                  o0_ref, o1_ref, o2_ref, l0_ref, l1_ref, l2_ref, wb_ref,
                  ga_ref, gb_ref, wout_ref, h_ref, stage_ref):
    y = jax.nn.gelu(_load_chunk_rows(stage_ref, ys_ref, SSM_WIDTH))
    y = y * jax.nn.sigmoid(_dot(y.astype(BF16), gluw_ref[...]) + glub_ref[...])
    ya = _dot(y.astype(BF16), wa_ref[...])

    o0, l0 = o0_ref[0].astype(F32), l0_ref[0]
    o1, l1 = _load_token_major(stage_ref, o1_ref), _load_token_major(stage_ref, l1_ref)
    o2, l2 = _load_token_major(stage_ref, o2_ref), _load_token_major(stage_ref, l2_ref)
    top = jnp.maximum(jnp.maximum(l0, l1), l2)
    w0, w1, w2 = jnp.exp(l0 - top), jnp.exp(l1 - top), jnp.exp(l2 - top)
    attn = (w0 * o0 + w1 * o1 + w2 * o2) / (w0 + w1 + w2)
    yb = _dot(attn.astype(BF16), wb_ref[...])

    mix = ga_ref[...].astype(F32) * ya + gb_ref[...].astype(F32) * yb
    h_ref[...] = x_ref[...] + _dot(mix.astype(BF16), wout_ref[...])


def _merge(x2, seq, ys_rows, glu_w, glu_b, w_a, attn_outs, w_b, ga, gb, w_out):
    n = x2.shape[0]
    tm = TOKEN_TILE
    tiles_per_seq = seq // tm
    row = lambda w: pl.BlockSpec((tm, w), lambda i: (i, 0))
    res = lambda d: pl.BlockSpec((None, d, tm // d, GROUP_WIDTH),
                                 lambda i: (i // tiles_per_seq, 0, i % tiles_per_seq, 0))
    (o0, l0), (o1, l1), (o2, l2) = attn_outs
    d0, d1, d2 = DILATIONS
    return pl.pallas_call(
        _merge_kernel,
        grid=(n // tm,),
        in_specs=[row(D_MODEL), pl.BlockSpec((tm // SSM_CHUNK, SSM_ROW), lambda i: (i, 0)),
                  _const_spec((SSM_WIDTH, SSM_WIDTH)), _const_spec((1, SSM_WIDTH)),
                  _const_spec((SSM_WIDTH, D_MODEL)),
                  res(d0), res(d1), res(d2), res(d0), res(d1), res(d2),
                  _const_spec((GROUP_WIDTH, D_MODEL)), row(D_MODEL), row(D_MODEL),
                  _const_spec((D_MODEL, D_MODEL))],
        out_specs=row(D_MODEL),
        out_shape=jax.ShapeDtypeStruct((n, D_MODEL), F32),
        scratch_shapes=[pltpu.VMEM((tm, LANES), F32)],
        compiler_params=_params("parallel"),
        name="merge",
    )(x2, ys_rows, glu_w.astype(BF16), glu_b.reshape(1, SSM_WIDTH),
      w_a.astype(BF16), o0, o1, o2, l0, l1, l2, w_b.astype(BF16), ga, gb, w_out.astype(BF16))


def _ffn_kernel(h_ref, g2_ref, wg_ref, wu_ref, cw_ref, cb_ref, wd_ref, g3_ref, wpg_ref,
                p_ref, wpp_ref, gf_ref, out_ref, act_ref, carry_ref, *, tiles_per_seq):
    tm = h_ref.shape[0]

    @pl.when(pl.program_id(0) % tiles_per_seq == 0)
    def _():
        carry_ref[...] = jnp.zeros_like(carry_ref)

    h = h_ref[...]
    u2 = _rms(h, g2_ref[...]).astype(BF16)
    row = lax.broadcasted_iota(jnp.int32, (SUBLANES, FFN_CHUNK), 0)
    for c in range(D_FF // FFN_CHUNK):
        sl = slice(c * FFN_CHUNK, (c + 1) * FFN_CHUNK)
        gp = _dot(u2, wg_ref[:, sl])
        up = _dot(u2, wu_ref[:, sl])
        prev = carry_ref[:, sl]
        r1 = pltpu.roll(gp, 1, 0)
        r2 = pltpu.roll(gp, 2, 0)
        r1 = jnp.concatenate([jnp.where(row < 1, pltpu.roll(prev, 1, 0), r1[:SUBLANES]), r1[SUBLANES:]], axis=0)
        r2 = jnp.concatenate([jnp.where(row < 2, pltpu.roll(prev, 2, 0), r2[:SUBLANES]), r2[SUBLANES:]], axis=0)
        carry_ref[:, sl] = gp[tm - SUBLANES:, :]
        gate = cw_ref[0:1, sl] * r2 + cw_ref[1:2, sl] * r1 + cw_ref[2:3, sl] * gp + cb_ref[:, sl]
        act_ref[:, sl] = (jax.nn.gelu(gate) * up).astype(BF16)
    h = h + _dot(act_ref[...], wd_ref[...])
    u3 = _rms(h, g3_ref[...]).astype(BF16)
    h = h + jax.nn.sigmoid(_dot(u3, wpg_ref[...])) * _dot(p_ref[...].astype(BF16), wpp_ref[...])
    out_ref[...] = _rms(h, gf_ref[...])


def _ffn(h1, seq, p2, norm_g, w_gate, w_up, conv_w, conv_b, w_down, ple_g, ple_w_gate, ple_w_proj, final_g):
    n = h1.shape[0]
    tm = TOKEN_TILE
    row = lambda w: pl.BlockSpec((tm, w), lambda i: (i, 0))
    vec = lambda a: a.reshape(1, -1)
    return pl.pallas_call(
        functools.partial(_ffn_kernel, tiles_per_seq=seq // tm),
        grid=(n // tm,),
        in_specs=[row(D_MODEL), _const_spec((1, D_MODEL)), _const_spec((D_MODEL, D_FF)),
                  _const_spec((D_MODEL, D_FF)), _const_spec((CONV_WIDTH, D_FF)), _const_spec((1, D_FF)),
                  _const_spec((D_FF, D_MODEL)), _const_spec((1, D_MODEL)), _const_spec((D_MODEL, D_MODEL)),
                  row(PLE_DIM), _const_spec((PLE_DIM, D_MODEL)), _const_spec((1, D_MODEL))],
        out_specs=row(D_MODEL),
        out_shape=jax.ShapeDtypeStruct((n, D_MODEL), F32),
        scratch_shapes=[pltpu.VMEM((tm, D_FF), BF16), pltpu.VMEM((SUBLANES, D_FF), F32)],
        compiler_params=_params("arbitrary"),
        name="ffn",
    )(h1, vec(norm_g), w_gate.astype(BF16), w_up.astype(BF16), conv_w, vec(conv_b),
      w_down.astype(BF16), vec(ple_g), ple_w_gate.astype(BF16), p2, ple_w_proj.astype(BF16), vec(final_g))


def _layer(h2, bsz, seq, p2, mix_norm_g, w_in, gate_b, ssm_lam_re, ssm_lam_im, ssm_log_dt, ssm_b_re,
           ssm_b_im, ssm_c_re, ssm_c_im, ssm_d, ssm_glu_w, ssm_glu_b, w_branch_a, w_branch_b, w_out,
           ffn_norm_g, ffn_w_gate, ffn_w_up, ffn_conv_w, ffn_conv_b, ffn_w_down,
           ple_norm_g, ple_w_gate, ple_w_proj, out_norm_g):
    u_rows, qkv, ga, gb = _in_proj(h2, seq, mix_norm_g, w_in, gate_b)
    ops = _ssm_operators(ssm_lam_re, ssm_lam_im, ssm_log_dt, ssm_b_re, ssm_b_im, ssm_c_re, ssm_c_im)
    ys_rows = _ssm_scan(u_rows, bsz, ssm_d.reshape(-1), ops)
    attn_outs = [_attn_group(*group) for group in qkv]
    h1 = _merge(h2, seq, ys_rows, ssm_glu_w, ssm_glu_b, w_branch_a, attn_outs, w_branch_b, ga, gb, w_out)
    return _ffn(h1, seq, p2, ffn_norm_g, ffn_w_gate, ffn_w_up, ffn_conv_w, ffn_conv_b, ffn_w_down,
                ple_norm_g, ple_w_gate, ple_w_proj, out_norm_g)


def kernel(x, p, mix_norm_g, w_in, gate_b, ssm_lam_re, ssm_lam_im, ssm_log_dt, ssm_b_re, ssm_b_im, ssm_c_re, ssm_c_im, ssm_d, ssm_glu_w, ssm_glu_b, w_branch_a, w_branch_b, w_out, ffn_norm_g, ffn_w_gate, ffn_w_up, ffn_conv_w, ffn_conv_b, ffn_w_down, ple_norm_g, ple_w_gate, ple_w_proj, final_norm_g):
    bsz, seq, _ = x.shape
    depth = p.shape[0]
    assert depth == 1, "the final norm is fused into the layer's last kernel"
    h2 = x.reshape(bsz * seq, D_MODEL)
    out = _layer(h2, bsz, seq, p[0].reshape(bsz * seq, PLE_DIM), mix_norm_g[0], w_in[0], gate_b[0],
                 ssm_lam_re[0], ssm_lam_im[0], ssm_log_dt[0], ssm_b_re[0], ssm_b_im[0], ssm_c_re[0],
                 ssm_c_im[0], ssm_d[0], ssm_glu_w[0], ssm_glu_b[0], w_branch_a[0], w_branch_b[0],
                 w_out[0], ffn_norm_g[0], ffn_w_gate[0], ffn_w_up[0], ffn_conv_w[0], ffn_conv_b[0],
                 ffn_w_down[0], ple_norm_g[0], ple_w_gate[0], ple_w_proj[0], final_norm_g)
    return out.reshape(bsz, seq, D_MODEL)
```

```python
import functools

import jax
import jax.numpy as jnp
from jax import lax
from jax.experimental import pallas as pl
from jax.experimental.pallas import tpu as pltpu

F32 = jnp.float32
BF16 = jnp.bfloat16

D_MODEL = 1024
EPS = 1e-6
PLE_DIM = 256
SSM_GROUP = 16
SSM_STATE = 64
SSM_WIDTH = 512
SSM_GROUPS = SSM_WIDTH // SSM_GROUP
HEAD_DIM = 64
DILATIONS = (1, 4, 16)
WINDOW_KEYS = 128
HEADS_PER_GROUP = 4
GROUP_WIDTH = HEADS_PER_GROUP * HEAD_DIM
ATTN_WIDTH = len(DILATIONS) * GROUP_WIDTH
ROT_DIM = HEAD_DIM // 4
ROPE_THETA = 500000.0
NEG_BIG = -1e30
D_FF = 2816
CONV_WIDTH = 3
OFF_Q = SSM_WIDTH
OFF_K = OFF_Q + ATTN_WIDTH
OFF_V = OFF_K + ATTN_WIDTH
OFF_GA = OFF_V + ATTN_WIDTH
OFF_GB = OFF_GA + D_MODEL
IN_WIDTH = OFF_GB + D_MODEL

LANES = 128
SUBLANES = 8
VMEM_LIMIT_BYTES = 56 * 1024 * 1024

TOKEN_TILE = 1024
SUB_TILE = 256
SSM_CHUNK = 16
SSM_CHUNKS_PER_TILE = LANES
SSM_TILE = SSM_CHUNK * SSM_CHUNKS_PER_TILE
SSM_CK = SSM_CHUNK * SSM_GROUP
SSM_ROW = SSM_CHUNK * SSM_WIDTH
SSM_PAIRS = SSM_GROUPS // 2
SSM_LOG_STEPS = 7
ATTN_QB = 128
ATTN_STEP_ROWS = 1024
FFN_CHUNK = 256


def _dot(a, b):
    return jnp.dot(a, b, preferred_element_type=F32)


def _rms(x, g):
    var = jnp.mean(x * x, axis=-1, keepdims=True)
    return x * lax.rsqrt(var + EPS) * g


def _const_spec(shape):
    nd = len(shape)
    return pl.BlockSpec(shape, lambda *_: (0,) * nd, pipeline_mode=pl.Buffered(1))


def _params(*sem):
    return pltpu.CompilerParams(dimension_semantics=sem, vmem_limit_bytes=VMEM_LIMIT_BYTES)


def _rope(z, cos, sin_lo, sin_hi):
    up = pltpu.roll(z, LANES - ROT_DIM // 2, 1)
    dn = pltpu.roll(z, ROT_DIM // 2, 1)
    return z * cos + up * sin_lo + dn * sin_hi


def _store_residue_major(stage_ref, out_ref, z, dil):
    rows = z.shape[0] // dil
    for j in range(z.shape[1] // LANES):
        sl = slice(j * LANES, (j + 1) * LANES)
        stage_ref[...] = z[:, sl]
        for r in range(dil):
            out_ref[r, :, sl] = stage_ref[pl.ds(r, rows, stride=dil), :].astype(out_ref.dtype)


def _store_chunk_rows(stage_ref, out_ref, z):
    rows, width = z.shape[0] // SSM_CHUNK, z.shape[1]
    for j in range(width // LANES):
        stage_ref[...] = z[:, j * LANES:(j + 1) * LANES]
        for t in range(SSM_CHUNK):
            col = t * width + j * LANES
            out_ref[:, col:col + LANES] = stage_ref[pl.ds(t, rows, stride=SSM_CHUNK), :]


def _load_chunk_rows(stage_ref, in_ref, width):
    rows = in_ref.shape[0]
    slabs = []
    for j in range(width // LANES):
        for t in range(SSM_CHUNK):
            col = t * width + j * LANES
            stage_ref[pl.ds(t, rows, stride=SSM_CHUNK), :] = in_ref[:, col:col + LANES]
        slabs.append(stage_ref[...])
    return jnp.concatenate(slabs, axis=1)


def _in_proj_kernel(x_ref, g_ref, ca_ref, sa_ref, cb_ref, sb_ref, sign_ref, w_ref, ba_ref, bb_ref,
                    s_ref, q0_ref, k0_ref, v0_ref, q1_ref, k1_ref, v1_ref, q2_ref, k2_ref, v2_ref,
                    ga_ref, gb_ref, stage_ref):
    u = _rms(x_ref[...], g_ref[...]).astype(BF16)
    _store_chunk_rows(stage_ref, s_ref, _dot(u, w_ref[:, :OFF_Q]))
    ca, sa, cb, sb = ca_ref[...], sa_ref[...], cb_ref[...], sb_ref[...]
    cos = ca * cb - sa * sb
    sin = sa * cb + ca * sb
    slo, shi = sin * sign_ref[0:1, :], sin * sign_ref[1:2, :]
    scale = HEAD_DIM ** -0.5

    def rope(z):
        return jnp.concatenate([_rope(z[:, j * LANES:(j + 1) * LANES], cos, slo, shi)
                                for j in range(z.shape[1] // LANES)], axis=1)

    q = rope(_dot(u, w_ref[:, OFF_Q:OFF_K])) * scale
    k = rope(_dot(u, w_ref[:, OFF_K:OFF_V]))
    v = _dot(u, w_ref[:, OFF_V:OFF_GA])
    for z, refs in ((q, (q0_ref, q1_ref, q2_ref)), (k, (k0_ref, k1_ref, k2_ref)), (v, (v0_ref, v1_ref, v2_ref))):
        refs[0][...] = z[:, :GROUP_WIDTH].astype(BF16)
        for grp in (1, 2):
            _store_residue_major(stage_ref, refs[grp], z[:, grp * GROUP_WIDTH:(grp + 1) * GROUP_WIDTH],
                                 DILATIONS[grp])
    ga_ref[...] = jax.nn.sigmoid(_dot(u, w_ref[:, OFF_GA:OFF_GB]) + ba_ref[...]).astype(BF16)
    gb_ref[...] = jax.nn.sigmoid(_dot(u, w_ref[:, OFF_GB:]) + bb_ref[...]).astype(BF16)


def _rope_tables(seq, tm):
    half = ROT_DIM // 2
    freqs = ROPE_THETA ** (-jnp.arange(half, dtype=F32) * (2.0 / ROT_DIM))
    head = jnp.concatenate([freqs, freqs, jnp.zeros((HEAD_DIM - ROT_DIM,), F32)])
    lane_freq = jnp.tile(head, LANES // HEAD_DIM)[None, :]
    base = jnp.arange(0, seq, tm, dtype=F32)[:, None] * lane_freq
    offs = jnp.arange(tm, dtype=F32)[:, None] * lane_freq
    in_head = jnp.arange(LANES) % HEAD_DIM
    sign = jnp.zeros((SUBLANES, LANES), F32)
    sign = sign.at[0].set(jnp.where(in_head < half, -1.0, 0.0))
    sign = sign.at[1].set(jnp.where((in_head >= half) & (in_head < ROT_DIM), 1.0, 0.0))
    n_tiles = seq // tm
    return (jnp.cos(base).reshape(n_tiles, 1, LANES), jnp.sin(base).reshape(n_tiles, 1, LANES),
            jnp.cos(offs), jnp.sin(offs), sign)


def _in_proj(x2, seq, norm_g, w_in, gate_b):
    n = x2.shape[0]
    tm = TOKEN_TILE
    tiles_per_seq = seq // tm
    bsz = n // seq
    ba = gate_b[:D_MODEL].reshape(1, D_MODEL)
    bb = gate_b[D_MODEL:].reshape(1, D_MODEL)
    cos_a, sin_a, cos_b, sin_b, sign = _rope_tables(seq, tm)

    row = lambda w: pl.BlockSpec((tm, w), lambda i: (i, 0))
    tile_tab = pl.BlockSpec((None, 1, LANES), lambda i: (i % tiles_per_seq, 0, 0))
    res = lambda d: pl.BlockSpec((None, d, tm // d, GROUP_WIDTH),
                                 lambda i: (i // tiles_per_seq, 0, i % tiles_per_seq, 0))
    res_shape = lambda d: jax.ShapeDtypeStruct((bsz, d, seq // d, GROUP_WIDTH), BF16)
    nat_shape = jax.ShapeDtypeStruct((n, GROUP_WIDTH), BF16)
    d1, d2 = DILATIONS[1], DILATIONS[2]
    outs = pl.pallas_call(
        _in_proj_kernel,
        grid=(n // tm,),
        in_specs=[row(D_MODEL), _const_spec((1, D_MODEL)), tile_tab, tile_tab,
                  _const_spec((tm, LANES)), _const_spec((tm, LANES)), _const_spec((SUBLANES, LANES)),
                  _const_spec((D_MODEL, IN_WIDTH)), _const_spec((1, D_MODEL)), _const_spec((1, D_MODEL))],
        out_specs=[pl.BlockSpec((tm // SSM_CHUNK, SSM_ROW), lambda i: (i, 0))]
                  + [row(GROUP_WIDTH)] * 3 + [res(d1)] * 3 + [res(d2)] * 3 + [row(D_MODEL), row(D_MODEL)],
        out_shape=[jax.ShapeDtypeStruct((n // SSM_CHUNK, SSM_ROW), F32)] + [nat_shape] * 3
                  + [res_shape(d1)] * 3 + [res_shape(d2)] * 3
                  + [jax.ShapeDtypeStruct((n, D_MODEL), BF16), jax.ShapeDtypeStruct((n, D_MODEL), BF16)],
        scratch_shapes=[pltpu.VMEM((tm, LANES), F32)],
        compiler_params=_params("parallel"),
        name="in_proj",
    )(x2, norm_g.reshape(1, D_MODEL), cos_a, sin_a, cos_b, sin_b, sign, w_in.astype(BF16), ba, bb)
    u, q0, k0, v0, q1, k1, v1, q2, k2, v2, ga, gb = outs
    nat4 = lambda a: a.reshape(bsz, 1, seq, GROUP_WIDTH)
    qkv = ((nat4(q0), nat4(k0), nat4(v0)), (q1, k1, v1), (q2, k2, v2))
    return u, qkv, ga, gb


def _cmul(ar, ai, br, bi):
    return ar * br - ai * bi, ar * bi + ai * br


def _discretise(lr, li, dt):
    mag = jnp.exp(lr * dt)
    ar = mag * jnp.cos(li * dt)
    ai = mag * jnp.sin(li * dt)
    den = lr * lr + li * li
    cr = ((ar - 1.0) * lr + ai * li) / den
    ci = (ai * lr - (ar - 1.0) * li) / den
    return ar, ai, cr, ci


def _ssm_prep_kernel(lr_ref, li_ref, ldt_ref, brt_ref, bit_ref, cre_ref, cim_ref, lrf_ref, lif_ref, ldtf_ref,
                     w2_ref, e2_ref, apsr_ref, apsi_ref, aptr_ref, apti_ref, er_ref, ei_ref):
    g_n, t_n, h_n, sw = SSM_GROUPS, SSM_CHUNK, SSM_GROUP, 2 * SSM_STATE
    ar, ai, cr, ci = _discretise(lr_ref[...], li_ref[...], jnp.exp(ldt_ref[...]))
    brt, bit = brt_ref[...], bit_ref[...]
    bbr = cr * brt - ci * bit
    bbi = cr * bit + ci * brt
    cre, cim = cre_ref[...], cim_ref[...]
    by_group = lambda a: a.reshape(g_n, h_n, sw)
    pr, pi = jnp.ones_like(ar), jnp.zeros_like(ai)
    for j in range(t_n):
        rows = slice((t_n - 1 - j) * h_n, (t_n - j) * h_n)
        rr, ri = _cmul(pr, pi, bbr, bbi)
        er_ref[:, rows, :] = by_group(rr)
        ei_ref[:, rows, :] = by_group(ri)
        e2_ref[:, rows, :sw] = by_group(rr).astype(BF16)
        e2_ref[:, rows, sw:] = by_group(ri).astype(BF16)
        pr, pi = _cmul(pr, pi, ar, ai)
        rows = slice(j * h_n, (j + 1) * h_n)
        w2_ref[:, rows, SSM_CK:SSM_CK + sw] = by_group(cre * pr - cim * pi).astype(BF16)
        w2_ref[:, rows, SSM_CK + sw:] = by_group(-cre * pi - cim * pr).astype(BF16)

    nt = (((1,), (1,)), ((), ()))
    hi = lax.Precision.HIGHEST

    def toeplitz(g, _):
        rows = pl.ds(pl.multiple_of(g * h_n, h_n), h_n)
        krev = (lax.dot_general(cre_ref[rows, :], er_ref[g], nt, precision=hi, preferred_element_type=F32)
                - lax.dot_general(cim_ref[rows, :], ei_ref[g], nt, precision=hi, preferred_element_type=F32))
        kext = jnp.concatenate([krev, jnp.zeros_like(krev)], axis=1)
        for t in range(t_n):
            off = (t_n - 1 - t) * h_n
            win = kext if off == 0 else pltpu.roll(kext, 2 * SSM_CK - off, 1)
            w2_ref[g, t * h_n:(t + 1) * h_n, :SSM_CK] = win[:, :SSM_CK].astype(BF16)
        return 0
    lax.fori_loop(0, g_n, toeplitz, 0, unroll=4)

    acr, aci, _, _ = _discretise(lrf_ref[...], lif_ref[...], jnp.exp(ldtf_ref[...]))
    for _ in range(4):
        acr, aci = _cmul(acr, aci, acr, aci)
    shape = (SUBLANES, acr.shape[1])
    row = lax.broadcasted_iota(jnp.int32, shape, 0)
    qr, qi = jnp.ones(shape, F32), jnp.zeros(shape, F32)
    for k in range(SSM_LOG_STEPS + 1):
        for p2 in range(SSM_PAIRS):
            apsr_ref[p2, k:k + 1, :] = acr[:, p2 * sw:(p2 + 1) * sw]
            apsi_ref[p2, k:k + 1, :] = aci[:, p2 * sw:(p2 + 1) * sw]
        if (1 << k) < SUBLANES:
            nr, ni = _cmul(qr, qi, acr, aci)
            bit_set = (row & (1 << k)) != 0
            qr, qi = jnp.where(bit_set, nr, qr), jnp.where(bit_set, ni, qi)
        if (1 << k) == SUBLANES:
            a8r, a8i = acr, aci
        acr, aci = _cmul(acr, aci, acr, aci)
    br, bi = jnp.ones_like(a8r), jnp.zeros_like(a8i)
    for b in range(SSM_CHUNKS_PER_TILE // SUBLANES):
        rows = slice(b * SUBLANES, (b + 1) * SUBLANES)
        blk_r, blk_i = _cmul(qr, qi, br, bi)
        for p2 in range(SSM_PAIRS):
            aptr_ref[p2, rows, :] = blk_r[:, p2 * sw:(p2 + 1) * sw]
            apti_ref[p2, rows, :] = blk_i[:, p2 * sw:(p2 + 1) * sw]
        br, bi = _cmul(br, bi, a8r, a8i)


def _pad_pair_lanes(a):
    z = jnp.zeros_like(a)
    even = (jnp.arange(a.shape[0]) % 2 == 0)[:, None, None]
    padded = jnp.where(even, jnp.concatenate([a, z], -1), jnp.concatenate([z, a], -1))
    return padded.reshape(a.shape[0] * a.shape[1], 2 * a.shape[2])


def _ssm_operators(lam_re, lam_im, log_dt, b_re, b_im, c_re, c_im):
    g, p, h = SSM_GROUPS, SSM_STATE, SSM_GROUP
    gp, sw = g * p, 2 * p
    rep = lambda a: jnp.repeat(jnp.tile(a, (1, 2)), h, axis=0)
    ldt2 = jnp.broadcast_to(log_dt[:, None], (g, p))
    full = lambda shape: pl.BlockSpec(shape, lambda: (0,) * len(shape))
    in_arrays = (rep(lam_re), rep(lam_im), rep(ldt2),
                 _pad_pair_lanes(b_re.transpose(0, 2, 1)), _pad_pair_lanes(b_im.transpose(0, 2, 1)),
                 _pad_pair_lanes(c_re), _pad_pair_lanes(c_im),
                 lam_re.reshape(1, gp), lam_im.reshape(1, gp), ldt2.reshape(1, gp))
    out_shapes = ([((g, SSM_CK, SSM_CK + 2 * sw), BF16), ((g, SSM_CK, 2 * sw), BF16)]
                  + [((SSM_PAIRS, SUBLANES, sw), F32)] * 2 + [((SSM_PAIRS, SSM_CHUNKS_PER_TILE, sw), F32)] * 2)
    w2, e2, apsr, apsi, aptr, apti = pl.pallas_call(
        _ssm_prep_kernel,
        in_specs=[full(a.shape) for a in in_arrays],
        out_specs=[full(s) for s, _ in out_shapes],
        out_shape=[jax.ShapeDtypeStruct(s, dt) for s, dt in out_shapes],
        scratch_shapes=[pltpu.VMEM((g, SSM_CK, sw), F32), pltpu.VMEM((g, SSM_CK, sw), F32)],
        compiler_params=pltpu.CompilerParams(vmem_limit_bytes=VMEM_LIMIT_BYTES),
        name="ssm_prep",
    )(*in_arrays)
    return w2, e2.reshape(SSM_PAIRS, 2 * SSM_CK, 2 * sw), apsr, apsi, aptr, apti


def _shift_rows(z, s, row):
    if s % SUBLANES == 0:
        return jnp.concatenate([jnp.zeros((s, z.shape[1]), z.dtype), z[:-s]], axis=0)
    return jnp.where(row >= s, pltpu.roll(z, s, 0), 0.0)


def _ssm_scan_kernel(u_ref, d_ref, w2_ref, e2_ref, apsr_ref, apsi_ref, aptr_ref, apti_ref,
                     y_ref, xs_ref, sc_ref, yt_ref, carry_ref, loc_ref, sin_ref):
    t_n, h_n, c_n = SSM_CHUNK, SSM_GROUP, SSM_CHUNKS_PER_TILE
    n_slab = SSM_WIDTH // LANES
    pairs_per_slab = LANES // (2 * h_n)
    sw = 2 * SSM_STATE
    blk_n = SUBLANES
    n_blk = c_n // blk_n
    log_blk = blk_n.bit_length() - 1

    @pl.when(pl.program_id(1) == 0)
    def _():
        carry_ref[...] = jnp.zeros_like(carry_ref)

    for t in range(t_n):
        for j in range(n_slab):
            col = t * SSM_WIDTH + j * LANES
            blk = u_ref[:, col:col + LANES].astype(BF16).T
            xs_ref[j * pairs_per_slab:(j + 1) * pairs_per_slab, :, t * h_n:(t + 1) * h_n, :] = (
                blk.reshape(pairs_per_slab, 2, h_n, c_n))

    row = lax.broadcasted_iota(jnp.int32, (n_blk, sw), 0)
    nt = (((1,), (1,)), ((), ()))
    tn = (((0,), (0,)), ((), ()))

    def local_states(pr, _):
        xp = xs_ref[pr].reshape(2 * SSM_CK, c_n)
        loc = lax.dot_general(xp, e2_ref[pr], tn, preferred_element_type=F32)
        loc_ref[pr, 0] = loc[:, :sw]
        loc_ref[pr, 1] = loc[:, sw:]
        return 0
    lax.fori_loop(0, SSM_PAIRS, local_states, 0, unroll=4)

    def chunk_scan(pr, _):
        slot = pr % 2
        power = lambda k: (apsr_ref[pr, k:k + 1, :], apsi_ref[pr, k:k + 1, :])
        zr, zi = [], []
        for lo in range(blk_n):
            rows = pl.ds(lo, n_blk, stride=blk_n)
            xr, xi = loc_ref[pr, 0, rows, :], loc_ref[pr, 1, rows, :]
            if lo:
                dr, di = _cmul(zr[-1], zi[-1], *power(0))
                xr, xi = xr + dr, xi + di
            zr.append(xr)
            zi.append(xi)
        er, ei = zr[-1], zi[-1]
        s = 1
        while s < n_blk:
            dr, di = _cmul(_shift_rows(er, s, row), _shift_rows(ei, s, row),
                           *power(log_blk + s.bit_length() - 1))
            er, ei = er + dr, ei + di
            s *= 2
        cr, ci = carry_ref[pr, 0:1, :], carry_ref[pr, 1:2, :]
        blk_rows = pl.ds(0, n_blk, stride=blk_n)
        hr, hi = _cmul(aptr_ref[pr, blk_rows, :], apti_ref[pr, blk_rows, :], cr, ci)
        br, bi = _shift_rows(er, 1, row) + hr, _shift_rows(ei, 1, row) + hi
        for lo in range(blk_n):
            sr, si = _cmul(br, bi, aptr_ref[pr, lo:lo + 1, :], apti_ref[pr, lo:lo + 1, :])
            if lo:
                sr, si = sr + zr[lo - 1], si + zi[lo - 1]
            rows = pl.ds(lo, n_blk, stride=blk_n)
            sin_ref[slot, 0, rows, :] = sr
            sin_ref[slot, 1, rows, :] = si
        sc_ref[pr, :, :sw] = sin_ref[slot, 0].astype(BF16)
        sc_ref[pr, :, sw:] = sin_ref[slot, 1].astype(BF16)
        nr, ni = _cmul(cr, ci, *power(SSM_LOG_STEPS))
        carry_ref[pr, 0:1, :] = er[n_blk - 1:n_blk, :] + nr
        carry_ref[pr, 1:2, :] = ei[n_blk - 1:n_blk, :] + ni
        return 0
    lax.fori_loop(0, SSM_PAIRS, chunk_scan, 0, unroll=2)

    def outputs(g, _):
        pr = g // 2
        yg = (_dot(w2_ref[g, :, :SSM_CK], xs_ref[pr, g % 2])
              + lax.dot_general(w2_ref[g, :, SSM_CK:], sc_ref[pr], nt, preferred_element_type=F32))
        yt_ref[:, pl.ds(pl.multiple_of(g * h_n, h_n), h_n), :] = yg.reshape(t_n, h_n, c_n)
        return 0
    lax.fori_loop(0, SSM_GROUPS, outputs, 0, unroll=4)

    for t in range(t_n):
        for j in range(n_slab):
            sl = slice(j * LANES, (j + 1) * LANES)
            col = t * SSM_WIDTH + j * LANES
            y_ref[:, col:col + LANES] = yt_ref[t, sl, :].T + d_ref[:, sl] * u_ref[:, col:col + LANES]


def _ssm_scan(u_rows, bsz, d_skip, ops):
    w2, e2, apsr, apsi, aptr, apti = ops
    g, p, c_n = SSM_GROUPS, SSM_STATE, SSM_CHUNKS_PER_TILE
    tiles = u_rows.shape[0] // (bsz * c_n)
    tile = pl.BlockSpec((c_n, SSM_ROW), lambda b, i: (b * tiles + i, 0))
    return pl.pallas_call(
        _ssm_scan_kernel,
        grid=(bsz, tiles),
        in_specs=[tile, _const_spec((1, SSM_WIDTH)),
                  _const_spec(w2.shape), _const_spec(e2.shape),
                  _const_spec(apsr.shape), _const_spec(apsi.shape),
                  _const_spec(aptr.shape), _const_spec(apti.shape)],
        out_specs=tile,
        out_shape=jax.ShapeDtypeStruct(u_rows.shape, F32),
        scratch_shapes=[pltpu.VMEM((SSM_PAIRS, 2, SSM_CK, c_n), BF16),
                        pltpu.VMEM((SSM_PAIRS, c_n, 4 * p), BF16),
                        pltpu.VMEM((SSM_CHUNK, SSM_WIDTH, c_n), F32),
                        pltpu.VMEM((SSM_PAIRS, SUBLANES, 2 * p), F32),
                        pltpu.VMEM((SSM_PAIRS, 2, c_n, 2 * p), F32),
                        pltpu.VMEM((2, 2, c_n, 2 * p), F32)],
        compiler_params=_params("parallel", "arbitrary"),
        name="ssm_scan",
    )(u_rows, d_skip.reshape(1, SSM_WIDTH), w2, e2, apsr, apsi, aptr, apti)


def _attn_kernel(q_ref, k_ref, v_ref, o_ref, lse_ref):
    step = pl.program_id(2)
    qb, nk = ATTN_QB, ATTN_QB + WINDOW_KEYS
    row = lax.broadcasted_iota(jnp.int32, (qb, nk), 0)
    col = lax.broadcasted_iota(jnp.int32, (qb, nk), 1)
    lane = lax.broadcasted_iota(jnp.int32, (qb, LANES), 1)
    first_head = lane < HEAD_DIM
    nt = (((1,), (1,)), ((), ()))
    for sb in range(o_ref.shape[0] // qb):
        rows = slice(sb * qb, (sb + 1) * qb)
        q_start = step * o_ref.shape[0] + sb * qb
        back = jnp.minimum(q_start, WINDOW_KEYS)
        k_start = pl.multiple_of(q_start - back, ATTN_QB)
        dist = row + back - col
        valid = (dist >= 0) & (dist <= WINDOW_KEYS)
        for pair in range(GROUP_WIDTH // LANES):
            cols = slice(pair * LANES, (pair + 1) * LANES)
            qp = q_ref[rows, cols]
            kp = k_ref[pl.ds(k_start, nk), cols]
            vp = v_ref[pl.ds(k_start, nk), cols]
            outs, lses = [], []
            for sel in (first_head, ~first_head):
                qm = jnp.where(sel, qp, jnp.zeros_like(qp))
                s = lax.dot_general(qm, kp, nt, preferred_element_type=F32)
                s = jnp.where(valid, s, NEG_BIG)
                m = jnp.max(s, axis=-1, keepdims=True)
                e = jnp.exp(s - m)
                den = jnp.sum(e, axis=-1, keepdims=True)
                outs.append(_dot(e.astype(BF16), vp) / den)
                lses.append(m + jnp.log(den))
            o_ref[rows, cols] = jnp.where(first_head, outs[0], outs[1]).astype(BF16)
            lse_ref[rows, cols] = jnp.where(first_head, lses[0], lses[1])


def _attn_group(q4, k4, v4):
    bsz, dil, lr, _ = q4.shape
    rows = min(ATTN_STEP_ROWS, lr)
    q_spec = pl.BlockSpec((None, None, rows, GROUP_WIDTH), lambda b, r, i: (b, r, i, 0))
    kv_spec = pl.BlockSpec((None, None, lr, GROUP_WIDTH), lambda b, r, i: (b, r, 0, 0))
    return pl.pallas_call(
        _attn_kernel,
        grid=(bsz, dil, lr // rows),
        in_specs=[q_spec, kv_spec, kv_spec],
        out_specs=[q_spec, q_spec],
        out_shape=[jax.ShapeDtypeStruct(q4.shape, BF16), jax.ShapeDtypeStruct(q4.shape, F32)],
        compiler_params=_params("parallel", "parallel", "arbitrary"),
        name=f"attn_d{dil}",
    )(q4, k4, v4)


def _load_token_major(stage_ref, in_ref):
    dil, rows, width = in_ref.shape
    slabs = []
    for j in range(width // LANES):
        for r in range(dil):
            stage_ref[pl.ds(r, rows, stride=dil), :] = in_ref[r, :, j * LANES:(j + 1) * LANES].astype(F32)
        slabs.append(stage_ref[...])
    return jnp.concatenate(slabs, axis=1)


def _merge_kernel(x_ref, ys_ref, gluw_ref, glub_ref, wa_ref,
                  o0_ref, o1_ref, o2_ref, l0_ref, l1_ref, l2_ref, wb_ref,
                  ga_ref, gb_ref, wout_ref, h_ref, stage_ref):
    y = jax.nn.gelu(_load_chunk_rows(stage_ref, ys_ref, SSM_WIDTH))
    y = y * jax.nn.sigmoid(_dot(y.astype(BF16), gluw_ref[...]) + glub_ref[...])
    ya = _dot(y.astype(BF16), wa_ref[...])

    o0, l0 = o0_ref[0].astype(F32), l0_ref[0]
    o1, l1 = _load_token_major(stage_ref, o1_ref), _load_token_major(stage_ref, l1_ref)
    o2, l2 = _load_token_major(stage_ref, o2_ref), _load_token_major(stage_ref, l2_ref)
    top = jnp.maximum(jnp.maximum(l0, l1), l2)
    w0, w1, w2 = jnp.exp(l0 - top), jnp.exp(l1 - top), jnp.exp(l2 - top)
    attn = (w0 * o0 + w1 * o1 + w2 * o2) / (w0 + w1 + w2)
    yb = _dot(attn.astype(BF16), wb_ref[...])

    mix = ga_ref[...].astype(F32) * ya + gb_ref[...].astype(F32) * yb
    h_ref[...] = x_ref[...] + _dot(mix.astype(BF16), wout_ref[...])


def _merge(x2, seq, ys_rows, glu_w, glu_b, w_a, attn_outs, w_b, ga, gb, w_out):
    n = x2.shape[0]
    tm = TOKEN_TILE
    tiles_per_seq = seq // tm
    row = lambda w: pl.BlockSpec((tm, w), lambda i: (i, 0))
    res = lambda d: pl.BlockSpec((None, d, tm // d, GROUP_WIDTH),
                                 lambda i: (i // tiles_per_seq, 0, i % tiles_per_seq, 0))
    (o0, l0), (o1, l1), (o2, l2) = attn_outs
    d0, d1, d2 = DILATIONS
    return pl.pallas_call(
        _merge_kernel,
        grid=(n // tm,),
        in_specs=[row(D_MODEL), pl.BlockSpec((tm // SSM_CHUNK, SSM_ROW), lambda i: (i, 0)),
                  _const_spec((SSM_WIDTH, SSM_WIDTH)), _const_spec((1, SSM_WIDTH)),
                  _const_spec((SSM_WIDTH, D_MODEL)),
                  res(d0), res(d1), res(d2), res(d0), res(d1), res(d2),
                  _const_spec((GROUP_WIDTH, D_MODEL)), row(D_MODEL), row(D_MODEL),
                  _const_spec((D_MODEL, D_MODEL))],
        out_specs=row(D_MODEL),
        out_shape=jax.ShapeDtypeStruct((n, D_MODEL), F32),
        scratch_shapes=[pltpu.VMEM((tm, LANES), F32)],
        compiler_params=_params("parallel"),
        name="merge",
    )(x2, ys_rows, glu_w.astype(BF16), glu_b.reshape(1, SSM_WIDTH),
      w_a.astype(BF16), o0, o1, o2, l0, l1, l2, w_b.astype(BF16), ga, gb, w_out.astype(BF16))


def _ffn_kernel(h_ref, g2_ref, wg_ref, wu_ref, cw_ref, cb_ref, wd_ref, g3_ref, wpg_ref,
                p_ref, wpp_ref, gf_ref, out_ref, act_ref, carry_ref, *, tiles_per_seq):
    tm = h_ref.shape[0]

    @pl.when(pl.program_id(0) % tiles_per_seq == 0)
    def _():
        carry_ref[...] = jnp.zeros_like(carry_ref)

    subs = [slice(s * SUB_TILE, (s + 1) * SUB_TILE) for s in range(tm // SUB_TILE)]
    hs = [h_ref[rows, :] for rows in subs]
    u2s = [_rms(h, g2_ref[...]).astype(BF16) for h in hs]
    row = lax.broadcasted_iota(jnp.int32, (SUBLANES, FFN_CHUNK), 0)
    for c in range(D_FF // FFN_CHUNK):
        sl = slice(c * FFN_CHUNK, (c + 1) * FFN_CHUNK)
        prev = carry_ref[:, sl]
        for rows, u2 in zip(subs, u2s):
            gp = _dot(u2, wg_ref[:, sl])
            up = _dot(u2, wu_ref[:, sl])
            r1 = pltpu.roll(gp, 1, 0)
            r2 = pltpu.roll(gp, 2, 0)
            r1 = jnp.concatenate([jnp.where(row < 1, pltpu.roll(prev, 1, 0), r1[:SUBLANES]), r1[SUBLANES:]], axis=0)
            r2 = jnp.concatenate([jnp.where(row < 2, pltpu.roll(prev, 2, 0), r2[:SUBLANES]), r2[SUBLANES:]], axis=0)
            gate = cw_ref[0:1, sl] * r2 + cw_ref[1:2, sl] * r1 + cw_ref[2:3, sl] * gp + cb_ref[:, sl]
            act_ref[rows, sl] = (jax.nn.gelu(gate) * up).astype(BF16)
            prev = gp[SUB_TILE - SUBLANES:, :]
        carry_ref[:, sl] = prev
    for rows, h in zip(subs, hs):
        h = h + _dot(act_ref[rows, :], wd_ref[...])
        u3 = _rms(h, g3_ref[...]).astype(BF16)
        h = h + jax.nn.sigmoid(_dot(u3, wpg_ref[...])) * _dot(p_ref[rows, :].astype(BF16), wpp_ref[...])
        out_ref[rows, :] = _rms(h, gf_ref[...])


def _ffn(h1, seq, p2, norm_g, w_gate, w_up, conv_w, conv_b, w_down, ple_g, ple_w_gate, ple_w_proj, final_g):
    n = h1.shape[0]
    tm = TOKEN_TILE
    row = lambda w: pl.BlockSpec((tm, w), lambda i: (i, 0))
    vec = lambda a: a.reshape(1, -1)
    return pl.pallas_call(
        functools.partial(_ffn_kernel, tiles_per_seq=seq // tm),
        grid=(n // tm,),
        in_specs=[row(D_MODEL), _const_spec((1, D_MODEL)), _const_spec((D_MODEL, D_FF)),
                  _const_spec((D_MODEL, D_FF)), _const_spec((CONV_WIDTH, D_FF)), _const_spec((1, D_FF)),
                  _const_spec((D_FF, D_MODEL)), _const_spec((1, D_MODEL)), _const_spec((D_MODEL, D_MODEL)),
                  row(PLE_DIM), _const_spec((PLE_DIM, D_MODEL)), _const_spec((1, D_MODEL))],
        out_specs=row(D_MODEL),
        out_shape=jax.ShapeDtypeStruct((n, D_MODEL), F32),
        scratch_shapes=[pltpu.VMEM((tm, D_FF), BF16), pltpu.VMEM((SUBLANES, D_FF), F32)],
        compiler_params=_params("arbitrary"),
        name="ffn",
    )(h1, vec(norm_g), w_gate.astype(BF16), w_up.astype(BF16), conv_w, vec(conv_b),
      w_down.astype(BF16), vec(ple_g), ple_w_gate.astype(BF16), p2, ple_w_proj.astype(BF16), vec(final_g))


def _layer(h2, bsz, seq, p2, mix_norm_g, w_in, gate_b, ssm_lam_re, ssm_lam_im, ssm_log_dt, ssm_b_re,
           ssm_b_im, ssm_c_re, ssm_c_im, ssm_d, ssm_glu_w, ssm_glu_b, w_branch_a, w_branch_b, w_out,
           ffn_norm_g, ffn_w_gate, ffn_w_up, ffn_conv_w, ffn_conv_b, ffn_w_down,
           ple_norm_g, ple_w_gate, ple_w_proj, out_norm_g):
    u_rows, qkv, ga, gb = _in_proj(h2, seq, mix_norm_g, w_in, gate_b)
    ops = _ssm_operators(ssm_lam_re, ssm_lam_im, ssm_log_dt, ssm_b_re, ssm_b_im, ssm_c_re, ssm_c_im)
    ys_rows = _ssm_scan(u_rows, bsz, ssm_d.reshape(-1), ops)
    attn_outs = [_attn_group(*group) for group in qkv]
    h1 = _merge(h2, seq, ys_rows, ssm_glu_w, ssm_glu_b, w_branch_a, attn_outs, w_branch_b, ga, gb, w_out)
    return _ffn(h1, seq, p2, ffn_norm_g, ffn_w_gate, ffn_w_up, ffn_conv_w, ffn_conv_b, ffn_w_down,
                ple_norm_g, ple_w_gate, ple_w_proj, out_norm_g)


def kernel(x, p, mix_norm_g, w_in, gate_b, ssm_lam_re, ssm_lam_im, ssm_log_dt, ssm_b_re, ssm_b_im, ssm_c_re, ssm_c_im, ssm_d, ssm_glu_w, ssm_glu_b, w_branch_a, w_branch_b, w_out, ffn_norm_g, ffn_w_gate, ffn_w_up, ffn_conv_w, ffn_conv_b, ffn_w_down, ple_norm_g, ple_w_gate, ple_w_proj, final_norm_g):
    bsz, seq, _ = x.shape
    depth = p.shape[0]
    assert depth == 1, "the final norm is fused into the layer's last kernel"
    h2 = x.reshape(bsz * seq, D_MODEL)
    out = _layer(h2, bsz, seq, p[0].reshape(bsz * seq, PLE_DIM), mix_norm_g[0], w_in[0], gate_b[0],
                 ssm_lam_re[0], ssm_lam_im[0], ssm_log_dt[0], ssm_b_re[0], ssm_b_im[0], ssm_c_re[0],
                 ssm_c_im[0], ssm_d[0], ssm_glu_w[0], ssm_glu_b[0], w_branch_a[0], w_branch_b[0],
                 w_out[0], ffn_norm_g[0], ffn_w_gate[0], ffn_w_up[0], ffn_conv_w[0], ffn_conv_b[0],
                 ffn_w_down[0], ple_norm_g[0], ple_w_gate[0], ple_w_proj[0], final_norm_g)
    return out.reshape(bsz, seq, D_MODEL)
```

```python
import functools

import jax
import jax.numpy as jnp
from jax import lax
from jax.experimental import pallas as pl
from jax.experimental.pallas import tpu as pltpu

F32 = jnp.float32
BF16 = jnp.bfloat16

D_MODEL = 1024
EPS = 1e-6
PLE_DIM = 256
SSM_GROUP = 16
SSM_STATE = 64
SSM_WIDTH = 512
SSM_GROUPS = SSM_WIDTH // SSM_GROUP
HEAD_DIM = 64
DILATIONS = (1, 4, 16)
WINDOW_KEYS = 128
HEADS_PER_GROUP = 4
GROUP_WIDTH = HEADS_PER_GROUP * HEAD_DIM
ATTN_WIDTH = len(DILATIONS) * GROUP_WIDTH
ROT_DIM = HEAD_DIM // 4
ROPE_THETA = 500000.0
NEG_BIG = -1e30
D_FF = 2816
CONV_WIDTH = 3
OFF_Q = SSM_WIDTH
OFF_K = OFF_Q + ATTN_WIDTH
OFF_V = OFF_K + ATTN_WIDTH
OFF_GA = OFF_V + ATTN_WIDTH
OFF_GB = OFF_GA + D_MODEL
IN_WIDTH = OFF_GB + D_MODEL

LANES = 128
SUBLANES = 8
VMEM_LIMIT_BYTES = 56 * 1024 * 1024

TOKEN_TILE = 1024
SUB_TILE = 256
COL_TILE = 256
SSM_CHUNK = 16
SSM_CHUNKS_PER_TILE = LANES
SSM_TILE = SSM_CHUNK * SSM_CHUNKS_PER_TILE
SSM_CK = SSM_CHUNK * SSM_GROUP
SSM_ROW = SSM_CHUNK * SSM_WIDTH
SSM_PAIRS = SSM_GROUPS // 2
SSM_LOG_STEPS = 7
ATTN_QB = 128
ATTN_STEP_ROWS = 1024
FFN_CHUNK = 256


def _dot(a, b):
    return jnp.dot(a, b, preferred_element_type=F32)


def _rms(x, g):
    var = jnp.mean(x * x, axis=-1, keepdims=True)
    return x * lax.rsqrt(var + EPS) * g


def _const_spec(shape):
    nd = len(shape)
    return pl.BlockSpec(shape, lambda *_: (0,) * nd, pipeline_mode=pl.Buffered(1))


def _params(*sem):
    return pltpu.CompilerParams(dimension_semantics=sem, vmem_limit_bytes=VMEM_LIMIT_BYTES)


def _rope(z, cos, sin_lo, sin_hi):
    up = pltpu.roll(z, LANES - ROT_DIM // 2, 1)
    dn = pltpu.roll(z, ROT_DIM // 2, 1)
    return z * cos + up * sin_lo + dn * sin_hi


def _store_residue_major(stage_ref, out_ref, z, dil, tok0):
    rows, first = z.shape[0] // dil, tok0 // dil
    for j in range(z.shape[1] // LANES):
        sl = slice(j * LANES, (j + 1) * LANES)
        stage_ref[...] = z[:, sl]
        for r in range(dil):
            out_ref[r, first:first + rows, sl] = stage_ref[pl.ds(r, rows, stride=dil), :].astype(out_ref.dtype)


def _store_chunk_rows(stage_ref, out_ref, z, c0):
    rows, width = z.shape[0] // SSM_CHUNK, z.shape[1]
    for j in range(width // LANES):
        stage_ref[...] = z[:, j * LANES:(j + 1) * LANES]
        for t in range(SSM_CHUNK):
            col = t * width + j * LANES
            out_ref[c0:c0 + rows, col:col + LANES] = stage_ref[pl.ds(t, rows, stride=SSM_CHUNK), :]


def _load_chunk_rows(stage_ref, in_ref, width, c0, rows):
    slabs = []
    for j in range(width // LANES):
        for t in range(SSM_CHUNK):
            col = t * width + j * LANES
            stage_ref[pl.ds(t, rows, stride=SSM_CHUNK), :] = in_ref[c0:c0 + rows, col:col + LANES]
        slabs.append(stage_ref[...])
    return jnp.concatenate(slabs, axis=1)


def _in_proj_kernel(x_ref, g_ref, ca_ref, sa_ref, cb_ref, sb_ref, sign_ref, w_ref,
                    s_ref, q0_ref, k0_ref, v0_ref, q1_ref, k1_ref, v1_ref, q2_ref, k2_ref, v2_ref,
                    stage_ref, u_ref):
    scale = HEAD_DIM ** -0.5
    ca, sa = ca_ref[...], sa_ref[...]
    for s in range(x_ref.shape[0] // SUB_TILE):
        tok0 = s * SUB_TILE
        rows = slice(tok0, tok0 + SUB_TILE)
        stage = stage_ref.at[s]
        u_ref[s] = _rms(x_ref[rows, :], g_ref[...]).astype(BF16)
        proj = lambda off, piece, width=COL_TILE: _dot(
            u_ref[s], w_ref[:, off + piece * width:off + (piece + 1) * width])
        _store_chunk_rows(stage, s_ref, jnp.concatenate(
            [proj(0, piece) for piece in range(SSM_WIDTH // COL_TILE)], axis=1), tok0 // SSM_CHUNK)
        cb, sb = cb_ref[rows, :], sb_ref[rows, :]
        cos = ca * cb - sa * sb
        sin = sa * cb + ca * sb
        slo, shi = sin * sign_ref[0:1, :], sin * sign_ref[1:2, :]

        def rope(z):
            return jnp.concatenate([_rope(z[:, j * LANES:(j + 1) * LANES], cos, slo, shi)
                                    for j in range(z.shape[1] // LANES)], axis=1)

        for grp, dil in enumerate(DILATIONS):
            for off, refs, rotate, gain in ((OFF_Q, (q0_ref, q1_ref, q2_ref), True, scale),
                                            (OFF_K, (k0_ref, k1_ref, k2_ref), True, None),
                                            (OFF_V, (v0_ref, v1_ref, v2_ref), False, None)):
                z = proj(off, grp, GROUP_WIDTH)
                if rotate:
                    z = rope(z)
                if gain is not None:
                    z = z * gain
                if dil == 1:
                    refs[grp][rows, :] = z.astype(BF16)
                else:
                    _store_residue_major(stage, refs[grp], z, dil, tok0)


def _rope_tables(seq, tm):
    half = ROT_DIM // 2
    freqs = ROPE_THETA ** (-jnp.arange(half, dtype=F32) * (2.0 / ROT_DIM))
    head = jnp.concatenate([freqs, freqs, jnp.zeros((HEAD_DIM - ROT_DIM,), F32)])
    lane_freq = jnp.tile(head, LANES // HEAD_DIM)[None, :]
    base = jnp.arange(0, seq, tm, dtype=F32)[:, None] * lane_freq
    offs = jnp.arange(tm, dtype=F32)[:, None] * lane_freq
    in_head = jnp.arange(LANES) % HEAD_DIM
    sign = jnp.zeros((SUBLANES, LANES), F32)
    sign = sign.at[0].set(jnp.where(in_head < half, -1.0, 0.0))
    sign = sign.at[1].set(jnp.where((in_head >= half) & (in_head < ROT_DIM), 1.0, 0.0))
    n_tiles = seq // tm
    return (jnp.cos(base).reshape(n_tiles, 1, LANES), jnp.sin(base).reshape(n_tiles, 1, LANES),
            jnp.cos(offs), jnp.sin(offs), sign)


def _in_proj(x2, seq, norm_g, w_in_bf):
    n = x2.shape[0]
    tm = TOKEN_TILE
    tiles_per_seq = seq // tm
    bsz = n // seq
    cos_a, sin_a, cos_b, sin_b, sign = _rope_tables(seq, tm)

    row = lambda w: pl.BlockSpec((tm, w), lambda i: (i, 0))
    tile_tab = pl.BlockSpec((None, 1, LANES), lambda i: (i % tiles_per_seq, 0, 0))
    res = lambda d: pl.BlockSpec((None, d, tm // d, GROUP_WIDTH),
                                 lambda i: (i // tiles_per_seq, 0, i % tiles_per_seq, 0))
    res_shape = lambda d: jax.ShapeDtypeStruct((bsz, d, seq // d, GROUP_WIDTH), BF16)
    nat_shape = jax.ShapeDtypeStruct((n, GROUP_WIDTH), BF16)
    d1, d2 = DILATIONS[1], DILATIONS[2]
    outs = pl.pallas_call(
        _in_proj_kernel,
        grid=(n // tm,),
        in_specs=[row(D_MODEL), _const_spec((1, D_MODEL)), tile_tab, tile_tab,
                  _const_spec((tm, LANES)), _const_spec((tm, LANES)), _const_spec((SUBLANES, LANES)),
                  _const_spec((D_MODEL, OFF_GA))],
        out_specs=[pl.BlockSpec((tm // SSM_CHUNK, SSM_ROW), lambda i: (i, 0))]
                  + [row(GROUP_WIDTH)] * 3 + [res(d1)] * 3 + [res(d2)] * 3,
        out_shape=[jax.ShapeDtypeStruct((n // SSM_CHUNK, SSM_ROW), F32)] + [nat_shape] * 3
                  + [res_shape(d1)] * 3 + [res_shape(d2)] * 3,
        scratch_shapes=[pltpu.VMEM((tm // SUB_TILE, SUB_TILE, LANES), F32),
                        pltpu.VMEM((tm // SUB_TILE, SUB_TILE, D_MODEL), BF16)],
        compiler_params=_params("parallel"),
        name="in_proj",
    )(x2, norm_g.reshape(1, D_MODEL), cos_a, sin_a, cos_b, sin_b, sign, w_in_bf)
    u, q0, k0, v0, q1, k1, v1, q2, k2, v2 = outs
    nat4 = lambda a: a.reshape(bsz, 1, seq, GROUP_WIDTH)
    qkv = ((nat4(q0), nat4(k0), nat4(v0)), (q1, k1, v1), (q2, k2, v2))
    return u, qkv


def _cmul(ar, ai, br, bi):
    return ar * br - ai * bi, ar * bi + ai * br


def _discretise(lr, li, dt):
    mag = jnp.exp(lr * dt)
    ar = mag * jnp.cos(li * dt)
    ai = mag * jnp.sin(li * dt)
    den = lr * lr + li * li
    cr = ((ar - 1.0) * lr + ai * li) / den
    ci = (ai * lr - (ar - 1.0) * li) / den
    return ar, ai, cr, ci


def _ssm_prep_kernel(lr_ref, li_ref, ldt_ref, brt_ref, bit_ref, cre_ref, cim_ref, lrf_ref, lif_ref, ldtf_ref,
                     w2_ref, e2_ref, apsr_ref, apsi_ref, aptr_ref, apti_ref, er_ref, ei_ref):
    g_n, t_n, h_n, sw = SSM_GROUPS, SSM_CHUNK, SSM_GROUP, 2 * SSM_STATE
    ar, ai, cr, ci = _discretise(lr_ref[...], li_ref[...], jnp.exp(ldt_ref[...]))
    brt, bit = brt_ref[...], bit_ref[...]
    bbr = cr * brt - ci * bit
    bbi = cr * bit + ci * brt
    cre, cim = cre_ref[...], cim_ref[...]
    by_group = lambda a: a.reshape(g_n, h_n, sw)
    pr, pi = jnp.ones_like(ar), jnp.zeros_like(ai)
    for j in range(t_n):
        rows = slice((t_n - 1 - j) * h_n, (t_n - j) * h_n)
        rr, ri = _cmul(pr, pi, bbr, bbi)
        er_ref[:, rows, :] = by_group(rr)
        ei_ref[:, rows, :] = by_group(ri)
        e2_ref[:, rows, :sw] = by_group(rr).astype(BF16)
        e2_ref[:, rows, sw:] = by_group(ri).astype(BF16)
        pr, pi = _cmul(pr, pi, ar, ai)
        rows = slice(j * h_n, (j + 1) * h_n)
        w2_ref[:, rows, SSM_CK:SSM_CK + sw] = by_group(cre * pr - cim * pi).astype(BF16)
        w2_ref[:, rows, SSM_CK + sw:] = by_group(-cre * pi - cim * pr).astype(BF16)

    nt = (((1,), (1,)), ((), ()))
    hi = lax.Precision.HIGHEST

    def toeplitz(g, _):
        rows = pl.ds(pl.multiple_of(g * h_n, h_n), h_n)
        krev = (lax.dot_general(cre_ref[rows, :], er_ref[g], nt, precision=hi, preferred_element_type=F32)
                - lax.dot_general(cim_ref[rows, :], ei_ref[g], nt, precision=hi, preferred_element_type=F32))
        kext = jnp.concatenate([krev, jnp.zeros_like(krev)], axis=1)
        for t in range(t_n):
            off = (t_n - 1 - t) * h_n
            win = kext if off == 0 else pltpu.roll(kext, 2 * SSM_CK - off, 1)
            w2_ref[g, t * h_n:(t + 1) * h_n, :SSM_CK] = win[:, :SSM_CK].astype(BF16)
        return 0
    lax.fori_loop(0, g_n, toeplitz, 0, unroll=4)

    acr, aci, _, _ = _discretise(lrf_ref[...], lif_ref[...], jnp.exp(ldtf_ref[...]))
    for _ in range(4):
        acr, aci = _cmul(acr, aci, acr, aci)
    shape = (SUBLANES, acr.shape[1])
    row = lax.broadcasted_iota(jnp.int32, shape, 0)
    qr, qi = jnp.ones(shape, F32), jnp.zeros(shape, F32)
    for k in range(SSM_LOG_STEPS + 1):
        for p2 in range(SSM_PAIRS):
            apsr_ref[p2, k:k + 1, :] = acr[:, p2 * sw:(p2 + 1) * sw]
            apsi_ref[p2, k:k + 1, :] = aci[:, p2 * sw:(p2 + 1) * sw]
        if (1 << k) < SUBLANES:
            nr, ni = _cmul(qr, qi, acr, aci)
            bit_set = (row & (1 << k)) != 0
            qr, qi = jnp.where(bit_set, nr, qr), jnp.where(bit_set, ni, qi)
        if (1 << k) == SUBLANES:
            a8r, a8i = acr, aci
        acr, aci = _cmul(acr, aci, acr, aci)
    br, bi = jnp.ones_like(a8r), jnp.zeros_like(a8i)
    for b in range(SSM_CHUNKS_PER_TILE // SUBLANES):
        rows = slice(b * SUBLANES, (b + 1) * SUBLANES)
        blk_r, blk_i = _cmul(qr, qi, br, bi)
        for p2 in range(SSM_PAIRS):
            aptr_ref[p2, rows, :] = blk_r[:, p2 * sw:(p2 + 1) * sw]
            apti_ref[p2, rows, :] = blk_i[:, p2 * sw:(p2 + 1) * sw]
        br, bi = _cmul(br, bi, a8r, a8i)


def _pad_pair_lanes(a):
    z = jnp.zeros_like(a)
    even = (jnp.arange(a.shape[0]) % 2 == 0)[:, None, None]
    padded = jnp.where(even, jnp.concatenate([a, z], -1), jnp.concatenate([z, a], -1))
    return padded.reshape(a.shape[0] * a.shape[1], 2 * a.shape[2])


def _ssm_operators(lam_re, lam_im, log_dt, b_re, b_im, c_re, c_im):
    g, p, h = SSM_GROUPS, SSM_STATE, SSM_GROUP
    gp, sw = g * p, 2 * p
    rep = lambda a: jnp.repeat(jnp.tile(a, (1, 2)), h, axis=0)
    ldt2 = jnp.broadcast_to(log_dt[:, None], (g, p))
    full = lambda shape: pl.BlockSpec(shape, lambda: (0,) * len(shape))
    in_arrays = (rep(lam_re), rep(lam_im), rep(ldt2),
                 _pad_pair_lanes(b_re.transpose(0, 2, 1)), _pad_pair_lanes(b_im.transpose(0, 2, 1)),
                 _pad_pair_lanes(c_re), _pad_pair_lanes(c_im),
                 lam_re.reshape(1, gp), lam_im.reshape(1, gp), ldt2.reshape(1, gp))
    out_shapes = ([((g, SSM_CK, SSM_CK + 2 * sw), BF16), ((g, SSM_CK, 2 * sw), BF16)]
                  + [((SSM_PAIRS, SUBLANES, sw), F32)] * 2 + [((SSM_PAIRS, SSM_CHUNKS_PER_TILE, sw), F32)] * 2)
    w2, e2, apsr, apsi, aptr, apti = pl.pallas_call(
        _ssm_prep_kernel,
        in_specs=[full(a.shape) for a in in_arrays],
        out_specs=[full(s) for s, _ in out_shapes],
        out_shape=[jax.ShapeDtypeStruct(s, dt) for s, dt in out_shapes],
        scratch_shapes=[pltpu.VMEM((g, SSM_CK, sw), F32), pltpu.VMEM((g, SSM_CK, sw), F32)],
        compiler_params=pltpu.CompilerParams(vmem_limit_bytes=VMEM_LIMIT_BYTES),
        name="ssm_prep",
    )(*in_arrays)
    return w2, e2.reshape(SSM_PAIRS, 2 * SSM_CK, 2 * sw), apsr, apsi, aptr, apti


def _shift_rows(z, s, row):
    if s % SUBLANES == 0:
        return jnp.concatenate([jnp.zeros((s, z.shape[1]), z.dtype), z[:-s]], axis=0)
    return jnp.where(row >= s, pltpu.roll(z, s, 0), 0.0)


def _ssm_scan_kernel(u_ref, d_ref, w2_ref, e2_ref, apsr_ref, apsi_ref, aptr_ref, apti_ref,
                     y_ref, xs_ref, sc_ref, yt_ref, carry_ref, loc_ref, sin_ref):
    t_n, h_n, c_n = SSM_CHUNK, SSM_GROUP, SSM_CHUNKS_PER_TILE
    n_slab = SSM_WIDTH // LANES
    pairs_per_slab = LANES // (2 * h_n)
    sw = 2 * SSM_STATE
    blk_n = SUBLANES
    n_blk = c_n // blk_n
    log_blk = blk_n.bit_length() - 1

    @pl.when(pl.program_id(1) == 0)
    def _():
        carry_ref[...] = jnp.zeros_like(carry_ref)

    for t in range(t_n):
        for j in range(n_slab):
            col = t * SSM_WIDTH + j * LANES
            blk = u_ref[:, col:col + LANES].astype(BF16).T
            xs_ref[j * pairs_per_slab:(j + 1) * pairs_per_slab, :, t * h_n:(t + 1) * h_n, :] = (
                blk.reshape(pairs_per_slab, 2, h_n, c_n))

    row = lax.broadcasted_iota(jnp.int32, (n_blk, sw), 0)
    nt = (((1,), (1,)), ((), ()))
    tn = (((0,), (0,)), ((), ()))

    def local_states(pr, _):
        xp = xs_ref[pr].reshape(2 * SSM_CK, c_n)
        loc = lax.dot_general(xp, e2_ref[pr], tn, preferred_element_type=F32)
        loc_ref[pr, 0] = loc[:, :sw]
        loc_ref[pr, 1] = loc[:, sw:]
        return 0
    lax.fori_loop(0, SSM_PAIRS, local_states, 0, unroll=4)

    def chunk_scan(pr, _):
        slot = pr % 2
        power = lambda k: (apsr_ref[pr, k:k + 1, :], apsi_ref[pr, k:k + 1, :])
        zr, zi = [], []
        for lo in range(blk_n):
            rows = pl.ds(lo, n_blk, stride=blk_n)
            xr, xi = loc_ref[pr, 0, rows, :], loc_ref[pr, 1, rows, :]
            if lo:
                dr, di = _cmul(zr[-1], zi[-1], *power(0))
                xr, xi = xr + dr, xi + di
            zr.append(xr)
            zi.append(xi)
        er, ei = zr[-1], zi[-1]
        s = 1
        while s < n_blk:
            dr, di = _cmul(_shift_rows(er, s, row), _shift_rows(ei, s, row),
                           *power(log_blk + s.bit_length() - 1))
            er, ei = er + dr, ei + di
            s *= 2
        cr, ci = carry_ref[pr, 0:1, :], carry_ref[pr, 1:2, :]
        blk_rows = pl.ds(0, n_blk, stride=blk_n)
        hr, hi = _cmul(aptr_ref[pr, blk_rows, :], apti_ref[pr, blk_rows, :], cr, ci)
        br, bi = _shift_rows(er, 1, row) + hr, _shift_rows(ei, 1, row) + hi
        for lo in range(blk_n):
            sr, si = _cmul(br, bi, aptr_ref[pr, lo:lo + 1, :], apti_ref[pr, lo:lo + 1, :])
            if lo:
                sr, si = sr + zr[lo - 1], si + zi[lo - 1]
            rows = pl.ds(lo, n_blk, stride=blk_n)
            sin_ref[slot, 0, rows, :] = sr
            sin_ref[slot, 1, rows, :] = si
        sc_ref[pr, :, :sw] = sin_ref[slot, 0].astype(BF16)
        sc_ref[pr, :, sw:] = sin_ref[slot, 1].astype(BF16)
        nr, ni = _cmul(cr, ci, *power(SSM_LOG_STEPS))
        carry_ref[pr, 0:1, :] = er[n_blk - 1:n_blk, :] + nr
        carry_ref[pr, 1:2, :] = ei[n_blk - 1:n_blk, :] + ni
        return 0
    lax.fori_loop(0, SSM_PAIRS, chunk_scan, 0, unroll=2)

    def outputs(g, _):
        pr = g // 2
        yg = (_dot(w2_ref[g, :, :SSM_CK], xs_ref[pr, g % 2])
              + lax.dot_general(w2_ref[g, :, SSM_CK:], sc_ref[pr], nt, preferred_element_type=F32))
        yt_ref[:, pl.ds(pl.multiple_of(g * h_n, h_n), h_n), :] = yg.reshape(t_n, h_n, c_n)
        return 0
    lax.fori_loop(0, SSM_GROUPS, outputs, 0, unroll=4)

    for t in range(t_n):
        for j in range(n_slab):
            sl = slice(j * LANES, (j + 1) * LANES)
            col = t * SSM_WIDTH + j * LANES
            y_ref[:, col:col + LANES] = yt_ref[t, sl, :].T + d_ref[:, sl] * u_ref[:, col:col + LANES]


def _ssm_scan(u_rows, bsz, d_skip, ops):
    w2, e2, apsr, apsi, aptr, apti = ops
    g, p, c_n = SSM_GROUPS, SSM_STATE, SSM_CHUNKS_PER_TILE
    tiles = u_rows.shape[0] // (bsz * c_n)
    tile = pl.BlockSpec((c_n, SSM_ROW), lambda b, i: (b * tiles + i, 0))
    return pl.pallas_call(
        _ssm_scan_kernel,
        grid=(bsz, tiles),
        in_specs=[tile, _const_spec((1, SSM_WIDTH)),
                  _const_spec(w2.shape), _const_spec(e2.shape),
                  _const_spec(apsr.shape), _const_spec(apsi.shape),
                  _const_spec(aptr.shape), _const_spec(apti.shape)],
        out_specs=tile,
        out_shape=jax.ShapeDtypeStruct(u_rows.shape, F32),
        scratch_shapes=[pltpu.VMEM((SSM_PAIRS, 2, SSM_CK, c_n), BF16),
                        pltpu.VMEM((SSM_PAIRS, c_n, 4 * p), BF16),
                        pltpu.VMEM((SSM_CHUNK, SSM_WIDTH, c_n), F32),
                        pltpu.VMEM((SSM_PAIRS, SUBLANES, 2 * p), F32),
                        pltpu.VMEM((SSM_PAIRS, 2, c_n, 2 * p), F32),
                        pltpu.VMEM((2, 2, c_n, 2 * p), F32)],
        compiler_params=_params("parallel", "arbitrary"),
        name="ssm_scan",
    )(u_rows, d_skip.reshape(1, SSM_WIDTH), w2, e2, apsr, apsi, aptr, apti)


def _attn_kernel(q_ref, k_ref, v_ref, o_ref, lse_ref):
    step = pl.program_id(2)
    qb, nk = ATTN_QB, ATTN_QB + WINDOW_KEYS
    row = lax.broadcasted_iota(jnp.int32, (qb, nk), 0)
    col = lax.broadcasted_iota(jnp.int32, (qb, nk), 1)
    lane = lax.broadcasted_iota(jnp.int32, (qb, LANES), 1)
    first_head = lane < HEAD_DIM
    nt = (((1,), (1,)), ((), ()))
    for sb in range(o_ref.shape[0] // qb):
        rows = slice(sb * qb, (sb + 1) * qb)
        q_start = step * o_ref.shape[0] + sb * qb
        back = jnp.minimum(q_start, WINDOW_KEYS)
        k_start = pl.multiple_of(q_start - back, ATTN_QB)
        dist = row + back - col
        valid = (dist >= 0) & (dist <= WINDOW_KEYS)
        for pair in range(GROUP_WIDTH // LANES):
            cols = slice(pair * LANES, (pair + 1) * LANES)
            qp = q_ref[rows, cols]
            kp = k_ref[pl.ds(k_start, nk), cols]
            vp = v_ref[pl.ds(k_start, nk), cols]
            outs, lses = [], []
            for sel in (first_head, ~first_head):
                qm = jnp.where(sel, qp, jnp.zeros_like(qp))
                s = lax.dot_general(qm, kp, nt, preferred_element_type=F32)
                s = jnp.where(valid, s, NEG_BIG)
                m = jnp.max(s, axis=-1, keepdims=True)
                e = jnp.exp(s - m)
                den = jnp.sum(e, axis=-1, keepdims=True)
                outs.append(_dot(e.astype(BF16), vp) / den)
                lses.append(m + jnp.log(den))
            o_ref[rows, cols] = jnp.where(first_head, outs[0], outs[1]).astype(BF16)
            lse_ref[rows, cols] = jnp.where(first_head, lses[0], lses[1])


def _attn_group(q4, k4, v4):
    bsz, dil, lr, _ = q4.shape
    rows = min(ATTN_STEP_ROWS, lr)
    q_spec = pl.BlockSpec((None, None, rows, GROUP_WIDTH), lambda b, r, i: (b, r, i, 0))
    kv_spec = pl.BlockSpec((None, None, lr, GROUP_WIDTH), lambda b, r, i: (b, r, 0, 0))
    return pl.pallas_call(
        _attn_kernel,
        grid=(bsz, dil, lr // rows),
        in_specs=[q_spec, kv_spec, kv_spec],
        out_specs=[q_spec, q_spec],
        out_shape=[jax.ShapeDtypeStruct(q4.shape, BF16), jax.ShapeDtypeStruct(q4.shape, F32)],
        compiler_params=_params("parallel", "parallel", "arbitrary"),
        name=f"attn_d{dil}",
    )(q4, k4, v4)


def _load_token_major(stage_ref, in_ref, tok0, ntok, lanes):
    dil = in_ref.shape[0]
    first, rows = tok0 // dil, ntok // dil
    if dil == 1:
        return in_ref[0, first:first + rows, lanes].astype(F32)
    for r in range(dil):
        stage_ref[pl.ds(r, rows, stride=dil), :] = in_ref[r, first:first + rows, lanes].astype(F32)
    return stage_ref[...]


def _merge_kernel(x_ref, g_ref, ys_ref, gluw_ref, glub_ref, wa_ref,
                  o0_ref, o1_ref, o2_ref, l0_ref, l1_ref, l2_ref, wb_ref,
                  wg_ref, bg_ref, wout_ref, h_ref, stage_ref, u_ref, y_ref, ya_ref, attn_ref, mix_ref):
    pieces = lambda width: [slice(c * COL_TILE, (c + 1) * COL_TILE) for c in range(width // COL_TILE)]
    for s in range(x_ref.shape[0] // SUB_TILE):
        tok0 = s * SUB_TILE
        rows = slice(tok0, tok0 + SUB_TILE)
        stage = stage_ref.at[s]
        u_ref[s] = _rms(x_ref[rows, :], g_ref[...]).astype(BF16)
        y_ref[s] = jax.nn.gelu(_load_chunk_rows(stage, ys_ref, SSM_WIDTH, tok0 // SSM_CHUNK, SUB_TILE // SSM_CHUNK))
        y_bf = y_ref[s].astype(BF16)
        for cols in pieces(SSM_WIDTH):
            gate = jax.nn.sigmoid(_dot(y_bf, gluw_ref[:, cols]) + glub_ref[:, cols])
            ya_ref[s, :, cols] = (y_ref[s, :, cols] * gate).astype(BF16)

        for j in range(GROUP_WIDTH // LANES):
            lanes = slice(j * LANES, (j + 1) * LANES)
            o, l = ([_load_token_major(stage, ref, tok0, SUB_TILE, lanes) for ref in refs]
                    for refs in ((o0_ref, o1_ref, o2_ref), (l0_ref, l1_ref, l2_ref)))
            top = jnp.maximum(jnp.maximum(l[0], l[1]), l[2])
            w = [jnp.exp(lg - top) for lg in l]
            attn_ref[s, :, lanes] = ((w[0] * o[0] + w[1] * o[1] + w[2] * o[2])
                                     / (w[0] + w[1] + w[2])).astype(BF16)

        for c, cols in enumerate(pieces(D_MODEL)):
            cols_b = slice(D_MODEL + c * COL_TILE, D_MODEL + (c + 1) * COL_TILE)
            gate_a = jax.nn.sigmoid(_dot(u_ref[s], wg_ref[:, cols]) + bg_ref[:, cols])
            gate_b = jax.nn.sigmoid(_dot(u_ref[s], wg_ref[:, cols_b]) + bg_ref[:, cols_b])
            mix = (gate_a * _dot(ya_ref[s], wa_ref[:, cols]) + gate_b * _dot(attn_ref[s], wb_ref[:, cols]))
            mix_ref[s, :, cols] = mix.astype(BF16)
        for cols in pieces(D_MODEL):
            h_ref[rows, cols] = x_ref[rows, cols] + _dot(mix_ref[s], wout_ref[:, cols])


def _merge(x2, seq, norm_g, ys_rows, glu_w, glu_b, w_a, attn_outs, w_b, w_gates_bf, gate_b, w_out):
    n = x2.shape[0]
    tm = TOKEN_TILE
    tiles_per_seq = seq // tm
    row = lambda w: pl.BlockSpec((tm, w), lambda i: (i, 0))
    res = lambda d: pl.BlockSpec((None, d, tm // d, GROUP_WIDTH),
                                 lambda i: (i // tiles_per_seq, 0, i % tiles_per_seq, 0))
    (o0, l0), (o1, l1), (o2, l2) = attn_outs
    d0, d1, d2 = DILATIONS
    return pl.pallas_call(
        _merge_kernel,
        grid=(n // tm,),
        in_specs=[row(D_MODEL), _const_spec((1, D_MODEL)),
                  pl.BlockSpec((tm // SSM_CHUNK, SSM_ROW), lambda i: (i, 0)),
                  _const_spec((SSM_WIDTH, SSM_WIDTH)), _const_spec((1, SSM_WIDTH)),
                  _const_spec((SSM_WIDTH, D_MODEL)),
                  res(d0), res(d1), res(d2), res(d0), res(d1), res(d2),
                  _const_spec((GROUP_WIDTH, D_MODEL)),
                  _const_spec((D_MODEL, 2 * D_MODEL)), _const_spec((1, 2 * D_MODEL)),
                  _const_spec((D_MODEL, D_MODEL))],
        out_specs=row(D_MODEL),
        out_shape=jax.ShapeDtypeStruct((n, D_MODEL), F32),
        scratch_shapes=[pltpu.VMEM((tm // SUB_TILE, SUB_TILE, LANES), F32),
                        pltpu.VMEM((tm // SUB_TILE, SUB_TILE, D_MODEL), BF16),
                        pltpu.VMEM((tm // SUB_TILE, SUB_TILE, SSM_WIDTH), F32),
                        pltpu.VMEM((tm // SUB_TILE, SUB_TILE, SSM_WIDTH), BF16),
                        pltpu.VMEM((tm // SUB_TILE, SUB_TILE, GROUP_WIDTH), BF16),
                        pltpu.VMEM((tm // SUB_TILE, SUB_TILE, D_MODEL), BF16)],
        compiler_params=_params("parallel"),
        name="merge",
    )(x2, norm_g.reshape(1, D_MODEL), ys_rows, glu_w.astype(BF16), glu_b.reshape(1, SSM_WIDTH),
      w_a.astype(BF16), o0, o1, o2, l0, l1, l2, w_b.astype(BF16),
      w_gates_bf, gate_b.reshape(1, 2 * D_MODEL), w_out.astype(BF16))


def _ffn_kernel(h_ref, g2_ref, wg_ref, wu_ref, cw_ref, cb_ref, wd_ref, g3_ref, wpg_ref,
                p_ref, wpp_ref, gf_ref, out_ref, act_ref, carry_ref, *, tiles_per_seq):
    tm = h_ref.shape[0]

    @pl.when(pl.program_id(0) % tiles_per_seq == 0)
    def _():
        carry_ref[...] = jnp.zeros_like(carry_ref)

    subs = [slice(s * SUB_TILE, (s + 1) * SUB_TILE) for s in range(tm // SUB_TILE)]
    hs = [h_ref[rows, :] for rows in subs]
    u2s = [_rms(h, g2_ref[...]).astype(BF16) for h in hs]
    row = lax.broadcasted_iota(jnp.int32, (SUBLANES, FFN_CHUNK), 0)
    for c in range(D_FF // FFN_CHUNK):
        sl = slice(c * FFN_CHUNK, (c + 1) * FFN_CHUNK)
        prev = carry_ref[:, sl]
        for rows, u2 in zip(subs, u2s):
            gp = _dot(u2, wg_ref[:, sl])
            up = _dot(u2, wu_ref[:, sl])
            r1 = pltpu.roll(gp, 1, 0)
            r2 = pltpu.roll(gp, 2, 0)
            r1 = jnp.concatenate([jnp.where(row < 1, pltpu.roll(prev, 1, 0), r1[:SUBLANES]), r1[SUBLANES:]], axis=0)
            r2 = jnp.concatenate([jnp.where(row < 2, pltpu.roll(prev, 2, 0), r2[:SUBLANES]), r2[SUBLANES:]], axis=0)
            gate = cw_ref[0:1, sl] * r2 + cw_ref[1:2, sl] * r1 + cw_ref[2:3, sl] * gp + cb_ref[:, sl]
            act_ref[rows, sl] = (jax.nn.gelu(gate) * up).astype(BF16)
            prev = gp[SUB_TILE - SUBLANES:, :]
        carry_ref[:, sl] = prev
    for rows, h in zip(subs, hs):
        h = h + _dot(act_ref[rows, :], wd_ref[...])
        u3 = _rms(h, g3_ref[...]).astype(BF16)
        h = h + jax.nn.sigmoid(_dot(u3, wpg_ref[...])) * _dot(p_ref[rows, :].astype(BF16), wpp_ref[...])
        out_ref[rows, :] = _rms(h, gf_ref[...])


def _ffn(h1, seq, p2, norm_g, w_gate, w_up, conv_w, conv_b, w_down, ple_g, ple_w_gate, ple_w_proj, final_g):
    n = h1.shape[0]
    tm = TOKEN_TILE
    row = lambda w: pl.BlockSpec((tm, w), lambda i: (i, 0))
    vec = lambda a: a.reshape(1, -1)
    return pl.pallas_call(
        functools.partial(_ffn_kernel, tiles_per_seq=seq // tm),
        grid=(n // tm,),
        in_specs=[row(D_MODEL), _const_spec((1, D_MODEL)), _const_spec((D_MODEL, D_FF)),
                  _const_spec((D_MODEL, D_FF)), _const_spec((CONV_WIDTH, D_FF)), _const_spec((1, D_FF)),
                  _const_spec((D_FF, D_MODEL)), _const_spec((1, D_MODEL)), _const_spec((D_MODEL, D_MODEL)),
                  row(PLE_DIM), _const_spec((PLE_DIM, D_MODEL)), _const_spec((1, D_MODEL))],
        out_specs=row(D_MODEL),
        out_shape=jax.ShapeDtypeStruct((n, D_MODEL), F32),
        scratch_shapes=[pltpu.VMEM((tm, D_FF), BF16), pltpu.VMEM((SUBLANES, D_FF), F32)],
        compiler_params=_params("arbitrary"),
        name="ffn",
    )(h1, vec(norm_g), w_gate.astype(BF16), w_up.astype(BF16), conv_w, vec(conv_b),
      w_down.astype(BF16), vec(ple_g), ple_w_gate.astype(BF16), p2, ple_w_proj.astype(BF16), vec(final_g))


def _layer(h2, bsz, seq, p2, mix_norm_g, w_in, gate_b, ssm_lam_re, ssm_lam_im, ssm_log_dt, ssm_b_re,
           ssm_b_im, ssm_c_re, ssm_c_im, ssm_d, ssm_glu_w, ssm_glu_b, w_branch_a, w_branch_b, w_out,
           ffn_norm_g, ffn_w_gate, ffn_w_up, ffn_conv_w, ffn_conv_b, ffn_w_down,
           ple_norm_g, ple_w_gate, ple_w_proj, out_norm_g):
    w_in_bf = w_in.astype(BF16)
    u_rows, qkv = _in_proj(h2, seq, mix_norm_g, w_in_bf)
    ops = _ssm_operators(ssm_lam_re, ssm_lam_im, ssm_log_dt, ssm_b_re, ssm_b_im, ssm_c_re, ssm_c_im)
    ys_rows = _ssm_scan(u_rows, bsz, ssm_d.reshape(-1), ops)
    attn_outs = [_attn_group(*group) for group in qkv]
    h1 = _merge(h2, seq, mix_norm_g, ys_rows, ssm_glu_w, ssm_glu_b, w_branch_a, attn_outs, w_branch_b,
                w_in_bf[:, OFF_GA:], gate_b, w_out)
    return _ffn(h1, seq, p2, ffn_norm_g, ffn_w_gate, ffn_w_up, ffn_conv_w, ffn_conv_b, ffn_w_down,
                ple_norm_g, ple_w_gate, ple_w_proj, out_norm_g)


def kernel(x, p, mix_norm_g, w_in, gate_b, ssm_lam_re, ssm_lam_im, ssm_log_dt, ssm_b_re, ssm_b_im, ssm_c_re, ssm_c_im, ssm_d, ssm_glu_w, ssm_glu_b, w_branch_a, w_branch_b, w_out, ffn_norm_g, ffn_w_gate, ffn_w_up, ffn_conv_w, ffn_conv_b, ffn_w_down, ple_norm_g, ple_w_gate, ple_w_proj, final_norm_g):
    bsz, seq, _ = x.shape
    depth = p.shape[0]
    assert depth == 1, "the final norm is fused into the layer's last kernel"
    h2 = x.reshape(bsz * seq, D_MODEL)
    out = _layer(h2, bsz, seq, p[0].reshape(bsz * seq, PLE_DIM), mix_norm_g[0], w_in[0], gate_b[0],
                 ssm_lam_re[0], ssm_lam_im[0], ssm_log_dt[0], ssm_b_re[0], ssm_b_im[0], ssm_c_re[0],
                 ssm_c_im[0], ssm_d[0], ssm_glu_w[0], ssm_glu_b[0], w_branch_a[0], w_branch_b[0],
                 w_out[0], ffn_norm_g[0], ffn_w_gate[0], ffn_w_up[0], ffn_conv_w[0], ffn_conv_b[0],
                 ffn_w_down[0], ple_norm_g[0], ple_w_gate[0], ple_w_proj[0], final_norm_g)
    return out.reshape(bsz, seq, D_MODEL)
```

```python
import functools

import jax
import jax.numpy as jnp
from jax import lax
from jax.experimental import pallas as pl
from jax.experimental.pallas import tpu as pltpu

F32 = jnp.float32
BF16 = jnp.bfloat16

D_MODEL = 1024
EPS = 1e-6
PLE_DIM = 256
SSM_GROUP = 16
SSM_STATE = 64
SSM_WIDTH = 512
SSM_GROUPS = SSM_WIDTH // SSM_GROUP
HEAD_DIM = 64
DILATIONS = (1, 4, 16)
WINDOW_KEYS = 128
HEADS_PER_GROUP = 4
GROUP_WIDTH = HEADS_PER_GROUP * HEAD_DIM
ATTN_WIDTH = len(DILATIONS) * GROUP_WIDTH
ROT_DIM = HEAD_DIM // 4
ROPE_THETA = 500000.0
NEG_BIG = -1e30
D_FF = 2816
CONV_WIDTH = 3
OFF_Q = SSM_WIDTH
OFF_K = OFF_Q + ATTN_WIDTH
OFF_V = OFF_K + ATTN_WIDTH
OFF_GA = OFF_V + ATTN_WIDTH
OFF_GB = OFF_GA + D_MODEL
IN_WIDTH = OFF_GB + D_MODEL

LANES = 128
SUBLANES = 8
VMEM_LIMIT_BYTES = 56 * 1024 * 1024

TOKEN_TILE = 1024
SUB_TILE = 256
COL_TILE = 256
SSM_CHUNK = 16
SSM_CHUNKS_PER_TILE = 256
SSM_TILE = SSM_CHUNK * SSM_CHUNKS_PER_TILE
SSM_CK = SSM_CHUNK * SSM_GROUP
SSM_ROW = SSM_CHUNK * SSM_WIDTH
SSM_PAIRS = SSM_GROUPS // 2
SSM_LOG_STEPS = 8
ATTN_QB = 128
ATTN_STEP_ROWS = 1024
FFN_CHUNK = 256


def _dot(a, b):
    return jnp.dot(a, b, preferred_element_type=F32)


def _rms(x, g):
    var = jnp.mean(x * x, axis=-1, keepdims=True)
    return x * lax.rsqrt(var + EPS) * g


def _const_spec(shape):
    nd = len(shape)
    return pl.BlockSpec(shape, lambda *_: (0,) * nd, pipeline_mode=pl.Buffered(1))


def _params(*sem):
    return pltpu.CompilerParams(dimension_semantics=sem, vmem_limit_bytes=VMEM_LIMIT_BYTES)


def _rope(z, cos, sin_lo, sin_hi):
    up = pltpu.roll(z, LANES - ROT_DIM // 2, 1)
    dn = pltpu.roll(z, ROT_DIM // 2, 1)
    return z * cos + up * sin_lo + dn * sin_hi


def _store_residue_major(stage_ref, out_ref, z, dil, tok0):
    rows, first = z.shape[0] // dil, tok0 // dil
    for j in range(z.shape[1] // LANES):
        sl = slice(j * LANES, (j + 1) * LANES)
        stage_ref[...] = z[:, sl]
        for r in range(dil):
            out_ref[r, first:first + rows, sl] = stage_ref[pl.ds(r, rows, stride=dil), :].astype(out_ref.dtype)


def _store_chunk_rows(stage_ref, out_ref, z, c0):
    rows, width = z.shape[0] // SSM_CHUNK, z.shape[1]
    for j in range(width // LANES):
        stage_ref[...] = z[:, j * LANES:(j + 1) * LANES]
        for t in range(SSM_CHUNK):
            col = t * width + j * LANES
            out_ref[c0:c0 + rows, col:col + LANES] = (
                stage_ref[pl.ds(t, rows, stride=SSM_CHUNK), :].astype(out_ref.dtype))


def _load_chunk_rows(stage_ref, in_ref, width, c0, rows):
    slabs = []
    for j in range(width // LANES):
        for t in range(SSM_CHUNK):
            col = t * width + j * LANES
            stage_ref[pl.ds(t, rows, stride=SSM_CHUNK), :] = in_ref[c0:c0 + rows, col:col + LANES].astype(F32)
        slabs.append(stage_ref[...])
    return jnp.concatenate(slabs, axis=1)


def _in_proj_kernel(x_ref, g_ref, ca_ref, sa_ref, cb_ref, sb_ref, sign_ref, w_ref, ba_ref, bb_ref,
                    s_ref, q0_ref, k0_ref, v0_ref, q1_ref, k1_ref, v1_ref, q2_ref, k2_ref, v2_ref,
                    ga_ref, gb_ref, stage_ref):
    u = _rms(x_ref[...], g_ref[...]).astype(BF16)
    _store_chunk_rows(stage_ref, s_ref, _dot(u, w_ref[:, :OFF_Q]), 0)
    ca, sa, cb, sb = ca_ref[...], sa_ref[...], cb_ref[...], sb_ref[...]
    cos = ca * cb - sa * sb
    sin = sa * cb + ca * sb
    slo, shi = sin * sign_ref[0:1, :], sin * sign_ref[1:2, :]
    scale = HEAD_DIM ** -0.5

    def rope(z):
        return jnp.concatenate([_rope(z[:, j * LANES:(j + 1) * LANES], cos, slo, shi)
                                for j in range(z.shape[1] // LANES)], axis=1)

    q = rope(_dot(u, w_ref[:, OFF_Q:OFF_K])) * scale
    k = rope(_dot(u, w_ref[:, OFF_K:OFF_V]))
    v = _dot(u, w_ref[:, OFF_V:OFF_GA])
    for z, refs in ((q, (q0_ref, q1_ref, q2_ref)), (k, (k0_ref, k1_ref, k2_ref)), (v, (v0_ref, v1_ref, v2_ref))):
        refs[0][...] = z[:, :GROUP_WIDTH].astype(BF16)
        for grp in (1, 2):
            _store_residue_major(stage_ref, refs[grp], z[:, grp * GROUP_WIDTH:(grp + 1) * GROUP_WIDTH],
                                 DILATIONS[grp], 0)
    ga_ref[...] = jax.nn.sigmoid(_dot(u, w_ref[:, OFF_GA:OFF_GB]) + ba_ref[...]).astype(BF16)
    gb_ref[...] = jax.nn.sigmoid(_dot(u, w_ref[:, OFF_GB:]) + bb_ref[...]).astype(BF16)


def _rope_tables(seq, tm):
    half = ROT_DIM // 2
    freqs = ROPE_THETA ** (-jnp.arange(half, dtype=F32) * (2.0 / ROT_DIM))
    head = jnp.concatenate([freqs, freqs, jnp.zeros((HEAD_DIM - ROT_DIM,), F32)])
    lane_freq = jnp.tile(head, LANES // HEAD_DIM)[None, :]
    base = jnp.arange(0, seq, tm, dtype=F32)[:, None] * lane_freq
    offs = jnp.arange(tm, dtype=F32)[:, None] * lane_freq
    in_head = jnp.arange(LANES) % HEAD_DIM
    sign = jnp.zeros((SUBLANES, LANES), F32)
    sign = sign.at[0].set(jnp.where(in_head < half, -1.0, 0.0))
    sign = sign.at[1].set(jnp.where((in_head >= half) & (in_head < ROT_DIM), 1.0, 0.0))
    n_tiles = seq // tm
    return (jnp.cos(base).reshape(n_tiles, 1, LANES), jnp.sin(base).reshape(n_tiles, 1, LANES),
            jnp.cos(offs), jnp.sin(offs), sign)


def _in_proj(x2, seq, norm_g, w_in, gate_b):
    n = x2.shape[0]
    tm = TOKEN_TILE
    tiles_per_seq = seq // tm
    bsz = n // seq
    ba = gate_b[:D_MODEL].reshape(1, D_MODEL)
    bb = gate_b[D_MODEL:].reshape(1, D_MODEL)
    cos_a, sin_a, cos_b, sin_b, sign = _rope_tables(seq, tm)

    row = lambda w: pl.BlockSpec((tm, w), lambda i: (i, 0))
    tile_tab = pl.BlockSpec((None, 1, LANES), lambda i: (i % tiles_per_seq, 0, 0))
    res = lambda d: pl.BlockSpec((None, d, tm // d, GROUP_WIDTH),
                                 lambda i: (i // tiles_per_seq, 0, i % tiles_per_seq, 0))
    res_shape = lambda d: jax.ShapeDtypeStruct((bsz, d, seq // d, GROUP_WIDTH), BF16)
    nat_shape = jax.ShapeDtypeStruct((n, GROUP_WIDTH), BF16)
    d1, d2 = DILATIONS[1], DILATIONS[2]
    outs = pl.pallas_call(
        _in_proj_kernel,
        grid=(n // tm,),
        in_specs=[row(D_MODEL), _const_spec((1, D_MODEL)), tile_tab, tile_tab,
                  _const_spec((tm, LANES)), _const_spec((tm, LANES)), _const_spec((SUBLANES, LANES)),
                  _const_spec((D_MODEL, IN_WIDTH)), _const_spec((1, D_MODEL)), _const_spec((1, D_MODEL))],
        out_specs=[pl.BlockSpec((tm // SSM_CHUNK, SSM_ROW), lambda i: (i, 0))]
                  + [row(GROUP_WIDTH)] * 3 + [res(d1)] * 3 + [res(d2)] * 3 + [row(D_MODEL), row(D_MODEL)],
        out_shape=[jax.ShapeDtypeStruct((n // SSM_CHUNK, SSM_ROW), BF16)] + [nat_shape] * 3
                  + [res_shape(d1)] * 3 + [res_shape(d2)] * 3
                  + [jax.ShapeDtypeStruct((n, D_MODEL), BF16), jax.ShapeDtypeStruct((n, D_MODEL), BF16)],
        scratch_shapes=[pltpu.VMEM((tm, LANES), F32)],
        compiler_params=_params("parallel"),
        name="in_proj",
    )(x2, norm_g.reshape(1, D_MODEL), cos_a, sin_a, cos_b, sin_b, sign, w_in.astype(BF16), ba, bb)
    u, q0, k0, v0, q1, k1, v1, q2, k2, v2, ga, gb = outs
    nat4 = lambda a: a.reshape(bsz, 1, seq, GROUP_WIDTH)
    qkv = ((nat4(q0), nat4(k0), nat4(v0)), (q1, k1, v1), (q2, k2, v2))
    return u, qkv, ga, gb


def _cmul(ar, ai, br, bi):
    return ar * br - ai * bi, ar * bi + ai * br


def _discretise(lr, li, dt):
    mag = jnp.exp(lr * dt)
    ar = mag * jnp.cos(li * dt)
    ai = mag * jnp.sin(li * dt)
    den = lr * lr + li * li
    cr = ((ar - 1.0) * lr + ai * li) / den
    ci = (ai * lr - (ar - 1.0) * li) / den
    return ar, ai, cr, ci


def _ssm_prep_kernel(lr_ref, li_ref, ldt_ref, brt_ref, bit_ref, cre_ref, cim_ref, lrf_ref, lif_ref, ldtf_ref,
                     w2_ref, e2_ref, apsr_ref, apsi_ref, aptr_ref, apti_ref, er_ref, ei_ref):
    g_n, t_n, h_n, sw = SSM_GROUPS, SSM_CHUNK, SSM_GROUP, 2 * SSM_STATE
    ar, ai, cr, ci = _discretise(lr_ref[...], li_ref[...], jnp.exp(ldt_ref[...]))
    brt, bit = brt_ref[...], bit_ref[...]
    bbr = cr * brt - ci * bit
    bbi = cr * bit + ci * brt
    cre, cim = cre_ref[...], cim_ref[...]
    by_group = lambda a: a.reshape(g_n, h_n, sw)
    pr, pi = jnp.ones_like(ar), jnp.zeros_like(ai)
    for j in range(t_n):
        rows = slice((t_n - 1 - j) * h_n, (t_n - j) * h_n)
        rr, ri = _cmul(pr, pi, bbr, bbi)
        er_ref[:, rows, :] = by_group(rr)
        ei_ref[:, rows, :] = by_group(ri)
        e2_ref[:, rows, :sw] = by_group(rr).astype(BF16)
        e2_ref[:, rows, sw:] = by_group(ri).astype(BF16)
        pr, pi = _cmul(pr, pi, ar, ai)
        rows = slice(j * h_n, (j + 1) * h_n)
        w2_ref[:, rows, SSM_CK:SSM_CK + sw] = by_group(cre * pr - cim * pi).astype(BF16)
        w2_ref[:, rows, SSM_CK + sw:] = by_group(-cre * pi - cim * pr).astype(BF16)

    nt = (((1,), (1,)), ((), ()))
    hi = lax.Precision.HIGHEST

    def toeplitz(g, _):
        rows = pl.ds(pl.multiple_of(g * h_n, h_n), h_n)
        krev = (lax.dot_general(cre_ref[rows, :], er_ref[g], nt, precision=hi, preferred_element_type=F32)
                - lax.dot_general(cim_ref[rows, :], ei_ref[g], nt, precision=hi, preferred_element_type=F32))
        kext = jnp.concatenate([krev, jnp.zeros_like(krev)], axis=1)
        for t in range(t_n):
            off = (t_n - 1 - t) * h_n
            win = kext if off == 0 else pltpu.roll(kext, 2 * SSM_CK - off, 1)
            w2_ref[g, t * h_n:(t + 1) * h_n, :SSM_CK] = win[:, :SSM_CK].astype(BF16)
        return 0
    lax.fori_loop(0, g_n, toeplitz, 0, unroll=4)

    acr, aci, _, _ = _discretise(lrf_ref[...], lif_ref[...], jnp.exp(ldtf_ref[...]))
    for _ in range(4):
        acr, aci = _cmul(acr, aci, acr, aci)
    shape = (SUBLANES, acr.shape[1])
    row = lax.broadcasted_iota(jnp.int32, shape, 0)
    qr, qi = jnp.ones(shape, F32), jnp.zeros(shape, F32)
    apsr_ref[...] = jnp.zeros_like(apsr_ref)
    apsi_ref[...] = jnp.zeros_like(apsi_ref)
    for k in range(SSM_LOG_STEPS + 1):
        for p2 in range(SSM_PAIRS):
            apsr_ref[p2, k:k + 1, :] = acr[:, p2 * sw:(p2 + 1) * sw]
            apsi_ref[p2, k:k + 1, :] = aci[:, p2 * sw:(p2 + 1) * sw]
        if (1 << k) < SUBLANES:
            nr, ni = _cmul(qr, qi, acr, aci)
            bit_set = (row & (1 << k)) != 0
            qr, qi = jnp.where(bit_set, nr, qr), jnp.where(bit_set, ni, qi)
        if (1 << k) == SUBLANES:
            a8r, a8i = acr, aci
        acr, aci = _cmul(acr, aci, acr, aci)
    br, bi = jnp.ones_like(a8r), jnp.zeros_like(a8i)
    for b in range(SSM_CHUNKS_PER_TILE // SUBLANES):
        rows = slice(b * SUBLANES, (b + 1) * SUBLANES)
        blk_r, blk_i = _cmul(qr, qi, br, bi)
        for p2 in range(SSM_PAIRS):
            aptr_ref[p2, rows, :] = blk_r[:, p2 * sw:(p2 + 1) * sw]
            apti_ref[p2, rows, :] = blk_i[:, p2 * sw:(p2 + 1) * sw]
        br, bi = _cmul(br, bi, a8r, a8i)


def _pad_pair_lanes(a):
    z = jnp.zeros_like(a)
    even = (jnp.arange(a.shape[0]) % 2 == 0)[:, None, None]
    padded = jnp.where(even, jnp.concatenate([a, z], -1), jnp.concatenate([z, a], -1))
    return padded.reshape(a.shape[0] * a.shape[1], 2 * a.shape[2])


def _ssm_operators(lam_re, lam_im, log_dt, b_re, b_im, c_re, c_im):
    g, p, h = SSM_GROUPS, SSM_STATE, SSM_GROUP
    gp, sw = g * p, 2 * p
    rep = lambda a: jnp.repeat(jnp.tile(a, (1, 2)), h, axis=0)
    ldt2 = jnp.broadcast_to(log_dt[:, None], (g, p))
    full = lambda shape: pl.BlockSpec(shape, lambda: (0,) * len(shape))
    in_arrays = (rep(lam_re), rep(lam_im), rep(ldt2),
                 _pad_pair_lanes(b_re.transpose(0, 2, 1)), _pad_pair_lanes(b_im.transpose(0, 2, 1)),
                 _pad_pair_lanes(c_re), _pad_pair_lanes(c_im),
                 lam_re.reshape(1, gp), lam_im.reshape(1, gp), ldt2.reshape(1, gp))
    out_shapes = ([((g, SSM_CK, SSM_CK + 2 * sw), BF16), ((g, SSM_CK, 2 * sw), BF16)]
                  + [((SSM_PAIRS, 2 * SUBLANES, sw), F32)] * 2
                  + [((SSM_PAIRS, SSM_CHUNKS_PER_TILE, sw), F32)] * 2)
    w2, e2, apsr, apsi, aptr, apti = pl.pallas_call(
        _ssm_prep_kernel,
        in_specs=[full(a.shape) for a in in_arrays],
        out_specs=[full(s) for s, _ in out_shapes],
        out_shape=[jax.ShapeDtypeStruct(s, dt) for s, dt in out_shapes],
        scratch_shapes=[pltpu.VMEM((g, SSM_CK, sw), F32), pltpu.VMEM((g, SSM_CK, sw), F32)],
        compiler_params=pltpu.CompilerParams(vmem_limit_bytes=VMEM_LIMIT_BYTES),
        name="ssm_prep",
    )(*in_arrays)
    return w2, e2.reshape(SSM_PAIRS, 2 * SSM_CK, 2 * sw), apsr, apsi, aptr, apti


def _shift_rows(z, s, row):
    if s % SUBLANES == 0:
        return jnp.concatenate([jnp.zeros((s, z.shape[1]), z.dtype), z[:-s]], axis=0)
    return jnp.where(row >= s, pltpu.roll(z, s, 0), 0.0)


def _ssm_scan_kernel(u_ref, d_ref, w2_ref, e2_ref, apsr_ref, apsi_ref, aptr_ref, apti_ref,
                     y_ref, xs_ref, sc_ref, yt_ref, carry_ref, loc_ref, sin_ref):
    t_n, h_n, c_n = SSM_CHUNK, SSM_GROUP, SSM_CHUNKS_PER_TILE
    n_slab = SSM_WIDTH // LANES
    pairs_per_slab = LANES // (2 * h_n)
    sw = 2 * SSM_STATE
    blk_n = SUBLANES
    n_blk = c_n // blk_n
    log_blk = blk_n.bit_length() - 1

    @pl.when(pl.program_id(1) == 0)
    def _():
        carry_ref[...] = jnp.zeros_like(carry_ref)

    for t in range(t_n):
        for j in range(n_slab):
            col = t * SSM_WIDTH + j * LANES
            blk = u_ref[:, col:col + LANES].T
            xs_ref[j * pairs_per_slab:(j + 1) * pairs_per_slab, :, t * h_n:(t + 1) * h_n, :] = (
                blk.reshape(pairs_per_slab, 2, h_n, c_n))

    row = lax.broadcasted_iota(jnp.int32, (n_blk, sw), 0)
    nt = (((1,), (1,)), ((), ()))
    tn = (((0,), (0,)), ((), ()))

    def local_states(pr, _):
        xp = xs_ref[pr].reshape(2 * SSM_CK, c_n)
        loc = lax.dot_general(xp, e2_ref[pr], tn, preferred_element_type=F32)
        loc_ref[pr, 0] = loc[:, :sw]
        loc_ref[pr, 1] = loc[:, sw:]
        return 0
    lax.fori_loop(0, SSM_PAIRS, local_states, 0, unroll=4)

    def chunk_scan(pr, _):
        slot = pr % 2
        power = lambda k: (apsr_ref[pr, k:k + 1, :], apsi_ref[pr, k:k + 1, :])
        zr, zi = [], []
        for lo in range(blk_n):
            rows = pl.ds(lo, n_blk, stride=blk_n)
            xr, xi = loc_ref[pr, 0, rows, :], loc_ref[pr, 1, rows, :]
            if lo:
                dr, di = _cmul(zr[-1], zi[-1], *power(0))
                xr, xi = xr + dr, xi + di
            zr.append(xr)
            zi.append(xi)
        er, ei = zr[-1], zi[-1]
        s = 1
        while s < n_blk:
            dr, di = _cmul(_shift_rows(er, s, row), _shift_rows(ei, s, row),
                           *power(log_blk + s.bit_length() - 1))
            er, ei = er + dr, ei + di
            s *= 2
        cr, ci = carry_ref[pr, 0:1, :], carry_ref[pr, 1:2, :]
        blk_rows = pl.ds(0, n_blk, stride=blk_n)
        hr, hi = _cmul(aptr_ref[pr, blk_rows, :], apti_ref[pr, blk_rows, :], cr, ci)
        br, bi = _shift_rows(er, 1, row) + hr, _shift_rows(ei, 1, row) + hi
        for lo in range(blk_n):
            sr, si = _cmul(br, bi, aptr_ref[pr, lo:lo + 1, :], apti_ref[pr, lo:lo + 1, :])
            if lo:
                sr, si = sr + zr[lo - 1], si + zi[lo - 1]
            rows = pl.ds(lo, n_blk, stride=blk_n)
            sin_ref[slot, 0, rows, :] = sr
            sin_ref[slot, 1, rows, :] = si
        sc_ref[pr, :, :sw] = sin_ref[slot, 0].astype(BF16)
        sc_ref[pr, :, sw:] = sin_ref[slot, 1].astype(BF16)
        nr, ni = _cmul(cr, ci, *power(SSM_LOG_STEPS))
        carry_ref[pr, 0:1, :] = er[n_blk - 1:n_blk, :] + nr
        carry_ref[pr, 1:2, :] = ei[n_blk - 1:n_blk, :] + ni
        return 0
    lax.fori_loop(0, SSM_PAIRS, chunk_scan, 0, unroll=2)

    def outputs(g, _):
        pr = g // 2
        yg = (_dot(w2_ref[g, :, :SSM_CK], xs_ref[pr, g % 2])
              + lax.dot_general(w2_ref[g, :, SSM_CK:], sc_ref[pr], nt, preferred_element_type=F32))
        yt_ref[:, pl.ds(pl.multiple_of(g * h_n, h_n), h_n), :] = yg.reshape(t_n, h_n, c_n)
        return 0
    lax.fori_loop(0, SSM_GROUPS, outputs, 0, unroll=4)

    for t in range(t_n):
        for j in range(n_slab):
            sl = slice(j * LANES, (j + 1) * LANES)
            col = t * SSM_WIDTH + j * LANES
            y_ref[:, col:col + LANES] = (
                yt_ref[t, sl, :].T + d_ref[:, sl] * u_ref[:, col:col + LANES].astype(F32)).astype(BF16)


def _ssm_scan(u_rows, bsz, d_skip, ops):
    w2, e2, apsr, apsi, aptr, apti = ops
    g, p, c_n = SSM_GROUPS, SSM_STATE, SSM_CHUNKS_PER_TILE
    tiles = u_rows.shape[0] // (bsz * c_n)
    tile = pl.BlockSpec((c_n, SSM_ROW), lambda b, i: (b * tiles + i, 0))
    return pl.pallas_call(
        _ssm_scan_kernel,
        grid=(bsz, tiles),
        in_specs=[tile, _const_spec((1, SSM_WIDTH)),
                  _const_spec(w2.shape), _const_spec(e2.shape),
                  _const_spec(apsr.shape), _const_spec(apsi.shape),
                  _const_spec(aptr.shape), _const_spec(apti.shape)],
        out_specs=tile,
        out_shape=jax.ShapeDtypeStruct(u_rows.shape, BF16),
        scratch_shapes=[pltpu.VMEM((SSM_PAIRS, 2, SSM_CK, c_n), BF16),
                        pltpu.VMEM((SSM_PAIRS, c_n, 4 * p), BF16),
                        pltpu.VMEM((SSM_CHUNK, SSM_WIDTH, c_n), F32),
                        pltpu.VMEM((SSM_PAIRS, SUBLANES, 2 * p), F32),
                        pltpu.VMEM((SSM_PAIRS, 2, c_n, 2 * p), F32),
                        pltpu.VMEM((2, 2, c_n, 2 * p), F32)],
        compiler_params=_params("parallel", "arbitrary"),
        name="ssm_scan",
    )(u_rows, d_skip.reshape(1, SSM_WIDTH), w2, e2, apsr, apsi, aptr, apti)


def _attn_kernel(q_ref, k_ref, v_ref, o_ref, lse_ref):
    step = pl.program_id(2)
    qb, nk = ATTN_QB, ATTN_QB + WINDOW_KEYS
    row = lax.broadcasted_iota(jnp.int32, (qb, nk), 0)
    col = lax.broadcasted_iota(jnp.int32, (qb, nk), 1)
    lane = lax.broadcasted_iota(jnp.int32, (qb, LANES), 1)
    first_head = lane < HEAD_DIM
    nt = (((1,), (1,)), ((), ()))
    for sb in range(o_ref.shape[0] // qb):
        rows = slice(sb * qb, (sb + 1) * qb)
        q_start = step * o_ref.shape[0] + sb * qb
        back = jnp.minimum(q_start, WINDOW_KEYS)
        k_start = pl.multiple_of(q_start - back, ATTN_QB)
        dist = row + back - col
        valid = (dist >= 0) & (dist <= WINDOW_KEYS)
        for pair in range(GROUP_WIDTH // LANES):
            cols = slice(pair * LANES, (pair + 1) * LANES)
            qp = q_ref[rows, cols]
            kp = k_ref[pl.ds(k_start, nk), cols]
            vp = v_ref[pl.ds(k_start, nk), cols]
            outs, lses = [], []
            for sel in (first_head, ~first_head):
                qm = jnp.where(sel, qp, jnp.zeros_like(qp))
                s = lax.dot_general(qm, kp, nt, preferred_element_type=F32)
                s = jnp.where(valid, s, NEG_BIG)
                m = jnp.max(s, axis=-1, keepdims=True)
                e = jnp.exp(s - m)
                den = jnp.sum(e, axis=-1, keepdims=True)
                outs.append(_dot(e.astype(BF16), vp) / den)
                lses.append(m + jnp.log(den))
            o_ref[rows, cols] = jnp.where(first_head, outs[0], outs[1]).astype(BF16)
            lse_ref[rows, cols] = jnp.where(first_head, lses[0], lses[1])


def _attn_group(q4, k4, v4):
    bsz, dil, lr, _ = q4.shape
    rows = min(ATTN_STEP_ROWS, lr)
    q_spec = pl.BlockSpec((None, None, rows, GROUP_WIDTH), lambda b, r, i: (b, r, i, 0))
    kv_spec = pl.BlockSpec((None, None, lr, GROUP_WIDTH), lambda b, r, i: (b, r, 0, 0))
    return pl.pallas_call(
        _attn_kernel,
        grid=(bsz, dil, lr // rows),
        in_specs=[q_spec, kv_spec, kv_spec],
        out_specs=[q_spec, q_spec],
        out_shape=[jax.ShapeDtypeStruct(q4.shape, BF16), jax.ShapeDtypeStruct(q4.shape, F32)],
        compiler_params=_params("parallel", "parallel", "arbitrary"),
        name=f"attn_d{dil}",
    )(q4, k4, v4)


def _load_token_major(stage_ref, in_ref, tok0, ntok, lanes):
    dil = in_ref.shape[0]
    first, rows = tok0 // dil, ntok // dil
    if dil == 1:
        return in_ref[0, first:first + rows, lanes].astype(F32)
    for r in range(dil):
        stage_ref[pl.ds(r, rows, stride=dil), :] = in_ref[r, first:first + rows, lanes].astype(F32)
    return stage_ref[...]


def _merge_kernel(x_ref, ys_ref, gluw_ref, glub_ref, wa_ref,
                  o0_ref, o1_ref, o2_ref, l0_ref, l1_ref, l2_ref, wb_ref,
                  ga_ref, gb_ref, wout_ref, h_ref, stage_ref, y_ref, ya_ref, attn_ref, mix_ref):
    pieces = lambda width: [slice(c * COL_TILE, (c + 1) * COL_TILE) for c in range(width // COL_TILE)]
    for s in range(x_ref.shape[0] // SUB_TILE):
        tok0 = s * SUB_TILE
        rows = slice(tok0, tok0 + SUB_TILE)
        stage = stage_ref.at[s]
        y_ref[s] = jax.nn.gelu(_load_chunk_rows(stage, ys_ref, SSM_WIDTH, tok0 // SSM_CHUNK, SUB_TILE // SSM_CHUNK))
        y_bf = y_ref[s].astype(BF16)
        for cols in pieces(SSM_WIDTH):
            gate = jax.nn.sigmoid(_dot(y_bf, gluw_ref[:, cols]) + glub_ref[:, cols])
            ya_ref[s, :, cols] = (y_ref[s, :, cols] * gate).astype(BF16)

        for j in range(GROUP_WIDTH // LANES):
            lanes = slice(j * LANES, (j + 1) * LANES)
            o, l = ([_load_token_major(stage, ref, tok0, SUB_TILE, lanes) for ref in refs]
                    for refs in ((o0_ref, o1_ref, o2_ref), (l0_ref, l1_ref, l2_ref)))
            top = jnp.maximum(jnp.maximum(l[0], l[1]), l[2])
            w = [jnp.exp(lg - top) for lg in l]
            attn_ref[s, :, lanes] = ((w[0] * o[0] + w[1] * o[1] + w[2] * o[2])
                                     / (w[0] + w[1] + w[2])).astype(BF16)

        for cols in pieces(D_MODEL):
            mix = (ga_ref[rows, cols].astype(F32) * _dot(ya_ref[s], wa_ref[:, cols])
                   + gb_ref[rows, cols].astype(F32) * _dot(attn_ref[s], wb_ref[:, cols]))
            mix_ref[s, :, cols] = mix.astype(BF16)
        for cols in pieces(D_MODEL):
            h_ref[rows, cols] = x_ref[rows, cols] + _dot(mix_ref[s], wout_ref[:, cols])


def _merge(x2, seq, ys_rows, glu_w, glu_b, w_a, attn_outs, w_b, ga, gb, w_out):
    n = x2.shape[0]
    tm = TOKEN_TILE
    tiles_per_seq = seq // tm
    row = lambda w: pl.BlockSpec((tm, w), lambda i: (i, 0))
    res = lambda d: pl.BlockSpec((None, d, tm // d, GROUP_WIDTH),
                                 lambda i: (i // tiles_per_seq, 0, i % tiles_per_seq, 0))
    (o0, l0), (o1, l1), (o2, l2) = attn_outs
    d0, d1, d2 = DILATIONS
    return pl.pallas_call(
        _merge_kernel,
        grid=(n // tm,),
        in_specs=[row(D_MODEL), pl.BlockSpec((tm // SSM_CHUNK, SSM_ROW), lambda i: (i, 0)),
                  _const_spec((SSM_WIDTH, SSM_WIDTH)), _const_spec((1, SSM_WIDTH)),
                  _const_spec((SSM_WIDTH, D_MODEL)),
                  res(d0), res(d1), res(d2), res(d0), res(d1), res(d2),
                  _const_spec((GROUP_WIDTH, D_MODEL)), row(D_MODEL), row(D_MODEL),
                  _const_spec((D_MODEL, D_MODEL))],
        out_specs=row(D_MODEL),
        out_shape=jax.ShapeDtypeStruct((n, D_MODEL), F32),
        scratch_shapes=[pltpu.VMEM((tm // SUB_TILE, SUB_TILE, LANES), F32),
                        pltpu.VMEM((tm // SUB_TILE, SUB_TILE, SSM_WIDTH), F32),
                        pltpu.VMEM((tm // SUB_TILE, SUB_TILE, SSM_WIDTH), BF16),
                        pltpu.VMEM((tm // SUB_TILE, SUB_TILE, GROUP_WIDTH), BF16),
                        pltpu.VMEM((tm // SUB_TILE, SUB_TILE, D_MODEL), BF16)],
        compiler_params=_params("parallel"),
        name="merge",
    )(x2, ys_rows, glu_w.astype(BF16), glu_b.reshape(1, SSM_WIDTH),
      w_a.astype(BF16), o0, o1, o2, l0, l1, l2, w_b.astype(BF16), ga, gb, w_out.astype(BF16))


def _ffn_kernel(h_ref, g2_ref, wg_ref, wu_ref, cw_ref, cb_ref, wd_ref, g3_ref, wpg_ref,
                p_ref, wpp_ref, gf_ref, out_ref, act_ref, carry_ref, *, tiles_per_seq):
    tm = h_ref.shape[0]

    @pl.when(pl.program_id(0) % tiles_per_seq == 0)
    def _():
        carry_ref[...] = jnp.zeros_like(carry_ref)

    subs = [slice(s * SUB_TILE, (s + 1) * SUB_TILE) for s in range(tm // SUB_TILE)]
    hs = [h_ref[rows, :] for rows in subs]
    u2s = [_rms(h, g2_ref[...]).astype(BF16) for h in hs]
    row = lax.broadcasted_iota(jnp.int32, (SUBLANES, FFN_CHUNK), 0)
    for c in range(D_FF // FFN_CHUNK):
        sl = slice(c * FFN_CHUNK, (c + 1) * FFN_CHUNK)
        prev = carry_ref[:, sl]
        for rows, u2 in zip(subs, u2s):
            gp = _dot(u2, wg_ref[:, sl])
            up = _dot(u2, wu_ref[:, sl])
            r1 = pltpu.roll(gp, 1, 0)
            r2 = pltpu.roll(gp, 2, 0)
            r1 = jnp.concatenate([jnp.where(row < 1, pltpu.roll(prev, 1, 0), r1[:SUBLANES]), r1[SUBLANES:]], axis=0)
            r2 = jnp.concatenate([jnp.where(row < 2, pltpu.roll(prev, 2, 0), r2[:SUBLANES]), r2[SUBLANES:]], axis=0)
            gate = cw_ref[0:1, sl] * r2 + cw_ref[1:2, sl] * r1 + cw_ref[2:3, sl] * gp + cb_ref[:, sl]
            act_ref[rows, sl] = (jax.nn.gelu(gate) * up).astype(BF16)
            prev = gp[SUB_TILE - SUBLANES:, :]
        carry_ref[:, sl] = prev
    for rows, h in zip(subs, hs):
        h = h + _dot(act_ref[rows, :], wd_ref[...])
        u3 = _rms(h, g3_ref[...]).astype(BF16)
        h = h + jax.nn.sigmoid(_dot(u3, wpg_ref[...])) * _dot(p_ref[rows, :].astype(BF16), wpp_ref[...])
        out_ref[rows, :] = _rms(h, gf_ref[...])


def _ffn(h1, seq, p2, norm_g, w_gate, w_up, conv_w, conv_b, w_down, ple_g, ple_w_gate, ple_w_proj, final_g):
    n = h1.shape[0]
    tm = TOKEN_TILE
    row = lambda w: pl.BlockSpec((tm, w), lambda i: (i, 0))
    vec = lambda a: a.reshape(1, -1)
    return pl.pallas_call(
        functools.partial(_ffn_kernel, tiles_per_seq=seq // tm),
        grid=(n // tm,),
        in_specs=[row(D_MODEL), _const_spec((1, D_MODEL)), _const_spec((D_MODEL, D_FF)),
                  _const_spec((D_MODEL, D_FF)), _const_spec((CONV_WIDTH, D_FF)), _const_spec((1, D_FF)),
                  _const_spec((D_FF, D_MODEL)), _const_spec((1, D_MODEL)), _const_spec((D_MODEL, D_MODEL)),
                  row(PLE_DIM), _const_spec((PLE_DIM, D_MODEL)), _const_spec((1, D_MODEL))],
        out_specs=row(D_MODEL),
        out_shape=jax.ShapeDtypeStruct((n, D_MODEL), F32),
        scratch_shapes=[pltpu.VMEM((tm, D_FF), BF16), pltpu.VMEM((SUBLANES, D_FF), F32)],
        compiler_params=_params("arbitrary"),
        name="ffn",
    )(h1, vec(norm_g), w_gate.astype(BF16), w_up.astype(BF16), conv_w, vec(conv_b),
      w_down.astype(BF16), vec(ple_g), ple_w_gate.astype(BF16), p2, ple_w_proj.astype(BF16), vec(final_g))


def _layer(h2, bsz, seq, p2, mix_norm_g, w_in, gate_b, ssm_lam_re, ssm_lam_im, ssm_log_dt, ssm_b_re,
           ssm_b_im, ssm_c_re, ssm_c_im, ssm_d, ssm_glu_w, ssm_glu_b, w_branch_a, w_branch_b, w_out,
           ffn_norm_g, ffn_w_gate, ffn_w_up, ffn_conv_w, ffn_conv_b, ffn_w_down,
           ple_norm_g, ple_w_gate, ple_w_proj, out_norm_g):
    u_rows, qkv, ga, gb = _in_proj(h2, seq, mix_norm_g, w_in, gate_b)
    ops = _ssm_operators(ssm_lam_re, ssm_lam_im, ssm_log_dt, ssm_b_re, ssm_b_im, ssm_c_re, ssm_c_im)
    ys_rows = _ssm_scan(u_rows, bsz, ssm_d.reshape(-1), ops)
    attn_outs = [_attn_group(*group) for group in qkv]
    h1 = _merge(h2, seq, ys_rows, ssm_glu_w, ssm_glu_b, w_branch_a, attn_outs, w_branch_b, ga, gb, w_out)
    return _ffn(h1, seq, p2, ffn_norm_g, ffn_w_gate, ffn_w_up, ffn_conv_w, ffn_conv_b, ffn_w_down,
                ple_norm_g, ple_w_gate, ple_w_proj, out_norm_g)


def kernel(x, p, mix_norm_g, w_in, gate_b, ssm_lam_re, ssm_lam_im, ssm_log_dt, ssm_b_re, ssm_b_im, ssm_c_re, ssm_c_im, ssm_d, ssm_glu_w, ssm_glu_b, w_branch_a, w_branch_b, w_out, ffn_norm_g, ffn_w_gate, ffn_w_up, ffn_conv_w, ffn_conv_b, ffn_w_down, ple_norm_g, ple_w_gate, ple_w_proj, final_norm_g):
    bsz, seq, _ = x.shape
    depth = p.shape[0]
    assert depth == 1, "the final norm is fused into the layer's last kernel"
    h2 = x.reshape(bsz * seq, D_MODEL)
    out = _layer(h2, bsz, seq, p[0].reshape(bsz * seq, PLE_DIM), mix_norm_g[0], w_in[0], gate_b[0],
                 ssm_lam_re[0], ssm_lam_im[0], ssm_log_dt[0], ssm_b_re[0], ssm_b_im[0], ssm_c_re[0],
                 ssm_c_im[0], ssm_d[0], ssm_glu_w[0], ssm_glu_b[0], w_branch_a[0], w_branch_b[0],
                 w_out[0], ffn_norm_g[0], ffn_w_gate[0], ffn_w_up[0], ffn_conv_w[0], ffn_conv_b[0],
                 ffn_w_down[0], ple_norm_g[0], ple_w_gate[0], ple_w_proj[0], final_norm_g)
    return out.reshape(bsz, seq, D_MODEL)
```

```python
import functools

import jax
import jax.numpy as jnp
from jax import lax
from jax.experimental import pallas as pl
from jax.experimental.pallas import tpu as pltpu

F32 = jnp.float32
BF16 = jnp.bfloat16

D_MODEL = 1024
EPS = 1e-6
PLE_DIM = 256
SSM_GROUP = 16
SSM_STATE = 64
SSM_WIDTH = 512
SSM_GROUPS = SSM_WIDTH // SSM_GROUP
HEAD_DIM = 64
DILATIONS = (1, 4, 16)
WINDOW_KEYS = 128
HEADS_PER_GROUP = 4
GROUP_WIDTH = HEADS_PER_GROUP * HEAD_DIM
ATTN_WIDTH = len(DILATIONS) * GROUP_WIDTH
ROT_DIM = HEAD_DIM // 4
ROPE_THETA = 500000.0
NEG_BIG = -1e30
D_FF = 2816
CONV_WIDTH = 3
OFF_Q = SSM_WIDTH
OFF_K = OFF_Q + ATTN_WIDTH
OFF_V = OFF_K + ATTN_WIDTH
OFF_GA = OFF_V + ATTN_WIDTH
OFF_GB = OFF_GA + D_MODEL
IN_WIDTH = OFF_GB + D_MODEL

LANES = 128
SUBLANES = 8
VMEM_LIMIT_BYTES = 56 * 1024 * 1024

TOKEN_TILE = 1024
SUB_TILE = 256
COL_TILE = 256
SSM_CHUNK = 16
SSM_CHUNKS_PER_TILE = 256
SSM_TILE = SSM_CHUNK * SSM_CHUNKS_PER_TILE
SSM_CK = SSM_CHUNK * SSM_GROUP
SSM_ROW = SSM_CHUNK * SSM_WIDTH
SSM_PAIRS = SSM_GROUPS // 2
SSM_LOG_STEPS = 8
ATTN_QB = 128
ATTN_STEP_ROWS = 1024
FFN_CHUNK = 256
WEIGHT_CAST_STEPS = 8


def _dot(a, b):
    return jnp.dot(a, b, preferred_element_type=F32)


def _rms(x, g):
    var = jnp.mean(x * x, axis=-1, keepdims=True)
    return x * lax.rsqrt(var + EPS) * g


def _const_spec(shape):
    nd = len(shape)
    return pl.BlockSpec(shape, lambda *_: (0,) * nd, pipeline_mode=pl.Buffered(1))


def _params(*sem):
    return pltpu.CompilerParams(dimension_semantics=sem, vmem_limit_bytes=VMEM_LIMIT_BYTES)


def _rope(z, cos, sin_lo, sin_hi):
    up = pltpu.roll(z, LANES - ROT_DIM // 2, 1)
    dn = pltpu.roll(z, ROT_DIM // 2, 1)
    return z * cos + up * sin_lo + dn * sin_hi


def _store_residue_major(stage_ref, out_ref, z, dil, tok0):
    rows, first = z.shape[0] // dil, tok0 // dil
    for j in range(z.shape[1] // LANES):
        sl = slice(j * LANES, (j + 1) * LANES)
        stage_ref[...] = z[:, sl]
        for r in range(dil):
            out_ref[r, first:first + rows, sl] = stage_ref[pl.ds(r, rows, stride=dil), :].astype(out_ref.dtype)


def _store_chunk_rows(stage_ref, out_ref, z, c0):
    rows, width = z.shape[0] // SSM_CHUNK, z.shape[1]
    for j in range(width // LANES):
        stage_ref[...] = z[:, j * LANES:(j + 1) * LANES]
        for t in range(SSM_CHUNK):
            col = t * width + j * LANES
            out_ref[c0:c0 + rows, col:col + LANES] = (
                stage_ref[pl.ds(t, rows, stride=SSM_CHUNK), :].astype(out_ref.dtype))


def _load_chunk_rows(stage_ref, in_ref, width, c0, rows):
    slabs = []
    for j in range(width // LANES):
        for t in range(SSM_CHUNK):
            col = t * width + j * LANES
            stage_ref[pl.ds(t, rows, stride=SSM_CHUNK), :] = in_ref[c0:c0 + rows, col:col + LANES].astype(F32)
        slabs.append(stage_ref[...])
    return jnp.concatenate(slabs, axis=1)


def _in_proj_kernel(x_ref, g_ref, ca_ref, sa_ref, cb_ref, sb_ref, sign_ref, w32_ref, ba_ref, bb_ref,
                    s_ref, q0_ref, k0_ref, v0_ref, q1_ref, k1_ref, v1_ref, q2_ref, k2_ref, v2_ref,
                    ga_ref, gb_ref, w_ref, stage_ref):
    step = pl.program_id(0)

    @pl.when(step < WEIGHT_CAST_STEPS)
    def _():
        _cast_weight_rows(step, (w32_ref,), (w_ref,))

    @pl.when(step >= WEIGHT_CAST_STEPS)
    def _():
        _in_proj_tile(x_ref, g_ref, ca_ref, sa_ref, cb_ref, sb_ref, sign_ref, w_ref, ba_ref, bb_ref,
                      s_ref, q0_ref, k0_ref, v0_ref, q1_ref, k1_ref, v1_ref, q2_ref, k2_ref, v2_ref,
                      ga_ref, gb_ref, stage_ref)


def _in_proj_tile(x_ref, g_ref, ca_ref, sa_ref, cb_ref, sb_ref, sign_ref, w_ref, ba_ref, bb_ref,
                  s_ref, q0_ref, k0_ref, v0_ref, q1_ref, k1_ref, v1_ref, q2_ref, k2_ref, v2_ref,
                  ga_ref, gb_ref, stage_ref):
    u = _rms(x_ref[...], g_ref[...]).astype(BF16)
    _store_chunk_rows(stage_ref, s_ref, _dot(u, w_ref[:, :OFF_Q]), 0)
    ca, sa, cb, sb = ca_ref[...], sa_ref[...], cb_ref[...], sb_ref[...]
    cos = ca * cb - sa * sb
    sin = sa * cb + ca * sb
    slo, shi = sin * sign_ref[0:1, :], sin * sign_ref[1:2, :]
    scale = HEAD_DIM ** -0.5

    def rope(z):
        return jnp.concatenate([_rope(z[:, j * LANES:(j + 1) * LANES], cos, slo, shi)
                                for j in range(z.shape[1] // LANES)], axis=1)

    q = rope(_dot(u, w_ref[:, OFF_Q:OFF_K])) * scale
    k = rope(_dot(u, w_ref[:, OFF_K:OFF_V]))
    v = _dot(u, w_ref[:, OFF_V:OFF_GA])
    for z, refs in ((q, (q0_ref, q1_ref, q2_ref)), (k, (k0_ref, k1_ref, k2_ref)), (v, (v0_ref, v1_ref, v2_ref))):
        refs[0][...] = z[:, :GROUP_WIDTH].astype(BF16)
        for grp in (1, 2):
            _store_residue_major(stage_ref, refs[grp], z[:, grp * GROUP_WIDTH:(grp + 1) * GROUP_WIDTH],
                                 DILATIONS[grp], 0)
    ga_ref[...] = jax.nn.sigmoid(_dot(u, w_ref[:, OFF_GA:OFF_GB]) + ba_ref[...]).astype(BF16)
    gb_ref[...] = jax.nn.sigmoid(_dot(u, w_ref[:, OFF_GB:]) + bb_ref[...]).astype(BF16)


def _rope_tables(seq, tm):
    half = ROT_DIM // 2
    freqs = ROPE_THETA ** (-jnp.arange(half, dtype=F32) * (2.0 / ROT_DIM))
    head = jnp.concatenate([freqs, freqs, jnp.zeros((HEAD_DIM - ROT_DIM,), F32)])
    lane_freq = jnp.tile(head, LANES // HEAD_DIM)[None, :]
    base = jnp.arange(0, seq, tm, dtype=F32)[:, None] * lane_freq
    offs = jnp.arange(tm, dtype=F32)[:, None] * lane_freq
    in_head = jnp.arange(LANES) % HEAD_DIM
    sign = jnp.zeros((SUBLANES, LANES), F32)
    sign = sign.at[0].set(jnp.where(in_head < half, -1.0, 0.0))
    sign = sign.at[1].set(jnp.where((in_head >= half) & (in_head < ROT_DIM), 1.0, 0.0))
    n_tiles = seq // tm
    return (jnp.cos(base).reshape(n_tiles, 1, LANES), jnp.sin(base).reshape(n_tiles, 1, LANES),
            jnp.cos(offs), jnp.sin(offs), sign)


def _in_proj(x2, seq, norm_g, w_in, gate_b):
    n = x2.shape[0]
    tm = TOKEN_TILE
    tiles_per_seq = seq // tm
    bsz = n // seq
    ba = gate_b[:D_MODEL].reshape(1, D_MODEL)
    bb = gate_b[D_MODEL:].reshape(1, D_MODEL)
    cos_a, sin_a, cos_b, sin_b, sign = _rope_tables(seq, tm)

    k = WEIGHT_CAST_STEPS
    tile = lambda i: jnp.maximum(i - k, 0)
    row = lambda w: pl.BlockSpec((tm, w), lambda i: (tile(i), 0))
    tile_tab = pl.BlockSpec((None, 1, LANES), lambda i: (tile(i) % tiles_per_seq, 0, 0))
    res = lambda d: pl.BlockSpec((None, d, tm // d, GROUP_WIDTH),
                                 lambda i: (tile(i) // tiles_per_seq, 0, tile(i) % tiles_per_seq, 0))
    res_shape = lambda d: jax.ShapeDtypeStruct((bsz, d, seq // d, GROUP_WIDTH), BF16)
    nat_shape = jax.ShapeDtypeStruct((n, GROUP_WIDTH), BF16)
    d1, d2 = DILATIONS[1], DILATIONS[2]
    outs = pl.pallas_call(
        _in_proj_kernel,
        grid=(k + n // tm,),
        in_specs=[row(D_MODEL), _const_spec((1, D_MODEL)), tile_tab, tile_tab,
                  _const_spec((tm, LANES)), _const_spec((tm, LANES)), _const_spec((SUBLANES, LANES)),
                  pl.BlockSpec((D_MODEL // k, IN_WIDTH), lambda i: (jnp.minimum(i, k - 1), 0)),
                  _const_spec((1, D_MODEL)), _const_spec((1, D_MODEL))],
        out_specs=[pl.BlockSpec((tm // SSM_CHUNK, SSM_ROW), lambda i: (tile(i), 0))]
                  + [row(GROUP_WIDTH)] * 3 + [res(d1)] * 3 + [res(d2)] * 3 + [row(D_MODEL), row(D_MODEL)],
        out_shape=[jax.ShapeDtypeStruct((n // SSM_CHUNK, SSM_ROW), BF16)] + [nat_shape] * 3
                  + [res_shape(d1)] * 3 + [res_shape(d2)] * 3
                  + [jax.ShapeDtypeStruct((n, D_MODEL), BF16), jax.ShapeDtypeStruct((n, D_MODEL), BF16)],
        scratch_shapes=[pltpu.VMEM((D_MODEL, IN_WIDTH), BF16), pltpu.VMEM((tm, LANES), F32)],
        compiler_params=_params("arbitrary"),
        name="in_proj",
    )(x2, norm_g.reshape(1, D_MODEL), cos_a, sin_a, cos_b, sin_b, sign, w_in, ba, bb)
    u, q0, k0, v0, q1, k1, v1, q2, k2, v2, ga, gb = outs
    nat4 = lambda a: a.reshape(bsz, 1, seq, GROUP_WIDTH)
    qkv = ((nat4(q0), nat4(k0), nat4(v0)), (q1, k1, v1), (q2, k2, v2))
    return u, qkv, ga, gb


def _cmul(ar, ai, br, bi):
    return ar * br - ai * bi, ar * bi + ai * br


def _discretise(lr, li, dt):
    mag = jnp.exp(lr * dt)
    ar = mag * jnp.cos(li * dt)
    ai = mag * jnp.sin(li * dt)
    den = lr * lr + li * li
    cr = ((ar - 1.0) * lr + ai * li) / den
    ci = (ai * lr - (ar - 1.0) * li) / den
    return ar, ai, cr, ci


def _ssm_prep_kernel(lr_ref, li_ref, ldt_ref, brt_ref, bit_ref, cre_ref, cim_ref, lrf_ref, lif_ref, ldtf_ref,
                     w2_ref, e2_ref, apsr_ref, apsi_ref, aptr_ref, apti_ref, er_ref, ei_ref):
    g_n, t_n, h_n, sw = SSM_GROUPS, SSM_CHUNK, SSM_GROUP, 2 * SSM_STATE
    ar, ai, cr, ci = _discretise(lr_ref[...], li_ref[...], jnp.exp(ldt_ref[...]))
    brt, bit = brt_ref[...], bit_ref[...]
    bbr = cr * brt - ci * bit
    bbi = cr * bit + ci * brt
    cre, cim = cre_ref[...], cim_ref[...]
    by_group = lambda a: a.reshape(g_n, h_n, sw)
    pr, pi = jnp.ones_like(ar), jnp.zeros_like(ai)
    for j in range(t_n):
        rows = slice((t_n - 1 - j) * h_n, (t_n - j) * h_n)
        rr, ri = _cmul(pr, pi, bbr, bbi)
        er_ref[:, rows, :] = by_group(rr)
        ei_ref[:, rows, :] = by_group(ri)
        e2_ref[:, rows, :sw] = by_group(rr).astype(BF16)
        e2_ref[:, rows, sw:] = by_group(ri).astype(BF16)
        pr, pi = _cmul(pr, pi, ar, ai)
        rows = slice(j * h_n, (j + 1) * h_n)
        w2_ref[:, rows, SSM_CK:SSM_CK + sw] = by_group(cre * pr - cim * pi).astype(BF16)
        w2_ref[:, rows, SSM_CK + sw:] = by_group(-cre * pi - cim * pr).astype(BF16)

    nt = (((1,), (1,)), ((), ()))
    hi = lax.Precision.HIGHEST

    def toeplitz(g, _):
        rows = pl.ds(pl.multiple_of(g * h_n, h_n), h_n)
        krev = (lax.dot_general(cre_ref[rows, :], er_ref[g], nt, precision=hi, preferred_element_type=F32)
                - lax.dot_general(cim_ref[rows, :], ei_ref[g], nt, precision=hi, preferred_element_type=F32))
        kext = jnp.concatenate([krev, jnp.zeros_like(krev)], axis=1)
        for t in range(t_n):
            off = (t_n - 1 - t) * h_n
            win = kext if off == 0 else pltpu.roll(kext, 2 * SSM_CK - off, 1)
            w2_ref[g, t * h_n:(t + 1) * h_n, :SSM_CK] = win[:, :SSM_CK].astype(BF16)
        return 0
    lax.fori_loop(0, g_n, toeplitz, 0, unroll=4)

    acr, aci, _, _ = _discretise(lrf_ref[...], lif_ref[...], jnp.exp(ldtf_ref[...]))
    for _ in range(4):
        acr, aci = _cmul(acr, aci, acr, aci)
    shape = (SUBLANES, acr.shape[1])
    row = lax.broadcasted_iota(jnp.int32, shape, 0)
    qr, qi = jnp.ones(shape, F32), jnp.zeros(shape, F32)
    apsr_ref[...] = jnp.zeros_like(apsr_ref)
    apsi_ref[...] = jnp.zeros_like(apsi_ref)
    for k in range(SSM_LOG_STEPS + 1):
        for p2 in range(SSM_PAIRS):
            apsr_ref[p2, k:k + 1, :] = acr[:, p2 * sw:(p2 + 1) * sw]
            apsi_ref[p2, k:k + 1, :] = aci[:, p2 * sw:(p2 + 1) * sw]
        if (1 << k) < SUBLANES:
            nr, ni = _cmul(qr, qi, acr, aci)
            bit_set = (row & (1 << k)) != 0
            qr, qi = jnp.where(bit_set, nr, qr), jnp.where(bit_set, ni, qi)
        if (1 << k) == SUBLANES:
            a8r, a8i = acr, aci
        acr, aci = _cmul(acr, aci, acr, aci)
    br, bi = jnp.ones_like(a8r), jnp.zeros_like(a8i)
    for b in range(SSM_CHUNKS_PER_TILE // SUBLANES):
        rows = slice(b * SUBLANES, (b + 1) * SUBLANES)
        blk_r, blk_i = _cmul(qr, qi, br, bi)
        for p2 in range(SSM_PAIRS):
            aptr_ref[p2, rows, :] = blk_r[:, p2 * sw:(p2 + 1) * sw]
            apti_ref[p2, rows, :] = blk_i[:, p2 * sw:(p2 + 1) * sw]
        br, bi = _cmul(br, bi, a8r, a8i)


def _pad_pair_lanes(a):
    z = jnp.zeros_like(a)
    even = (jnp.arange(a.shape[0]) % 2 == 0)[:, None, None]
    padded = jnp.where(even, jnp.concatenate([a, z], -1), jnp.concatenate([z, a], -1))
    return padded.reshape(a.shape[0] * a.shape[1], 2 * a.shape[2])


def _ssm_operators(lam_re, lam_im, log_dt, b_re, b_im, c_re, c_im):
    g, p, h = SSM_GROUPS, SSM_STATE, SSM_GROUP
    gp, sw = g * p, 2 * p
    rep = lambda a: jnp.repeat(jnp.tile(a, (1, 2)), h, axis=0)
    ldt2 = jnp.broadcast_to(log_dt[:, None], (g, p))
    full = lambda shape: pl.BlockSpec(shape, lambda: (0,) * len(shape))
    in_arrays = (rep(lam_re), rep(lam_im), rep(ldt2),
                 _pad_pair_lanes(b_re.transpose(0, 2, 1)), _pad_pair_lanes(b_im.transpose(0, 2, 1)),
                 _pad_pair_lanes(c_re), _pad_pair_lanes(c_im),
                 lam_re.reshape(1, gp), lam_im.reshape(1, gp), ldt2.reshape(1, gp))
    out_shapes = ([((g, SSM_CK, SSM_CK + 2 * sw), BF16), ((g, SSM_CK, 2 * sw), BF16)]
                  + [((SSM_PAIRS, 2 * SUBLANES, sw), F32)] * 2
                  + [((SSM_PAIRS, SSM_CHUNKS_PER_TILE, sw), F32)] * 2)
    w2, e2, apsr, apsi, aptr, apti = pl.pallas_call(
        _ssm_prep_kernel,
        in_specs=[full(a.shape) for a in in_arrays],
        out_specs=[full(s) for s, _ in out_shapes],
        out_shape=[jax.ShapeDtypeStruct(s, dt) for s, dt in out_shapes],
        scratch_shapes=[pltpu.VMEM((g, SSM_CK, sw), F32), pltpu.VMEM((g, SSM_CK, sw), F32)],
        compiler_params=pltpu.CompilerParams(vmem_limit_bytes=VMEM_LIMIT_BYTES),
        name="ssm_prep",
    )(*in_arrays)
    return w2, e2.reshape(SSM_PAIRS, 2 * SSM_CK, 2 * sw), apsr, apsi, aptr, apti


def _shift_rows(z, s, row):
    if s % SUBLANES == 0:
        return jnp.concatenate([jnp.zeros((s, z.shape[1]), z.dtype), z[:-s]], axis=0)
    return jnp.where(row >= s, pltpu.roll(z, s, 0), 0.0)


def _ssm_scan_kernel(u_ref, d_ref, w2_ref, e2_ref, apsr_ref, apsi_ref, aptr_ref, apti_ref,
                     y_ref, xs_ref, sc_ref, yt_ref, carry_ref, loc_ref, sin_ref):
    t_n, h_n, c_n = SSM_CHUNK, SSM_GROUP, SSM_CHUNKS_PER_TILE
    n_slab = SSM_WIDTH // LANES
    pairs_per_slab = LANES // (2 * h_n)
    sw = 2 * SSM_STATE
    blk_n = SUBLANES
    n_blk = c_n // blk_n
    log_blk = blk_n.bit_length() - 1

    @pl.when(pl.program_id(1) == 0)
    def _():
        carry_ref[...] = jnp.zeros_like(carry_ref)

    for t in range(t_n):
        for j in range(n_slab):
            col = t * SSM_WIDTH + j * LANES
            blk = u_ref[:, col:col + LANES].T
            xs_ref[j * pairs_per_slab:(j + 1) * pairs_per_slab, :, t * h_n:(t + 1) * h_n, :] = (
                blk.reshape(pairs_per_slab, 2, h_n, c_n))

    row = lax.broadcasted_iota(jnp.int32, (n_blk, sw), 0)
    nt = (((1,), (1,)), ((), ()))
    tn = (((0,), (0,)), ((), ()))

    def local_states(pr, _):
        xp = xs_ref[pr].reshape(2 * SSM_CK, c_n)
        loc = lax.dot_general(xp, e2_ref[pr], tn, preferred_element_type=F32)
        loc_ref[pr, 0] = loc[:, :sw]
        loc_ref[pr, 1] = loc[:, sw:]
        return 0
    lax.fori_loop(0, SSM_PAIRS, local_states, 0, unroll=4)

    def chunk_scan(pr, _):
        slot = pr % 2
        power = lambda k: (apsr_ref[pr, k:k + 1, :], apsi_ref[pr, k:k + 1, :])
        zr, zi = [], []
        for lo in range(blk_n):
            rows = pl.ds(lo, n_blk, stride=blk_n)
            xr, xi = loc_ref[pr, 0, rows, :], loc_ref[pr, 1, rows, :]
            if lo:
                dr, di = _cmul(zr[-1], zi[-1], *power(0))
                xr, xi = xr + dr, xi + di
            zr.append(xr)
            zi.append(xi)
        er, ei = zr[-1], zi[-1]
        s = 1
        while s < n_blk:
            dr, di = _cmul(_shift_rows(er, s, row), _shift_rows(ei, s, row),
                           *power(log_blk + s.bit_length() - 1))
            er, ei = er + dr, ei + di
            s *= 2
        cr, ci = carry_ref[pr, 0:1, :], carry_ref[pr, 1:2, :]
        blk_rows = pl.ds(0, n_blk, stride=blk_n)
        hr, hi = _cmul(aptr_ref[pr, blk_rows, :], apti_ref[pr, blk_rows, :], cr, ci)
        br, bi = _shift_rows(er, 1, row) + hr, _shift_rows(ei, 1, row) + hi
        for lo in range(blk_n):
            sr, si = _cmul(br, bi, aptr_ref[pr, lo:lo + 1, :], apti_ref[pr, lo:lo + 1, :])
            if lo:
                sr, si = sr + zr[lo - 1], si + zi[lo - 1]
            rows = pl.ds(lo, n_blk, stride=blk_n)
            sin_ref[slot, 0, rows, :] = sr
            sin_ref[slot, 1, rows, :] = si
        sc_ref[pr, :, :sw] = sin_ref[slot, 0].astype(BF16)
        sc_ref[pr, :, sw:] = sin_ref[slot, 1].astype(BF16)
        nr, ni = _cmul(cr, ci, *power(SSM_LOG_STEPS))
        carry_ref[pr, 0:1, :] = er[n_blk - 1:n_blk, :] + nr
        carry_ref[pr, 1:2, :] = ei[n_blk - 1:n_blk, :] + ni
        return 0
    lax.fori_loop(0, SSM_PAIRS, chunk_scan, 0, unroll=2)

    def outputs(g, _):
        pr = g // 2
        yg = (_dot(w2_ref[g, :, :SSM_CK], xs_ref[pr, g % 2])
              + lax.dot_general(w2_ref[g, :, SSM_CK:], sc_ref[pr], nt, preferred_element_type=F32))
        yt_ref[:, pl.ds(pl.multiple_of(g * h_n, h_n), h_n), :] = yg.reshape(t_n, h_n, c_n)
        return 0
    lax.fori_loop(0, SSM_GROUPS, outputs, 0, unroll=4)

    for t in range(t_n):
        for j in range(n_slab):
            sl = slice(j * LANES, (j + 1) * LANES)
            col = t * SSM_WIDTH + j * LANES
            y_ref[:, col:col + LANES] = (
                yt_ref[t, sl, :].T + d_ref[:, sl] * u_ref[:, col:col + LANES].astype(F32)).astype(BF16)


def _ssm_scan(u_rows, bsz, d_skip, ops):
    w2, e2, apsr, apsi, aptr, apti = ops
    g, p, c_n = SSM_GROUPS, SSM_STATE, SSM_CHUNKS_PER_TILE
    tiles = u_rows.shape[0] // (bsz * c_n)
    tile = pl.BlockSpec((c_n, SSM_ROW), lambda b, i: (b * tiles + i, 0))
    return pl.pallas_call(
        _ssm_scan_kernel,
        grid=(bsz, tiles),
        in_specs=[tile, _const_spec((1, SSM_WIDTH)),
                  _const_spec(w2.shape), _const_spec(e2.shape),
                  _const_spec(apsr.shape), _const_spec(apsi.shape),
                  _const_spec(aptr.shape), _const_spec(apti.shape)],
        out_specs=tile,
        out_shape=jax.ShapeDtypeStruct(u_rows.shape, BF16),
        scratch_shapes=[pltpu.VMEM((SSM_PAIRS, 2, SSM_CK, c_n), BF16),
                        pltpu.VMEM((SSM_PAIRS, c_n, 4 * p), BF16),
                        pltpu.VMEM((SSM_CHUNK, SSM_WIDTH, c_n), F32),
                        pltpu.VMEM((SSM_PAIRS, SUBLANES, 2 * p), F32),
                        pltpu.VMEM((SSM_PAIRS, 2, c_n, 2 * p), F32),
                        pltpu.VMEM((2, 2, c_n, 2 * p), F32)],
        compiler_params=_params("parallel", "arbitrary"),
        name="ssm_scan",
    )(u_rows, d_skip.reshape(1, SSM_WIDTH), w2, e2, apsr, apsi, aptr, apti)


def _attn_kernel(q_ref, k_ref, v_ref, o_ref, lse_ref):
    step = pl.program_id(2)
    qb, nk = ATTN_QB, ATTN_QB + WINDOW_KEYS
    row = lax.broadcasted_iota(jnp.int32, (qb, nk), 0)
    col = lax.broadcasted_iota(jnp.int32, (qb, nk), 1)
    lane = lax.broadcasted_iota(jnp.int32, (qb, LANES), 1)
    first_head = lane < HEAD_DIM
    nt = (((1,), (1,)), ((), ()))
    for sb in range(o_ref.shape[0] // qb):
        rows = slice(sb * qb, (sb + 1) * qb)
        q_start = step * o_ref.shape[0] + sb * qb
        back = jnp.minimum(q_start, WINDOW_KEYS)
        k_start = pl.multiple_of(q_start - back, ATTN_QB)
        dist = row + back - col
        valid = (dist >= 0) & (dist <= WINDOW_KEYS)
        for pair in range(GROUP_WIDTH // LANES):
            cols = slice(pair * LANES, (pair + 1) * LANES)
            qp = q_ref[rows, cols]
            kp = k_ref[pl.ds(k_start, nk), cols]
            vp = v_ref[pl.ds(k_start, nk), cols]
            outs, lses = [], []
            for sel in (first_head, ~first_head):
                qm = jnp.where(sel, qp, jnp.zeros_like(qp))
                s = lax.dot_general(qm, kp, nt, preferred_element_type=F32)
                s = jnp.where(valid, s, NEG_BIG)
                m = jnp.max(s, axis=-1, keepdims=True)
                e = jnp.exp(s - m)
                den = jnp.sum(e, axis=-1, keepdims=True)
                outs.append(_dot(e.astype(BF16), vp) / den)
                lses.append(m + jnp.log(den))
            o_ref[rows, cols] = jnp.where(first_head, outs[0], outs[1]).astype(BF16)
            lse_ref[rows, cols] = jnp.where(first_head, lses[0], lses[1])


def _attn_group(q4, k4, v4):
    bsz, dil, lr, _ = q4.shape
    rows = min(ATTN_STEP_ROWS, lr)
    q_spec = pl.BlockSpec((None, None, rows, GROUP_WIDTH), lambda b, r, i: (b, r, i, 0))
    kv_spec = pl.BlockSpec((None, None, lr, GROUP_WIDTH), lambda b, r, i: (b, r, 0, 0))
    return pl.pallas_call(
        _attn_kernel,
        grid=(bsz, dil, lr // rows),
        in_specs=[q_spec, kv_spec, kv_spec],
        out_specs=[q_spec, q_spec],
        out_shape=[jax.ShapeDtypeStruct(q4.shape, BF16), jax.ShapeDtypeStruct(q4.shape, F32)],
        compiler_params=_params("parallel", "parallel", "arbitrary"),
        name=f"attn_d{dil}",
    )(q4, k4, v4)


def _load_token_major(stage_ref, in_ref, tok0, ntok, lanes):
    dil = in_ref.shape[0]
    first, rows = tok0 // dil, ntok // dil
    if dil == 1:
        return in_ref[0, first:first + rows, lanes].astype(F32)
    for r in range(dil):
        stage_ref[pl.ds(r, rows, stride=dil), :] = in_ref[r, first:first + rows, lanes].astype(F32)
    return stage_ref[...]


def _merge_kernel(x_ref, ys_ref, gluw_ref, glub_ref, wa_ref,
                  o0_ref, o1_ref, o2_ref, l0_ref, l1_ref, l2_ref, wb_ref,
                  ga_ref, gb_ref, wout_ref, h_ref, stage_ref, y_ref, ya_ref, attn_ref, mix_ref):
    pieces = lambda width: [slice(c * COL_TILE, (c + 1) * COL_TILE) for c in range(width // COL_TILE)]
    for s in range(x_ref.shape[0] // SUB_TILE):
        tok0 = s * SUB_TILE
        rows = slice(tok0, tok0 + SUB_TILE)
        stage = stage_ref.at[s]
        y_ref[s] = jax.nn.gelu(_load_chunk_rows(stage, ys_ref, SSM_WIDTH, tok0 // SSM_CHUNK, SUB_TILE // SSM_CHUNK))
        y_bf = y_ref[s].astype(BF16)
        for cols in pieces(SSM_WIDTH):
            gate = jax.nn.sigmoid(_dot(y_bf, gluw_ref[:, cols]) + glub_ref[:, cols])
            ya_ref[s, :, cols] = (y_ref[s, :, cols] * gate).astype(BF16)

        for j in range(GROUP_WIDTH // LANES):
            lanes = slice(j * LANES, (j + 1) * LANES)
            o, l = ([_load_token_major(stage, ref, tok0, SUB_TILE, lanes) for ref in refs]
                    for refs in ((o0_ref, o1_ref, o2_ref), (l0_ref, l1_ref, l2_ref)))
            top = jnp.maximum(jnp.maximum(l[0], l[1]), l[2])
            w = [jnp.exp(lg - top) for lg in l]
            attn_ref[s, :, lanes] = ((w[0] * o[0] + w[1] * o[1] + w[2] * o[2])
                                     / (w[0] + w[1] + w[2])).astype(BF16)

        for cols in pieces(D_MODEL):
            mix = (ga_ref[rows, cols].astype(F32) * _dot(ya_ref[s], wa_ref[:, cols])
                   + gb_ref[rows, cols].astype(F32) * _dot(attn_ref[s], wb_ref[:, cols]))
            mix_ref[s, :, cols] = mix.astype(BF16)
        for cols in pieces(D_MODEL):
            h_ref[rows, cols] = x_ref[rows, cols] + _dot(mix_ref[s], wout_ref[:, cols])


def _merge(x2, seq, ys_rows, glu_w, glu_b, w_a, attn_outs, w_b, ga, gb, w_out):
    n = x2.shape[0]
    tm = TOKEN_TILE
    tiles_per_seq = seq // tm
    row = lambda w: pl.BlockSpec((tm, w), lambda i: (i, 0))
    res = lambda d: pl.BlockSpec((None, d, tm // d, GROUP_WIDTH),
                                 lambda i: (i // tiles_per_seq, 0, i % tiles_per_seq, 0))
    (o0, l0), (o1, l1), (o2, l2) = attn_outs
    d0, d1, d2 = DILATIONS
    return pl.pallas_call(
        _merge_kernel,
        grid=(n // tm,),
        in_specs=[row(D_MODEL), pl.BlockSpec((tm // SSM_CHUNK, SSM_ROW), lambda i: (i, 0)),
                  _const_spec((SSM_WIDTH, SSM_WIDTH)), _const_spec((1, SSM_WIDTH)),
                  _const_spec((SSM_WIDTH, D_MODEL)),
                  res(d0), res(d1), res(d2), res(d0), res(d1), res(d2),
                  _const_spec((GROUP_WIDTH, D_MODEL)), row(D_MODEL), row(D_MODEL),
                  _const_spec((D_MODEL, D_MODEL))],
        out_specs=row(D_MODEL),
        out_shape=jax.ShapeDtypeStruct((n, D_MODEL), F32),
        scratch_shapes=[pltpu.VMEM((tm // SUB_TILE, SUB_TILE, LANES), F32),
                        pltpu.VMEM((tm // SUB_TILE, SUB_TILE, SSM_WIDTH), F32),
                        pltpu.VMEM((tm // SUB_TILE, SUB_TILE, SSM_WIDTH), BF16),
                        pltpu.VMEM((tm // SUB_TILE, SUB_TILE, GROUP_WIDTH), BF16),
                        pltpu.VMEM((tm // SUB_TILE, SUB_TILE, D_MODEL), BF16)],
        compiler_params=_params("parallel"),
        name="merge",
    )(x2, ys_rows, glu_w.astype(BF16), glu_b.reshape(1, SSM_WIDTH),
      w_a.astype(BF16), o0, o1, o2, l0, l1, l2, w_b.astype(BF16), ga, gb, w_out.astype(BF16))


def _cast_weight_rows(step, src_refs, dst_refs):
    for src, dst in zip(src_refs, dst_refs):
        rb = src.shape[0]
        dst[pl.ds(pl.multiple_of(step * rb, rb), rb), :] = src[...].astype(BF16)


def _ffn_kernel(h_ref, g2_ref, wg32_ref, wu32_ref, cw_ref, cb_ref, wd32_ref, g3_ref, wpg32_ref,
                p_ref, wpp32_ref, gf_ref, out_ref, wg_ref, wu_ref, wd_ref, wpg_ref, wpp_ref,
                act_ref, carry_ref, *, tiles_per_seq):
    step = pl.program_id(0)

    @pl.when(step < WEIGHT_CAST_STEPS)
    def _():
        _cast_weight_rows(step, (wg32_ref, wu32_ref, wd32_ref, wpg32_ref, wpp32_ref),
                          (wg_ref, wu_ref, wd_ref, wpg_ref, wpp_ref))

    @pl.when(step >= WEIGHT_CAST_STEPS)
    def _():
        _ffn_tile(step - WEIGHT_CAST_STEPS, h_ref, g2_ref, wg_ref, wu_ref, cw_ref, cb_ref, wd_ref, g3_ref, wpg_ref,
                  p_ref, wpp_ref, gf_ref, out_ref, act_ref, carry_ref, tiles_per_seq)


def _ffn_tile(tile, h_ref, g2_ref, wg_ref, wu_ref, cw_ref, cb_ref, wd_ref, g3_ref, wpg_ref,
              p_ref, wpp_ref, gf_ref, out_ref, act_ref, carry_ref, tiles_per_seq):
    tm = h_ref.shape[0]

    @pl.when(tile % tiles_per_seq == 0)
    def _():
        carry_ref[...] = jnp.zeros_like(carry_ref)

    subs = [slice(s * SUB_TILE, (s + 1) * SUB_TILE) for s in range(tm // SUB_TILE)]
    hs = [h_ref[rows, :] for rows in subs]
    u2s = [_rms(h, g2_ref[...]).astype(BF16) for h in hs]
    row = lax.broadcasted_iota(jnp.int32, (SUBLANES, FFN_CHUNK), 0)
    for c in range(D_FF // FFN_CHUNK):
        sl = slice(c * FFN_CHUNK, (c + 1) * FFN_CHUNK)
        prev = carry_ref[:, sl]
        for rows, u2 in zip(subs, u2s):
            gp = _dot(u2, wg_ref[:, sl])
            up = _dot(u2, wu_ref[:, sl])
            r1 = pltpu.roll(gp, 1, 0)
            r2 = pltpu.roll(gp, 2, 0)
            r1 = jnp.concatenate([jnp.where(row < 1, pltpu.roll(prev, 1, 0), r1[:SUBLANES]), r1[SUBLANES:]], axis=0)
            r2 = jnp.concatenate([jnp.where(row < 2, pltpu.roll(prev, 2, 0), r2[:SUBLANES]), r2[SUBLANES:]], axis=0)
            gate = cw_ref[0:1, sl] * r2 + cw_ref[1:2, sl] * r1 + cw_ref[2:3, sl] * gp + cb_ref[:, sl]
            act_ref[rows, sl] = (jax.nn.gelu(gate) * up).astype(BF16)
            prev = gp[SUB_TILE - SUBLANES:, :]
        carry_ref[:, sl] = prev
    for rows, h in zip(subs, hs):
        h = h + _dot(act_ref[rows, :], wd_ref[...])
        u3 = _rms(h, g3_ref[...]).astype(BF16)
        h = h + jax.nn.sigmoid(_dot(u3, wpg_ref[...])) * _dot(p_ref[rows, :].astype(BF16), wpp_ref[...])
        out_ref[rows, :] = _rms(h, gf_ref[...])


def _ffn(h1, seq, p2, norm_g, w_gate, w_up, conv_w, conv_b, w_down, ple_g, ple_w_gate, ple_w_proj, final_g):
    n = h1.shape[0]
    tm = TOKEN_TILE
    k = WEIGHT_CAST_STEPS
    row = lambda w: pl.BlockSpec((tm, w), lambda i: (jnp.maximum(i - k, 0), 0))
    wrows = lambda a: pl.BlockSpec((a.shape[0] // k, a.shape[1]), lambda i: (jnp.minimum(i, k - 1), 0))
    resident = lambda a: pltpu.VMEM(a.shape, BF16)
    vec = lambda a: a.reshape(1, -1)
    weights = (w_gate, w_up, w_down, ple_w_gate, ple_w_proj)
    return pl.pallas_call(
        functools.partial(_ffn_kernel, tiles_per_seq=seq // tm),
        grid=(k + n // tm,),
        in_specs=[row(D_MODEL), _const_spec((1, D_MODEL)), wrows(w_gate),
                  wrows(w_up), _const_spec((CONV_WIDTH, D_FF)), _const_spec((1, D_FF)),
                  wrows(w_down), _const_spec((1, D_MODEL)), wrows(ple_w_gate),
                  row(PLE_DIM), wrows(ple_w_proj), _const_spec((1, D_MODEL))],
        out_specs=row(D_MODEL),
        out_shape=jax.ShapeDtypeStruct((n, D_MODEL), F32),
        scratch_shapes=[resident(w) for w in weights]
                       + [pltpu.VMEM((tm, D_FF), BF16), pltpu.VMEM((SUBLANES, D_FF), F32)],
        compiler_params=_params("arbitrary"),
        name="ffn",
    )(h1, vec(norm_g), w_gate, w_up, conv_w, vec(conv_b),
      w_down, vec(ple_g), ple_w_gate, p2, ple_w_proj, vec(final_g))


def _layer(h2, bsz, seq, p2, mix_norm_g, w_in, gate_b, ssm_lam_re, ssm_lam_im, ssm_log_dt, ssm_b_re,
           ssm_b_im, ssm_c_re, ssm_c_im, ssm_d, ssm_glu_w, ssm_glu_b, w_branch_a, w_branch_b, w_out,
           ffn_norm_g, ffn_w_gate, ffn_w_up, ffn_conv_w, ffn_conv_b, ffn_w_down,
           ple_norm_g, ple_w_gate, ple_w_proj, out_norm_g):
    u_rows, qkv, ga, gb = _in_proj(h2, seq, mix_norm_g, w_in, gate_b)
    ops = _ssm_operators(ssm_lam_re, ssm_lam_im, ssm_log_dt, ssm_b_re, ssm_b_im, ssm_c_re, ssm_c_im)
    ys_rows = _ssm_scan(u_rows, bsz, ssm_d.reshape(-1), ops)
    attn_outs = [_attn_group(*group) for group in qkv]
    h1 = _merge(h2, seq, ys_rows, ssm_glu_w, ssm_glu_b, w_branch_a, attn_outs, w_branch_b, ga, gb, w_out)
    return _ffn(h1, seq, p2, ffn_norm_g, ffn_w_gate, ffn_w_up, ffn_conv_w, ffn_conv_b, ffn_w_down,
                ple_norm_g, ple_w_gate, ple_w_proj, out_norm_g)


def kernel(x, p, mix_norm_g, w_in, gate_b, ssm_lam_re, ssm_lam_im, ssm_log_dt, ssm_b_re, ssm_b_im, ssm_c_re, ssm_c_im, ssm_d, ssm_glu_w, ssm_glu_b, w_branch_a, w_branch_b, w_out, ffn_norm_g, ffn_w_gate, ffn_w_up, ffn_conv_w, ffn_conv_b, ffn_w_down, ple_norm_g, ple_w_gate, ple_w_proj, final_norm_g):
    bsz, seq, _ = x.shape
    depth = p.shape[0]
    assert depth == 1, "the final norm is fused into the layer's last kernel"
    h2 = x.reshape(bsz * seq, D_MODEL)
    out = _layer(h2, bsz, seq, p[0].reshape(bsz * seq, PLE_DIM), mix_norm_g[0], w_in[0], gate_b[0],
                 ssm_lam_re[0], ssm_lam_im[0], ssm_log_dt[0], ssm_b_re[0], ssm_b_im[0], ssm_c_re[0],
                 ssm_c_im[0], ssm_d[0], ssm_glu_w[0], ssm_glu_b[0], w_branch_a[0], w_branch_b[0],
                 w_out[0], ffn_norm_g[0], ffn_w_gate[0], ffn_w_up[0], ffn_conv_w[0], ffn_conv_b[0],
                 ffn_w_down[0], ple_norm_g[0], ple_w_gate[0], ple_w_proj[0], final_norm_g)
    return out.reshape(bsz, seq, D_MODEL)
```

```python
import functools

import jax
import jax.numpy as jnp
from jax import lax
from jax.experimental import pallas as pl
from jax.experimental.pallas import tpu as pltpu

F32 = jnp.float32
BF16 = jnp.bfloat16

D_MODEL = 1024
EPS = 1e-6
PLE_DIM = 256
SSM_GROUP = 16
SSM_STATE = 64
SSM_WIDTH = 512
SSM_GROUPS = SSM_WIDTH // SSM_GROUP
HEAD_DIM = 64
DILATIONS = (1, 4, 16)
WINDOW_KEYS = 128
HEADS_PER_GROUP = 4
GROUP_WIDTH = HEADS_PER_GROUP * HEAD_DIM
ATTN_WIDTH = len(DILATIONS) * GROUP_WIDTH
ROT_DIM = HEAD_DIM // 4
ROPE_THETA = 500000.0
NEG_BIG = -1e30
D_FF = 2816
CONV_WIDTH = 3
OFF_Q = SSM_WIDTH
OFF_K = OFF_Q + ATTN_WIDTH
OFF_V = OFF_K + ATTN_WIDTH
OFF_GA = OFF_V + ATTN_WIDTH
OFF_GB = OFF_GA + D_MODEL
IN_WIDTH = OFF_GB + D_MODEL

LANES = 128
SUBLANES = 8
VMEM_LIMIT_BYTES = 56 * 1024 * 1024

TOKEN_TILE = 1024
SUB_TILE = 256
COL_TILE = 256
SSM_CHUNK = 16
SSM_CHUNKS_PER_TILE = 256
SSM_TILE = SSM_CHUNK * SSM_CHUNKS_PER_TILE
SSM_CK = SSM_CHUNK * SSM_GROUP
SSM_ROW = SSM_CHUNK * SSM_WIDTH
SSM_PAIRS = SSM_GROUPS // 2
SSM_LOG_STEPS = 8
ATTN_QB = 128
ATTN_STEP_ROWS = 1024
FFN_CHUNK = 256
WEIGHT_CAST_STEPS = 8


def _dot(a, b):
    return jnp.dot(a, b, preferred_element_type=F32)


def _rms(x, g):
    var = jnp.mean(x * x, axis=-1, keepdims=True)
    return x * lax.rsqrt(var + EPS) * g


def _const_spec(shape):
    nd = len(shape)
    return pl.BlockSpec(shape, lambda *_: (0,) * nd, pipeline_mode=pl.Buffered(1))


def _params(*sem):
    return pltpu.CompilerParams(dimension_semantics=sem, vmem_limit_bytes=VMEM_LIMIT_BYTES)


def _rope(z, cos, sin_lo, sin_hi):
    up = pltpu.roll(z, LANES - ROT_DIM // 2, 1)
    dn = pltpu.roll(z, ROT_DIM // 2, 1)
    return z * cos + up * sin_lo + dn * sin_hi


def _store_residue_major(stage_ref, out_ref, z, dil, tok0):
    rows, first = z.shape[0] // dil, tok0 // dil
    for j in range(z.shape[1] // LANES):
        sl = slice(j * LANES, (j + 1) * LANES)
        stage_ref[...] = z[:, sl]
        for r in range(dil):
            out_ref[r, first:first + rows, sl] = stage_ref[pl.ds(r, rows, stride=dil), :].astype(out_ref.dtype)


def _store_chunk_rows(stage_ref, out_ref, z, c0):
    rows, width = z.shape[0] // SSM_CHUNK, z.shape[1]
    for j in range(width // LANES):
        stage_ref[...] = z[:, j * LANES:(j + 1) * LANES]
        for t in range(SSM_CHUNK):
            col = t * width + j * LANES
            out_ref[c0:c0 + rows, col:col + LANES] = (
                stage_ref[pl.ds(t, rows, stride=SSM_CHUNK), :].astype(out_ref.dtype))


def _load_chunk_rows(stage_ref, in_ref, width, c0, rows):
    slabs = []
    for j in range(width // LANES):
        for t in range(SSM_CHUNK):
            col = t * width + j * LANES
            stage_ref[pl.ds(t, rows, stride=SSM_CHUNK), :] = in_ref[c0:c0 + rows, col:col + LANES].astype(F32)
        slabs.append(stage_ref[...])
    return jnp.concatenate(slabs, axis=1)


def _in_proj_kernel(x_ref, g_ref, ca_ref, sa_ref, cb_ref, sb_ref, sign_ref, w32_ref, bg_ref,
                    s_ref, q0_ref, k0_ref, v0_ref, q1_ref, k1_ref, v1_ref, q2_ref, k2_ref, v2_ref,
                    ga_ref, gb_ref, w_ref, stage_ref):
    step = pl.program_id(0)

    @pl.when(step < WEIGHT_CAST_STEPS)
    def _():
        _cast_weight_rows(step, (w32_ref,), (w_ref,))

    @pl.when(step >= WEIGHT_CAST_STEPS)
    def _():
        _in_proj_tile(x_ref, g_ref, ca_ref, sa_ref, cb_ref, sb_ref, sign_ref, w_ref, bg_ref,
                      s_ref, q0_ref, k0_ref, v0_ref, q1_ref, k1_ref, v1_ref, q2_ref, k2_ref, v2_ref,
                      ga_ref, gb_ref, stage_ref)


def _in_proj_tile(x_ref, g_ref, ca_ref, sa_ref, cb_ref, sb_ref, sign_ref, w_ref, bg_ref,
                  s_ref, q0_ref, k0_ref, v0_ref, q1_ref, k1_ref, v1_ref, q2_ref, k2_ref, v2_ref,
                  ga_ref, gb_ref, stage_ref):
    u = _rms(x_ref[...], g_ref[...]).astype(BF16)
    _store_chunk_rows(stage_ref, s_ref, _dot(u, w_ref[:, :OFF_Q]), 0)
    ca, sa, cb, sb = ca_ref[...], sa_ref[...], cb_ref[...], sb_ref[...]
    cos = ca * cb - sa * sb
    sin = sa * cb + ca * sb
    slo, shi = sin * sign_ref[0:1, :], sin * sign_ref[1:2, :]
    scale = HEAD_DIM ** -0.5

    def rope(z):
        return jnp.concatenate([_rope(z[:, j * LANES:(j + 1) * LANES], cos, slo, shi)
                                for j in range(z.shape[1] // LANES)], axis=1)

    q = rope(_dot(u, w_ref[:, OFF_Q:OFF_K])) * scale
    k = rope(_dot(u, w_ref[:, OFF_K:OFF_V]))
    v = _dot(u, w_ref[:, OFF_V:OFF_GA])
    for z, refs in ((q, (q0_ref, q1_ref, q2_ref)), (k, (k0_ref, k1_ref, k2_ref)), (v, (v0_ref, v1_ref, v2_ref))):
        refs[0][...] = z[:, :GROUP_WIDTH].astype(BF16)
        for grp in (1, 2):
            _store_residue_major(stage_ref, refs[grp], z[:, grp * GROUP_WIDTH:(grp + 1) * GROUP_WIDTH],
                                 DILATIONS[grp], 0)
    ga_ref[...] = jax.nn.sigmoid(_dot(u, w_ref[:, OFF_GA:OFF_GB]) + bg_ref[:, :D_MODEL]).astype(BF16)
    gb_ref[...] = jax.nn.sigmoid(_dot(u, w_ref[:, OFF_GB:]) + bg_ref[:, D_MODEL:]).astype(BF16)


def _rope_tables(seq, tm):
    half = ROT_DIM // 2
    freqs = ROPE_THETA ** (-jnp.arange(half, dtype=F32) * (2.0 / ROT_DIM))
    head = jnp.concatenate([freqs, freqs, jnp.zeros((HEAD_DIM - ROT_DIM,), F32)])
    lane_freq = jnp.tile(head, LANES // HEAD_DIM)[None, :]
    base = jnp.arange(0, seq, tm, dtype=F32)[:, None] * lane_freq
    offs = jnp.arange(tm, dtype=F32)[:, None] * lane_freq
    in_head = jnp.arange(LANES) % HEAD_DIM
    sign = jnp.zeros((SUBLANES, LANES), F32)
    sign = sign.at[0].set(jnp.where(in_head < half, -1.0, 0.0))
    sign = sign.at[1].set(jnp.where((in_head >= half) & (in_head < ROT_DIM), 1.0, 0.0))
    n_tiles = seq // tm
    return (jnp.cos(base).reshape(n_tiles, 1, LANES), jnp.sin(base).reshape(n_tiles, 1, LANES),
            jnp.cos(offs), jnp.sin(offs), sign)


def _in_proj(x2, seq, norm_g, w_in, gate_b):
    n = x2.shape[0]
    tm = TOKEN_TILE
    tiles_per_seq = seq // tm
    bsz = n // seq
    cos_a, sin_a, cos_b, sin_b, sign = _rope_tables(seq, tm)

    k = WEIGHT_CAST_STEPS
    tile = lambda i: jnp.maximum(i - k, 0)
    row = lambda w: pl.BlockSpec((tm, w), lambda i: (tile(i), 0))
    tile_tab = pl.BlockSpec((None, 1, LANES), lambda i: (tile(i) % tiles_per_seq, 0, 0))
    res = lambda d: pl.BlockSpec((None, d, tm // d, GROUP_WIDTH),
                                 lambda i: (tile(i) // tiles_per_seq, 0, tile(i) % tiles_per_seq, 0))
    res_shape = lambda d: jax.ShapeDtypeStruct((bsz, d, seq // d, GROUP_WIDTH), BF16)
    nat_shape = jax.ShapeDtypeStruct((n, GROUP_WIDTH), BF16)
    d1, d2 = DILATIONS[1], DILATIONS[2]
    outs = pl.pallas_call(
        _in_proj_kernel,
        grid=(k + n // tm,),
        in_specs=[row(D_MODEL), _const_spec((1, D_MODEL)), tile_tab, tile_tab,
                  _const_spec((tm, LANES)), _const_spec((tm, LANES)), _const_spec((SUBLANES, LANES)),
                  pl.BlockSpec((D_MODEL // k, IN_WIDTH), lambda i: (jnp.minimum(i, k - 1), 0)),
                  _const_spec((1, 2 * D_MODEL))],
        out_specs=[pl.BlockSpec((tm // SSM_CHUNK, SSM_ROW), lambda i: (tile(i), 0))]
                  + [row(GROUP_WIDTH)] * 3 + [res(d1)] * 3 + [res(d2)] * 3 + [row(D_MODEL), row(D_MODEL)],
        out_shape=[jax.ShapeDtypeStruct((n // SSM_CHUNK, SSM_ROW), BF16)] + [nat_shape] * 3
                  + [res_shape(d1)] * 3 + [res_shape(d2)] * 3
                  + [jax.ShapeDtypeStruct((n, D_MODEL), BF16), jax.ShapeDtypeStruct((n, D_MODEL), BF16)],
        scratch_shapes=[pltpu.VMEM((D_MODEL, IN_WIDTH), BF16), pltpu.VMEM((tm, LANES), F32)],
        compiler_params=_params("arbitrary"),
        name="in_proj",
    )(x2, norm_g.reshape(1, D_MODEL), cos_a, sin_a, cos_b, sin_b, sign, w_in, gate_b.reshape(1, 2 * D_MODEL))
    u, q0, k0, v0, q1, k1, v1, q2, k2, v2, ga, gb = outs
    nat4 = lambda a: a.reshape(bsz, 1, seq, GROUP_WIDTH)
    qkv = ((nat4(q0), nat4(k0), nat4(v0)), (q1, k1, v1), (q2, k2, v2))
    return u, qkv, ga, gb


def _cmul(ar, ai, br, bi):
    return ar * br - ai * bi, ar * bi + ai * br


def _discretise(lr, li, dt):
    mag = jnp.exp(lr * dt)
    ar = mag * jnp.cos(li * dt)
    ai = mag * jnp.sin(li * dt)
    den = lr * lr + li * li
    cr = ((ar - 1.0) * lr + ai * li) / den
    ci = (ai * lr - (ar - 1.0) * li) / den
    return ar, ai, cr, ci


def _ssm_prep_kernel(lr_ref, li_ref, ldt_ref, brt_ref, bit_ref, cre_ref, cim_ref, lrf_ref, lif_ref, ldtf_ref,
                     w2_ref, e2_ref, apsr_ref, apsi_ref, aplr_ref, apli_ref, apbr_ref, apbi_ref, er_ref, ei_ref):
    g_n, t_n, h_n, sw = SSM_GROUPS, SSM_CHUNK, SSM_GROUP, 2 * SSM_STATE
    ar, ai, cr, ci = _discretise(lr_ref[...], li_ref[...], jnp.exp(ldt_ref[...]))
    brt, bit = brt_ref[...], bit_ref[...]
    bbr = cr * brt - ci * bit
    bbi = cr * bit + ci * brt
    cre, cim = cre_ref[...], cim_ref[...]
    by_group = lambda a: a.reshape(g_n, h_n, sw)
    pr, pi = jnp.ones_like(ar), jnp.zeros_like(ai)
    for j in range(t_n):
        rows = slice((t_n - 1 - j) * h_n, (t_n - j) * h_n)
        rr, ri = _cmul(pr, pi, bbr, bbi)
        er_ref[:, rows, :] = by_group(rr)
        ei_ref[:, rows, :] = by_group(ri)
        e2_ref[:, rows, :sw] = by_group(rr).astype(BF16)
        e2_ref[:, rows, sw:] = by_group(ri).astype(BF16)
        pr, pi = _cmul(pr, pi, ar, ai)
        rows = slice(j * h_n, (j + 1) * h_n)
        w2_ref[:, rows, SSM_CK:SSM_CK + sw] = by_group(cre * pr - cim * pi).astype(BF16)
        w2_ref[:, rows, SSM_CK + sw:] = by_group(-cre * pi - cim * pr).astype(BF16)

    nt = (((1,), (1,)), ((), ()))
    hi = lax.Precision.HIGHEST

    def toeplitz(g, _):
        rows = pl.ds(pl.multiple_of(g * h_n, h_n), h_n)
        krev = (lax.dot_general(cre_ref[rows, :], er_ref[g], nt, precision=hi, preferred_element_type=F32)
                - lax.dot_general(cim_ref[rows, :], ei_ref[g], nt, precision=hi, preferred_element_type=F32))
        kext = jnp.concatenate([krev, jnp.zeros_like(krev)], axis=1)
        for t in range(t_n):
            off = (t_n - 1 - t) * h_n
            win = kext if off == 0 else pltpu.roll(kext, 2 * SSM_CK - off, 1)
            w2_ref[g, t * h_n:(t + 1) * h_n, :SSM_CK] = win[:, :SSM_CK].astype(BF16)
        return 0
    lax.fori_loop(0, g_n, toeplitz, 0, unroll=4)

    acr, aci, _, _ = _discretise(lrf_ref[...], lif_ref[...], jnp.exp(ldtf_ref[...]))
    for _ in range(4):
        acr, aci = _cmul(acr, aci, acr, aci)
    shape = (SUBLANES, acr.shape[1])
    row = lax.broadcasted_iota(jnp.int32, shape, 0)
    qr, qi = jnp.ones(shape, F32), jnp.zeros(shape, F32)
    apsr_ref[...] = jnp.zeros_like(apsr_ref)
    apsi_ref[...] = jnp.zeros_like(apsi_ref)
    for k in range(SSM_LOG_STEPS + 1):
        for p2 in range(SSM_PAIRS):
            apsr_ref[p2, k:k + 1, :] = acr[:, p2 * sw:(p2 + 1) * sw]
            apsi_ref[p2, k:k + 1, :] = aci[:, p2 * sw:(p2 + 1) * sw]
        if (1 << k) < SUBLANES:
            nr, ni = _cmul(qr, qi, acr, aci)
            bit_set = (row & (1 << k)) != 0
            qr, qi = jnp.where(bit_set, nr, qr), jnp.where(bit_set, ni, qi)
        if (1 << k) == SUBLANES:
            a8r, a8i = acr, aci
        acr, aci = _cmul(acr, aci, acr, aci)
    br, bi = jnp.ones_like(a8r), jnp.zeros_like(a8i)
    blk_r, blk_i = [], []
    for _ in range(SSM_CHUNKS_PER_TILE // SUBLANES):
        blk_r.append(br)
        blk_i.append(bi)
        br, bi = _cmul(br, bi, a8r, a8i)
    blk_r, blk_i = jnp.concatenate(blk_r, axis=0), jnp.concatenate(blk_i, axis=0)
    for p2 in range(SSM_PAIRS):
        lanes = slice(p2 * sw, (p2 + 1) * sw)
        aplr_ref[p2], apli_ref[p2] = qr[:, lanes], qi[:, lanes]
        apbr_ref[p2], apbi_ref[p2] = blk_r[:, lanes], blk_i[:, lanes]


def _pad_pair_lanes(a):
    z = jnp.zeros_like(a)
    even = (jnp.arange(a.shape[0]) % 2 == 0)[:, None, None]
    padded = jnp.where(even, jnp.concatenate([a, z], -1), jnp.concatenate([z, a], -1))
    return padded.reshape(a.shape[0] * a.shape[1], 2 * a.shape[2])


def _ssm_operators(lam_re, lam_im, log_dt, b_re, b_im, c_re, c_im):
    g, p, h = SSM_GROUPS, SSM_STATE, SSM_GROUP
    gp, sw = g * p, 2 * p
    rep = lambda a: jnp.repeat(jnp.tile(a, (1, 2)), h, axis=0)
    ldt2 = jnp.broadcast_to(log_dt[:, None], (g, p))
    full = lambda shape: pl.BlockSpec(shape, lambda: (0,) * len(shape))
    in_arrays = (rep(lam_re), rep(lam_im), rep(ldt2),
                 _pad_pair_lanes(b_re.transpose(0, 2, 1)), _pad_pair_lanes(b_im.transpose(0, 2, 1)),
                 _pad_pair_lanes(c_re), _pad_pair_lanes(c_im),
                 lam_re.reshape(1, gp), lam_im.reshape(1, gp), ldt2.reshape(1, gp))
    out_shapes = ([((g, SSM_CK, SSM_CK + 2 * sw), BF16), ((g, SSM_CK, 2 * sw), BF16)]
                  + [((SSM_PAIRS, 2 * SUBLANES, sw), F32)] * 2
                  + [((SSM_PAIRS, SUBLANES, sw), F32)] * 2
                  + [((SSM_PAIRS, SSM_CHUNKS_PER_TILE // SUBLANES, sw), F32)] * 2)
    w2, e2, *powers = pl.pallas_call(
        _ssm_prep_kernel,
        in_specs=[full(a.shape) for a in in_arrays],
        out_specs=[full(s) for s, _ in out_shapes],
        out_shape=[jax.ShapeDtypeStruct(s, dt) for s, dt in out_shapes],
        scratch_shapes=[pltpu.VMEM((g, SSM_CK, sw), F32), pltpu.VMEM((g, SSM_CK, sw), F32)],
        compiler_params=pltpu.CompilerParams(vmem_limit_bytes=VMEM_LIMIT_BYTES),
        name="ssm_prep",
    )(*in_arrays)
    return (w2, e2.reshape(SSM_PAIRS, 2 * SSM_CK, 2 * sw), *powers)


def _shift_rows(z, s, row):
    if s % SUBLANES == 0:
        return jnp.concatenate([jnp.zeros((s, z.shape[1]), z.dtype), z[:-s]], axis=0)
    return jnp.where(row >= s, pltpu.roll(z, s, 0), 0.0)


def _ssm_scan_kernel(u_ref, d_ref, w2_ref, e2_ref, apsr_ref, apsi_ref, aplr_ref, apli_ref, apbr_ref, apbi_ref,
                     y_ref, xs_ref, sc_ref, yt_ref, carry_ref, loc_ref, sin_ref):
    t_n, h_n, c_n = SSM_CHUNK, SSM_GROUP, SSM_CHUNKS_PER_TILE
    n_slab = SSM_WIDTH // LANES
    pairs_per_slab = LANES // (2 * h_n)
    sw = 2 * SSM_STATE
    blk_n = SUBLANES
    n_blk = c_n // blk_n
    log_blk = blk_n.bit_length() - 1

    @pl.when(pl.program_id(1) == 0)
    def _():
        carry_ref[...] = jnp.zeros_like(carry_ref)

    for t in range(t_n):
        for j in range(n_slab):
            col = t * SSM_WIDTH + j * LANES
            blk = u_ref[:, col:col + LANES].T
            xs_ref[j * pairs_per_slab:(j + 1) * pairs_per_slab, :, t * h_n:(t + 1) * h_n, :] = (
                blk.reshape(pairs_per_slab, 2, h_n, c_n))

    row = lax.broadcasted_iota(jnp.int32, (n_blk, sw), 0)
    nt = (((1,), (1,)), ((), ()))
    tn = (((0,), (0,)), ((), ()))

    def local_states(pr, _):
        xp = xs_ref[pr].reshape(2 * SSM_CK, c_n)
        loc = lax.dot_general(xp, e2_ref[pr], tn, preferred_element_type=F32)
        loc_ref[pr, 0] = loc[:, :sw]
        loc_ref[pr, 1] = loc[:, sw:]
        return 0
    lax.fori_loop(0, SSM_PAIRS, local_states, 0, unroll=4)

    def chunk_scan(pr, _):
        slot = pr % 2
        power = lambda k: (apsr_ref[pr, k:k + 1, :], apsi_ref[pr, k:k + 1, :])
        zr, zi = [], []
        for lo in range(blk_n):
            rows = pl.ds(lo, n_blk, stride=blk_n)
            xr, xi = loc_ref[pr, 0, rows, :], loc_ref[pr, 1, rows, :]
            if lo:
                dr, di = _cmul(zr[-1], zi[-1], *power(0))
                xr, xi = xr + dr, xi + di
            zr.append(xr)
            zi.append(xi)
        er, ei = zr[-1], zi[-1]
        s = 1
        while s < n_blk:
            dr, di = _cmul(_shift_rows(er, s, row), _shift_rows(ei, s, row),
                           *power(log_blk + s.bit_length() - 1))
            er, ei = er + dr, ei + di
            s *= 2
        cr, ci = carry_ref[pr, 0:1, :], carry_ref[pr, 1:2, :]
        hr, hi = _cmul(apbr_ref[pr], apbi_ref[pr], cr, ci)
        br, bi = _shift_rows(er, 1, row) + hr, _shift_rows(ei, 1, row) + hi
        for lo in range(blk_n):
            sr, si = _cmul(br, bi, aplr_ref[pr, lo:lo + 1, :], apli_ref[pr, lo:lo + 1, :])
            if lo:
                sr, si = sr + zr[lo - 1], si + zi[lo - 1]
            rows = pl.ds(lo, n_blk, stride=blk_n)
            sin_ref[slot, 0, rows, :] = sr
            sin_ref[slot, 1, rows, :] = si
        sc_ref[pr, :, :sw] = sin_ref[slot, 0].astype(BF16)
        sc_ref[pr, :, sw:] = sin_ref[slot, 1].astype(BF16)
        nr, ni = _cmul(cr, ci, *power(SSM_LOG_STEPS))
        carry_ref[pr, 0:1, :] = er[n_blk - 1:n_blk, :] + nr
        carry_ref[pr, 1:2, :] = ei[n_blk - 1:n_blk, :] + ni
        return 0
    lax.fori_loop(0, SSM_PAIRS, chunk_scan, 0, unroll=2)

    def outputs(g, _):
        pr = g // 2
        yg = (_dot(w2_ref[g, :, :SSM_CK], xs_ref[pr, g % 2])
              + lax.dot_general(w2_ref[g, :, SSM_CK:], sc_ref[pr], nt, preferred_element_type=F32))
        yt_ref[:, pl.ds(pl.multiple_of(g * h_n, h_n), h_n), :] = yg.reshape(t_n, h_n, c_n)
        return 0
    lax.fori_loop(0, SSM_GROUPS, outputs, 0, unroll=4)

    for t in range(t_n):
        for j in range(n_slab):
            sl = slice(j * LANES, (j + 1) * LANES)
            col = t * SSM_WIDTH + j * LANES
            y_ref[:, col:col + LANES] = (
                yt_ref[t, sl, :].T + d_ref[:, sl] * u_ref[:, col:col + LANES].astype(F32)).astype(BF16)


def _ssm_scan(u_rows, bsz, d_skip, ops):
    g, p, c_n = SSM_GROUPS, SSM_STATE, SSM_CHUNKS_PER_TILE
    tiles = u_rows.shape[0] // (bsz * c_n)
    tile = pl.BlockSpec((c_n, SSM_ROW), lambda b, i: (b * tiles + i, 0))
    return pl.pallas_call(
        _ssm_scan_kernel,
        grid=(bsz, tiles),
        in_specs=[tile, _const_spec((1, SSM_WIDTH))] + [_const_spec(op.shape) for op in ops],
        out_specs=tile,
        out_shape=jax.ShapeDtypeStruct(u_rows.shape, BF16),
        scratch_shapes=[pltpu.VMEM((SSM_PAIRS, 2, SSM_CK, c_n), BF16),
                        pltpu.VMEM((SSM_PAIRS, c_n, 4 * p), BF16),
                        pltpu.VMEM((SSM_CHUNK, SSM_WIDTH, c_n), F32),
                        pltpu.VMEM((SSM_PAIRS, SUBLANES, 2 * p), F32),
                        pltpu.VMEM((SSM_PAIRS, 2, c_n, 2 * p), F32),
                        pltpu.VMEM((2, 2, c_n, 2 * p), F32)],
        compiler_params=_params("parallel", "arbitrary"),
        name="ssm_scan",
    )(u_rows, d_skip.reshape(1, SSM_WIDTH), *ops)


def _attn_kernel(q_ref, k_ref, v_ref, o_ref, lse_ref):
    step = pl.program_id(2)
    qb, nk = ATTN_QB, ATTN_QB + WINDOW_KEYS
    row = lax.broadcasted_iota(jnp.int32, (qb, nk), 0)
    col = lax.broadcasted_iota(jnp.int32, (qb, nk), 1)
    lane = lax.broadcasted_iota(jnp.int32, (qb, LANES), 1)
    first_head = lane < HEAD_DIM
    nt = (((1,), (1,)), ((), ()))
    n_res, n_rows = o_ref.shape[0], o_ref.shape[1]
    for res in range(n_res):
        for sb in range(n_rows // qb):
            rows = slice(sb * qb, (sb + 1) * qb)
            q_start = step * n_rows + sb * qb
            back = jnp.minimum(q_start, WINDOW_KEYS)
            k_start = pl.multiple_of(q_start - back, ATTN_QB)
            dist = row + back - col
            valid = (dist >= 0) & (dist <= WINDOW_KEYS)
            for pair in range(GROUP_WIDTH // LANES):
                cols = slice(pair * LANES, (pair + 1) * LANES)
                qp = q_ref[res, rows, cols]
                kp = k_ref[res, pl.ds(k_start, nk), cols]
                vp = v_ref[res, pl.ds(k_start, nk), cols]
                outs, lses = [], []
                for sel in (first_head, ~first_head):
                    qm = jnp.where(sel, qp, jnp.zeros_like(qp))
                    s = lax.dot_general(qm, kp, nt, preferred_element_type=F32)
                    s = jnp.where(valid, s, NEG_BIG)
                    m = jnp.max(s, axis=-1, keepdims=True)
                    e = jnp.exp(s - m)
                    den = jnp.sum(e, axis=-1, keepdims=True)
                    outs.append(_dot(e.astype(BF16), vp) / den)
                    lses.append(m + jnp.log(den))
                o_ref[res, rows, cols] = jnp.where(first_head, outs[0], outs[1]).astype(BF16)
                lse_ref[res, rows, cols] = jnp.where(first_head, lses[0], lses[1])


def _attn_group(q4, k4, v4):
    bsz, dil, lr, _ = q4.shape
    rows = min(ATTN_STEP_ROWS, lr)
    n_res = ATTN_STEP_ROWS // rows
    q_spec = pl.BlockSpec((None, n_res, rows, GROUP_WIDTH), lambda b, r, i: (b, r, i, 0))
    kv_spec = pl.BlockSpec((None, n_res, lr, GROUP_WIDTH), lambda b, r, i: (b, r, 0, 0))
    return pl.pallas_call(
        _attn_kernel,
        grid=(bsz, dil // n_res, lr // rows),
        in_specs=[q_spec, kv_spec, kv_spec],
        out_specs=[q_spec, q_spec],
        out_shape=[jax.ShapeDtypeStruct(q4.shape, BF16), jax.ShapeDtypeStruct(q4.shape, F32)],
        compiler_params=_params("parallel", "parallel", "arbitrary"),
        name=f"attn_d{dil}",
    )(q4, k4, v4)


def _load_token_major(stage_ref, in_ref, tok0, ntok, lanes):
    dil = in_ref.shape[0]
    first, rows = tok0 // dil, ntok // dil
    if dil == 1:
        return in_ref[0, first:first + rows, lanes].astype(F32)
    for r in range(dil):
        stage_ref[pl.ds(r, rows, stride=dil), :] = in_ref[r, first:first + rows, lanes].astype(F32)
    return stage_ref[...]


def _merge_kernel(x_ref, ys_ref, gluw_ref, glub_ref, wa_ref,
                  o0_ref, o1_ref, o2_ref, l0_ref, l1_ref, l2_ref, wb_ref,
                  ga_ref, gb_ref, wout_ref, h_ref, stage_ref, y_ref, ya_ref, attn_ref, mix_ref):
    pieces = lambda width: [slice(c * COL_TILE, (c + 1) * COL_TILE) for c in range(width // COL_TILE)]
    for s in range(x_ref.shape[0] // SUB_TILE):
        tok0 = s * SUB_TILE
        rows = slice(tok0, tok0 + SUB_TILE)
        stage = stage_ref.at[s]
        y_ref[s] = jax.nn.gelu(_load_chunk_rows(stage, ys_ref, SSM_WIDTH, tok0 // SSM_CHUNK, SUB_TILE // SSM_CHUNK))
        y_bf = y_ref[s].astype(BF16)
        for cols in pieces(SSM_WIDTH):
            gate = jax.nn.sigmoid(_dot(y_bf, gluw_ref[:, cols]) + glub_ref[:, cols])
            ya_ref[s, :, cols] = (y_ref[s, :, cols] * gate).astype(BF16)

        for j in range(GROUP_WIDTH // LANES):
            lanes = slice(j * LANES, (j + 1) * LANES)
            o, l = ([_load_token_major(stage, ref, tok0, SUB_TILE, lanes) for ref in refs]
                    for refs in ((o0_ref, o1_ref, o2_ref), (l0_ref, l1_ref, l2_ref)))
            top = jnp.maximum(jnp.maximum(l[0], l[1]), l[2])
            w = [jnp.exp(lg - top) for lg in l]
            attn_ref[s, :, lanes] = ((w[0] * o[0] + w[1] * o[1] + w[2] * o[2])
                                     / (w[0] + w[1] + w[2])).astype(BF16)

        for cols in pieces(D_MODEL):
            mix = (ga_ref[rows, cols].astype(F32) * _dot(ya_ref[s], wa_ref[:, cols])
                   + gb_ref[rows, cols].astype(F32) * _dot(attn_ref[s], wb_ref[:, cols]))
            mix_ref[s, :, cols] = mix.astype(BF16)
        for cols in pieces(D_MODEL):
            h_ref[rows, cols] = x_ref[rows, cols] + _dot(mix_ref[s], wout_ref[:, cols])


def _merge(x2, seq, ys_rows, glu_w, glu_b, w_a, attn_outs, w_b, ga, gb, w_out):
    n = x2.shape[0]
    tm = TOKEN_TILE
    tiles_per_seq = seq // tm
    row = lambda w: pl.BlockSpec((tm, w), lambda i: (i, 0))
    res = lambda d: pl.BlockSpec((None, d, tm // d, GROUP_WIDTH),
                                 lambda i: (i // tiles_per_seq, 0, i % tiles_per_seq, 0))
    (o0, l0), (o1, l1), (o2, l2) = attn_outs
    d0, d1, d2 = DILATIONS
    return pl.pallas_call(
        _merge_kernel,
        grid=(n // tm,),
        in_specs=[row(D_MODEL), pl.BlockSpec((tm // SSM_CHUNK, SSM_ROW), lambda i: (i, 0)),
                  _const_spec((SSM_WIDTH, SSM_WIDTH)), _const_spec((1, SSM_WIDTH)),
                  _const_spec((SSM_WIDTH, D_MODEL)),
                  res(d0), res(d1), res(d2), res(d0), res(d1), res(d2),
                  _const_spec((GROUP_WIDTH, D_MODEL)), row(D_MODEL), row(D_MODEL),
                  _const_spec((D_MODEL, D_MODEL))],
        out_specs=row(D_MODEL),
        out_shape=jax.ShapeDtypeStruct((n, D_MODEL), F32),
        scratch_shapes=[pltpu.VMEM((tm // SUB_TILE, SUB_TILE, LANES), F32),
                        pltpu.VMEM((tm // SUB_TILE, SUB_TILE, SSM_WIDTH), F32),
                        pltpu.VMEM((tm // SUB_TILE, SUB_TILE, SSM_WIDTH), BF16),
                        pltpu.VMEM((tm // SUB_TILE, SUB_TILE, GROUP_WIDTH), BF16),
                        pltpu.VMEM((tm // SUB_TILE, SUB_TILE, D_MODEL), BF16)],
        compiler_params=_params("parallel"),
        name="merge",
    )(x2, ys_rows, glu_w.astype(BF16), glu_b.reshape(1, SSM_WIDTH),
      w_a.astype(BF16), o0, o1, o2, l0, l1, l2, w_b.astype(BF16), ga, gb, w_out.astype(BF16))


def _cast_weight_rows(step, src_refs, dst_refs):
    for src, dst in zip(src_refs, dst_refs):
        rb = src.shape[0]
        dst[pl.ds(pl.multiple_of(step * rb, rb), rb), :] = src[...].astype(BF16)


def _ffn_kernel(h_ref, g2_ref, wg32_ref, wu32_ref, cw_ref, cb_ref, wd32_ref, g3_ref, wpg32_ref,
                p_ref, wpp32_ref, gf_ref, out_ref, wg_ref, wu_ref, wd_ref, wpg_ref, wpp_ref,
                act_ref, carry_ref, *, tiles_per_seq):
    step = pl.program_id(0)

    @pl.when(step < WEIGHT_CAST_STEPS)
    def _():
        _cast_weight_rows(step, (wg32_ref, wu32_ref, wd32_ref, wpg32_ref, wpp32_ref),
                          (wg_ref, wu_ref, wd_ref, wpg_ref, wpp_ref))

    @pl.when(step >= WEIGHT_CAST_STEPS)
    def _():
        _ffn_tile(step - WEIGHT_CAST_STEPS, h_ref, g2_ref, wg_ref, wu_ref, cw_ref, cb_ref, wd_ref, g3_ref, wpg_ref,
                  p_ref, wpp_ref, gf_ref, out_ref, act_ref, carry_ref, tiles_per_seq)


def _ffn_tile(tile, h_ref, g2_ref, wg_ref, wu_ref, cw_ref, cb_ref, wd_ref, g3_ref, wpg_ref,
              p_ref, wpp_ref, gf_ref, out_ref, act_ref, carry_ref, tiles_per_seq):
    tm = h_ref.shape[0]

    @pl.when(tile % tiles_per_seq == 0)
    def _():
        carry_ref[...] = jnp.zeros_like(carry_ref)

    subs = [slice(s * SUB_TILE, (s + 1) * SUB_TILE) for s in range(tm // SUB_TILE)]
    hs = [h_ref[rows, :] for rows in subs]
    u2s = [_rms(h, g2_ref[...]).astype(BF16) for h in hs]
    row = lax.broadcasted_iota(jnp.int32, (SUBLANES, FFN_CHUNK), 0)
    for c in range(D_FF // FFN_CHUNK):
        sl = slice(c * FFN_CHUNK, (c + 1) * FFN_CHUNK)
        prev = carry_ref[:, sl]
        for rows, u2 in zip(subs, u2s):
            gp = _dot(u2, wg_ref[:, sl])
            up = _dot(u2, wu_ref[:, sl])
            r1 = pltpu.roll(gp, 1, 0)
            r2 = pltpu.roll(gp, 2, 0)
            r1 = jnp.concatenate([jnp.where(row < 1, pltpu.roll(prev, 1, 0), r1[:SUBLANES]), r1[SUBLANES:]], axis=0)
            r2 = jnp.concatenate([jnp.where(row < 2, pltpu.roll(prev, 2, 0), r2[:SUBLANES]), r2[SUBLANES:]], axis=0)
            gate = cw_ref[0:1, sl] * r2 + cw_ref[1:2, sl] * r1 + cw_ref[2:3, sl] * gp + cb_ref[:, sl]
            act_ref[rows, sl] = (jax.nn.gelu(gate) * up).astype(BF16)
            prev = gp[SUB_TILE - SUBLANES:, :]
        carry_ref[:, sl] = prev
    for rows, h in zip(subs, hs):
        h = h + _dot(act_ref[rows, :], wd_ref[...])
        u3 = _rms(h, g3_ref[...]).astype(BF16)
        h = h + jax.nn.sigmoid(_dot(u3, wpg_ref[...])) * _dot(p_ref[rows, :].astype(BF16), wpp_ref[...])
        out_ref[rows, :] = _rms(h, gf_ref[...])


def _ffn(h1, seq, p2, norm_g, w_gate, w_up, conv_w, conv_b, w_down, ple_g, ple_w_gate, ple_w_proj, final_g):
    n = h1.shape[0]
    tm = TOKEN_TILE
    k = WEIGHT_CAST_STEPS
    row = lambda w: pl.BlockSpec((tm, w), lambda i: (jnp.maximum(i - k, 0), 0))
    wrows = lambda a: pl.BlockSpec((a.shape[0] // k, a.shape[1]), lambda i: (jnp.minimum(i, k - 1), 0))
    resident = lambda a: pltpu.VMEM(a.shape, BF16)
    vec = lambda a: a.reshape(1, -1)
    weights = (w_gate, w_up, w_down, ple_w_gate, ple_w_proj)
    return pl.pallas_call(
        functools.partial(_ffn_kernel, tiles_per_seq=seq // tm),
        grid=(k + n // tm,),
        in_specs=[row(D_MODEL), _const_spec((1, D_MODEL)), wrows(w_gate),
                  wrows(w_up), _const_spec((CONV_WIDTH, D_FF)), _const_spec((1, D_FF)),
                  wrows(w_down), _const_spec((1, D_MODEL)), wrows(ple_w_gate),
                  row(PLE_DIM), wrows(ple_w_proj), _const_spec((1, D_MODEL))],
        out_specs=row(D_MODEL),
        out_shape=jax.ShapeDtypeStruct((n, D_MODEL), F32),
        scratch_shapes=[resident(w) for w in weights]
                       + [pltpu.VMEM((tm, D_FF), BF16), pltpu.VMEM((SUBLANES, D_FF), F32)],
        compiler_params=_params("arbitrary"),
        name="ffn",
    )(h1, vec(norm_g), w_gate, w_up, conv_w, vec(conv_b),
      w_down, vec(ple_g), ple_w_gate, p2, ple_w_proj, vec(final_g))


def _layer(h2, bsz, seq, p2, mix_norm_g, w_in, gate_b, ssm_lam_re, ssm_lam_im, ssm_log_dt, ssm_b_re,
           ssm_b_im, ssm_c_re, ssm_c_im, ssm_d, ssm_glu_w, ssm_glu_b, w_branch_a, w_branch_b, w_out,
           ffn_norm_g, ffn_w_gate, ffn_w_up, ffn_conv_w, ffn_conv_b, ffn_w_down,
           ple_norm_g, ple_w_gate, ple_w_proj, out_norm_g):
    u_rows, qkv, ga, gb = _in_proj(h2, seq, mix_norm_g, w_in, gate_b)
    ops = _ssm_operators(ssm_lam_re, ssm_lam_im, ssm_log_dt, ssm_b_re, ssm_b_im, ssm_c_re, ssm_c_im)
    ys_rows = _ssm_scan(u_rows, bsz, ssm_d.reshape(-1), ops)
    attn_outs = [_attn_group(*group) for group in qkv]
    h1 = _merge(h2, seq, ys_rows, ssm_glu_w, ssm_glu_b, w_branch_a, attn_outs, w_branch_b, ga, gb, w_out)
    return _ffn(h1, seq, p2, ffn_norm_g, ffn_w_gate, ffn_w_up, ffn_conv_w, ffn_conv_b, ffn_w_down,
                ple_norm_g, ple_w_gate, ple_w_proj, out_norm_g)


def kernel(x, p, mix_norm_g, w_in, gate_b, ssm_lam_re, ssm_lam_im, ssm_log_dt, ssm_b_re, ssm_b_im, ssm_c_re, ssm_c_im, ssm_d, ssm_glu_w, ssm_glu_b, w_branch_a, w_branch_b, w_out, ffn_norm_g, ffn_w_gate, ffn_w_up, ffn_conv_w, ffn_conv_b, ffn_w_down, ple_norm_g, ple_w_gate, ple_w_proj, final_norm_g):
    bsz, seq, _ = x.shape
    depth = p.shape[0]
    assert depth == 1, "the final norm is fused into the layer's last kernel"
    h2 = x.reshape(bsz * seq, D_MODEL)
    out = _layer(h2, bsz, seq, p[0].reshape(bsz * seq, PLE_DIM), mix_norm_g[0], w_in[0], gate_b[0],
                 ssm_lam_re[0], ssm_lam_im[0], ssm_log_dt[0], ssm_b_re[0], ssm_b_im[0], ssm_c_re[0],
                 ssm_c_im[0], ssm_d[0], ssm_glu_w[0], ssm_glu_b[0], w_branch_a[0], w_branch_b[0],
                 w_out[0], ffn_norm_g[0], ffn_w_gate[0], ffn_w_up[0], ffn_conv_w[0], ffn_conv_b[0],
                 ffn_w_down[0], ple_norm_g[0], ple_w_gate[0], ple_w_proj[0], final_norm_g)
    return out.reshape(bsz, seq, D_MODEL)
```

```python
import functools

import jax
import jax.numpy as jnp
from jax import lax
from jax.experimental import pallas as pl
from jax.experimental.pallas import tpu as pltpu

F32 = jnp.float32
BF16 = jnp.bfloat16

D_MODEL = 1024
EPS = 1e-6
PLE_DIM = 256
SSM_GROUP = 16
SSM_STATE = 64
SSM_WIDTH = 512
SSM_GROUPS = SSM_WIDTH // SSM_GROUP
HEAD_DIM = 64
DILATIONS = (1, 4, 16)
WINDOW_KEYS = 128
HEADS_PER_GROUP = 4
GROUP_WIDTH = HEADS_PER_GROUP * HEAD_DIM
ATTN_WIDTH = len(DILATIONS) * GROUP_WIDTH
ROT_DIM = HEAD_DIM // 4
ROPE_THETA = 500000.0
NEG_BIG = -1e30
D_FF = 2816
CONV_WIDTH = 3
OFF_Q = SSM_WIDTH
OFF_K = OFF_Q + ATTN_WIDTH
OFF_V = OFF_K + ATTN_WIDTH
OFF_GA = OFF_V + ATTN_WIDTH
OFF_GB = OFF_GA + D_MODEL
IN_WIDTH = OFF_GB + D_MODEL

LANES = 128
SUBLANES = 8
VMEM_LIMIT_BYTES = 56 * 1024 * 1024

TOKEN_TILE = 1024
SUB_TILE = 256
COL_TILE = 256
SSM_CHUNK = 16
SSM_CHUNKS_PER_TILE = 256
SSM_TILE = SSM_CHUNK * SSM_CHUNKS_PER_TILE
SSM_CK = SSM_CHUNK * SSM_GROUP
SSM_ROW = SSM_CHUNK * SSM_WIDTH
SSM_PAIRS = SSM_GROUPS // 2
SSM_LOG_STEPS = 8
ATTN_QB = 128
ATTN_STEP_ROWS = 2048
FFN_CHUNK = 256
WEIGHT_CAST_STEPS = 8


def _dot(a, b):
    return jnp.dot(a, b, preferred_element_type=F32)


def _rms(x, g):
    var = jnp.mean(x * x, axis=-1, keepdims=True)
    return x * lax.rsqrt(var + EPS) * g


def _const_spec(shape):
    nd = len(shape)
    return pl.BlockSpec(shape, lambda *_: (0,) * nd, pipeline_mode=pl.Buffered(1))


def _params(*sem):
    return pltpu.CompilerParams(dimension_semantics=sem, vmem_limit_bytes=VMEM_LIMIT_BYTES)


def _rope(z, cos, sin_lo, sin_hi):
    up = pltpu.roll(z, LANES - ROT_DIM // 2, 1)
    dn = pltpu.roll(z, ROT_DIM // 2, 1)
    return z * cos + up * sin_lo + dn * sin_hi


def _store_residue_major(stage_ref, out_ref, z, dil, tok0):
    rows, first = z.shape[0] // dil, tok0 // dil
    for j in range(z.shape[1] // LANES):
        sl = slice(j * LANES, (j + 1) * LANES)
        stage_ref[...] = z[:, sl]
        for r in range(dil):
            out_ref[r, first:first + rows, sl] = stage_ref[pl.ds(r, rows, stride=dil), :].astype(out_ref.dtype)


def _store_chunk_rows(stage_ref, out_ref, z, c0):
    rows, width = z.shape[0] // SSM_CHUNK, z.shape[1]
    for j in range(width // LANES):
        stage_ref[...] = z[:, j * LANES:(j + 1) * LANES]
        for t in range(SSM_CHUNK):
            col = t * width + j * LANES
            out_ref[c0:c0 + rows, col:col + LANES] = (
                stage_ref[pl.ds(t, rows, stride=SSM_CHUNK), :].astype(out_ref.dtype))


def _load_chunk_rows(stage_ref, in_ref, width, c0, rows):
    slabs = []
    for j in range(width // LANES):
        for t in range(SSM_CHUNK):
            col = t * width + j * LANES
            stage_ref[pl.ds(t, rows, stride=SSM_CHUNK), :] = in_ref[c0:c0 + rows, col:col + LANES].astype(F32)
        slabs.append(stage_ref[...])
    return jnp.concatenate(slabs, axis=1)


def _in_proj_kernel(x_ref, g_ref, ca_ref, sa_ref, cb_ref, sb_ref, sign_ref, w32_ref, bg_ref,
                    s_ref, q0_ref, k0_ref, v0_ref, q1_ref, k1_ref, v1_ref, q2_ref, k2_ref, v2_ref,
                    ga_ref, gb_ref, w_ref, stage_ref):
    step = pl.program_id(0)

    @pl.when(step < WEIGHT_CAST_STEPS)
    def _():
        _cast_weight_rows(step, (w32_ref,), (w_ref,))

    @pl.when(step >= WEIGHT_CAST_STEPS)
    def _():
        _in_proj_tile(x_ref, g_ref, ca_ref, sa_ref, cb_ref, sb_ref, sign_ref, w_ref, bg_ref,
                      s_ref, q0_ref, k0_ref, v0_ref, q1_ref, k1_ref, v1_ref, q2_ref, k2_ref, v2_ref,
                      ga_ref, gb_ref, stage_ref)


def _in_proj_tile(x_ref, g_ref, ca_ref, sa_ref, cb_ref, sb_ref, sign_ref, w_ref, bg_ref,
                  s_ref, q0_ref, k0_ref, v0_ref, q1_ref, k1_ref, v1_ref, q2_ref, k2_ref, v2_ref,
                  ga_ref, gb_ref, stage_ref):
    u = _rms(x_ref[...], g_ref[...]).astype(BF16)
    _store_chunk_rows(stage_ref, s_ref, _dot(u, w_ref[:, :OFF_Q]), 0)
    ca, sa, cb, sb = ca_ref[...], sa_ref[...], cb_ref[...], sb_ref[...]
    cos = ca * cb - sa * sb
    sin = sa * cb + ca * sb
    slo, shi = sin * sign_ref[0:1, :], sin * sign_ref[1:2, :]
    scale = HEAD_DIM ** -0.5

    def rope(z):
        return jnp.concatenate([_rope(z[:, j * LANES:(j + 1) * LANES], cos, slo, shi)
                                for j in range(z.shape[1] // LANES)], axis=1)

    q = rope(_dot(u, w_ref[:, OFF_Q:OFF_K])) * scale
    k = rope(_dot(u, w_ref[:, OFF_K:OFF_V]))
    v = _dot(u, w_ref[:, OFF_V:OFF_GA])
    for z, refs in ((q, (q0_ref, q1_ref, q2_ref)), (k, (k0_ref, k1_ref, k2_ref)), (v, (v0_ref, v1_ref, v2_ref))):
        refs[0][...] = z[:, :GROUP_WIDTH].astype(BF16)
        for grp in (1, 2):
            _store_residue_major(stage_ref, refs[grp], z[:, grp * GROUP_WIDTH:(grp + 1) * GROUP_WIDTH],
                                 DILATIONS[grp], 0)
    ga_ref[...] = jax.nn.sigmoid(_dot(u, w_ref[:, OFF_GA:OFF_GB]) + bg_ref[:, :D_MODEL]).astype(BF16)
    gb_ref[...] = jax.nn.sigmoid(_dot(u, w_ref[:, OFF_GB:]) + bg_ref[:, D_MODEL:]).astype(BF16)


def _rope_tables(seq, tm):
    half = ROT_DIM // 2
    freqs = ROPE_THETA ** (-jnp.arange(half, dtype=F32) * (2.0 / ROT_DIM))
    head = jnp.concatenate([freqs, freqs, jnp.zeros((HEAD_DIM - ROT_DIM,), F32)])
    lane_freq = jnp.tile(head, LANES // HEAD_DIM)[None, :]
    base = jnp.arange(0, seq, tm, dtype=F32)[:, None] * lane_freq
    offs = jnp.arange(tm, dtype=F32)[:, None] * lane_freq
    in_head = jnp.arange(LANES) % HEAD_DIM
    sign = jnp.zeros((SUBLANES, LANES), F32)
    sign = sign.at[0].set(jnp.where(in_head < half, -1.0, 0.0))
    sign = sign.at[1].set(jnp.where((in_head >= half) & (in_head < ROT_DIM), 1.0, 0.0))
    n_tiles = seq // tm
    return (jnp.cos(base).reshape(n_tiles, 1, LANES), jnp.sin(base).reshape(n_tiles, 1, LANES),
            jnp.cos(offs), jnp.sin(offs), sign)


def _in_proj(x2, seq, norm_g, w_in, gate_b):
    n = x2.shape[0]
    tm = TOKEN_TILE
    tiles_per_seq = seq // tm
    bsz = n // seq
    cos_a, sin_a, cos_b, sin_b, sign = _rope_tables(seq, tm)

    k = WEIGHT_CAST_STEPS
    tile = lambda i: jnp.maximum(i - k, 0)
    row = lambda w: pl.BlockSpec((tm, w), lambda i: (tile(i), 0))
    tile_tab = pl.BlockSpec((None, 1, LANES), lambda i: (tile(i) % tiles_per_seq, 0, 0))
    res = lambda d: pl.BlockSpec((None, d, tm // d, GROUP_WIDTH),
                                 lambda i: (tile(i) // tiles_per_seq, 0, tile(i) % tiles_per_seq, 0))
    res_shape = lambda d: jax.ShapeDtypeStruct((bsz, d, seq // d, GROUP_WIDTH), BF16)
    nat_shape = jax.ShapeDtypeStruct((n, GROUP_WIDTH), BF16)
    d1, d2 = DILATIONS[1], DILATIONS[2]
    outs = pl.pallas_call(
        _in_proj_kernel,
        grid=(k + n // tm,),
        in_specs=[row(D_MODEL), _const_spec((1, D_MODEL)), tile_tab, tile_tab,
                  _const_spec((tm, LANES)), _const_spec((tm, LANES)), _const_spec((SUBLANES, LANES)),
                  pl.BlockSpec((D_MODEL // k, IN_WIDTH), lambda i: (jnp.minimum(i, k - 1), 0)),
                  _const_spec((1, 2 * D_MODEL))],
        out_specs=[pl.BlockSpec((tm // SSM_CHUNK, SSM_ROW), lambda i: (tile(i), 0))]
                  + [row(GROUP_WIDTH)] * 3 + [res(d1)] * 3 + [res(d2)] * 3 + [row(D_MODEL), row(D_MODEL)],
        out_shape=[jax.ShapeDtypeStruct((n // SSM_CHUNK, SSM_ROW), BF16)] + [nat_shape] * 3
                  + [res_shape(d1)] * 3 + [res_shape(d2)] * 3
                  + [jax.ShapeDtypeStruct((n, D_MODEL), BF16), jax.ShapeDtypeStruct((n, D_MODEL), BF16)],
        scratch_shapes=[pltpu.VMEM((D_MODEL, IN_WIDTH), BF16), pltpu.VMEM((tm, LANES), F32)],
        compiler_params=_params("arbitrary"),
        name="in_proj",
    )(x2, norm_g.reshape(1, D_MODEL), cos_a, sin_a, cos_b, sin_b, sign, w_in, gate_b.reshape(1, 2 * D_MODEL))
    u, q0, k0, v0, q1, k1, v1, q2, k2, v2, ga, gb = outs
    nat4 = lambda a: a.reshape(bsz, 1, seq, GROUP_WIDTH)
    qkv = ((nat4(q0), nat4(k0), nat4(v0)), (q1, k1, v1), (q2, k2, v2))
    return u, qkv, ga, gb


def _cmul(ar, ai, br, bi):
    return ar * br - ai * bi, ar * bi + ai * br


def _discretise(lr, li, dt):
    mag = jnp.exp(lr * dt)
    ar = mag * jnp.cos(li * dt)
    ai = mag * jnp.sin(li * dt)
    den = lr * lr + li * li
    cr = ((ar - 1.0) * lr + ai * li) / den
    ci = (ai * lr - (ar - 1.0) * li) / den
    return ar, ai, cr, ci


def _ssm_prep_kernel(lr_ref, li_ref, ldt_ref, brt_ref, bit_ref, cre_ref, cim_ref, lrf_ref, lif_ref, ldtf_ref,
                     w2_ref, e2_ref, apsr_ref, apsi_ref, aplr_ref, apli_ref, apbr_ref, apbi_ref, er_ref, ei_ref):
    g_n, t_n, h_n, sw = SSM_GROUPS, SSM_CHUNK, SSM_GROUP, 2 * SSM_STATE
    ar, ai, cr, ci = _discretise(lr_ref[...], li_ref[...], jnp.exp(ldt_ref[...]))
    brt, bit = brt_ref[...], bit_ref[...]
    bbr = cr * brt - ci * bit
    bbi = cr * bit + ci * brt
    cre, cim = cre_ref[...], cim_ref[...]
    by_group = lambda a: a.reshape(g_n, h_n, sw)
    pr, pi = jnp.ones_like(ar), jnp.zeros_like(ai)
    for j in range(t_n):
        rows = slice((t_n - 1 - j) * h_n, (t_n - j) * h_n)
        rr, ri = _cmul(pr, pi, bbr, bbi)
        er_ref[:, rows, :] = by_group(rr)
        ei_ref[:, rows, :] = by_group(ri)
        e2_ref[:, rows, :sw] = by_group(rr).astype(BF16)
        e2_ref[:, rows, sw:] = by_group(ri).astype(BF16)
        pr, pi = _cmul(pr, pi, ar, ai)
        rows = slice(j * h_n, (j + 1) * h_n)
        w2_ref[:, rows, SSM_CK:SSM_CK + sw] = by_group(cre * pr - cim * pi).astype(BF16)
        w2_ref[:, rows, SSM_CK + sw:] = by_group(-cre * pi - cim * pr).astype(BF16)

    nt = (((1,), (1,)), ((), ()))
    hi = lax.Precision.HIGHEST

    def toeplitz(g, _):
        rows = pl.ds(pl.multiple_of(g * h_n, h_n), h_n)
        krev = (lax.dot_general(cre_ref[rows, :], er_ref[g], nt, precision=hi, preferred_element_type=F32)
                - lax.dot_general(cim_ref[rows, :], ei_ref[g], nt, precision=hi, preferred_element_type=F32))
        kext = jnp.concatenate([krev, jnp.zeros_like(krev)], axis=1)
        for t in range(t_n):
            off = (t_n - 1 - t) * h_n
            win = kext if off == 0 else pltpu.roll(kext, 2 * SSM_CK - off, 1)
            w2_ref[g, t * h_n:(t + 1) * h_n, :SSM_CK] = win[:, :SSM_CK].astype(BF16)
        return 0
    lax.fori_loop(0, g_n, toeplitz, 0, unroll=4)

    acr, aci, _, _ = _discretise(lrf_ref[...], lif_ref[...], jnp.exp(ldtf_ref[...]))
    for _ in range(4):
        acr, aci = _cmul(acr, aci, acr, aci)
    shape = (SUBLANES, acr.shape[1])
    row = lax.broadcasted_iota(jnp.int32, shape, 0)
    qr, qi = jnp.ones(shape, F32), jnp.zeros(shape, F32)
    apsr_ref[...] = jnp.zeros_like(apsr_ref)
    apsi_ref[...] = jnp.zeros_like(apsi_ref)
    for k in range(SSM_LOG_STEPS + 1):
        for p2 in range(SSM_PAIRS):
            apsr_ref[p2, k:k + 1, :] = acr[:, p2 * sw:(p2 + 1) * sw]
            apsi_ref[p2, k:k + 1, :] = aci[:, p2 * sw:(p2 + 1) * sw]
        if (1 << k) < SUBLANES:
            nr, ni = _cmul(qr, qi, acr, aci)
            bit_set = (row & (1 << k)) != 0
            qr, qi = jnp.where(bit_set, nr, qr), jnp.where(bit_set, ni, qi)
        if (1 << k) == SUBLANES:
            a8r, a8i = acr, aci
        acr, aci = _cmul(acr, aci, acr, aci)
    br, bi = jnp.ones_like(a8r), jnp.zeros_like(a8i)
    blk_r, blk_i = [], []
    for _ in range(SSM_CHUNKS_PER_TILE // SUBLANES):
        blk_r.append(br)
        blk_i.append(bi)
        br, bi = _cmul(br, bi, a8r, a8i)
    blk_r, blk_i = jnp.concatenate(blk_r, axis=0), jnp.concatenate(blk_i, axis=0)
    for p2 in range(SSM_PAIRS):
        lanes = slice(p2 * sw, (p2 + 1) * sw)
        aplr_ref[p2], apli_ref[p2] = qr[:, lanes], qi[:, lanes]
        apbr_ref[p2], apbi_ref[p2] = blk_r[:, lanes], blk_i[:, lanes]


def _pad_pair_lanes(a):
    z = jnp.zeros_like(a)
    even = (jnp.arange(a.shape[0]) % 2 == 0)[:, None, None]
    padded = jnp.where(even, jnp.concatenate([a, z], -1), jnp.concatenate([z, a], -1))
    return padded.reshape(a.shape[0] * a.shape[1], 2 * a.shape[2])


def _ssm_operators(lam_re, lam_im, log_dt, b_re, b_im, c_re, c_im):
    g, p, h = SSM_GROUPS, SSM_STATE, SSM_GROUP
    gp, sw = g * p, 2 * p
    rep = lambda a: jnp.repeat(jnp.tile(a, (1, 2)), h, axis=0)
    ldt2 = jnp.broadcast_to(log_dt[:, None], (g, p))
    full = lambda shape: pl.BlockSpec(shape, lambda: (0,) * len(shape))
    in_arrays = (rep(lam_re), rep(lam_im), rep(ldt2),
                 _pad_pair_lanes(b_re.transpose(0, 2, 1)), _pad_pair_lanes(b_im.transpose(0, 2, 1)),
                 _pad_pair_lanes(c_re), _pad_pair_lanes(c_im),
                 lam_re.reshape(1, gp), lam_im.reshape(1, gp), ldt2.reshape(1, gp))
    out_shapes = ([((g, SSM_CK, SSM_CK + 2 * sw), BF16), ((g, SSM_CK, 2 * sw), BF16)]
                  + [((SSM_PAIRS, 2 * SUBLANES, sw), F32)] * 2
                  + [((SSM_PAIRS, SUBLANES, sw), F32)] * 2
                  + [((SSM_PAIRS, SSM_CHUNKS_PER_TILE // SUBLANES, sw), F32)] * 2)
    w2, e2, *powers = pl.pallas_call(
        _ssm_prep_kernel,
        in_specs=[full(a.shape) for a in in_arrays],
        out_specs=[full(s) for s, _ in out_shapes],
        out_shape=[jax.ShapeDtypeStruct(s, dt) for s, dt in out_shapes],
        scratch_shapes=[pltpu.VMEM((g, SSM_CK, sw), F32), pltpu.VMEM((g, SSM_CK, sw), F32)],
        compiler_params=pltpu.CompilerParams(vmem_limit_bytes=VMEM_LIMIT_BYTES),
        name="ssm_prep",
    )(*in_arrays)
    return (w2, e2.reshape(SSM_PAIRS, 2 * SSM_CK, 2 * sw), *powers)


def _shift_rows(z, s, row):
    if s % SUBLANES == 0:
        return jnp.concatenate([jnp.zeros((s, z.shape[1]), z.dtype), z[:-s]], axis=0)
    return jnp.where(row >= s, pltpu.roll(z, s, 0), 0.0)


def _ssm_scan_kernel(u_ref, d_ref, w2_ref, e2_ref, apsr_ref, apsi_ref, aplr_ref, apli_ref, apbr_ref, apbi_ref,
                     y_ref, xs_ref, sc_ref, yt_ref, carry_ref, loc_ref, sin_ref):
    t_n, h_n, c_n = SSM_CHUNK, SSM_GROUP, SSM_CHUNKS_PER_TILE
    n_slab = SSM_WIDTH // LANES
    pairs_per_slab = LANES // (2 * h_n)
    sw = 2 * SSM_STATE
    blk_n = SUBLANES
    n_blk = c_n // blk_n
    log_blk = blk_n.bit_length() - 1

    @pl.when(pl.program_id(1) == 0)
    def _():
        carry_ref[...] = jnp.zeros_like(carry_ref)

    for t in range(t_n):
        for j in range(n_slab):
            col = t * SSM_WIDTH + j * LANES
            blk = u_ref[:, col:col + LANES].T
            xs_ref[j * pairs_per_slab:(j + 1) * pairs_per_slab, :, t * h_n:(t + 1) * h_n, :] = (
                blk.reshape(pairs_per_slab, 2, h_n, c_n))

    row = lax.broadcasted_iota(jnp.int32, (n_blk, sw), 0)
    nt = (((1,), (1,)), ((), ()))
    tn = (((0,), (0,)), ((), ()))

    def local_states(pr, _):
        xp = xs_ref[pr].reshape(2 * SSM_CK, c_n)
        loc = lax.dot_general(xp, e2_ref[pr], tn, preferred_element_type=F32)
        loc_ref[pr, 0] = loc[:, :sw]
        loc_ref[pr, 1] = loc[:, sw:]
        return 0
    lax.fori_loop(0, SSM_PAIRS, local_states, 0, unroll=4)

    def chunk_scan(pr, _):
        slot = pr % 2
        power = lambda k: (apsr_ref[pr, k:k + 1, :], apsi_ref[pr, k:k + 1, :])
        zr, zi = [], []
        for lo in range(blk_n):
            rows = pl.ds(lo, n_blk, stride=blk_n)
            xr, xi = loc_ref[pr, 0, rows, :], loc_ref[pr, 1, rows, :]
            if lo:
                dr, di = _cmul(zr[-1], zi[-1], *power(0))
                xr, xi = xr + dr, xi + di
            zr.append(xr)
            zi.append(xi)
        er, ei = zr[-1], zi[-1]
        s = 1
        while s < n_blk:
            dr, di = _cmul(_shift_rows(er, s, row), _shift_rows(ei, s, row),
                           *power(log_blk + s.bit_length() - 1))
            er, ei = er + dr, ei + di
            s *= 2
        cr, ci = carry_ref[pr, 0:1, :], carry_ref[pr, 1:2, :]
        hr, hi = _cmul(apbr_ref[pr], apbi_ref[pr], cr, ci)
        br, bi = _shift_rows(er, 1, row) + hr, _shift_rows(ei, 1, row) + hi
        for lo in range(blk_n):
            sr, si = _cmul(br, bi, aplr_ref[pr, lo:lo + 1, :], apli_ref[pr, lo:lo + 1, :])
            if lo:
                sr, si = sr + zr[lo - 1], si + zi[lo - 1]
            rows = pl.ds(lo, n_blk, stride=blk_n)
            sin_ref[slot, 0, rows, :] = sr
            sin_ref[slot, 1, rows, :] = si
        sc_ref[pr, :, :sw] = sin_ref[slot, 0].astype(BF16)
        sc_ref[pr, :, sw:] = sin_ref[slot, 1].astype(BF16)
        nr, ni = _cmul(cr, ci, *power(SSM_LOG_STEPS))
        carry_ref[pr, 0:1, :] = er[n_blk - 1:n_blk, :] + nr
        carry_ref[pr, 1:2, :] = ei[n_blk - 1:n_blk, :] + ni
        return 0
    lax.fori_loop(0, SSM_PAIRS, chunk_scan, 0, unroll=2)

    def outputs(g, _):
        pr = g // 2
        yg = (_dot(w2_ref[g, :, :SSM_CK], xs_ref[pr, g % 2])
              + lax.dot_general(w2_ref[g, :, SSM_CK:], sc_ref[pr], nt, preferred_element_type=F32))
        yt_ref[:, pl.ds(pl.multiple_of(g * h_n, h_n), h_n), :] = yg.reshape(t_n, h_n, c_n)
        return 0
    lax.fori_loop(0, SSM_GROUPS, outputs, 0, unroll=4)

    for t in range(t_n):
        for j in range(n_slab):
            sl = slice(j * LANES, (j + 1) * LANES)
            col = t * SSM_WIDTH + j * LANES
            y_ref[:, col:col + LANES] = (
                yt_ref[t, sl, :].T + d_ref[:, sl] * u_ref[:, col:col + LANES].astype(F32)).astype(BF16)


def _ssm_scan(u_rows, bsz, d_skip, ops):
    g, p, c_n = SSM_GROUPS, SSM_STATE, SSM_CHUNKS_PER_TILE
    tiles = u_rows.shape[0] // (bsz * c_n)
    tile = pl.BlockSpec((c_n, SSM_ROW), lambda b, i: (b * tiles + i, 0))
    return pl.pallas_call(
        _ssm_scan_kernel,
        grid=(bsz, tiles),
        in_specs=[tile, _const_spec((1, SSM_WIDTH))] + [_const_spec(op.shape) for op in ops],
        out_specs=tile,
        out_shape=jax.ShapeDtypeStruct(u_rows.shape, BF16),
        scratch_shapes=[pltpu.VMEM((SSM_PAIRS, 2, SSM_CK, c_n), BF16),
                        pltpu.VMEM((SSM_PAIRS, c_n, 4 * p), BF16),
                        pltpu.VMEM((SSM_CHUNK, SSM_WIDTH, c_n), F32),
                        pltpu.VMEM((SSM_PAIRS, SUBLANES, 2 * p), F32),
                        pltpu.VMEM((SSM_PAIRS, 2, c_n, 2 * p), F32),
                        pltpu.VMEM((2, 2, c_n, 2 * p), F32)],
        compiler_params=_params("parallel", "arbitrary"),
        name="ssm_scan",
    )(u_rows, d_skip.reshape(1, SSM_WIDTH), *ops)


def _attn_kernel(q_ref, kc_ref, kp_ref, vc_ref, vp_ref, o_ref, lse_ref):
    qb, nk = ATTN_QB, ATTN_QB + WINDOW_KEYS
    row = lax.broadcasted_iota(jnp.int32, (qb, nk), 0)
    col = lax.broadcasted_iota(jnp.int32, (qb, nk), 1)
    dist = row + WINDOW_KEYS - col
    in_band = (dist >= 0) & (dist <= WINDOW_KEYS)
    in_band_first = in_band & ((col >= WINDOW_KEYS) | (pl.program_id(2) > 0))
    lane = lax.broadcasted_iota(jnp.int32, (qb, LANES), 1)
    first_head = lane < HEAD_DIM
    nt = (((1,), (1,)), ((), ()))
    n_res, n_rows = o_ref.shape[0], o_ref.shape[1]
    for res in range(n_res):
        for sb in range(n_rows // qb):
            rows = slice(sb * qb, (sb + 1) * qb)
            valid = in_band if sb else in_band_first
            for pair in range(GROUP_WIDTH // LANES):
                cols = slice(pair * LANES, (pair + 1) * LANES)
                qp = q_ref[res, rows, cols]
                if sb:
                    window = slice(sb * qb - WINDOW_KEYS, (sb + 1) * qb)
                    kp, vp = kc_ref[res, window, cols], vc_ref[res, window, cols]
                else:
                    kp = jnp.concatenate([kp_ref[res, :, cols], kc_ref[res, :qb, cols]], axis=0)
                    vp = jnp.concatenate([vp_ref[res, :, cols], vc_ref[res, :qb, cols]], axis=0)
                outs, lses = [], []
                for sel in (first_head, ~first_head):
                    qm = jnp.where(sel, qp, jnp.zeros_like(qp))
                    s = lax.dot_general(qm, kp, nt, preferred_element_type=F32)
                    s = jnp.where(valid, s, NEG_BIG)
                    m = jnp.max(s, axis=-1, keepdims=True)
                    e = jnp.exp(s - m)
                    den = jnp.sum(e, axis=-1, keepdims=True)
                    outs.append(_dot(e.astype(BF16), vp) / den)
                    lses.append(m + jnp.log(den))
                o_ref[res, rows, cols] = jnp.where(first_head, outs[0], outs[1]).astype(BF16)
                lse_ref[res, rows, cols] = jnp.where(first_head, lses[0], lses[1])


def _attn_group(q4, k4, v4):
    bsz, dil, lr, _ = q4.shape
    rows = min(ATTN_STEP_ROWS, lr)
    n_res = ATTN_STEP_ROWS // rows
    back = rows // WINDOW_KEYS
    q_spec = pl.BlockSpec((None, n_res, rows, GROUP_WIDTH), lambda b, r, i: (b, r, i, 0))
    prev_spec = pl.BlockSpec((None, n_res, WINDOW_KEYS, GROUP_WIDTH),
                             lambda b, r, i: (b, r, jnp.maximum(i * back - 1, 0), 0))
    return pl.pallas_call(
        _attn_kernel,
        grid=(bsz, dil // n_res, lr // rows),
        in_specs=[q_spec, q_spec, prev_spec, q_spec, prev_spec],
        out_specs=[q_spec, q_spec],
        out_shape=[jax.ShapeDtypeStruct(q4.shape, BF16), jax.ShapeDtypeStruct(q4.shape, F32)],
        compiler_params=_params("parallel", "parallel", "arbitrary"),
        name=f"attn_d{dil}",
    )(q4, k4, k4, v4, v4)


def _load_token_major(stage_ref, in_ref, tok0, ntok, lanes):
    dil = in_ref.shape[0]
    first, rows = tok0 // dil, ntok // dil
    if dil == 1:
        return in_ref[0, first:first + rows, lanes].astype(F32)
    for r in range(dil):
        stage_ref[pl.ds(r, rows, stride=dil), :] = in_ref[r, first:first + rows, lanes].astype(F32)
    return stage_ref[...]


def _merge_kernel(x_ref, ys_ref, gluw_ref, glub_ref, wa_ref,
                  o0_ref, o1_ref, o2_ref, l0_ref, l1_ref, l2_ref, wb_ref,
                  ga_ref, gb_ref, wout_ref, h_ref, stage_ref, y_ref, ya_ref, attn_ref, mix_ref):
    pieces = lambda width: [slice(c * COL_TILE, (c + 1) * COL_TILE) for c in range(width // COL_TILE)]
    for s in range(x_ref.shape[0] // SUB_TILE):
        tok0 = s * SUB_TILE
        rows = slice(tok0, tok0 + SUB_TILE)
        stage = stage_ref.at[s]
        y_ref[s] = jax.nn.gelu(_load_chunk_rows(stage, ys_ref, SSM_WIDTH, tok0 // SSM_CHUNK, SUB_TILE // SSM_CHUNK))
        y_bf = y_ref[s].astype(BF16)
        for cols in pieces(SSM_WIDTH):
            gate = jax.nn.sigmoid(_dot(y_bf, gluw_ref[:, cols]) + glub_ref[:, cols])
            ya_ref[s, :, cols] = (y_ref[s, :, cols] * gate).astype(BF16)

        for j in range(GROUP_WIDTH // LANES):
            lanes = slice(j * LANES, (j + 1) * LANES)
            o, l = ([_load_token_major(stage, ref, tok0, SUB_TILE, lanes) for ref in refs]
                    for refs in ((o0_ref, o1_ref, o2_ref), (l0_ref, l1_ref, l2_ref)))
            top = jnp.maximum(jnp.maximum(l[0], l[1]), l[2])
            w = [jnp.exp(lg - top) for lg in l]
            attn_ref[s, :, lanes] = ((w[0] * o[0] + w[1] * o[1] + w[2] * o[2])
                                     / (w[0] + w[1] + w[2])).astype(BF16)

        for cols in pieces(D_MODEL):
            mix = (ga_ref[rows, cols].astype(F32) * _dot(ya_ref[s], wa_ref[:, cols])
                   + gb_ref[rows, cols].astype(F32) * _dot(attn_ref[s], wb_ref[:, cols]))
            mix_ref[s, :, cols] = mix.astype(BF16)
        for cols in pieces(D_MODEL):
            h_ref[rows, cols] = x_ref[rows, cols] + _dot(mix_ref[s], wout_ref[:, cols])


def _merge(x2, seq, ys_rows, glu_w, glu_b, w_a, attn_outs, w_b, ga, gb, w_out):
    n = x2.shape[0]
    tm = TOKEN_TILE
    tiles_per_seq = seq // tm
    row = lambda w: pl.BlockSpec((tm, w), lambda i: (i, 0))
    res = lambda d: pl.BlockSpec((None, d, tm // d, GROUP_WIDTH),
                                 lambda i: (i // tiles_per_seq, 0, i % tiles_per_seq, 0))
    (o0, l0), (o1, l1), (o2, l2) = attn_outs
    d0, d1, d2 = DILATIONS
    return pl.pallas_call(
        _merge_kernel,
        grid=(n // tm,),
        in_specs=[row(D_MODEL), pl.BlockSpec((tm // SSM_CHUNK, SSM_ROW), lambda i: (i, 0)),
                  _const_spec((SSM_WIDTH, SSM_WIDTH)), _const_spec((1, SSM_WIDTH)),
                  _const_spec((SSM_WIDTH, D_MODEL)),
                  res(d0), res(d1), res(d2), res(d0), res(d1), res(d2),
                  _const_spec((GROUP_WIDTH, D_MODEL)), row(D_MODEL), row(D_MODEL),
                  _const_spec((D_MODEL, D_MODEL))],
        out_specs=row(D_MODEL),
        out_shape=jax.ShapeDtypeStruct((n, D_MODEL), F32),
        scratch_shapes=[pltpu.VMEM((tm // SUB_TILE, SUB_TILE, LANES), F32),
                        pltpu.VMEM((tm // SUB_TILE, SUB_TILE, SSM_WIDTH), F32),
                        pltpu.VMEM((tm // SUB_TILE, SUB_TILE, SSM_WIDTH), BF16),
                        pltpu.VMEM((tm // SUB_TILE, SUB_TILE, GROUP_WIDTH), BF16),
                        pltpu.VMEM((tm // SUB_TILE, SUB_TILE, D_MODEL), BF16)],
        compiler_params=_params("parallel"),
        name="merge",
    )(x2, ys_rows, glu_w.astype(BF16), glu_b.reshape(1, SSM_WIDTH),
      w_a.astype(BF16), o0, o1, o2, l0, l1, l2, w_b.astype(BF16), ga, gb, w_out.astype(BF16))


def _cast_weight_rows(step, src_refs, dst_refs):
    for src, dst in zip(src_refs, dst_refs):
        rb = src.shape[0]
        dst[pl.ds(pl.multiple_of(step * rb, rb), rb), :] = src[...].astype(BF16)


def _ffn_kernel(h_ref, g2_ref, wg32_ref, wu32_ref, cw_ref, cb_ref, wd32_ref, g3_ref, wpg32_ref,
                p_ref, wpp32_ref, gf_ref, out_ref, wg_ref, wu_ref, wd_ref, wpg_ref, wpp_ref,
                act_ref, carry_ref, *, tiles_per_seq):
    step = pl.program_id(0)

    @pl.when(step < WEIGHT_CAST_STEPS)
    def _():
        _cast_weight_rows(step, (wg32_ref, wu32_ref, wd32_ref, wpg32_ref, wpp32_ref),
                          (wg_ref, wu_ref, wd_ref, wpg_ref, wpp_ref))

    @pl.when(step >= WEIGHT_CAST_STEPS)
    def _():
        _ffn_tile(step - WEIGHT_CAST_STEPS, h_ref, g2_ref, wg_ref, wu_ref, cw_ref, cb_ref, wd_ref, g3_ref, wpg_ref,
                  p_ref, wpp_ref, gf_ref, out_ref, act_ref, carry_ref, tiles_per_seq)


def _ffn_tile(tile, h_ref, g2_ref, wg_ref, wu_ref, cw_ref, cb_ref, wd_ref, g3_ref, wpg_ref,
              p_ref, wpp_ref, gf_ref, out_ref, act_ref, carry_ref, tiles_per_seq):
    tm = h_ref.shape[0]

    @pl.when(tile % tiles_per_seq == 0)
    def _():
        carry_ref[...] = jnp.zeros_like(carry_ref)

    subs = [slice(s * SUB_TILE, (s + 1) * SUB_TILE) for s in range(tm // SUB_TILE)]
    hs = [h_ref[rows, :] for rows in subs]
    u2s = [_rms(h, g2_ref[...]).astype(BF16) for h in hs]
    row = lax.broadcasted_iota(jnp.int32, (SUBLANES, FFN_CHUNK), 0)
    for c in range(D_FF // FFN_CHUNK):
        sl = slice(c * FFN_CHUNK, (c + 1) * FFN_CHUNK)
        prev = carry_ref[:, sl]
        for rows, u2 in zip(subs, u2s):
            gp = _dot(u2, wg_ref[:, sl])
            up = _dot(u2, wu_ref[:, sl])
            r1 = pltpu.roll(gp, 1, 0)
            r2 = pltpu.roll(gp, 2, 0)
            r1 = jnp.concatenate([jnp.where(row < 1, pltpu.roll(prev, 1, 0), r1[:SUBLANES]), r1[SUBLANES:]], axis=0)
            r2 = jnp.concatenate([jnp.where(row < 2, pltpu.roll(prev, 2, 0), r2[:SUBLANES]), r2[SUBLANES:]], axis=0)
            gate = cw_ref[0:1, sl] * r2 + cw_ref[1:2, sl] * r1 + cw_ref[2:3, sl] * gp + cb_ref[:, sl]
            act_ref[rows, sl] = (jax.nn.gelu(gate) * up).astype(BF16)
            prev = gp[SUB_TILE - SUBLANES:, :]
        carry_ref[:, sl] = prev
    for rows, h in zip(subs, hs):
        h = h + _dot(act_ref[rows, :], wd_ref[...])
        u3 = _rms(h, g3_ref[...]).astype(BF16)
        h = h + jax.nn.sigmoid(_dot(u3, wpg_ref[...])) * _dot(p_ref[rows, :].astype(BF16), wpp_ref[...])
        out_ref[rows, :] = _rms(h, gf_ref[...])


def _ffn(h1, seq, p2, norm_g, w_gate, w_up, conv_w, conv_b, w_down, ple_g, ple_w_gate, ple_w_proj, final_g):
    n = h1.shape[0]
    tm = TOKEN_TILE
    k = WEIGHT_CAST_STEPS
    row = lambda w: pl.BlockSpec((tm, w), lambda i: (jnp.maximum(i - k, 0), 0))
    wrows = lambda a: pl.BlockSpec((a.shape[0] // k, a.shape[1]), lambda i: (jnp.minimum(i, k - 1), 0))
    resident = lambda a: pltpu.VMEM(a.shape, BF16)
    vec = lambda a: a.reshape(1, -1)
    weights = (w_gate, w_up, w_down, ple_w_gate, ple_w_proj)
    return pl.pallas_call(
        functools.partial(_ffn_kernel, tiles_per_seq=seq // tm),
        grid=(k + n // tm,),
        in_specs=[row(D_MODEL), _const_spec((1, D_MODEL)), wrows(w_gate),
                  wrows(w_up), _const_spec((CONV_WIDTH, D_FF)), _const_spec((1, D_FF)),
                  wrows(w_down), _const_spec((1, D_MODEL)), wrows(ple_w_gate),
                  row(PLE_DIM), wrows(ple_w_proj), _const_spec((1, D_MODEL))],
        out_specs=row(D_MODEL),
        out_shape=jax.ShapeDtypeStruct((n, D_MODEL), F32),
        scratch_shapes=[resident(w) for w in weights]
                       + [pltpu.VMEM((tm, D_FF), BF16), pltpu.VMEM((SUBLANES, D_FF), F32)],
        compiler_params=_params("arbitrary"),
        name="ffn",
    )(h1, vec(norm_g), w_gate, w_up, conv_w, vec(conv_b),
      w_down, vec(ple_g), ple_w_gate, p2, ple_w_proj, vec(final_g))


def _layer(h2, bsz, seq, p2, mix_norm_g, w_in, gate_b, ssm_lam_re, ssm_lam_im, ssm_log_dt, ssm_b_re,
           ssm_b_im, ssm_c_re, ssm_c_im, ssm_d, ssm_glu_w, ssm_glu_b, w_branch_a, w_branch_b, w_out,
           ffn_norm_g, ffn_w_gate, ffn_w_up, ffn_conv_w, ffn_conv_b, ffn_w_down,
           ple_norm_g, ple_w_gate, ple_w_proj, out_norm_g):
    u_rows, qkv, ga, gb = _in_proj(h2, seq, mix_norm_g, w_in, gate_b)
    ops = _ssm_operators(ssm_lam_re, ssm_lam_im, ssm_log_dt, ssm_b_re, ssm_b_im, ssm_c_re, ssm_c_im)
    ys_rows = _ssm_scan(u_rows, bsz, ssm_d.reshape(-1), ops)
    attn_outs = [_attn_group(*group) for group in qkv]
    h1 = _merge(h2, seq, ys_rows, ssm_glu_w, ssm_glu_b, w_branch_a, attn_outs, w_branch_b, ga, gb, w_out)
    return _ffn(h1, seq, p2, ffn_norm_g, ffn_w_gate, ffn_w_up, ffn_conv_w, ffn_conv_b, ffn_w_down,
                ple_norm_g, ple_w_gate, ple_w_proj, out_norm_g)


def kernel(x, p, mix_norm_g, w_in, gate_b, ssm_lam_re, ssm_lam_im, ssm_log_dt, ssm_b_re, ssm_b_im, ssm_c_re, ssm_c_im, ssm_d, ssm_glu_w, ssm_glu_b, w_branch_a, w_branch_b, w_out, ffn_norm_g, ffn_w_gate, ffn_w_up, ffn_conv_w, ffn_conv_b, ffn_w_down, ple_norm_g, ple_w_gate, ple_w_proj, final_norm_g):
    bsz, seq, _ = x.shape
    depth = p.shape[0]
    assert depth == 1, "the final norm is fused into the layer's last kernel"
    h2 = x.reshape(bsz * seq, D_MODEL)
    out = _layer(h2, bsz, seq, p[0].reshape(bsz * seq, PLE_DIM), mix_norm_g[0], w_in[0], gate_b[0],
                 ssm_lam_re[0], ssm_lam_im[0], ssm_log_dt[0], ssm_b_re[0], ssm_b_im[0], ssm_c_re[0],
                 ssm_c_im[0], ssm_d[0], ssm_glu_w[0], ssm_glu_b[0], w_branch_a[0], w_branch_b[0],
                 w_out[0], ffn_norm_g[0], ffn_w_gate[0], ffn_w_up[0], ffn_conv_w[0], ffn_conv_b[0],
                 ffn_w_down[0], ple_norm_g[0], ple_w_gate[0], ple_w_proj[0], final_norm_g)
    return out.reshape(bsz, seq, D_MODEL)
```

```python
import functools

import jax
import jax.numpy as jnp
from jax import lax
from jax.experimental import pallas as pl
from jax.experimental.pallas import tpu as pltpu

F32 = jnp.float32
BF16 = jnp.bfloat16

D_MODEL = 1024
EPS = 1e-6
PLE_DIM = 256
SSM_GROUP = 16
SSM_STATE = 64
SSM_WIDTH = 512
SSM_GROUPS = SSM_WIDTH // SSM_GROUP
HEAD_DIM = 64
DILATIONS = (1, 4, 16)
WINDOW_KEYS = 128
HEADS_PER_GROUP = 4
GROUP_WIDTH = HEADS_PER_GROUP * HEAD_DIM
ATTN_WIDTH = len(DILATIONS) * GROUP_WIDTH
ROT_DIM = HEAD_DIM // 4
ROPE_THETA = 500000.0
NEG_BIG = -1e30
D_FF = 2816
CONV_WIDTH = 3
OFF_Q = SSM_WIDTH
OFF_K = OFF_Q + ATTN_WIDTH
OFF_V = OFF_K + ATTN_WIDTH
OFF_GA = OFF_V + ATTN_WIDTH
OFF_GB = OFF_GA + D_MODEL
IN_WIDTH = OFF_GB + D_MODEL

LANES = 128
SUBLANES = 8
VMEM_LIMIT_BYTES = 60 * 1024 * 1024

TOKEN_TILE = 1024
SUB_TILE = 256
COL_TILE = 256
SSM_CHUNK = 16
SSM_CHUNKS_PER_TILE = 256
SSM_TILE = SSM_CHUNK * SSM_CHUNKS_PER_TILE
SSM_CK = SSM_CHUNK * SSM_GROUP
SSM_ROW = SSM_CHUNK * SSM_WIDTH
SSM_PAIRS = SSM_GROUPS // 2
SSM_LOG_STEPS = 8
ATTN_QB = 128
ATTN_STEP_ROWS = 2048
FFN_CHUNK = 256
WEIGHT_CAST_STEPS = 16


def _dot(a, b):
    return jnp.dot(a, b, preferred_element_type=F32)


def _rms(x, g):
    var = jnp.mean(x * x, axis=-1, keepdims=True)
    return x * lax.rsqrt(var + EPS) * g


def _const_spec(shape):
    nd = len(shape)
    return pl.BlockSpec(shape, lambda *_: (0,) * nd, pipeline_mode=pl.Buffered(1))


def _params(*sem):
    return pltpu.CompilerParams(dimension_semantics=sem, vmem_limit_bytes=VMEM_LIMIT_BYTES)


def _rope(z, cos, sin_lo, sin_hi):
    up = pltpu.roll(z, LANES - ROT_DIM // 2, 1)
    dn = pltpu.roll(z, ROT_DIM // 2, 1)
    return z * cos + up * sin_lo + dn * sin_hi


def _store_residue_major(stage_ref, out_ref, z, dil, tok0):
    rows, first = z.shape[0] // dil, tok0 // dil
    for j in range(z.shape[1] // LANES):
        sl = slice(j * LANES, (j + 1) * LANES)
        stage_ref[...] = z[:, sl]
        for r in range(dil):
            out_ref[r, first:first + rows, sl] = stage_ref[pl.ds(r, rows, stride=dil), :].astype(out_ref.dtype)


def _store_chunk_rows(stage_ref, out_ref, z, c0):
    rows, width = z.shape[0] // SSM_CHUNK, z.shape[1]
    for j in range(width // LANES):
        stage_ref[...] = z[:, j * LANES:(j + 1) * LANES]
        for t in range(SSM_CHUNK):
            col = t * width + j * LANES
            out_ref[c0:c0 + rows, col:col + LANES] = (
                stage_ref[pl.ds(t, rows, stride=SSM_CHUNK), :].astype(out_ref.dtype))


def _load_chunk_rows(stage_ref, in_ref, width, c0, rows):
    slabs = []
    for j in range(width // LANES):
        for t in range(SSM_CHUNK):
            col = t * width + j * LANES
            stage_ref[pl.ds(t, rows, stride=SSM_CHUNK), :] = in_ref[c0:c0 + rows, col:col + LANES].astype(F32)
        slabs.append(stage_ref[...])
    return jnp.concatenate(slabs, axis=1)


def _in_proj_kernel(x_ref, g_ref, ca_ref, sa_ref, cb_ref, sb_ref, sign_ref, w32_ref, bg_ref,
                    s_ref, q0_ref, k0_ref, v0_ref, q1_ref, k1_ref, v1_ref, q2_ref, k2_ref, v2_ref,
                    ga_ref, gb_ref, w_ref, stage_ref):
    step = pl.program_id(0)

    @pl.when(step < WEIGHT_CAST_STEPS)
    def _():
        _cast_weight_rows(step, (w32_ref,), (w_ref,))

    @pl.when(step >= WEIGHT_CAST_STEPS)
    def _():
        _in_proj_tile(x_ref, g_ref, ca_ref, sa_ref, cb_ref, sb_ref, sign_ref, w_ref, bg_ref,
                      s_ref, q0_ref, k0_ref, v0_ref, q1_ref, k1_ref, v1_ref, q2_ref, k2_ref, v2_ref,
                      ga_ref, gb_ref, stage_ref)


def _in_proj_tile(x_ref, g_ref, ca_ref, sa_ref, cb_ref, sb_ref, sign_ref, w_ref, bg_ref,
                  s_ref, q0_ref, k0_ref, v0_ref, q1_ref, k1_ref, v1_ref, q2_ref, k2_ref, v2_ref,
                  ga_ref, gb_ref, stage_ref):
    u = _rms(x_ref[...], g_ref[...]).astype(BF16)
    _store_chunk_rows(stage_ref, s_ref, _dot(u, w_ref[:, :OFF_Q]), 0)
    ca, sa, cb, sb = ca_ref[...], sa_ref[...], cb_ref[...], sb_ref[...]
    cos = ca * cb - sa * sb
    sin = sa * cb + ca * sb
    slo, shi = sin * sign_ref[0:1, :], sin * sign_ref[1:2, :]
    scale = HEAD_DIM ** -0.5

    def rope(z):
        return jnp.concatenate([_rope(z[:, j * LANES:(j + 1) * LANES], cos, slo, shi)
                                for j in range(z.shape[1] // LANES)], axis=1)

    q = rope(_dot(u, w_ref[:, OFF_Q:OFF_K])) * scale
    k = rope(_dot(u, w_ref[:, OFF_K:OFF_V]))
    v = _dot(u, w_ref[:, OFF_V:OFF_GA])
    for z, refs in ((q, (q0_ref, q1_ref, q2_ref)), (k, (k0_ref, k1_ref, k2_ref)), (v, (v0_ref, v1_ref, v2_ref))):
        refs[0][...] = z[:, :GROUP_WIDTH].astype(BF16)
        for grp in (1, 2):
            _store_residue_major(stage_ref, refs[grp], z[:, grp * GROUP_WIDTH:(grp + 1) * GROUP_WIDTH],
                                 DILATIONS[grp], 0)
    ga_ref[...] = jax.nn.sigmoid(_dot(u, w_ref[:, OFF_GA:OFF_GB]) + bg_ref[:, :D_MODEL]).astype(BF16)
    gb_ref[...] = jax.nn.sigmoid(_dot(u, w_ref[:, OFF_GB:]) + bg_ref[:, D_MODEL:]).astype(BF16)


def _rope_tables(seq, tm):
    half = ROT_DIM // 2
    freqs = ROPE_THETA ** (-jnp.arange(half, dtype=F32) * (2.0 / ROT_DIM))
    head = jnp.concatenate([freqs, freqs, jnp.zeros((HEAD_DIM - ROT_DIM,), F32)])
    lane_freq = jnp.tile(head, LANES // HEAD_DIM)[None, :]
    base = jnp.arange(0, seq, tm, dtype=F32)[:, None] * lane_freq
    offs = jnp.arange(tm, dtype=F32)[:, None] * lane_freq
    in_head = jnp.arange(LANES) % HEAD_DIM
    sign = jnp.zeros((SUBLANES, LANES), F32)
    sign = sign.at[0].set(jnp.where(in_head < half, -1.0, 0.0))
    sign = sign.at[1].set(jnp.where((in_head >= half) & (in_head < ROT_DIM), 1.0, 0.0))
    n_tiles = seq // tm
    return (jnp.cos(base).reshape(n_tiles, 1, LANES), jnp.sin(base).reshape(n_tiles, 1, LANES),
            jnp.cos(offs), jnp.sin(offs), sign)


def _in_proj(x2, seq, norm_g, w_in, gate_b):
    n = x2.shape[0]
    tm = TOKEN_TILE
    tiles_per_seq = seq // tm
    bsz = n // seq
    cos_a, sin_a, cos_b, sin_b, sign = _rope_tables(seq, tm)

    k = WEIGHT_CAST_STEPS
    tile = lambda i: jnp.maximum(i - k, 0)
    row = lambda w: pl.BlockSpec((tm, w), lambda i: (tile(i), 0))
    tile_tab = pl.BlockSpec((None, 1, LANES), lambda i: (tile(i) % tiles_per_seq, 0, 0))
    res = lambda d: pl.BlockSpec((None, d, tm // d, GROUP_WIDTH),
                                 lambda i: (tile(i) // tiles_per_seq, 0, tile(i) % tiles_per_seq, 0))
    res_shape = lambda d: jax.ShapeDtypeStruct((bsz, d, seq // d, GROUP_WIDTH), BF16)
    nat_shape = jax.ShapeDtypeStruct((n, GROUP_WIDTH), BF16)
    d1, d2 = DILATIONS[1], DILATIONS[2]
    outs = pl.pallas_call(
        _in_proj_kernel,
        grid=(k + n // tm,),
        in_specs=[row(D_MODEL), _const_spec((1, D_MODEL)), tile_tab, tile_tab,
                  _const_spec((tm, LANES)), _const_spec((tm, LANES)), _const_spec((SUBLANES, LANES)),
                  pl.BlockSpec((D_MODEL // k, IN_WIDTH), lambda i: (jnp.minimum(i, k - 1), 0)),
                  _const_spec((1, 2 * D_MODEL))],
        out_specs=[pl.BlockSpec((tm // SSM_CHUNK, SSM_ROW), lambda i: (tile(i), 0))]
                  + [row(GROUP_WIDTH)] * 3 + [res(d1)] * 3 + [res(d2)] * 3 + [row(D_MODEL), row(D_MODEL)],
        out_shape=[jax.ShapeDtypeStruct((n // SSM_CHUNK, SSM_ROW), BF16)] + [nat_shape] * 3
                  + [res_shape(d1)] * 3 + [res_shape(d2)] * 3
                  + [jax.ShapeDtypeStruct((n, D_MODEL), BF16), jax.ShapeDtypeStruct((n, D_MODEL), BF16)],
        scratch_shapes=[pltpu.VMEM((D_MODEL, IN_WIDTH), BF16), pltpu.VMEM((tm, LANES), F32)],
        compiler_params=_params("arbitrary"),
        name="in_proj",
    )(x2, norm_g.reshape(1, D_MODEL), cos_a, sin_a, cos_b, sin_b, sign, w_in, gate_b.reshape(1, 2 * D_MODEL))
    u, q0, k0, v0, q1, k1, v1, q2, k2, v2, ga, gb = outs
    nat4 = lambda a: a.reshape(bsz, 1, seq, GROUP_WIDTH)
    qkv = ((nat4(q0), nat4(k0), nat4(v0)), (q1, k1, v1), (q2, k2, v2))
    return u, qkv, ga, gb


def _cmul(ar, ai, br, bi):
    return ar * br - ai * bi, ar * bi + ai * br


def _discretise(lr, li, dt):
    mag = jnp.exp(lr * dt)
    ar = mag * jnp.cos(li * dt)
    ai = mag * jnp.sin(li * dt)
    den = lr * lr + li * li
    cr = ((ar - 1.0) * lr + ai * li) / den
    ci = (ai * lr - (ar - 1.0) * li) / den
    return ar, ai, cr, ci


def _ssm_prep_kernel(lr_ref, li_ref, ldt_ref, brt_ref, bit_ref, cre_ref, cim_ref, lrf_ref, lif_ref, ldtf_ref,
                     w2_ref, e2_ref, apsr_ref, apsi_ref, aplr_ref, apli_ref, apbr_ref, apbi_ref, er_ref, ei_ref):
    g_n, t_n, h_n, sw = SSM_GROUPS, SSM_CHUNK, SSM_GROUP, 2 * SSM_STATE
    ar, ai, cr, ci = _discretise(lr_ref[...], li_ref[...], jnp.exp(ldt_ref[...]))
    brt, bit = brt_ref[...], bit_ref[...]
    bbr = cr * brt - ci * bit
    bbi = cr * bit + ci * brt
    cre, cim = cre_ref[...], cim_ref[...]
    by_group = lambda a: a.reshape(g_n, h_n, sw)
    pr, pi = jnp.ones_like(ar), jnp.zeros_like(ai)
    for j in range(t_n):
        rows = slice((t_n - 1 - j) * h_n, (t_n - j) * h_n)
        rr, ri = _cmul(pr, pi, bbr, bbi)
        er_ref[:, rows, :] = by_group(rr)
        ei_ref[:, rows, :] = by_group(ri)
        e2_ref[:, rows, :sw] = by_group(rr).astype(BF16)
        e2_ref[:, rows, sw:] = by_group(ri).astype(BF16)
        pr, pi = _cmul(pr, pi, ar, ai)
        rows = slice(j * h_n, (j + 1) * h_n)
        w2_ref[:, rows, SSM_CK:SSM_CK + sw] = by_group(cre * pr - cim * pi).astype(BF16)
        w2_ref[:, rows, SSM_CK + sw:] = by_group(-cre * pi - cim * pr).astype(BF16)

    nt = (((1,), (1,)), ((), ()))
    hi = lax.Precision.HIGHEST

    def toeplitz(g, _):
        rows = pl.ds(pl.multiple_of(g * h_n, h_n), h_n)
        krev = (lax.dot_general(cre_ref[rows, :], er_ref[g], nt, precision=hi, preferred_element_type=F32)
                - lax.dot_general(cim_ref[rows, :], ei_ref[g], nt, precision=hi, preferred_element_type=F32))
        kext = jnp.concatenate([krev, jnp.zeros_like(krev)], axis=1)
        for t in range(t_n):
            off = (t_n - 1 - t) * h_n
            win = kext if off == 0 else pltpu.roll(kext, 2 * SSM_CK - off, 1)
            w2_ref[g, t * h_n:(t + 1) * h_n, :SSM_CK] = win[:, :SSM_CK].astype(BF16)
        return 0
    lax.fori_loop(0, g_n, toeplitz, 0, unroll=4)

    acr, aci, _, _ = _discretise(lrf_ref[...], lif_ref[...], jnp.exp(ldtf_ref[...]))
    for _ in range(4):
        acr, aci = _cmul(acr, aci, acr, aci)
    shape = (SUBLANES, acr.shape[1])
    row = lax.broadcasted_iota(jnp.int32, shape, 0)
    qr, qi = jnp.ones(shape, F32), jnp.zeros(shape, F32)
    apsr_ref[...] = jnp.zeros_like(apsr_ref)
    apsi_ref[...] = jnp.zeros_like(apsi_ref)
    for k in range(SSM_LOG_STEPS + 1):
        for p2 in range(SSM_PAIRS):
            apsr_ref[p2, k:k + 1, :] = acr[:, p2 * sw:(p2 + 1) * sw]
            apsi_ref[p2, k:k + 1, :] = aci[:, p2 * sw:(p2 + 1) * sw]
        if (1 << k) < SUBLANES:
            nr, ni = _cmul(qr, qi, acr, aci)
            bit_set = (row & (1 << k)) != 0
            qr, qi = jnp.where(bit_set, nr, qr), jnp.where(bit_set, ni, qi)
        if (1 << k) == SUBLANES:
            a8r, a8i = acr, aci
        acr, aci = _cmul(acr, aci, acr, aci)
    br, bi = jnp.ones_like(a8r), jnp.zeros_like(a8i)
    blk_r, blk_i = [], []
    for _ in range(SSM_CHUNKS_PER_TILE // SUBLANES):
        blk_r.append(br)
        blk_i.append(bi)
        br, bi = _cmul(br, bi, a8r, a8i)
    blk_r, blk_i = jnp.concatenate(blk_r, axis=0), jnp.concatenate(blk_i, axis=0)
    for p2 in range(SSM_PAIRS):
        lanes = slice(p2 * sw, (p2 + 1) * sw)
        aplr_ref[p2], apli_ref[p2] = qr[:, lanes], qi[:, lanes]
        apbr_ref[p2], apbi_ref[p2] = blk_r[:, lanes], blk_i[:, lanes]


def _pad_pair_lanes(a):
    z = jnp.zeros_like(a)
    even = (jnp.arange(a.shape[0]) % 2 == 0)[:, None, None]
    padded = jnp.where(even, jnp.concatenate([a, z], -1), jnp.concatenate([z, a], -1))
    return padded.reshape(a.shape[0] * a.shape[1], 2 * a.shape[2])


def _ssm_operators(lam_re, lam_im, log_dt, b_re, b_im, c_re, c_im):
    g, p, h = SSM_GROUPS, SSM_STATE, SSM_GROUP
    gp, sw = g * p, 2 * p
    rep = lambda a: jnp.repeat(jnp.tile(a, (1, 2)), h, axis=0)
    ldt2 = jnp.broadcast_to(log_dt[:, None], (g, p))
    full = lambda shape: pl.BlockSpec(shape, lambda: (0,) * len(shape))
    in_arrays = (rep(lam_re), rep(lam_im), rep(ldt2),
                 _pad_pair_lanes(b_re.transpose(0, 2, 1)), _pad_pair_lanes(b_im.transpose(0, 2, 1)),
                 _pad_pair_lanes(c_re), _pad_pair_lanes(c_im),
                 lam_re.reshape(1, gp), lam_im.reshape(1, gp), ldt2.reshape(1, gp))
    out_shapes = ([((g, SSM_CK, SSM_CK + 2 * sw), BF16), ((g, SSM_CK, 2 * sw), BF16)]
                  + [((SSM_PAIRS, 2 * SUBLANES, sw), F32)] * 2
                  + [((SSM_PAIRS, SUBLANES, sw), F32)] * 2
                  + [((SSM_PAIRS, SSM_CHUNKS_PER_TILE // SUBLANES, sw), F32)] * 2)
    w2, e2, *powers = pl.pallas_call(
        _ssm_prep_kernel,
        in_specs=[full(a.shape) for a in in_arrays],
        out_specs=[full(s) for s, _ in out_shapes],
        out_shape=[jax.ShapeDtypeStruct(s, dt) for s, dt in out_shapes],
        scratch_shapes=[pltpu.VMEM((g, SSM_CK, sw), F32), pltpu.VMEM((g, SSM_CK, sw), F32)],
        compiler_params=pltpu.CompilerParams(vmem_limit_bytes=VMEM_LIMIT_BYTES),
        name="ssm_prep",
    )(*in_arrays)
    return (w2, e2.reshape(SSM_PAIRS, 2 * SSM_CK, 2 * sw), *powers)


def _shift_rows(z, s, row):
    if s % SUBLANES == 0:
        return jnp.concatenate([jnp.zeros((s, z.shape[1]), z.dtype), z[:-s]], axis=0)
    return jnp.where(row >= s, pltpu.roll(z, s, 0), 0.0)


def _ssm_scan_kernel(u_ref, d_ref, w2_ref, e2_ref, apsr_ref, apsi_ref, aplr_ref, apli_ref, apbr_ref, apbi_ref,
                     y_ref, xs_ref, sc_ref, yt_ref, carry_ref, loc_ref, sin_ref):
    t_n, h_n, c_n = SSM_CHUNK, SSM_GROUP, SSM_CHUNKS_PER_TILE
    n_slab = SSM_WIDTH // LANES
    pairs_per_slab = LANES // (2 * h_n)
    sw = 2 * SSM_STATE
    blk_n = SUBLANES
    n_blk = c_n // blk_n
    log_blk = blk_n.bit_length() - 1

    @pl.when(pl.program_id(1) == 0)
    def _():
        carry_ref[...] = jnp.zeros_like(carry_ref)

    for t in range(t_n):
        for j in range(n_slab):
            col = t * SSM_WIDTH + j * LANES
            blk = u_ref[:, col:col + LANES].T
            xs_ref[j * pairs_per_slab:(j + 1) * pairs_per_slab, :, t * h_n:(t + 1) * h_n, :] = (
                blk.reshape(pairs_per_slab, 2, h_n, c_n))

    row = lax.broadcasted_iota(jnp.int32, (n_blk, sw), 0)
    nt = (((1,), (1,)), ((), ()))
    tn = (((0,), (0,)), ((), ()))

    def local_states(pr, _):
        xp = xs_ref[pr].reshape(2 * SSM_CK, c_n)
        loc = lax.dot_general(xp, e2_ref[pr], tn, preferred_element_type=F32)
        loc_ref[pr, 0] = loc[:, :sw]
        loc_ref[pr, 1] = loc[:, sw:]
        return 0
    lax.fori_loop(0, SSM_PAIRS, local_states, 0, unroll=4)

    def chunk_scan(pr, _):
        slot = pr % 2
        power = lambda k: (apsr_ref[pr, k:k + 1, :], apsi_ref[pr, k:k + 1, :])
        zr, zi = [], []
        for lo in range(blk_n):
            rows = pl.ds(lo, n_blk, stride=blk_n)
            xr, xi = loc_ref[pr, 0, rows, :], loc_ref[pr, 1, rows, :]
            if lo:
                dr, di = _cmul(zr[-1], zi[-1], *power(0))
                xr, xi = xr + dr, xi + di
            zr.append(xr)
            zi.append(xi)
        er, ei = zr[-1], zi[-1]
        s = 1
        while s < n_blk:
            dr, di = _cmul(_shift_rows(er, s, row), _shift_rows(ei, s, row),
                           *power(log_blk + s.bit_length() - 1))
            er, ei = er + dr, ei + di
            s *= 2
        cr, ci = carry_ref[pr, 0:1, :], carry_ref[pr, 1:2, :]
        hr, hi = _cmul(apbr_ref[pr], apbi_ref[pr], cr, ci)
        br, bi = _shift_rows(er, 1, row) + hr, _shift_rows(ei, 1, row) + hi
        for lo in range(blk_n):
            sr, si = _cmul(br, bi, aplr_ref[pr, lo:lo + 1, :], apli_ref[pr, lo:lo + 1, :])
            if lo:
                sr, si = sr + zr[lo - 1], si + zi[lo - 1]
            rows = pl.ds(lo, n_blk, stride=blk_n)
            sin_ref[slot, 0, rows, :] = sr
            sin_ref[slot, 1, rows, :] = si
        sc_ref[pr, :, :sw] = sin_ref[slot, 0].astype(BF16)
        sc_ref[pr, :, sw:] = sin_ref[slot, 1].astype(BF16)
        nr, ni = _cmul(cr, ci, *power(SSM_LOG_STEPS))
        carry_ref[pr, 0:1, :] = er[n_blk - 1:n_blk, :] + nr
        carry_ref[pr, 1:2, :] = ei[n_blk - 1:n_blk, :] + ni
        return 0
    lax.fori_loop(0, SSM_PAIRS, chunk_scan, 0, unroll=2)

    def outputs(g, _):
        pr = g // 2
        yg = (_dot(w2_ref[g, :, :SSM_CK], xs_ref[pr, g % 2])
              + lax.dot_general(w2_ref[g, :, SSM_CK:], sc_ref[pr], nt, preferred_element_type=F32))
        yt_ref[:, pl.ds(pl.multiple_of(g * h_n, h_n), h_n), :] = yg.reshape(t_n, h_n, c_n)
        return 0
    lax.fori_loop(0, SSM_GROUPS, outputs, 0, unroll=4)

    for t in range(t_n):
        for j in range(n_slab):
            sl = slice(j * LANES, (j + 1) * LANES)
            col = t * SSM_WIDTH + j * LANES
            y_ref[:, col:col + LANES] = (
                yt_ref[t, sl, :].T + d_ref[:, sl] * u_ref[:, col:col + LANES].astype(F32)).astype(BF16)


def _ssm_scan(u_rows, bsz, d_skip, ops):
    g, p, c_n = SSM_GROUPS, SSM_STATE, SSM_CHUNKS_PER_TILE
    tiles = u_rows.shape[0] // (bsz * c_n)
    tile = pl.BlockSpec((c_n, SSM_ROW), lambda b, i: (b * tiles + i, 0))
    return pl.pallas_call(
        _ssm_scan_kernel,
        grid=(bsz, tiles),
        in_specs=[tile, _const_spec((1, SSM_WIDTH))] + [_const_spec(op.shape) for op in ops],
        out_specs=tile,
        out_shape=jax.ShapeDtypeStruct(u_rows.shape, BF16),
        scratch_shapes=[pltpu.VMEM((SSM_PAIRS, 2, SSM_CK, c_n), BF16),
                        pltpu.VMEM((SSM_PAIRS, c_n, 4 * p), BF16),
                        pltpu.VMEM((SSM_CHUNK, SSM_WIDTH, c_n), F32),
                        pltpu.VMEM((SSM_PAIRS, SUBLANES, 2 * p), F32),
                        pltpu.VMEM((SSM_PAIRS, 2, c_n, 2 * p), F32),
                        pltpu.VMEM((2, 2, c_n, 2 * p), F32)],
        compiler_params=_params("parallel", "arbitrary"),
        name="ssm_scan",
    )(u_rows, d_skip.reshape(1, SSM_WIDTH), *ops)


def _attn_kernel(q_ref, kc_ref, kp_ref, vc_ref, vp_ref, o_ref, lse_ref):
    qb, nk = ATTN_QB, ATTN_QB + WINDOW_KEYS
    row = lax.broadcasted_iota(jnp.int32, (qb, nk), 0)
    col = lax.broadcasted_iota(jnp.int32, (qb, nk), 1)
    dist = row + WINDOW_KEYS - col
    in_band = (dist >= 0) & (dist <= WINDOW_KEYS)
    in_band_first = in_band & ((col >= WINDOW_KEYS) | (pl.program_id(2) > 0))
    lane = lax.broadcasted_iota(jnp.int32, (qb, LANES), 1)
    first_head = lane < HEAD_DIM
    nt = (((1,), (1,)), ((), ()))
    n_res, n_rows = o_ref.shape[0], o_ref.shape[1]
    for res in range(n_res):
        for sb in range(n_rows // qb):
            rows = slice(sb * qb, (sb + 1) * qb)
            valid = in_band if sb else in_band_first
            for pair in range(GROUP_WIDTH // LANES):
                cols = slice(pair * LANES, (pair + 1) * LANES)
                qp = q_ref[res, rows, cols]
                if sb:
                    window = slice(sb * qb - WINDOW_KEYS, (sb + 1) * qb)
                    kp, vp = kc_ref[res, window, cols], vc_ref[res, window, cols]
                else:
                    kp = jnp.concatenate([kp_ref[res, :, cols], kc_ref[res, :qb, cols]], axis=0)
                    vp = jnp.concatenate([vp_ref[res, :, cols], vc_ref[res, :qb, cols]], axis=0)
                outs, lses = [], []
                for sel in (first_head, ~first_head):
                    qm = jnp.where(sel, qp, jnp.zeros_like(qp))
                    s = lax.dot_general(qm, kp, nt, preferred_element_type=F32)
                    s = jnp.where(valid, s, NEG_BIG)
                    m = jnp.max(s, axis=-1, keepdims=True)
                    e = jnp.exp(s - m)
                    den = jnp.sum(e, axis=-1, keepdims=True)
                    outs.append(_dot(e.astype(BF16), vp) / den)
                    lses.append(m + jnp.log(den))
                o_ref[res, rows, cols] = jnp.where(first_head, outs[0], outs[1]).astype(BF16)
                lse_ref[res, rows, cols] = jnp.where(first_head, lses[0], lses[1])


def _attn_group(q4, k4, v4):
    bsz, dil, lr, _ = q4.shape
    rows = min(ATTN_STEP_ROWS, lr)
    n_res = ATTN_STEP_ROWS // rows
    back = rows // WINDOW_KEYS
    q_spec = pl.BlockSpec((None, n_res, rows, GROUP_WIDTH), lambda b, r, i: (b, r, i, 0))
    prev_spec = pl.BlockSpec((None, n_res, WINDOW_KEYS, GROUP_WIDTH),
                             lambda b, r, i: (b, r, jnp.maximum(i * back - 1, 0), 0))
    return pl.pallas_call(
        _attn_kernel,
        grid=(bsz, dil // n_res, lr // rows),
        in_specs=[q_spec, q_spec, prev_spec, q_spec, prev_spec],
        out_specs=[q_spec, q_spec],
        out_shape=[jax.ShapeDtypeStruct(q4.shape, BF16), jax.ShapeDtypeStruct(q4.shape, F32)],
        compiler_params=_params("parallel", "parallel", "arbitrary"),
        name=f"attn_d{dil}",
    )(q4, k4, k4, v4, v4)


def _load_token_major(stage_ref, in_ref, tok0, ntok, lanes):
    dil = in_ref.shape[0]
    first, rows = tok0 // dil, ntok // dil
    if dil == 1:
        return in_ref[0, first:first + rows, lanes].astype(F32)
    for r in range(dil):
        stage_ref[pl.ds(r, rows, stride=dil), :] = in_ref[r, first:first + rows, lanes].astype(F32)
    return stage_ref[...]


def _merge_kernel(ys_ref, gluw_ref, glub_ref, wa_ref,
                  o0_ref, o1_ref, o2_ref, l0_ref, l1_ref, l2_ref, wb_ref,
                  ga_ref, gb_ref, wout_ref, h_ref, stage_ref, y_ref, ya_ref, attn_ref, mix_ref):
    pieces = lambda width: [slice(c * COL_TILE, (c + 1) * COL_TILE) for c in range(width // COL_TILE)]
    for s in range(h_ref.shape[0] // SUB_TILE):
        tok0 = s * SUB_TILE
        rows = slice(tok0, tok0 + SUB_TILE)
        stage = stage_ref.at[s]
        y_ref[s] = jax.nn.gelu(_load_chunk_rows(stage, ys_ref, SSM_WIDTH, tok0 // SSM_CHUNK, SUB_TILE // SSM_CHUNK))
        y_bf = y_ref[s].astype(BF16)
        for cols in pieces(SSM_WIDTH):
            gate = jax.nn.sigmoid(_dot(y_bf, gluw_ref[:, cols]) + glub_ref[:, cols])
            ya_ref[s, :, cols] = (y_ref[s, :, cols] * gate).astype(BF16)

        for j in range(GROUP_WIDTH // LANES):
            lanes = slice(j * LANES, (j + 1) * LANES)
            o, l = ([_load_token_major(stage, ref, tok0, SUB_TILE, lanes) for ref in refs]
                    for refs in ((o0_ref, o1_ref, o2_ref), (l0_ref, l1_ref, l2_ref)))
            top = jnp.maximum(jnp.maximum(l[0], l[1]), l[2])
            w = [jnp.exp(lg - top) for lg in l]
            attn_ref[s, :, lanes] = ((w[0] * o[0] + w[1] * o[1] + w[2] * o[2])
                                     / (w[0] + w[1] + w[2])).astype(BF16)

        for cols in pieces(D_MODEL):
            mix = (ga_ref[rows, cols].astype(F32) * _dot(ya_ref[s], wa_ref[:, cols])
                   + gb_ref[rows, cols].astype(F32) * _dot(attn_ref[s], wb_ref[:, cols]))
            mix_ref[s, :, cols] = mix.astype(BF16)
        for cols in pieces(D_MODEL):
            h_ref[rows, cols] = _dot(mix_ref[s], wout_ref[:, cols]).astype(h_ref.dtype)


def _merge(seq, ys_rows, glu_w, glu_b, w_a, attn_outs, w_b, ga, gb, w_out):
    n = ga.shape[0]
    tm = TOKEN_TILE
    tiles_per_seq = seq // tm
    row = lambda w: pl.BlockSpec((tm, w), lambda i: (i, 0))
    res = lambda d: pl.BlockSpec((None, d, tm // d, GROUP_WIDTH),
                                 lambda i: (i // tiles_per_seq, 0, i % tiles_per_seq, 0))
    (o0, l0), (o1, l1), (o2, l2) = attn_outs
    d0, d1, d2 = DILATIONS
    return pl.pallas_call(
        _merge_kernel,
        grid=(n // tm,),
        in_specs=[pl.BlockSpec((tm // SSM_CHUNK, SSM_ROW), lambda i: (i, 0)),
                  _const_spec((SSM_WIDTH, SSM_WIDTH)), _const_spec((1, SSM_WIDTH)),
                  _const_spec((SSM_WIDTH, D_MODEL)),
                  res(d0), res(d1), res(d2), res(d0), res(d1), res(d2),
                  _const_spec((GROUP_WIDTH, D_MODEL)), row(D_MODEL), row(D_MODEL),
                  _const_spec((D_MODEL, D_MODEL))],
        out_specs=row(D_MODEL),
        out_shape=jax.ShapeDtypeStruct((n, D_MODEL), BF16),
        scratch_shapes=[pltpu.VMEM((tm // SUB_TILE, SUB_TILE, LANES), F32),
                        pltpu.VMEM((tm // SUB_TILE, SUB_TILE, SSM_WIDTH), F32),
                        pltpu.VMEM((tm // SUB_TILE, SUB_TILE, SSM_WIDTH), BF16),
                        pltpu.VMEM((tm // SUB_TILE, SUB_TILE, GROUP_WIDTH), BF16),
                        pltpu.VMEM((tm // SUB_TILE, SUB_TILE, D_MODEL), BF16)],
        compiler_params=_params("parallel"),
        name="merge",
    )(ys_rows, glu_w.astype(BF16), glu_b.reshape(1, SSM_WIDTH),
      w_a.astype(BF16), o0, o1, o2, l0, l1, l2, w_b.astype(BF16), ga, gb, w_out.astype(BF16))


def _cast_weight_rows(step, src_refs, dst_refs):
    for src, dst in zip(src_refs, dst_refs):
        rb = src.shape[0]
        dst[pl.ds(pl.multiple_of(step * rb, rb), rb), :] = src[...].astype(BF16)


def _ffn_kernel(x_ref, dx_ref, g2_ref, wg32_ref, wu32_ref, cw_ref, cb_ref, wd32_ref, g3_ref, wpg32_ref,
                p_ref, wpp32_ref, gf_ref, out_ref, wg_ref, wu_ref, wd_ref, wpg_ref, wpp_ref,
                act_ref, carry_ref, *, tiles_per_seq):
    step = pl.program_id(0)

    @pl.when(step < WEIGHT_CAST_STEPS)
    def _():
        _cast_weight_rows(step, (wg32_ref, wu32_ref, wd32_ref, wpg32_ref, wpp32_ref),
                          (wg_ref, wu_ref, wd_ref, wpg_ref, wpp_ref))

    @pl.when(step >= WEIGHT_CAST_STEPS)
    def _():
        _ffn_tile(step - WEIGHT_CAST_STEPS, x_ref, dx_ref, g2_ref, wg_ref, wu_ref, cw_ref, cb_ref, wd_ref, g3_ref, wpg_ref,
                  p_ref, wpp_ref, gf_ref, out_ref, act_ref, carry_ref, tiles_per_seq)


def _ffn_tile(tile, x_ref, dx_ref, g2_ref, wg_ref, wu_ref, cw_ref, cb_ref, wd_ref, g3_ref, wpg_ref,
              p_ref, wpp_ref, gf_ref, out_ref, act_ref, carry_ref, tiles_per_seq):
    tm = x_ref.shape[0]

    @pl.when(tile % tiles_per_seq == 0)
    def _():
        carry_ref[...] = jnp.zeros_like(carry_ref)

    subs = [slice(s * SUB_TILE, (s + 1) * SUB_TILE) for s in range(tm // SUB_TILE)]
    hs = [x_ref[rows, :] + dx_ref[rows, :].astype(F32) for rows in subs]
    u2s = [_rms(h, g2_ref[...]).astype(BF16) for h in hs]
    row = lax.broadcasted_iota(jnp.int32, (SUBLANES, FFN_CHUNK), 0)
    for c in range(D_FF // FFN_CHUNK):
        sl = slice(c * FFN_CHUNK, (c + 1) * FFN_CHUNK)
        prev = carry_ref[:, sl]
        for rows, u2 in zip(subs, u2s):
            gp = _dot(u2, wg_ref[:, sl])
            up = _dot(u2, wu_ref[:, sl])
            r1 = pltpu.roll(gp, 1, 0)
            r2 = pltpu.roll(gp, 2, 0)
            r1 = jnp.concatenate([jnp.where(row < 1, pltpu.roll(prev, 1, 0), r1[:SUBLANES]), r1[SUBLANES:]], axis=0)
            r2 = jnp.concatenate([jnp.where(row < 2, pltpu.roll(prev, 2, 0), r2[:SUBLANES]), r2[SUBLANES:]], axis=0)
            gate = cw_ref[0:1, sl] * r2 + cw_ref[1:2, sl] * r1 + cw_ref[2:3, sl] * gp + cb_ref[:, sl]
            act_ref[rows, sl] = (jax.nn.gelu(gate) * up).astype(BF16)
            prev = gp[SUB_TILE - SUBLANES:, :]
        carry_ref[:, sl] = prev
    for rows, h in zip(subs, hs):
        h = h + _dot(act_ref[rows, :], wd_ref[...])
        u3 = _rms(h, g3_ref[...]).astype(BF16)
        h = h + jax.nn.sigmoid(_dot(u3, wpg_ref[...])) * _dot(p_ref[rows, :].astype(BF16), wpp_ref[...])
        out_ref[rows, :] = _rms(h, gf_ref[...])


def _ffn(x2, dx, seq, p2, norm_g, w_gate, w_up, conv_w, conv_b, w_down, ple_g, ple_w_gate, ple_w_proj, final_g):
    n = x2.shape[0]
    tm = TOKEN_TILE
    k = WEIGHT_CAST_STEPS
    row = lambda w: pl.BlockSpec((tm, w), lambda i: (jnp.maximum(i - k, 0), 0))
    wrows = lambda a: pl.BlockSpec((a.shape[0] // k, a.shape[1]), lambda i: (jnp.minimum(i, k - 1), 0))
    resident = lambda a: pltpu.VMEM(a.shape, BF16)
    vec = lambda a: a.reshape(1, -1)
    weights = (w_gate, w_up, w_down, ple_w_gate, ple_w_proj)
    return pl.pallas_call(
        functools.partial(_ffn_kernel, tiles_per_seq=seq // tm),
        grid=(k + n // tm,),
        in_specs=[row(D_MODEL), row(D_MODEL), _const_spec((1, D_MODEL)), wrows(w_gate),
                  wrows(w_up), _const_spec((CONV_WIDTH, D_FF)), _const_spec((1, D_FF)),
                  wrows(w_down), _const_spec((1, D_MODEL)), wrows(ple_w_gate),
                  row(PLE_DIM), wrows(ple_w_proj), _const_spec((1, D_MODEL))],
        out_specs=row(D_MODEL),
        out_shape=jax.ShapeDtypeStruct((n, D_MODEL), F32),
        scratch_shapes=[resident(w) for w in weights]
                       + [pltpu.VMEM((tm, D_FF), BF16), pltpu.VMEM((SUBLANES, D_FF), F32)],
        compiler_params=_params("arbitrary"),
        name="ffn",
    )(x2, dx, vec(norm_g), w_gate, w_up, conv_w, vec(conv_b),
      w_down, vec(ple_g), ple_w_gate, p2, ple_w_proj, vec(final_g))


def _layer(h2, bsz, seq, p2, mix_norm_g, w_in, gate_b, ssm_lam_re, ssm_lam_im, ssm_log_dt, ssm_b_re,
           ssm_b_im, ssm_c_re, ssm_c_im, ssm_d, ssm_glu_w, ssm_glu_b, w_branch_a, w_branch_b, w_out,
           ffn_norm_g, ffn_w_gate, ffn_w_up, ffn_conv_w, ffn_conv_b, ffn_w_down,
           ple_norm_g, ple_w_gate, ple_w_proj, out_norm_g):
    u_rows, qkv, ga, gb = _in_proj(h2, seq, mix_norm_g, w_in, gate_b)
    ops = _ssm_operators(ssm_lam_re, ssm_lam_im, ssm_log_dt, ssm_b_re, ssm_b_im, ssm_c_re, ssm_c_im)
    ys_rows = _ssm_scan(u_rows, bsz, ssm_d.reshape(-1), ops)
    attn_outs = [_attn_group(*group) for group in qkv]
    dx = _merge(seq, ys_rows, ssm_glu_w, ssm_glu_b, w_branch_a, attn_outs, w_branch_b, ga, gb, w_out)
    return _ffn(h2, dx, seq, p2, ffn_norm_g, ffn_w_gate, ffn_w_up, ffn_conv_w, ffn_conv_b, ffn_w_down,
                ple_norm_g, ple_w_gate, ple_w_proj, out_norm_g)


def kernel(x, p, mix_norm_g, w_in, gate_b, ssm_lam_re, ssm_lam_im, ssm_log_dt, ssm_b_re, ssm_b_im, ssm_c_re, ssm_c_im, ssm_d, ssm_glu_w, ssm_glu_b, w_branch_a, w_branch_b, w_out, ffn_norm_g, ffn_w_gate, ffn_w_up, ffn_conv_w, ffn_conv_b, ffn_w_down, ple_norm_g, ple_w_gate, ple_w_proj, final_norm_g):
    bsz, seq, _ = x.shape
    depth = p.shape[0]
    assert depth == 1, "the final norm is fused into the layer's last kernel"
    h2 = x.reshape(bsz * seq, D_MODEL)
    out = _layer(h2, bsz, seq, p[0].reshape(bsz * seq, PLE_DIM), mix_norm_g[0], w_in[0], gate_b[0],
                 ssm_lam_re[0], ssm_lam_im[0], ssm_log_dt[0], ssm_b_re[0], ssm_b_im[0], ssm_c_re[0],
                 ssm_c_im[0], ssm_d[0], ssm_glu_w[0], ssm_glu_b[0], w_branch_a[0], w_branch_b[0],
                 w_out[0], ffn_norm_g[0], ffn_w_gate[0], ffn_w_up[0], ffn_conv_w[0], ffn_conv_b[0],
                 ffn_w_down[0], ple_norm_g[0], ple_w_gate[0], ple_w_proj[0], final_norm_g)
    return out.reshape(bsz, seq, D_MODEL)
```

```python
import functools

import jax
import jax.numpy as jnp
from jax import lax
from jax.experimental import pallas as pl
from jax.experimental.pallas import tpu as pltpu

F32 = jnp.float32
BF16 = jnp.bfloat16

D_MODEL = 1024
EPS = 1e-6
PLE_DIM = 256
SSM_GROUP = 16
SSM_STATE = 64
SSM_WIDTH = 512
SSM_GROUPS = SSM_WIDTH // SSM_GROUP
HEAD_DIM = 64
DILATIONS = (1, 4, 16)
WINDOW_KEYS = 128
HEADS_PER_GROUP = 4
GROUP_WIDTH = HEADS_PER_GROUP * HEAD_DIM
ATTN_WIDTH = len(DILATIONS) * GROUP_WIDTH
ROT_DIM = HEAD_DIM // 4
ROPE_THETA = 500000.0
NEG_BIG = -1e30
D_FF = 2816
CONV_WIDTH = 3
OFF_Q = SSM_WIDTH
OFF_K = OFF_Q + ATTN_WIDTH
OFF_V = OFF_K + ATTN_WIDTH
OFF_GA = OFF_V + ATTN_WIDTH
OFF_GB = OFF_GA + D_MODEL
IN_WIDTH = OFF_GB + D_MODEL

LANES = 128
SUBLANES = 8
VMEM_LIMIT_BYTES = 56 * 1024 * 1024

TOKEN_TILE = 1024
SUB_TILE = 256
COL_TILE = 256
SSM_CHUNK = 16
SSM_CHUNKS_PER_TILE = 256
SSM_TILE = SSM_CHUNK * SSM_CHUNKS_PER_TILE
SSM_CK = SSM_CHUNK * SSM_GROUP
SSM_ROW = SSM_CHUNK * SSM_WIDTH
SSM_PAIRS = SSM_GROUPS // 2
SSM_LOG_STEPS = 8
ATTN_QB = 128
ATTN_STEP_ROWS = 4096
FFN_CHUNK = 256
WEIGHT_CAST_STEPS = 8


def _dot(a, b):
    return jnp.dot(a, b, preferred_element_type=F32)


def _rms(x, g):
    var = jnp.mean(x * x, axis=-1, keepdims=True)
    return x * lax.rsqrt(var + EPS) * g


def _const_spec(shape):
    nd = len(shape)
    return pl.BlockSpec(shape, lambda *_: (0,) * nd, pipeline_mode=pl.Buffered(1))


def _params(*sem):
    return pltpu.CompilerParams(dimension_semantics=sem, vmem_limit_bytes=VMEM_LIMIT_BYTES)


def _rope(z, cos, sin_lo, sin_hi):
    up = pltpu.roll(z, LANES - ROT_DIM // 2, 1)
    dn = pltpu.roll(z, ROT_DIM // 2, 1)
    return z * cos + up * sin_lo + dn * sin_hi


def _store_residue_major(stage_ref, out_ref, z, dil, tok0):
    rows, first = z.shape[0] // dil, tok0 // dil
    for j in range(z.shape[1] // LANES):
        sl = slice(j * LANES, (j + 1) * LANES)
        stage_ref[...] = z[:, sl]
        for r in range(dil):
            out_ref[r, first:first + rows, sl] = stage_ref[pl.ds(r, rows, stride=dil), :].astype(out_ref.dtype)


def _store_chunk_rows(stage_ref, out_ref, z, c0):
    rows, width = z.shape[0] // SSM_CHUNK, z.shape[1]
    for j in range(width // LANES):
        stage_ref[...] = z[:, j * LANES:(j + 1) * LANES]
        for t in range(SSM_CHUNK):
            col = t * width + j * LANES
            out_ref[c0:c0 + rows, col:col + LANES] = (
                stage_ref[pl.ds(t, rows, stride=SSM_CHUNK), :].astype(out_ref.dtype))


def _load_chunk_rows(stage_ref, in_ref, width, c0, rows):
    slabs = []
    for j in range(width // LANES):
        for t in range(SSM_CHUNK):
            col = t * width + j * LANES
            stage_ref[pl.ds(t, rows, stride=SSM_CHUNK), :] = in_ref[c0:c0 + rows, col:col + LANES].astype(F32)
        slabs.append(stage_ref[...])
    return jnp.concatenate(slabs, axis=1)


def _in_proj_kernel(x_ref, g_ref, ca_ref, sa_ref, cb_ref, sb_ref, sign_ref, w32_ref, bg_ref,
                    s_ref, q0_ref, k0_ref, v0_ref, q1_ref, k1_ref, v1_ref, q2_ref, k2_ref, v2_ref,
                    ga_ref, gb_ref, w_ref, stage_ref):
    step = pl.program_id(0)

    @pl.when(step < WEIGHT_CAST_STEPS)
    def _():
        _cast_weight_rows(step, (w32_ref,), (w_ref,))

    @pl.when(step >= WEIGHT_CAST_STEPS)
    def _():
        _in_proj_tile(x_ref, g_ref, ca_ref, sa_ref, cb_ref, sb_ref, sign_ref, w_ref, bg_ref,
                      s_ref, q0_ref, k0_ref, v0_ref, q1_ref, k1_ref, v1_ref, q2_ref, k2_ref, v2_ref,
                      ga_ref, gb_ref, stage_ref)


def _in_proj_tile(x_ref, g_ref, ca_ref, sa_ref, cb_ref, sb_ref, sign_ref, w_ref, bg_ref,
                  s_ref, q0_ref, k0_ref, v0_ref, q1_ref, k1_ref, v1_ref, q2_ref, k2_ref, v2_ref,
                  ga_ref, gb_ref, stage_ref):
    u = _rms(x_ref[...], g_ref[...]).astype(BF16)
    _store_chunk_rows(stage_ref, s_ref, _dot(u, w_ref[:, :OFF_Q]), 0)
    ca, sa, cb, sb = ca_ref[...], sa_ref[...], cb_ref[...], sb_ref[...]
    cos = ca * cb - sa * sb
    sin = sa * cb + ca * sb
    slo, shi = sin * sign_ref[0:1, :], sin * sign_ref[1:2, :]
    scale = HEAD_DIM ** -0.5

    def rope(z):
        return jnp.concatenate([_rope(z[:, j * LANES:(j + 1) * LANES], cos, slo, shi)
                                for j in range(z.shape[1] // LANES)], axis=1)

    q = rope(_dot(u, w_ref[:, OFF_Q:OFF_K])) * scale
    k = rope(_dot(u, w_ref[:, OFF_K:OFF_V]))
    v = _dot(u, w_ref[:, OFF_V:OFF_GA])
    for z, refs in ((q, (q0_ref, q1_ref, q2_ref)), (k, (k0_ref, k1_ref, k2_ref)), (v, (v0_ref, v1_ref, v2_ref))):
        refs[0][...] = z[:, :GROUP_WIDTH].astype(BF16)
        for grp in (1, 2):
            _store_residue_major(stage_ref, refs[grp], z[:, grp * GROUP_WIDTH:(grp + 1) * GROUP_WIDTH],
                                 DILATIONS[grp], 0)
    ga_ref[...] = jax.nn.sigmoid(_dot(u, w_ref[:, OFF_GA:OFF_GB]) + bg_ref[:, :D_MODEL]).astype(BF16)
    gb_ref[...] = jax.nn.sigmoid(_dot(u, w_ref[:, OFF_GB:]) + bg_ref[:, D_MODEL:]).astype(BF16)


def _rope_tables(seq, tm):
    half = ROT_DIM // 2
    freqs = ROPE_THETA ** (-jnp.arange(half, dtype=F32) * (2.0 / ROT_DIM))
    head = jnp.concatenate([freqs, freqs, jnp.zeros((HEAD_DIM - ROT_DIM,), F32)])
    lane_freq = jnp.tile(head, LANES // HEAD_DIM)[None, :]
    base = jnp.arange(0, seq, tm, dtype=F32)[:, None] * lane_freq
    offs = jnp.arange(tm, dtype=F32)[:, None] * lane_freq
    in_head = jnp.arange(LANES) % HEAD_DIM
    sign = jnp.zeros((SUBLANES, LANES), F32)
    sign = sign.at[0].set(jnp.where(in_head < half, -1.0, 0.0))
    sign = sign.at[1].set(jnp.where((in_head >= half) & (in_head < ROT_DIM), 1.0, 0.0))
    n_tiles = seq // tm
    return (jnp.cos(base).reshape(n_tiles, 1, LANES), jnp.sin(base).reshape(n_tiles, 1, LANES),
            jnp.cos(offs), jnp.sin(offs), sign)


def _in_proj(x2, seq, norm_g, w_in, gate_b):
    n = x2.shape[0]
    tm = TOKEN_TILE
    tiles_per_seq = seq // tm
    bsz = n // seq
    cos_a, sin_a, cos_b, sin_b, sign = _rope_tables(seq, tm)

    k = WEIGHT_CAST_STEPS
    tile = lambda i: jnp.maximum(i - k, 0)
    row = lambda w: pl.BlockSpec((tm, w), lambda i: (tile(i), 0))
    tile_tab = pl.BlockSpec((None, 1, LANES), lambda i: (tile(i) % tiles_per_seq, 0, 0))
    res = lambda d: pl.BlockSpec((None, d, tm // d, GROUP_WIDTH),
                                 lambda i: (tile(i) // tiles_per_seq, 0, tile(i) % tiles_per_seq, 0))
    res_shape = lambda d: jax.ShapeDtypeStruct((bsz, d, seq // d, GROUP_WIDTH), BF16)
    nat_shape = jax.ShapeDtypeStruct((n, GROUP_WIDTH), BF16)
    d1, d2 = DILATIONS[1], DILATIONS[2]
    outs = pl.pallas_call(
        _in_proj_kernel,
        grid=(k + n // tm,),
        in_specs=[row(D_MODEL), _const_spec((1, D_MODEL)), tile_tab, tile_tab,
                  _const_spec((tm, LANES)), _const_spec((tm, LANES)), _const_spec((SUBLANES, LANES)),
                  pl.BlockSpec((D_MODEL // k, IN_WIDTH), lambda i: (jnp.minimum(i, k - 1), 0)),
                  _const_spec((1, 2 * D_MODEL))],
        out_specs=[pl.BlockSpec((tm // SSM_CHUNK, SSM_ROW), lambda i: (tile(i), 0))]
                  + [row(GROUP_WIDTH)] * 3 + [res(d1)] * 3 + [res(d2)] * 3 + [row(D_MODEL), row(D_MODEL)],
        out_shape=[jax.ShapeDtypeStruct((n // SSM_CHUNK, SSM_ROW), BF16)] + [nat_shape] * 3
                  + [res_shape(d1)] * 3 + [res_shape(d2)] * 3
                  + [jax.ShapeDtypeStruct((n, D_MODEL), BF16), jax.ShapeDtypeStruct((n, D_MODEL), BF16)],
        scratch_shapes=[pltpu.VMEM((D_MODEL, IN_WIDTH), BF16), pltpu.VMEM((tm, LANES), F32)],
        compiler_params=_params("arbitrary"),
        name="in_proj",
    )(x2, norm_g.reshape(1, D_MODEL), cos_a, sin_a, cos_b, sin_b, sign, w_in, gate_b.reshape(1, 2 * D_MODEL))
    u, q0, k0, v0, q1, k1, v1, q2, k2, v2, ga, gb = outs
    nat4 = lambda a: a.reshape(bsz, 1, seq, GROUP_WIDTH)
    qkv = ((nat4(q0), nat4(k0), nat4(v0)), (q1, k1, v1), (q2, k2, v2))
    return u, qkv, ga, gb


def _cmul(ar, ai, br, bi):
    return ar * br - ai * bi, ar * bi + ai * br


def _discretise(lr, li, dt):
    mag = jnp.exp(lr * dt)
    ar = mag * jnp.cos(li * dt)
    ai = mag * jnp.sin(li * dt)
    den = lr * lr + li * li
    cr = ((ar - 1.0) * lr + ai * li) / den
    ci = (ai * lr - (ar - 1.0) * li) / den
    return ar, ai, cr, ci


def _ssm_prep_kernel(lr_ref, li_ref, ldt_ref, brt_ref, bit_ref, cre_ref, cim_ref, lrf_ref, lif_ref, ldtf_ref,
                     w2_ref, e2_ref, apsr_ref, apsi_ref, aplr_ref, apli_ref, apbr_ref, apbi_ref, er_ref, ei_ref):
    g_n, t_n, h_n, sw = SSM_GROUPS, SSM_CHUNK, SSM_GROUP, 2 * SSM_STATE
    ar, ai, cr, ci = _discretise(lr_ref[...], li_ref[...], jnp.exp(ldt_ref[...]))
    brt, bit = brt_ref[...], bit_ref[...]
    bbr = cr * brt - ci * bit
    bbi = cr * bit + ci * brt
    cre, cim = cre_ref[...], cim_ref[...]
    by_group = lambda a: a.reshape(g_n, h_n, sw)
    pr, pi = jnp.ones_like(ar), jnp.zeros_like(ai)
    for j in range(t_n):
        rows = slice((t_n - 1 - j) * h_n, (t_n - j) * h_n)
        rr, ri = _cmul(pr, pi, bbr, bbi)
        er_ref[:, rows, :] = by_group(rr)
        ei_ref[:, rows, :] = by_group(ri)
        e2_ref[:, rows, :sw] = by_group(rr).astype(BF16)
        e2_ref[:, rows, sw:] = by_group(ri).astype(BF16)
        pr, pi = _cmul(pr, pi, ar, ai)
        rows = slice(j * h_n, (j + 1) * h_n)
        w2_ref[:, rows, SSM_CK:SSM_CK + sw] = by_group(cre * pr - cim * pi).astype(BF16)
        w2_ref[:, rows, SSM_CK + sw:] = by_group(-cre * pi - cim * pr).astype(BF16)

    nt = (((1,), (1,)), ((), ()))
    hi = lax.Precision.HIGHEST

    def toeplitz(g, _):
        rows = pl.ds(pl.multiple_of(g * h_n, h_n), h_n)
        krev = (lax.dot_general(cre_ref[rows, :], er_ref[g], nt, precision=hi, preferred_element_type=F32)
                - lax.dot_general(cim_ref[rows, :], ei_ref[g], nt, precision=hi, preferred_element_type=F32))
        kext = jnp.concatenate([krev, jnp.zeros_like(krev)], axis=1)
        for t in range(t_n):
            off = (t_n - 1 - t) * h_n
            win = kext if off == 0 else pltpu.roll(kext, 2 * SSM_CK - off, 1)
            w2_ref[g, t * h_n:(t + 1) * h_n, :SSM_CK] = win[:, :SSM_CK].astype(BF16)
        return 0
    lax.fori_loop(0, g_n, toeplitz, 0, unroll=4)

    acr, aci, _, _ = _discretise(lrf_ref[...], lif_ref[...], jnp.exp(ldtf_ref[...]))
    for _ in range(4):
        acr, aci = _cmul(acr, aci, acr, aci)
    shape = (SUBLANES, acr.shape[1])
    row = lax.broadcasted_iota(jnp.int32, shape, 0)
    qr, qi = jnp.ones(shape, F32), jnp.zeros(shape, F32)
    apsr_ref[...] = jnp.zeros_like(apsr_ref)
    apsi_ref[...] = jnp.zeros_like(apsi_ref)
    for k in range(SSM_LOG_STEPS + 1):
        for p2 in range(SSM_PAIRS):
            apsr_ref[p2, k:k + 1, :] = acr[:, p2 * sw:(p2 + 1) * sw]
            apsi_ref[p2, k:k + 1, :] = aci[:, p2 * sw:(p2 + 1) * sw]
        if (1 << k) < SUBLANES:
            nr, ni = _cmul(qr, qi, acr, aci)
            bit_set = (row & (1 << k)) != 0
            qr, qi = jnp.where(bit_set, nr, qr), jnp.where(bit_set, ni, qi)
        if (1 << k) == SUBLANES:
            a8r, a8i = acr, aci
        acr, aci = _cmul(acr, aci, acr, aci)
    br, bi = jnp.ones_like(a8r), jnp.zeros_like(a8i)
    blk_r, blk_i = [], []
    for _ in range(SSM_CHUNKS_PER_TILE // SUBLANES):
        blk_r.append(br)
        blk_i.append(bi)
        br, bi = _cmul(br, bi, a8r, a8i)
    blk_r, blk_i = jnp.concatenate(blk_r, axis=0), jnp.concatenate(blk_i, axis=0)
    for p2 in range(SSM_PAIRS):
        lanes = slice(p2 * sw, (p2 + 1) * sw)
        aplr_ref[p2], apli_ref[p2] = qr[:, lanes], qi[:, lanes]
        apbr_ref[p2], apbi_ref[p2] = blk_r[:, lanes], blk_i[:, lanes]


def _pad_pair_lanes(a):
    z = jnp.zeros_like(a)
    even = (jnp.arange(a.shape[0]) % 2 == 0)[:, None, None]
    padded = jnp.where(even, jnp.concatenate([a, z], -1), jnp.concatenate([z, a], -1))
    return padded.reshape(a.shape[0] * a.shape[1], 2 * a.shape[2])


def _ssm_operators(lam_re, lam_im, log_dt, b_re, b_im, c_re, c_im):
    g, p, h = SSM_GROUPS, SSM_STATE, SSM_GROUP
    gp, sw = g * p, 2 * p
    rep = lambda a: jnp.repeat(jnp.tile(a, (1, 2)), h, axis=0)
    ldt2 = jnp.broadcast_to(log_dt[:, None], (g, p))
    full = lambda shape: pl.BlockSpec(shape, lambda: (0,) * len(shape))
    in_arrays = (rep(lam_re), rep(lam_im), rep(ldt2),
                 _pad_pair_lanes(b_re.transpose(0, 2, 1)), _pad_pair_lanes(b_im.transpose(0, 2, 1)),
                 _pad_pair_lanes(c_re), _pad_pair_lanes(c_im),
                 lam_re.reshape(1, gp), lam_im.reshape(1, gp), ldt2.reshape(1, gp))
    out_shapes = ([((g, SSM_CK, SSM_CK + 2 * sw), BF16), ((g, SSM_CK, 2 * sw), BF16)]
                  + [((SSM_PAIRS, 2 * SUBLANES, sw), F32)] * 2
                  + [((SSM_PAIRS, SUBLANES, sw), F32)] * 2
                  + [((SSM_PAIRS, SSM_CHUNKS_PER_TILE // SUBLANES, sw), F32)] * 2)
    w2, e2, *powers = pl.pallas_call(
        _ssm_prep_kernel,
        in_specs=[full(a.shape) for a in in_arrays],
        out_specs=[full(s) for s, _ in out_shapes],
        out_shape=[jax.ShapeDtypeStruct(s, dt) for s, dt in out_shapes],
        scratch_shapes=[pltpu.VMEM((g, SSM_CK, sw), F32), pltpu.VMEM((g, SSM_CK, sw), F32)],
        compiler_params=pltpu.CompilerParams(vmem_limit_bytes=VMEM_LIMIT_BYTES),
        name="ssm_prep",
    )(*in_arrays)
    return (w2, e2.reshape(SSM_PAIRS, 2 * SSM_CK, 2 * sw), *powers)


def _shift_rows(z, s, row):
    if s % SUBLANES == 0:
        return jnp.concatenate([jnp.zeros((s, z.shape[1]), z.dtype), z[:-s]], axis=0)
    return jnp.where(row >= s, pltpu.roll(z, s, 0), 0.0)


def _ssm_scan_kernel(u_ref, d_ref, w2_ref, e2_ref, apsr_ref, apsi_ref, aplr_ref, apli_ref, apbr_ref, apbi_ref,
                     y_ref, xs_ref, sc_ref, yt_ref, carry_ref, loc_ref, sin_ref):
    t_n, h_n, c_n = SSM_CHUNK, SSM_GROUP, SSM_CHUNKS_PER_TILE
    n_slab = SSM_WIDTH // LANES
    pairs_per_slab = LANES // (2 * h_n)
    sw = 2 * SSM_STATE
    blk_n = SUBLANES
    n_blk = c_n // blk_n
    log_blk = blk_n.bit_length() - 1

    @pl.when(pl.program_id(1) == 0)
    def _():
        carry_ref[...] = jnp.zeros_like(carry_ref)

    for t in range(t_n):
        for j in range(n_slab):
            col = t * SSM_WIDTH + j * LANES
            blk = u_ref[:, col:col + LANES].T
            xs_ref[j * pairs_per_slab:(j + 1) * pairs_per_slab, :, t * h_n:(t + 1) * h_n, :] = (
                blk.reshape(pairs_per_slab, 2, h_n, c_n))

    row = lax.broadcasted_iota(jnp.int32, (n_blk, sw), 0)
    nt = (((1,), (1,)), ((), ()))
    tn = (((0,), (0,)), ((), ()))

    def local_states(pr, _):
        xp = xs_ref[pr].reshape(2 * SSM_CK, c_n)
        loc = lax.dot_general(xp, e2_ref[pr], tn, preferred_element_type=F32)
        loc_ref[pr, 0] = loc[:, :sw]
        loc_ref[pr, 1] = loc[:, sw:]
        return 0
    lax.fori_loop(0, SSM_PAIRS, local_states, 0, unroll=8)

    def chunk_scan(pr, _):
        slot = pr % 2
        power = lambda k: (apsr_ref[pr, k:k + 1, :], apsi_ref[pr, k:k + 1, :])
        zr, zi = [], []
        for lo in range(blk_n):
            rows = pl.ds(lo, n_blk, stride=blk_n)
            xr, xi = loc_ref[pr, 0, rows, :], loc_ref[pr, 1, rows, :]
            if lo:
                dr, di = _cmul(zr[-1], zi[-1], *power(0))
                xr, xi = xr + dr, xi + di
            zr.append(xr)
            zi.append(xi)
        er, ei = zr[-1], zi[-1]
        s = 1
        while s < n_blk:
            dr, di = _cmul(_shift_rows(er, s, row), _shift_rows(ei, s, row),
                           *power(log_blk + s.bit_length() - 1))
            er, ei = er + dr, ei + di
            s *= 2
        cr, ci = carry_ref[pr, 0:1, :], carry_ref[pr, 1:2, :]
        hr, hi = _cmul(apbr_ref[pr], apbi_ref[pr], cr, ci)
        br, bi = _shift_rows(er, 1, row) + hr, _shift_rows(ei, 1, row) + hi
        for lo in range(blk_n):
            sr, si = _cmul(br, bi, aplr_ref[pr, lo:lo + 1, :], apli_ref[pr, lo:lo + 1, :])
            if lo:
                sr, si = sr + zr[lo - 1], si + zi[lo - 1]
            rows = pl.ds(lo, n_blk, stride=blk_n)
            sin_ref[slot, 0, rows, :] = sr
            sin_ref[slot, 1, rows, :] = si
        sc_ref[pr, :, :sw] = sin_ref[slot, 0].astype(BF16)
        sc_ref[pr, :, sw:] = sin_ref[slot, 1].astype(BF16)
        nr, ni = _cmul(cr, ci, *power(SSM_LOG_STEPS))
        carry_ref[pr, 0:1, :] = er[n_blk - 1:n_blk, :] + nr
        carry_ref[pr, 1:2, :] = ei[n_blk - 1:n_blk, :] + ni
        return 0
    lax.fori_loop(0, SSM_PAIRS, chunk_scan, 0, unroll=2)

    def outputs(g, _):
        pr = g // 2
        yg = (_dot(w2_ref[g, :, :SSM_CK], xs_ref[pr, g % 2])
              + lax.dot_general(w2_ref[g, :, SSM_CK:], sc_ref[pr], nt, preferred_element_type=F32))
        yt_ref[:, pl.ds(pl.multiple_of(g * h_n, h_n), h_n), :] = yg.reshape(t_n, h_n, c_n)
        return 0
    lax.fori_loop(0, SSM_GROUPS, outputs, 0, unroll=8)

    for t in range(t_n):
        for j in range(n_slab):
            sl = slice(j * LANES, (j + 1) * LANES)
            col = t * SSM_WIDTH + j * LANES
            y_ref[:, col:col + LANES] = (
                yt_ref[t, sl, :].T + d_ref[:, sl] * u_ref[:, col:col + LANES].astype(F32)).astype(BF16)


def _ssm_scan(u_rows, bsz, d_skip, ops):
    g, p, c_n = SSM_GROUPS, SSM_STATE, SSM_CHUNKS_PER_TILE
    tiles = u_rows.shape[0] // (bsz * c_n)
    tile = pl.BlockSpec((c_n, SSM_ROW), lambda b, i: (b * tiles + i, 0))
    return pl.pallas_call(
        _ssm_scan_kernel,
        grid=(bsz, tiles),
        in_specs=[tile, _const_spec((1, SSM_WIDTH))] + [_const_spec(op.shape) for op in ops],
        out_specs=tile,
        out_shape=jax.ShapeDtypeStruct(u_rows.shape, BF16),
        scratch_shapes=[pltpu.VMEM((SSM_PAIRS, 2, SSM_CK, c_n), BF16),
                        pltpu.VMEM((SSM_PAIRS, c_n, 4 * p), BF16),
                        pltpu.VMEM((SSM_CHUNK, SSM_WIDTH, c_n), F32),
                        pltpu.VMEM((SSM_PAIRS, SUBLANES, 2 * p), F32),
                        pltpu.VMEM((SSM_PAIRS, 2, c_n, 2 * p), F32),
                        pltpu.VMEM((2, 2, c_n, 2 * p), F32)],
        compiler_params=_params("parallel", "arbitrary"),
        name="ssm_scan",
    )(u_rows, d_skip.reshape(1, SSM_WIDTH), *ops)


def _attn_kernel(q_ref, kc_ref, kp_ref, vc_ref, vp_ref, o_ref, lse_ref):
    qb, nk = ATTN_QB, ATTN_QB + WINDOW_KEYS
    row = lax.broadcasted_iota(jnp.int32, (qb, nk), 0)
    col = lax.broadcasted_iota(jnp.int32, (qb, nk), 1)
    dist = row + WINDOW_KEYS - col
    in_band = (dist >= 0) & (dist <= WINDOW_KEYS)
    in_band_first = in_band & ((col >= WINDOW_KEYS) | (pl.program_id(2) > 0))
    lane = lax.broadcasted_iota(jnp.int32, (qb, LANES), 1)
    first_head = lane < HEAD_DIM
    nt = (((1,), (1,)), ((), ()))
    n_res, n_rows = o_ref.shape[0], o_ref.shape[1]
    for res in range(n_res):
        for sb in range(n_rows // qb):
            rows = slice(sb * qb, (sb + 1) * qb)
            valid = in_band if sb else in_band_first
            for pair in range(GROUP_WIDTH // LANES):
                cols = slice(pair * LANES, (pair + 1) * LANES)
                qp = q_ref[res, rows, cols]
                if sb:
                    window = slice(sb * qb - WINDOW_KEYS, (sb + 1) * qb)
                    kp, vp = kc_ref[res, window, cols], vc_ref[res, window, cols]
                else:
                    kp = jnp.concatenate([kp_ref[res, :, cols], kc_ref[res, :qb, cols]], axis=0)
                    vp = jnp.concatenate([vp_ref[res, :, cols], vc_ref[res, :qb, cols]], axis=0)
                outs, lses = [], []
                for sel in (first_head, ~first_head):
                    qm = jnp.where(sel, qp, jnp.zeros_like(qp))
                    s = lax.dot_general(qm, kp, nt, preferred_element_type=F32)
                    s = jnp.where(valid, s, NEG_BIG)
                    m = jnp.max(s, axis=-1, keepdims=True)
                    e = jnp.exp(s - m)
                    den = jnp.sum(e, axis=-1, keepdims=True)
                    outs.append(_dot(e.astype(BF16), vp) / den)
                    lses.append(m + jnp.log(den))
                o_ref[res, rows, cols] = jnp.where(first_head, outs[0], outs[1]).astype(BF16)
                lse_ref[res, rows, cols] = jnp.where(first_head, lses[0], lses[1])


def _attn_group(q4, k4, v4):
    bsz, dil, lr, _ = q4.shape
    rows = min(ATTN_STEP_ROWS, lr)
    n_res = ATTN_STEP_ROWS // rows
    back = rows // WINDOW_KEYS
    q_spec = pl.BlockSpec((None, n_res, rows, GROUP_WIDTH), lambda b, r, i: (b, r, i, 0))
    prev_spec = pl.BlockSpec((None, n_res, WINDOW_KEYS, GROUP_WIDTH),
                             lambda b, r, i: (b, r, jnp.maximum(i * back - 1, 0), 0))
    return pl.pallas_call(
        _attn_kernel,
        grid=(bsz, dil // n_res, lr // rows),
        in_specs=[q_spec, q_spec, prev_spec, q_spec, prev_spec],
        out_specs=[q_spec, q_spec],
        out_shape=[jax.ShapeDtypeStruct(q4.shape, BF16), jax.ShapeDtypeStruct(q4.shape, F32)],
        compiler_params=_params("parallel", "parallel", "arbitrary"),
        name=f"attn_d{dil}",
    )(q4, k4, k4, v4, v4)


def _load_token_major(stage_ref, in_ref, tok0, ntok, lanes):
    dil = in_ref.shape[0]
    first, rows = tok0 // dil, ntok // dil
    if dil == 1:
        return in_ref[0, first:first + rows, lanes].astype(F32)
    for r in range(dil):
        stage_ref[pl.ds(r, rows, stride=dil), :] = in_ref[r, first:first + rows, lanes].astype(F32)
    return stage_ref[...]


def _merge_kernel(x_ref, ys_ref, gluw_ref, glub_ref, wa_ref,
                  o0_ref, o1_ref, o2_ref, l0_ref, l1_ref, l2_ref, wb_ref,
                  ga_ref, gb_ref, wout_ref, h_ref, stage_ref, y_ref, ya_ref, attn_ref, mix_ref):
    pieces = lambda width: [slice(c * COL_TILE, (c + 1) * COL_TILE) for c in range(width // COL_TILE)]
    for s in range(x_ref.shape[0] // SUB_TILE):
        tok0 = s * SUB_TILE
        rows = slice(tok0, tok0 + SUB_TILE)
        stage = stage_ref.at[s]
        y_ref[s] = jax.nn.gelu(_load_chunk_rows(stage, ys_ref, SSM_WIDTH, tok0 // SSM_CHUNK, SUB_TILE // SSM_CHUNK))
        y_bf = y_ref[s].astype(BF16)
        for cols in pieces(SSM_WIDTH):
            gate = jax.nn.sigmoid(_dot(y_bf, gluw_ref[:, cols]) + glub_ref[:, cols])
            ya_ref[s, :, cols] = (y_ref[s, :, cols] * gate).astype(BF16)

        for j in range(GROUP_WIDTH // LANES):
            lanes = slice(j * LANES, (j + 1) * LANES)
            o, l = ([_load_token_major(stage, ref, tok0, SUB_TILE, lanes) for ref in refs]
                    for refs in ((o0_ref, o1_ref, o2_ref), (l0_ref, l1_ref, l2_ref)))
            top = jnp.maximum(jnp.maximum(l[0], l[1]), l[2])
            w = [jnp.exp(lg - top) for lg in l]
            attn_ref[s, :, lanes] = ((w[0] * o[0] + w[1] * o[1] + w[2] * o[2])
                                     / (w[0] + w[1] + w[2])).astype(BF16)

        for cols in pieces(D_MODEL):
            mix = (ga_ref[rows, cols].astype(F32) * _dot(ya_ref[s], wa_ref[:, cols])
                   + gb_ref[rows, cols].astype(F32) * _dot(attn_ref[s], wb_ref[:, cols]))
            mix_ref[s, :, cols] = mix.astype(BF16)
        for cols in pieces(D_MODEL):
            h_ref[rows, cols] = x_ref[rows, cols] + _dot(mix_ref[s], wout_ref[:, cols])


def _merge(x2, seq, ys_rows, glu_w, glu_b, w_a, attn_outs, w_b, ga, gb, w_out):
    n = x2.shape[0]
    tm = TOKEN_TILE
    tiles_per_seq = seq // tm
    row = lambda w: pl.BlockSpec((tm, w), lambda i: (i, 0))
    res = lambda d: pl.BlockSpec((None, d, tm // d, GROUP_WIDTH),
                                 lambda i: (i // tiles_per_seq, 0, i % tiles_per_seq, 0))
    (o0, l0), (o1, l1), (o2, l2) = attn_outs
    d0, d1, d2 = DILATIONS
    return pl.pallas_call(
        _merge_kernel,
        grid=(n // tm,),
        in_specs=[row(D_MODEL), pl.BlockSpec((tm // SSM_CHUNK, SSM_ROW), lambda i: (i, 0)),
                  _const_spec((SSM_WIDTH, SSM_WIDTH)), _const_spec((1, SSM_WIDTH)),
                  _const_spec((SSM_WIDTH, D_MODEL)),
                  res(d0), res(d1), res(d2), res(d0), res(d1), res(d2),
                  _const_spec((GROUP_WIDTH, D_MODEL)), row(D_MODEL), row(D_MODEL),
                  _const_spec((D_MODEL, D_MODEL))],
        out_specs=row(D_MODEL),
        out_shape=jax.ShapeDtypeStruct((n, D_MODEL), F32),
        scratch_shapes=[pltpu.VMEM((tm // SUB_TILE, SUB_TILE, LANES), F32),
                        pltpu.VMEM((tm // SUB_TILE, SUB_TILE, SSM_WIDTH), F32),
                        pltpu.VMEM((tm // SUB_TILE, SUB_TILE, SSM_WIDTH), BF16),
                        pltpu.VMEM((tm // SUB_TILE, SUB_TILE, GROUP_WIDTH), BF16),
                        pltpu.VMEM((tm // SUB_TILE, SUB_TILE, D_MODEL), BF16)],
        compiler_params=_params("parallel"),
        name="merge",
    )(x2, ys_rows, glu_w.astype(BF16), glu_b.reshape(1, SSM_WIDTH),
      w_a.astype(BF16), o0, o1, o2, l0, l1, l2, w_b.astype(BF16), ga, gb, w_out.astype(BF16))


def _cast_weight_rows(step, src_refs, dst_refs):
    for src, dst in zip(src_refs, dst_refs):
        rb = src.shape[0]
        dst[pl.ds(pl.multiple_of(step * rb, rb), rb), :] = src[...].astype(BF16)


def _ffn_kernel(h_ref, g2_ref, wg32_ref, wu32_ref, cw_ref, cb_ref, wd32_ref, g3_ref, wpg32_ref,
                p_ref, wpp32_ref, gf_ref, out_ref, wg_ref, wu_ref, wd_ref, wpg_ref, wpp_ref,
                act_ref, carry_ref, *, tiles_per_seq):
    step = pl.program_id(0)

    @pl.when(step < WEIGHT_CAST_STEPS)
    def _():
        _cast_weight_rows(step, (wg32_ref, wu32_ref, wd32_ref, wpg32_ref, wpp32_ref),
                          (wg_ref, wu_ref, wd_ref, wpg_ref, wpp_ref))

    @pl.when(step >= WEIGHT_CAST_STEPS)
    def _():
        _ffn_tile(step - WEIGHT_CAST_STEPS, h_ref, g2_ref, wg_ref, wu_ref, cw_ref, cb_ref, wd_ref, g3_ref, wpg_ref,
                  p_ref, wpp_ref, gf_ref, out_ref, act_ref, carry_ref, tiles_per_seq)


def _ffn_tile(tile, h_ref, g2_ref, wg_ref, wu_ref, cw_ref, cb_ref, wd_ref, g3_ref, wpg_ref,
              p_ref, wpp_ref, gf_ref, out_ref, act_ref, carry_ref, tiles_per_seq):
    tm = h_ref.shape[0]

    @pl.when(tile % tiles_per_seq == 0)
    def _():
        carry_ref[...] = jnp.zeros_like(carry_ref)

    subs = [slice(s * SUB_TILE, (s + 1) * SUB_TILE) for s in range(tm // SUB_TILE)]
    hs = [h_ref[rows, :] for rows in subs]
    u2s = [_rms(h, g2_ref[...]).astype(BF16) for h in hs]
    row = lax.broadcasted_iota(jnp.int32, (SUBLANES, FFN_CHUNK), 0)
    for c in range(D_FF // FFN_CHUNK):
        sl = slice(c * FFN_CHUNK, (c + 1) * FFN_CHUNK)
        prev = carry_ref[:, sl]
        for rows, u2 in zip(subs, u2s):
            gp = _dot(u2, wg_ref[:, sl])
            up = _dot(u2, wu_ref[:, sl])
            r1 = pltpu.roll(gp, 1, 0)
            r2 = pltpu.roll(gp, 2, 0)
            r1 = jnp.concatenate([jnp.where(row < 1, pltpu.roll(prev, 1, 0), r1[:SUBLANES]), r1[SUBLANES:]], axis=0)
            r2 = jnp.concatenate([jnp.where(row < 2, pltpu.roll(prev, 2, 0), r2[:SUBLANES]), r2[SUBLANES:]], axis=0)
            gate = cw_ref[0:1, sl] * r2 + cw_ref[1:2, sl] * r1 + cw_ref[2:3, sl] * gp + cb_ref[:, sl]
            act_ref[rows, sl] = (jax.nn.gelu(gate) * up).astype(BF16)
            prev = gp[SUB_TILE - SUBLANES:, :]
        carry_ref[:, sl] = prev
    for rows, h in zip(subs, hs):
        h = h + _dot(act_ref[rows, :], wd_ref[...])
        u3 = _rms(h, g3_ref[...]).astype(BF16)
        h = h + jax.nn.sigmoid(_dot(u3, wpg_ref[...])) * _dot(p_ref[rows, :].astype(BF16), wpp_ref[...])
        out_ref[rows, :] = _rms(h, gf_ref[...])


def _ffn(h1, seq, p2, norm_g, w_gate, w_up, conv_w, conv_b, w_down, ple_g, ple_w_gate, ple_w_proj, final_g):
    n = h1.shape[0]
    tm = TOKEN_TILE
    k = WEIGHT_CAST_STEPS
    row = lambda w: pl.BlockSpec((tm, w), lambda i: (jnp.maximum(i - k, 0), 0))
    wrows = lambda a: pl.BlockSpec((a.shape[0] // k, a.shape[1]), lambda i: (jnp.minimum(i, k - 1), 0))
    resident = lambda a: pltpu.VMEM(a.shape, BF16)
    vec = lambda a: a.reshape(1, -1)
    weights = (w_gate, w_up, w_down, ple_w_gate, ple_w_proj)
    return pl.pallas_call(
        functools.partial(_ffn_kernel, tiles_per_seq=seq // tm),
        grid=(k + n // tm,),
        in_specs=[row(D_MODEL), _const_spec((1, D_MODEL)), wrows(w_gate),
                  wrows(w_up), _const_spec((CONV_WIDTH, D_FF)), _const_spec((1, D_FF)),
                  wrows(w_down), _const_spec((1, D_MODEL)), wrows(ple_w_gate),
                  row(PLE_DIM), wrows(ple_w_proj), _const_spec((1, D_MODEL))],
        out_specs=row(D_MODEL),
        out_shape=jax.ShapeDtypeStruct((n, D_MODEL), F32),
        scratch_shapes=[resident(w) for w in weights]
                       + [pltpu.VMEM((tm, D_FF), BF16), pltpu.VMEM((SUBLANES, D_FF), F32)],
        compiler_params=_params("arbitrary"),
        name="ffn",
    )(h1, vec(norm_g), w_gate, w_up, conv_w, vec(conv_b),
      w_down, vec(ple_g), ple_w_gate, p2, ple_w_proj, vec(final_g))


def _layer(h2, bsz, seq, p2, mix_norm_g, w_in, gate_b, ssm_lam_re, ssm_lam_im, ssm_log_dt, ssm_b_re,
           ssm_b_im, ssm_c_re, ssm_c_im, ssm_d, ssm_glu_w, ssm_glu_b, w_branch_a, w_branch_b, w_out,
           ffn_norm_g, ffn_w_gate, ffn_w_up, ffn_conv_w, ffn_conv_b, ffn_w_down,
           ple_norm_g, ple_w_gate, ple_w_proj, out_norm_g):
    u_rows, qkv, ga, gb = _in_proj(h2, seq, mix_norm_g, w_in, gate_b)
    ops = _ssm_operators(ssm_lam_re, ssm_lam_im, ssm_log_dt, ssm_b_re, ssm_b_im, ssm_c_re, ssm_c_im)
    ys_rows = _ssm_scan(u_rows, bsz, ssm_d.reshape(-1), ops)
    attn_outs = [_attn_group(*group) for group in qkv]
    h1 = _merge(h2, seq, ys_rows, ssm_glu_w, ssm_glu_b, w_branch_a, attn_outs, w_branch_b, ga, gb, w_out)
    return _ffn(h1, seq, p2, ffn_norm_g, ffn_w_gate, ffn_w_up, ffn_conv_w, ffn_conv_b, ffn_w_down,
                ple_norm_g, ple_w_gate, ple_w_proj, out_norm_g)


def kernel(x, p, mix_norm_g, w_in, gate_b, ssm_lam_re, ssm_lam_im, ssm_log_dt, ssm_b_re, ssm_b_im, ssm_c_re, ssm_c_im, ssm_d, ssm_glu_w, ssm_glu_b, w_branch_a, w_branch_b, w_out, ffn_norm_g, ffn_w_gate, ffn_w_up, ffn_conv_w, ffn_conv_b, ffn_w_down, ple_norm_g, ple_w_gate, ple_w_proj, final_norm_g):
    bsz, seq, _ = x.shape
    depth = p.shape[0]
    assert depth == 1, "the final norm is fused into the layer's last kernel"
    h2 = x.reshape(bsz * seq, D_MODEL)
    out = _layer(h2, bsz, seq, p[0].reshape(bsz * seq, PLE_DIM), mix_norm_g[0], w_in[0], gate_b[0],
                 ssm_lam_re[0], ssm_lam_im[0], ssm_log_dt[0], ssm_b_re[0], ssm_b_im[0], ssm_c_re[0],
                 ssm_c_im[0], ssm_d[0], ssm_glu_w[0], ssm_glu_b[0], w_branch_a[0], w_branch_b[0],
                 w_out[0], ffn_norm_g[0], ffn_w_gate[0], ffn_w_up[0], ffn_conv_w[0], ffn_conv_b[0],
                 ffn_w_down[0], ple_norm_g[0], ple_w_gate[0], ple_w_proj[0], final_norm_g)
    return out.reshape(bsz, seq, D_MODEL)
```

```python
import functools

import jax
import jax.numpy as jnp
from jax import lax
from jax.experimental import pallas as pl
from jax.experimental.pallas import tpu as pltpu

F32 = jnp.float32
BF16 = jnp.bfloat16

D_MODEL = 1024
EPS = 1e-6
PLE_DIM = 256
SSM_GROUP = 16
SSM_STATE = 64
SSM_WIDTH = 512
SSM_GROUPS = SSM_WIDTH // SSM_GROUP
HEAD_DIM = 64
DILATIONS = (1, 4, 16)
WINDOW_KEYS = 128
HEADS_PER_GROUP = 4
GROUP_WIDTH = HEADS_PER_GROUP * HEAD_DIM
ATTN_WIDTH = len(DILATIONS) * GROUP_WIDTH
ROT_DIM = HEAD_DIM // 4
ROPE_THETA = 500000.0
NEG_BIG = -1e30
D_FF = 2816
CONV_WIDTH = 3
OFF_Q = SSM_WIDTH
OFF_K = OFF_Q + ATTN_WIDTH
OFF_V = OFF_K + ATTN_WIDTH
OFF_GA = OFF_V + ATTN_WIDTH
OFF_GB = OFF_GA + D_MODEL
IN_WIDTH = OFF_GB + D_MODEL

LANES = 128
SUBLANES = 8
VMEM_LIMIT_BYTES = 56 * 1024 * 1024

TOKEN_TILE = 1024
SUB_TILE = 256
COL_TILE = 256
SSM_CHUNK = 16
SSM_CHUNKS_PER_TILE = 256
SSM_TILE = SSM_CHUNK * SSM_CHUNKS_PER_TILE
SSM_CK = SSM_CHUNK * SSM_GROUP
SSM_ROW = SSM_CHUNK * SSM_WIDTH
SSM_PAIRS = SSM_GROUPS // 2
SSM_LOG_STEPS = 8
ATTN_QB = 128
ATTN_STEP_ROWS = 2048
FFN_CHUNK = 256
WEIGHT_CAST_STEPS = 8


def _dot(a, b):
    return jnp.dot(a, b, preferred_element_type=F32)


def _rms(x, g):
    var = jnp.mean(x * x, axis=-1, keepdims=True)
    return x * lax.rsqrt(var + EPS) * g


def _const_spec(shape):
    nd = len(shape)
    return pl.BlockSpec(shape, lambda *_: (0,) * nd, pipeline_mode=pl.Buffered(1))


def _params(*sem):
    return pltpu.CompilerParams(dimension_semantics=sem, vmem_limit_bytes=VMEM_LIMIT_BYTES)


def _rope(z, cos, sin_lo, sin_hi):
    up = pltpu.roll(z, LANES - ROT_DIM // 2, 1)
    dn = pltpu.roll(z, ROT_DIM // 2, 1)
    return z * cos + up * sin_lo + dn * sin_hi


def _store_residue_major(stage_ref, out_ref, z, dil, tok0):
    rows, first = z.shape[0] // dil, tok0 // dil
    for j in range(z.shape[1] // LANES):
        sl = slice(j * LANES, (j + 1) * LANES)
        stage_ref[...] = z[:, sl]
        for r in range(dil):
            out_ref[r, first:first + rows, sl] = stage_ref[pl.ds(r, rows, stride=dil), :].astype(out_ref.dtype)


def _store_chunk_rows(stage_ref, out_ref, z, c0):
    rows, width = z.shape[0] // SSM_CHUNK, z.shape[1]
    for j in range(width // LANES):
        stage_ref[...] = z[:, j * LANES:(j + 1) * LANES]
        for t in range(SSM_CHUNK):
            col = t * width + j * LANES
            out_ref[c0:c0 + rows, col:col + LANES] = (
                stage_ref[pl.ds(t, rows, stride=SSM_CHUNK), :].astype(out_ref.dtype))


def _load_chunk_rows(stage_ref, in_ref, width, c0, rows):
    slabs = []
    for j in range(width // LANES):
        for t in range(SSM_CHUNK):
            col = t * width + j * LANES
            stage_ref[pl.ds(t, rows, stride=SSM_CHUNK), :] = in_ref[c0:c0 + rows, col:col + LANES].astype(F32)
        slabs.append(stage_ref[...])
    return jnp.concatenate(slabs, axis=1)


def _in_proj_kernel(x_ref, g_ref, ca_ref, sa_ref, cb_ref, sb_ref, sign_ref, w32_ref, bg_ref,
                    s_ref, q0_ref, k0_ref, v0_ref, q1_ref, k1_ref, v1_ref, q2_ref, k2_ref, v2_ref,
                    ga_ref, gb_ref, w_ref, stage_ref):
    step = pl.program_id(0)

    @pl.when(step < WEIGHT_CAST_STEPS)
    def _():
        _cast_weight_rows(step, (w32_ref,), (w_ref,))

    @pl.when(step >= WEIGHT_CAST_STEPS)
    def _():
        _in_proj_tile(x_ref, g_ref, ca_ref, sa_ref, cb_ref, sb_ref, sign_ref, w_ref, bg_ref,
                      s_ref, q0_ref, k0_ref, v0_ref, q1_ref, k1_ref, v1_ref, q2_ref, k2_ref, v2_ref,
                      ga_ref, gb_ref, stage_ref)


def _in_proj_tile(x_ref, g_ref, ca_ref, sa_ref, cb_ref, sb_ref, sign_ref, w_ref, bg_ref,
                  s_ref, q0_ref, k0_ref, v0_ref, q1_ref, k1_ref, v1_ref, q2_ref, k2_ref, v2_ref,
                  ga_ref, gb_ref, stage_ref):
    u = _rms(x_ref[...], g_ref[...]).astype(BF16)
    _store_chunk_rows(stage_ref, s_ref, _dot(u, w_ref[:, :OFF_Q]), 0)
    ca, sa, cb, sb = ca_ref[...], sa_ref[...], cb_ref[...], sb_ref[...]
    cos = ca * cb - sa * sb
    sin = sa * cb + ca * sb
    slo, shi = sin * sign_ref[0:1, :], sin * sign_ref[1:2, :]
    scale = HEAD_DIM ** -0.5

    def rope(z):
        return jnp.concatenate([_rope(z[:, j * LANES:(j + 1) * LANES], cos, slo, shi)
                                for j in range(z.shape[1] // LANES)], axis=1)

    q = rope(_dot(u, w_ref[:, OFF_Q:OFF_K])) * scale
    k = rope(_dot(u, w_ref[:, OFF_K:OFF_V]))
    v = _dot(u, w_ref[:, OFF_V:OFF_GA])
    for z, refs in ((q, (q0_ref, q1_ref, q2_ref)), (k, (k0_ref, k1_ref, k2_ref)), (v, (v0_ref, v1_ref, v2_ref))):
        refs[0][...] = z[:, :GROUP_WIDTH].astype(BF16)
        for grp in (1, 2):
            _store_residue_major(stage_ref, refs[grp], z[:, grp * GROUP_WIDTH:(grp + 1) * GROUP_WIDTH],
                                 DILATIONS[grp], 0)
    ga_ref[...] = jax.nn.sigmoid(_dot(u, w_ref[:, OFF_GA:OFF_GB]) + bg_ref[:, :D_MODEL]).astype(BF16)
    gb_ref[...] = jax.nn.sigmoid(_dot(u, w_ref[:, OFF_GB:]) + bg_ref[:, D_MODEL:]).astype(BF16)


def _rope_tables(seq, tm):
    half = ROT_DIM // 2
    freqs = ROPE_THETA ** (-jnp.arange(half, dtype=F32) * (2.0 / ROT_DIM))
    head = jnp.concatenate([freqs, freqs, jnp.zeros((HEAD_DIM - ROT_DIM,), F32)])
    lane_freq = jnp.tile(head, LANES // HEAD_DIM)[None, :]
    base = jnp.arange(0, seq, tm, dtype=F32)[:, None] * lane_freq
    offs = jnp.arange(tm, dtype=F32)[:, None] * lane_freq
    in_head = jnp.arange(LANES) % HEAD_DIM
    sign = jnp.zeros((SUBLANES, LANES), F32)
    sign = sign.at[0].set(jnp.where(in_head < half, -1.0, 0.0))
    sign = sign.at[1].set(jnp.where((in_head >= half) & (in_head < ROT_DIM), 1.0, 0.0))
    n_tiles = seq // tm
    return (jnp.cos(base).reshape(n_tiles, 1, LANES), jnp.sin(base).reshape(n_tiles, 1, LANES),
            jnp.cos(offs), jnp.sin(offs), sign)


def _in_proj(x2, seq, norm_g, w_in, gate_b):
    n = x2.shape[0]
    tm = TOKEN_TILE
    tiles_per_seq = seq // tm
    bsz = n // seq
    cos_a, sin_a, cos_b, sin_b, sign = _rope_tables(seq, tm)

    k = WEIGHT_CAST_STEPS
    tile = lambda i: jnp.maximum(i - k, 0)
    row = lambda w: pl.BlockSpec((tm, w), lambda i: (tile(i), 0))
    tile_tab = pl.BlockSpec((None, 1, LANES), lambda i: (tile(i) % tiles_per_seq, 0, 0))
    res = lambda d: pl.BlockSpec((None, d, tm // d, GROUP_WIDTH),
                                 lambda i: (tile(i) // tiles_per_seq, 0, tile(i) % tiles_per_seq, 0))
    res_shape = lambda d: jax.ShapeDtypeStruct((bsz, d, seq // d, GROUP_WIDTH), BF16)
    nat_shape = jax.ShapeDtypeStruct((n, GROUP_WIDTH), BF16)
    d1, d2 = DILATIONS[1], DILATIONS[2]
    outs = pl.pallas_call(
        _in_proj_kernel,
        grid=(k + n // tm,),
        in_specs=[row(D_MODEL), _const_spec((1, D_MODEL)), tile_tab, tile_tab,
                  _const_spec((tm, LANES)), _const_spec((tm, LANES)), _const_spec((SUBLANES, LANES)),
                  pl.BlockSpec((D_MODEL // k, IN_WIDTH), lambda i: (jnp.minimum(i, k - 1), 0)),
                  _const_spec((1, 2 * D_MODEL))],
        out_specs=[pl.BlockSpec((tm // SSM_CHUNK, SSM_ROW), lambda i: (tile(i), 0))]
                  + [row(GROUP_WIDTH)] * 3 + [res(d1)] * 3 + [res(d2)] * 3 + [row(D_MODEL), row(D_MODEL)],
        out_shape=[jax.ShapeDtypeStruct((n // SSM_CHUNK, SSM_ROW), BF16)] + [nat_shape] * 3
                  + [res_shape(d1)] * 3 + [res_shape(d2)] * 3
                  + [jax.ShapeDtypeStruct((n, D_MODEL), BF16), jax.ShapeDtypeStruct((n, D_MODEL), BF16)],
        scratch_shapes=[pltpu.VMEM((D_MODEL, IN_WIDTH), BF16), pltpu.VMEM((tm, LANES), F32)],
        compiler_params=_params("arbitrary"),
        name="in_proj",
    )(x2, norm_g.reshape(1, D_MODEL), cos_a, sin_a, cos_b, sin_b, sign, w_in, gate_b.reshape(1, 2 * D_MODEL))
    u, q0, k0, v0, q1, k1, v1, q2, k2, v2, ga, gb = outs
    nat4 = lambda a: a.reshape(bsz, 1, seq, GROUP_WIDTH)
    qkv = ((nat4(q0), nat4(k0), nat4(v0)), (q1, k1, v1), (q2, k2, v2))
    return u, qkv, ga, gb


def _cmul(ar, ai, br, bi):
    return ar * br - ai * bi, ar * bi + ai * br


def _discretise(lr, li, dt):
    mag = jnp.exp(lr * dt)
    ar = mag * jnp.cos(li * dt)
    ai = mag * jnp.sin(li * dt)
    den = lr * lr + li * li
    cr = ((ar - 1.0) * lr + ai * li) / den
    ci = (ai * lr - (ar - 1.0) * li) / den
    return ar, ai, cr, ci


def _ssm_prep_kernel(lr_ref, li_ref, ldt_ref, brt_ref, bit_ref, cre_ref, cim_ref, lrf_ref, lif_ref, ldtf_ref,
                     w2_ref, e2_ref, apsr_ref, apsi_ref, aplr_ref, apli_ref, apbr_ref, apbi_ref, er_ref, ei_ref):
    g_n, t_n, h_n, sw = SSM_GROUPS, SSM_CHUNK, SSM_GROUP, 2 * SSM_STATE
    ar, ai, cr, ci = _discretise(lr_ref[...], li_ref[...], jnp.exp(ldt_ref[...]))
    brt, bit = brt_ref[...], bit_ref[...]
    bbr = cr * brt - ci * bit
    bbi = cr * bit + ci * brt
    cre, cim = cre_ref[...], cim_ref[...]
    by_group = lambda a: a.reshape(g_n, h_n, sw)
    pr, pi = jnp.ones_like(ar), jnp.zeros_like(ai)
    for j in range(t_n):
        rows = slice((t_n - 1 - j) * h_n, (t_n - j) * h_n)
        rr, ri = _cmul(pr, pi, bbr, bbi)
        er_ref[:, rows, :] = by_group(rr)
        ei_ref[:, rows, :] = by_group(ri)
        e2_ref[:, rows, :sw] = by_group(rr).astype(BF16)
        e2_ref[:, rows, sw:] = by_group(ri).astype(BF16)
        pr, pi = _cmul(pr, pi, ar, ai)
        rows = slice(j * h_n, (j + 1) * h_n)
        w2_ref[:, rows, SSM_CK:SSM_CK + sw] = by_group(cre * pr - cim * pi).astype(BF16)
        w2_ref[:, rows, SSM_CK + sw:] = by_group(-cre * pi - cim * pr).astype(BF16)

    nt = (((1,), (1,)), ((), ()))
    hi = lax.Precision.HIGHEST

    def toeplitz(g, _):
        rows = pl.ds(pl.multiple_of(g * h_n, h_n), h_n)
        krev = (lax.dot_general(cre_ref[rows, :], er_ref[g], nt, precision=hi, preferred_element_type=F32)
                - lax.dot_general(cim_ref[rows, :], ei_ref[g], nt, precision=hi, preferred_element_type=F32))
        kext = jnp.concatenate([krev, jnp.zeros_like(krev)], axis=1)
        for t in range(t_n):
            off = (t_n - 1 - t) * h_n
            win = kext if off == 0 else pltpu.roll(kext, 2 * SSM_CK - off, 1)
            w2_ref[g, t * h_n:(t + 1) * h_n, :SSM_CK] = win[:, :SSM_CK].astype(BF16)
        return 0
    lax.fori_loop(0, g_n, toeplitz, 0, unroll=8)

    acr, aci, _, _ = _discretise(lrf_ref[...], lif_ref[...], jnp.exp(ldtf_ref[...]))
    for _ in range(4):
        acr, aci = _cmul(acr, aci, acr, aci)
    shape = (SUBLANES, acr.shape[1])
    row = lax.broadcasted_iota(jnp.int32, shape, 0)
    qr, qi = jnp.ones(shape, F32), jnp.zeros(shape, F32)
    apsr_ref[...] = jnp.zeros_like(apsr_ref)
    apsi_ref[...] = jnp.zeros_like(apsi_ref)
    for k in range(SSM_LOG_STEPS + 1):
        for p2 in range(SSM_PAIRS):
            apsr_ref[p2, k:k + 1, :] = acr[:, p2 * sw:(p2 + 1) * sw]
            apsi_ref[p2, k:k + 1, :] = aci[:, p2 * sw:(p2 + 1) * sw]
        if (1 << k) < SUBLANES:
            nr, ni = _cmul(qr, qi, acr, aci)
            bit_set = (row & (1 << k)) != 0
            qr, qi = jnp.where(bit_set, nr, qr), jnp.where(bit_set, ni, qi)
        if (1 << k) == SUBLANES:
            a8r, a8i = acr, aci
        acr, aci = _cmul(acr, aci, acr, aci)
    br, bi = jnp.ones_like(a8r), jnp.zeros_like(a8i)
    blk_r, blk_i = [], []
    for _ in range(SSM_CHUNKS_PER_TILE // SUBLANES):
        blk_r.append(br)
        blk_i.append(bi)
        br, bi = _cmul(br, bi, a8r, a8i)
    blk_r, blk_i = jnp.concatenate(blk_r, axis=0), jnp.concatenate(blk_i, axis=0)
    for p2 in range(SSM_PAIRS):
        lanes = slice(p2 * sw, (p2 + 1) * sw)
        aplr_ref[p2], apli_ref[p2] = qr[:, lanes], qi[:, lanes]
        apbr_ref[p2], apbi_ref[p2] = blk_r[:, lanes], blk_i[:, lanes]


def _pad_pair_lanes(a):
    z = jnp.zeros_like(a)
    even = (jnp.arange(a.shape[0]) % 2 == 0)[:, None, None]
    padded = jnp.where(even, jnp.concatenate([a, z], -1), jnp.concatenate([z, a], -1))
    return padded.reshape(a.shape[0] * a.shape[1], 2 * a.shape[2])


def _ssm_operators(lam_re, lam_im, log_dt, b_re, b_im, c_re, c_im):
    g, p, h = SSM_GROUPS, SSM_STATE, SSM_GROUP
    gp, sw = g * p, 2 * p
    rep = lambda a: jnp.repeat(jnp.tile(a, (1, 2)), h, axis=0)
    ldt2 = jnp.broadcast_to(log_dt[:, None], (g, p))
    full = lambda shape: pl.BlockSpec(shape, lambda: (0,) * len(shape))
    in_arrays = (rep(lam_re), rep(lam_im), rep(ldt2),
                 _pad_pair_lanes(b_re.transpose(0, 2, 1)), _pad_pair_lanes(b_im.transpose(0, 2, 1)),
                 _pad_pair_lanes(c_re), _pad_pair_lanes(c_im),
                 lam_re.reshape(1, gp), lam_im.reshape(1, gp), ldt2.reshape(1, gp))
    out_shapes = ([((g, SSM_CK, SSM_CK + 2 * sw), BF16), ((g, SSM_CK, 2 * sw), BF16)]
                  + [((SSM_PAIRS, 2 * SUBLANES, sw), F32)] * 2
                  + [((SSM_PAIRS, SUBLANES, sw), F32)] * 2
                  + [((SSM_PAIRS, SSM_CHUNKS_PER_TILE // SUBLANES, sw), F32)] * 2)
    w2, e2, *powers = pl.pallas_call(
        _ssm_prep_kernel,
        in_specs=[full(a.shape) for a in in_arrays],
        out_specs=[full(s) for s, _ in out_shapes],
        out_shape=[jax.ShapeDtypeStruct(s, dt) for s, dt in out_shapes],
        scratch_shapes=[pltpu.VMEM((g, SSM_CK, sw), F32), pltpu.VMEM((g, SSM_CK, sw), F32)],
        compiler_params=pltpu.CompilerParams(vmem_limit_bytes=VMEM_LIMIT_BYTES),
        name="ssm_prep",
    )(*in_arrays)
    return (w2, e2.reshape(SSM_PAIRS, 2 * SSM_CK, 2 * sw), *powers)


def _shift_rows(z, s, row):
    if s % SUBLANES == 0:
        return jnp.concatenate([jnp.zeros((s, z.shape[1]), z.dtype), z[:-s]], axis=0)
    return jnp.where(row >= s, pltpu.roll(z, s, 0), 0.0)


def _ssm_scan_kernel(u_ref, d_ref, w2_ref, e2_ref, apsr_ref, apsi_ref, aplr_ref, apli_ref, apbr_ref, apbi_ref,
                     y_ref, xs_ref, sc_ref, yt_ref, carry_ref, loc_ref, sin_ref):
    t_n, h_n, c_n = SSM_CHUNK, SSM_GROUP, SSM_CHUNKS_PER_TILE
    n_slab = SSM_WIDTH // LANES
    pairs_per_slab = LANES // (2 * h_n)
    sw = 2 * SSM_STATE
    blk_n = SUBLANES
    n_blk = c_n // blk_n
    log_blk = blk_n.bit_length() - 1

    @pl.when(pl.program_id(1) == 0)
    def _():
        carry_ref[...] = jnp.zeros_like(carry_ref)

    for t in range(t_n):
        for j in range(n_slab):
            col = t * SSM_WIDTH + j * LANES
            blk = u_ref[:, col:col + LANES].T
            xs_ref[j * pairs_per_slab:(j + 1) * pairs_per_slab, :, t * h_n:(t + 1) * h_n, :] = (
                blk.reshape(pairs_per_slab, 2, h_n, c_n))

    row = lax.broadcasted_iota(jnp.int32, (n_blk, sw), 0)
    nt = (((1,), (1,)), ((), ()))
    tn = (((0,), (0,)), ((), ()))

    def local_states(pr, _):
        xp = xs_ref[pr].reshape(2 * SSM_CK, c_n)
        loc = lax.dot_general(xp, e2_ref[pr], tn, preferred_element_type=F32)
        loc_ref[pr, 0] = loc[:, :sw]
        loc_ref[pr, 1] = loc[:, sw:]
        return 0
    lax.fori_loop(0, SSM_PAIRS, local_states, 0, unroll=8)

    def chunk_scan(pr, _):
        slot = pr % sin_ref.shape[0]
        power = lambda k: (apsr_ref[pr, k:k + 1, :], apsi_ref[pr, k:k + 1, :])
        zr, zi = [], []
        for lo in range(blk_n):
            rows = pl.ds(lo, n_blk, stride=blk_n)
            xr, xi = loc_ref[pr, 0, rows, :], loc_ref[pr, 1, rows, :]
            if lo:
                dr, di = _cmul(zr[-1], zi[-1], *power(0))
                xr, xi = xr + dr, xi + di
            zr.append(xr)
            zi.append(xi)
        er, ei = zr[-1], zi[-1]
        s = 1
        while s < n_blk:
            dr, di = _cmul(_shift_rows(er, s, row), _shift_rows(ei, s, row),
                           *power(log_blk + s.bit_length() - 1))
            er, ei = er + dr, ei + di
            s *= 2
        cr, ci = carry_ref[pr, 0:1, :], carry_ref[pr, 1:2, :]
        hr, hi = _cmul(apbr_ref[pr], apbi_ref[pr], cr, ci)
        br, bi = _shift_rows(er, 1, row) + hr, _shift_rows(ei, 1, row) + hi
        for lo in range(blk_n):
            sr, si = _cmul(br, bi, aplr_ref[pr, lo:lo + 1, :], apli_ref[pr, lo:lo + 1, :])
            if lo:
                sr, si = sr + zr[lo - 1], si + zi[lo - 1]
            rows = pl.ds(lo, n_blk, stride=blk_n)
            sin_ref[slot, 0, rows, :] = sr
            sin_ref[slot, 1, rows, :] = si
        sc_ref[pr, :, :sw] = sin_ref[slot, 0].astype(BF16)
        sc_ref[pr, :, sw:] = sin_ref[slot, 1].astype(BF16)
        nr, ni = _cmul(cr, ci, *power(SSM_LOG_STEPS))
        carry_ref[pr, 0:1, :] = er[n_blk - 1:n_blk, :] + nr
        carry_ref[pr, 1:2, :] = ei[n_blk - 1:n_blk, :] + ni
        return 0
    lax.fori_loop(0, SSM_PAIRS, chunk_scan, 0, unroll=4)

    def outputs(g, _):
        pr = g // 2
        yg = (_dot(w2_ref[g, :, :SSM_CK], xs_ref[pr, g % 2])
              + lax.dot_general(w2_ref[g, :, SSM_CK:], sc_ref[pr], nt, preferred_element_type=F32))
        yt_ref[:, pl.ds(pl.multiple_of(g * h_n, h_n), h_n), :] = yg.reshape(t_n, h_n, c_n)
        return 0
    lax.fori_loop(0, SSM_GROUPS, outputs, 0, unroll=8)

    for t in range(t_n):
        for j in range(n_slab):
            sl = slice(j * LANES, (j + 1) * LANES)
            col = t * SSM_WIDTH + j * LANES
            y_ref[:, col:col + LANES] = (
                yt_ref[t, sl, :].T + d_ref[:, sl] * u_ref[:, col:col + LANES].astype(F32)).astype(BF16)


def _ssm_scan(u_rows, bsz, d_skip, ops):
    g, p, c_n = SSM_GROUPS, SSM_STATE, SSM_CHUNKS_PER_TILE
    tiles = u_rows.shape[0] // (bsz * c_n)
    tile = pl.BlockSpec((c_n, SSM_ROW), lambda b, i: (b * tiles + i, 0))
    return pl.pallas_call(
        _ssm_scan_kernel,
        grid=(bsz, tiles),
        in_specs=[tile, _const_spec((1, SSM_WIDTH))] + [_const_spec(op.shape) for op in ops],
        out_specs=tile,
        out_shape=jax.ShapeDtypeStruct(u_rows.shape, BF16),
        scratch_shapes=[pltpu.VMEM((SSM_PAIRS, 2, SSM_CK, c_n), BF16),
                        pltpu.VMEM((SSM_PAIRS, c_n, 4 * p), BF16),
                        pltpu.VMEM((SSM_CHUNK, SSM_WIDTH, c_n), F32),
                        pltpu.VMEM((SSM_PAIRS, SUBLANES, 2 * p), F32),
                        pltpu.VMEM((SSM_PAIRS, 2, c_n, 2 * p), F32),
                        pltpu.VMEM((4, 2, c_n, 2 * p), F32)],
        compiler_params=_params("parallel", "arbitrary"),
        name="ssm_scan",
    )(u_rows, d_skip.reshape(1, SSM_WIDTH), *ops)


def _attn_kernel(q_ref, kc_ref, kp_ref, vc_ref, vp_ref, o_ref, lse_ref):
    qb, nk = ATTN_QB, ATTN_QB + WINDOW_KEYS
    row = lax.broadcasted_iota(jnp.int32, (qb, nk), 0)
    col = lax.broadcasted_iota(jnp.int32, (qb, nk), 1)
    dist = row + WINDOW_KEYS - col
    in_band = (dist >= 0) & (dist <= WINDOW_KEYS)
    in_band_first = in_band & ((col >= WINDOW_KEYS) | (pl.program_id(2) > 0))
    lane = lax.broadcasted_iota(jnp.int32, (qb, LANES), 1)
    first_head = lane < HEAD_DIM
    nt = (((1,), (1,)), ((), ()))
    n_res, n_rows = o_ref.shape[0], o_ref.shape[1]
    for res in range(n_res):
        for sb in range(n_rows // qb):
            rows = slice(sb * qb, (sb + 1) * qb)
            valid = in_band if sb else in_band_first
            for pair in range(GROUP_WIDTH // LANES):
                cols = slice(pair * LANES, (pair + 1) * LANES)
                qp = q_ref[res, rows, cols]
                if sb:
                    window = slice(sb * qb - WINDOW_KEYS, (sb + 1) * qb)
                    kp, vp = kc_ref[res, window, cols], vc_ref[res, window, cols]
                else:
                    kp = jnp.concatenate([kp_ref[res, :, cols], kc_ref[res, :qb, cols]], axis=0)
                    vp = jnp.concatenate([vp_ref[res, :, cols], vc_ref[res, :qb, cols]], axis=0)
                outs, lses = [], []
                for sel in (first_head, ~first_head):
                    qm = jnp.where(sel, qp, jnp.zeros_like(qp))
                    s = lax.dot_general(qm, kp, nt, preferred_element_type=F32)
                    s = jnp.where(valid, s, NEG_BIG)
                    m = jnp.max(s, axis=-1, keepdims=True)
                    e = jnp.exp(s - m)
                    den = jnp.sum(e, axis=-1, keepdims=True)
                    outs.append(_dot(e.astype(BF16), vp) / den)
                    lses.append(m + jnp.log(den))
                o_ref[res, rows, cols] = jnp.where(first_head, outs[0], outs[1]).astype(BF16)
                lse_ref[res, rows, cols] = jnp.where(first_head, lses[0], lses[1])


def _attn_group(q4, k4, v4):
    bsz, dil, lr, _ = q4.shape
    rows = min(ATTN_STEP_ROWS, lr)
    n_res = ATTN_STEP_ROWS // rows
    back = rows // WINDOW_KEYS
    q_spec = pl.BlockSpec((None, n_res, rows, GROUP_WIDTH), lambda b, r, i: (b, r, i, 0))
    prev_spec = pl.BlockSpec((None, n_res, WINDOW_KEYS, GROUP_WIDTH),
                             lambda b, r, i: (b, r, jnp.maximum(i * back - 1, 0), 0))
    return pl.pallas_call(
        _attn_kernel,
        grid=(bsz, dil // n_res, lr // rows),
        in_specs=[q_spec, q_spec, prev_spec, q_spec, prev_spec],
        out_specs=[q_spec, q_spec],
        out_shape=[jax.ShapeDtypeStruct(q4.shape, BF16), jax.ShapeDtypeStruct(q4.shape, F32)],
        compiler_params=_params("parallel", "parallel", "arbitrary"),
        name=f"attn_d{dil}",
    )(q4, k4, k4, v4, v4)


def _load_token_major(stage_ref, in_ref, tok0, ntok, lanes):
    dil = in_ref.shape[0]
    first, rows = tok0 // dil, ntok // dil
    if dil == 1:
        return in_ref[0, first:first + rows, lanes].astype(F32)
    for r in range(dil):
        stage_ref[pl.ds(r, rows, stride=dil), :] = in_ref[r, first:first + rows, lanes].astype(F32)
    return stage_ref[...]


def _merge_kernel(x_ref, ys_ref, gluw_ref, glub_ref, wa_ref,
                  o0_ref, o1_ref, o2_ref, l0_ref, l1_ref, l2_ref, wb_ref,
                  ga_ref, gb_ref, wout_ref, h_ref, stage_ref, y_ref, ya_ref, attn_ref, mix_ref):
    pieces = lambda width: [slice(c * COL_TILE, (c + 1) * COL_TILE) for c in range(width // COL_TILE)]
    for s in range(x_ref.shape[0] // SUB_TILE):
        tok0 = s * SUB_TILE
        rows = slice(tok0, tok0 + SUB_TILE)
        stage = stage_ref.at[s]
        y_ref[s] = jax.nn.gelu(_load_chunk_rows(stage, ys_ref, SSM_WIDTH, tok0 // SSM_CHUNK, SUB_TILE // SSM_CHUNK))
        y_bf = y_ref[s].astype(BF16)
        for cols in pieces(SSM_WIDTH):
            gate = jax.nn.sigmoid(_dot(y_bf, gluw_ref[:, cols]) + glub_ref[:, cols])
            ya_ref[s, :, cols] = (y_ref[s, :, cols] * gate).astype(BF16)

        for j in range(GROUP_WIDTH // LANES):
            lanes = slice(j * LANES, (j + 1) * LANES)
            o, l = ([_load_token_major(stage, ref, tok0, SUB_TILE, lanes) for ref in refs]
                    for refs in ((o0_ref, o1_ref, o2_ref), (l0_ref, l1_ref, l2_ref)))
            top = jnp.maximum(jnp.maximum(l[0], l[1]), l[2])
            w = [jnp.exp(lg - top) for lg in l]
            attn_ref[s, :, lanes] = ((w[0] * o[0] + w[1] * o[1] + w[2] * o[2])
                                     / (w[0] + w[1] + w[2])).astype(BF16)

        for cols in pieces(D_MODEL):
            mix = (ga_ref[rows, cols].astype(F32) * _dot(ya_ref[s], wa_ref[:, cols])
                   + gb_ref[rows, cols].astype(F32) * _dot(attn_ref[s], wb_ref[:, cols]))
            mix_ref[s, :, cols] = mix.astype(BF16)
        for cols in pieces(D_MODEL):
            h_ref[rows, cols] = x_ref[rows, cols] + _dot(mix_ref[s], wout_ref[:, cols])


def _merge(x2, seq, ys_rows, glu_w, glu_b, w_a, attn_outs, w_b, ga, gb, w_out):
    n = x2.shape[0]
    tm = TOKEN_TILE
    tiles_per_seq = seq // tm
    row = lambda w: pl.BlockSpec((tm, w), lambda i: (i, 0))
    res = lambda d: pl.BlockSpec((None, d, tm // d, GROUP_WIDTH),
                                 lambda i: (i // tiles_per_seq, 0, i % tiles_per_seq, 0))
    (o0, l0), (o1, l1), (o2, l2) = attn_outs
    d0, d1, d2 = DILATIONS
    return pl.pallas_call(
        _merge_kernel,
        grid=(n // tm,),
        in_specs=[row(D_MODEL), pl.BlockSpec((tm // SSM_CHUNK, SSM_ROW), lambda i: (i, 0)),
                  _const_spec((SSM_WIDTH, SSM_WIDTH)), _const_spec((1, SSM_WIDTH)),
                  _const_spec((SSM_WIDTH, D_MODEL)),
                  res(d0), res(d1), res(d2), res(d0), res(d1), res(d2),
                  _const_spec((GROUP_WIDTH, D_MODEL)), row(D_MODEL), row(D_MODEL),
                  _const_spec((D_MODEL, D_MODEL))],
        out_specs=row(D_MODEL),
        out_shape=jax.ShapeDtypeStruct((n, D_MODEL), F32),
        scratch_shapes=[pltpu.VMEM((tm // SUB_TILE, SUB_TILE, LANES), F32),
                        pltpu.VMEM((tm // SUB_TILE, SUB_TILE, SSM_WIDTH), F32),
                        pltpu.VMEM((tm // SUB_TILE, SUB_TILE, SSM_WIDTH), BF16),
                        pltpu.VMEM((tm // SUB_TILE, SUB_TILE, GROUP_WIDTH), BF16),
                        pltpu.VMEM((tm // SUB_TILE, SUB_TILE, D_MODEL), BF16)],
        compiler_params=_params("parallel"),
        name="merge",
    )(x2, ys_rows, glu_w.astype(BF16), glu_b.reshape(1, SSM_WIDTH),
      w_a.astype(BF16), o0, o1, o2, l0, l1, l2, w_b.astype(BF16), ga, gb, w_out.astype(BF16))


def _cast_weight_rows(step, src_refs, dst_refs):
    for src, dst in zip(src_refs, dst_refs):
        rb = src.shape[0]
        dst[pl.ds(pl.multiple_of(step * rb, rb), rb), :] = src[...].astype(BF16)


def _ffn_kernel(h_ref, g2_ref, wg32_ref, wu32_ref, cw_ref, cb_ref, wd32_ref, g3_ref, wpg32_ref,
                p_ref, wpp32_ref, gf_ref, out_ref, wg_ref, wu_ref, wd_ref, wpg_ref, wpp_ref,
                act_ref, carry_ref, *, tiles_per_seq):
    step = pl.program_id(0)

    @pl.when(step < WEIGHT_CAST_STEPS)
    def _():
        _cast_weight_rows(step, (wg32_ref, wu32_ref, wd32_ref, wpg32_ref, wpp32_ref),
                          (wg_ref, wu_ref, wd_ref, wpg_ref, wpp_ref))

    @pl.when(step >= WEIGHT_CAST_STEPS)
    def _():
        _ffn_tile(step - WEIGHT_CAST_STEPS, h_ref, g2_ref, wg_ref, wu_ref, cw_ref, cb_ref, wd_ref, g3_ref, wpg_ref,
                  p_ref, wpp_ref, gf_ref, out_ref, act_ref, carry_ref, tiles_per_seq)


def _ffn_tile(tile, h_ref, g2_ref, wg_ref, wu_ref, cw_ref, cb_ref, wd_ref, g3_ref, wpg_ref,
              p_ref, wpp_ref, gf_ref, out_ref, act_ref, carry_ref, tiles_per_seq):
    tm = h_ref.shape[0]

    @pl.when(tile % tiles_per_seq == 0)
    def _():
        carry_ref[...] = jnp.zeros_like(carry_ref)

    subs = [slice(s * SUB_TILE, (s + 1) * SUB_TILE) for s in range(tm // SUB_TILE)]
    hs = [h_ref[rows, :] for rows in subs]
    u2s = [_rms(h, g2_ref[...]).astype(BF16) for h in hs]
    row = lax.broadcasted_iota(jnp.int32, (SUBLANES, FFN_CHUNK), 0)
    for c in range(D_FF // FFN_CHUNK):
        sl = slice(c * FFN_CHUNK, (c + 1) * FFN_CHUNK)
        prev = carry_ref[:, sl]
        for rows, u2 in zip(subs, u2s):
            gp = _dot(u2, wg_ref[:, sl])
            up = _dot(u2, wu_ref[:, sl])
            r1 = pltpu.roll(gp, 1, 0)
            r2 = pltpu.roll(gp, 2, 0)
            r1 = jnp.concatenate([jnp.where(row < 1, pltpu.roll(prev, 1, 0), r1[:SUBLANES]), r1[SUBLANES:]], axis=0)
            r2 = jnp.concatenate([jnp.where(row < 2, pltpu.roll(prev, 2, 0), r2[:SUBLANES]), r2[SUBLANES:]], axis=0)
            gate = cw_ref[0:1, sl] * r2 + cw_ref[1:2, sl] * r1 + cw_ref[2:3, sl] * gp + cb_ref[:, sl]
            act_ref[rows, sl] = (jax.nn.gelu(gate) * up).astype(BF16)
            prev = gp[SUB_TILE - SUBLANES:, :]
        carry_ref[:, sl] = prev
    for rows, h in zip(subs, hs):
        h = h + _dot(act_ref[rows, :], wd_ref[...])
        u3 = _rms(h, g3_ref[...]).astype(BF16)
        h = h + jax.nn.sigmoid(_dot(u3, wpg_ref[...])) * _dot(p_ref[rows, :].astype(BF16), wpp_ref[...])
        out_ref[rows, :] = _rms(h, gf_ref[...])


def _ffn(h1, seq, p2, norm_g, w_gate, w_up, conv_w, conv_b, w_down, ple_g, ple_w_gate, ple_w_proj, final_g):
    n = h1.shape[0]
    tm = TOKEN_TILE
    k = WEIGHT_CAST_STEPS
    row = lambda w: pl.BlockSpec((tm, w), lambda i: (jnp.maximum(i - k, 0), 0))
    wrows = lambda a: pl.BlockSpec((a.shape[0] // k, a.shape[1]), lambda i: (jnp.minimum(i, k - 1), 0))
    resident = lambda a: pltpu.VMEM(a.shape, BF16)
    vec = lambda a: a.reshape(1, -1)
    weights = (w_gate, w_up, w_down, ple_w_gate, ple_w_proj)
    return pl.pallas_call(
        functools.partial(_ffn_kernel, tiles_per_seq=seq // tm),
        grid=(k + n // tm,),
        in_specs=[row(D_MODEL), _const_spec((1, D_MODEL)), wrows(w_gate),
                  wrows(w_up), _const_spec((CONV_WIDTH, D_FF)), _const_spec((1, D_FF)),
                  wrows(w_down), _const_spec((1, D_MODEL)), wrows(ple_w_gate),
                  row(PLE_DIM), wrows(ple_w_proj), _const_spec((1, D_MODEL))],
        out_specs=row(D_MODEL),
        out_shape=jax.ShapeDtypeStruct((n, D_MODEL), F32),
        scratch_shapes=[resident(w) for w in weights]
                       + [pltpu.VMEM((tm, D_FF), BF16), pltpu.VMEM((SUBLANES, D_FF), F32)],
        compiler_params=_params("arbitrary"),
        name="ffn",
    )(h1, vec(norm_g), w_gate, w_up, conv_w, vec(conv_b),
      w_down, vec(ple_g), ple_w_gate, p2, ple_w_proj, vec(final_g))


def _layer(h2, bsz, seq, p2, mix_norm_g, w_in, gate_b, ssm_lam_re, ssm_lam_im, ssm_log_dt, ssm_b_re,
           ssm_b_im, ssm_c_re, ssm_c_im, ssm_d, ssm_glu_w, ssm_glu_b, w_branch_a, w_branch_b, w_out,
           ffn_norm_g, ffn_w_gate, ffn_w_up, ffn_conv_w, ffn_conv_b, ffn_w_down,
           ple_norm_g, ple_w_gate, ple_w_proj, out_norm_g):
    u_rows, qkv, ga, gb = _in_proj(h2, seq, mix_norm_g, w_in, gate_b)
    ops = _ssm_operators(ssm_lam_re, ssm_lam_im, ssm_log_dt, ssm_b_re, ssm_b_im, ssm_c_re, ssm_c_im)
    ys_rows = _ssm_scan(u_rows, bsz, ssm_d.reshape(-1), ops)
    attn_outs = [_attn_group(*group) for group in qkv]
    h1 = _merge(h2, seq, ys_rows, ssm_glu_w, ssm_glu_b, w_branch_a, attn_outs, w_branch_b, ga, gb, w_out)
    return _ffn(h1, seq, p2, ffn_norm_g, ffn_w_gate, ffn_w_up, ffn_conv_w, ffn_conv_b, ffn_w_down,
                ple_norm_g, ple_w_gate, ple_w_proj, out_norm_g)


def kernel(x, p, mix_norm_g, w_in, gate_b, ssm_lam_re, ssm_lam_im, ssm_log_dt, ssm_b_re, ssm_b_im, ssm_c_re, ssm_c_im, ssm_d, ssm_glu_w, ssm_glu_b, w_branch_a, w_branch_b, w_out, ffn_norm_g, ffn_w_gate, ffn_w_up, ffn_conv_w, ffn_conv_b, ffn_w_down, ple_norm_g, ple_w_gate, ple_w_proj, final_norm_g):
    bsz, seq, _ = x.shape
    depth = p.shape[0]
    assert depth == 1, "the final norm is fused into the layer's last kernel"
    h2 = x.reshape(bsz * seq, D_MODEL)
    out = _layer(h2, bsz, seq, p[0].reshape(bsz * seq, PLE_DIM), mix_norm_g[0], w_in[0], gate_b[0],
                 ssm_lam_re[0], ssm_lam_im[0], ssm_log_dt[0], ssm_b_re[0], ssm_b_im[0], ssm_c_re[0],
                 ssm_c_im[0], ssm_d[0], ssm_glu_w[0], ssm_glu_b[0], w_branch_a[0], w_branch_b[0],
                 w_out[0], ffn_norm_g[0], ffn_w_gate[0], ffn_w_up[0], ffn_conv_w[0], ffn_conv_b[0],
                 ffn_w_down[0], ple_norm_g[0], ple_w_gate[0], ple_w_proj[0], final_norm_g)
    return out.reshape(bsz, seq, D_MODEL)
```

```python
import functools

import jax
import jax.numpy as jnp
from jax import lax
from jax.experimental import pallas as pl
from jax.experimental.pallas import tpu as pltpu

F32 = jnp.float32
BF16 = jnp.bfloat16

D_MODEL = 1024
EPS = 1e-6
PLE_DIM = 256
SSM_GROUP = 16
SSM_STATE = 64
SSM_WIDTH = 512
SSM_GROUPS = SSM_WIDTH // SSM_GROUP
HEAD_DIM = 64
DILATIONS = (1, 4, 16)
WINDOW_KEYS = 128
HEADS_PER_GROUP = 4
GROUP_WIDTH = HEADS_PER_GROUP * HEAD_DIM
ATTN_WIDTH = len(DILATIONS) * GROUP_WIDTH
ROT_DIM = HEAD_DIM // 4
ROPE_THETA = 500000.0
NEG_BIG = -1e30
D_FF = 2816
CONV_WIDTH = 3
OFF_Q = SSM_WIDTH
OFF_K = OFF_Q + ATTN_WIDTH
OFF_V = OFF_K + ATTN_WIDTH
OFF_GA = OFF_V + ATTN_WIDTH
OFF_GB = OFF_GA + D_MODEL
IN_WIDTH = OFF_GB + D_MODEL

LANES = 128
SUBLANES = 8
VMEM_LIMIT_BYTES = 56 * 1024 * 1024

TOKEN_TILE = 1024
SUB_TILE = 256
COL_TILE = 256
SSM_CHUNK = 16
SSM_CHUNKS_PER_TILE = 256
SSM_TILE = SSM_CHUNK * SSM_CHUNKS_PER_TILE
SSM_CK = SSM_CHUNK * SSM_GROUP
SSM_ROW = SSM_CHUNK * SSM_WIDTH
SSM_PAIRS = SSM_GROUPS // 2
SSM_LOG_STEPS = 8
ATTN_QB = 128
ATTN_STEP_ROWS = 2048
FFN_CHUNK = 256
WEIGHT_CAST_STEPS = 8


def _dot(a, b):
    return jnp.dot(a, b, preferred_element_type=F32)


def _rms(x, g):
    var = jnp.mean(x * x, axis=-1, keepdims=True)
    return x * lax.rsqrt(var + EPS) * g


def _const_spec(shape):
    nd = len(shape)
    return pl.BlockSpec(shape, lambda *_: (0,) * nd, pipeline_mode=pl.Buffered(1))


def _params(*sem):
    return pltpu.CompilerParams(dimension_semantics=sem, vmem_limit_bytes=VMEM_LIMIT_BYTES)


def _rope(z, cos, sin_lo, sin_hi):
    up = pltpu.roll(z, LANES - ROT_DIM // 2, 1)
    dn = pltpu.roll(z, ROT_DIM // 2, 1)
    return z * cos + up * sin_lo + dn * sin_hi


def _store_residue_major(stage_ref, out_ref, z, dil, tok0, col0=0):
    rows, first = z.shape[0] // dil, tok0 // dil
    for j in range(z.shape[1] // LANES):
        stage_ref[...] = z[:, j * LANES:(j + 1) * LANES]
        sl = slice(col0 + j * LANES, col0 + (j + 1) * LANES)
        for r in range(dil):
            out_ref[r, first:first + rows, sl] = stage_ref[pl.ds(r, rows, stride=dil), :].astype(out_ref.dtype)


def _store_chunk_rows(stage_ref, out_ref, z, c0):
    rows, width = z.shape[0] // SSM_CHUNK, z.shape[1]
    for j in range(width // LANES):
        stage_ref[...] = z[:, j * LANES:(j + 1) * LANES]
        for t in range(SSM_CHUNK):
            col = t * width + j * LANES
            out_ref[c0:c0 + rows, col:col + LANES] = (
                stage_ref[pl.ds(t, rows, stride=SSM_CHUNK), :].astype(out_ref.dtype))


def _load_chunk_rows(stage_ref, in_ref, width, c0, rows):
    slabs = []
    for j in range(width // LANES):
        for t in range(SSM_CHUNK):
            col = t * width + j * LANES
            stage_ref[pl.ds(t, rows, stride=SSM_CHUNK), :] = in_ref[c0:c0 + rows, col:col + LANES].astype(F32)
        slabs.append(stage_ref[...])
    return jnp.concatenate(slabs, axis=1)


def _in_proj_kernel(x_ref, g_ref, ca_ref, sa_ref, cb_ref, sb_ref, sign_ref, w32_ref, bg_ref,
                    s_ref, qkv0_ref, qkv1_ref, qkv2_ref, ga_ref, gb_ref, w_ref, stage_ref):
    step = pl.program_id(0)

    @pl.when(step < WEIGHT_CAST_STEPS)
    def _():
        _cast_weight_rows(step, (w32_ref,), (w_ref,))

    @pl.when(step >= WEIGHT_CAST_STEPS)
    def _():
        _in_proj_tile(x_ref, g_ref, ca_ref, sa_ref, cb_ref, sb_ref, sign_ref, w_ref, bg_ref,
                      s_ref, qkv0_ref, qkv1_ref, qkv2_ref, ga_ref, gb_ref, stage_ref)


def _in_proj_tile(x_ref, g_ref, ca_ref, sa_ref, cb_ref, sb_ref, sign_ref, w_ref, bg_ref,
                  s_ref, qkv0_ref, qkv1_ref, qkv2_ref, ga_ref, gb_ref, stage_ref):
    u = _rms(x_ref[...], g_ref[...]).astype(BF16)
    _store_chunk_rows(stage_ref, s_ref, _dot(u, w_ref[:, :OFF_Q]), 0)
    ca, sa, cb, sb = ca_ref[...], sa_ref[...], cb_ref[...], sb_ref[...]
    cos = ca * cb - sa * sb
    sin = sa * cb + ca * sb
    slo, shi = sin * sign_ref[0:1, :], sin * sign_ref[1:2, :]
    scale = HEAD_DIM ** -0.5

    def rope(z):
        return jnp.concatenate([_rope(z[:, j * LANES:(j + 1) * LANES], cos, slo, shi)
                                for j in range(z.shape[1] // LANES)], axis=1)

    q = rope(_dot(u, w_ref[:, OFF_Q:OFF_K])) * scale
    k = rope(_dot(u, w_ref[:, OFF_K:OFF_V]))
    v = _dot(u, w_ref[:, OFF_V:OFF_GA])
    for part, z in enumerate((q, k, v)):
        col0 = part * GROUP_WIDTH
        qkv0_ref[:, col0:col0 + GROUP_WIDTH] = z[:, :GROUP_WIDTH].astype(BF16)
        for grp, out_ref in ((1, qkv1_ref), (2, qkv2_ref)):
            _store_residue_major(stage_ref, out_ref, z[:, grp * GROUP_WIDTH:(grp + 1) * GROUP_WIDTH],
                                 DILATIONS[grp], 0, col0)
    ga_ref[...] = jax.nn.sigmoid(_dot(u, w_ref[:, OFF_GA:OFF_GB]) + bg_ref[:, :D_MODEL]).astype(BF16)
    gb_ref[...] = jax.nn.sigmoid(_dot(u, w_ref[:, OFF_GB:]) + bg_ref[:, D_MODEL:]).astype(BF16)


def _rope_tables(seq, tm):
    half = ROT_DIM // 2
    freqs = ROPE_THETA ** (-jnp.arange(half, dtype=F32) * (2.0 / ROT_DIM))
    head = jnp.concatenate([freqs, freqs, jnp.zeros((HEAD_DIM - ROT_DIM,), F32)])
    lane_freq = jnp.tile(head, LANES // HEAD_DIM)[None, :]
    base = jnp.arange(0, seq, tm, dtype=F32)[:, None] * lane_freq
    offs = jnp.arange(tm, dtype=F32)[:, None] * lane_freq
    in_head = jnp.arange(LANES) % HEAD_DIM
    sign = jnp.zeros((SUBLANES, LANES), F32)
    sign = sign.at[0].set(jnp.where(in_head < half, -1.0, 0.0))
    sign = sign.at[1].set(jnp.where((in_head >= half) & (in_head < ROT_DIM), 1.0, 0.0))
    n_tiles = seq // tm
    return (jnp.cos(base).reshape(n_tiles, 1, LANES), jnp.sin(base).reshape(n_tiles, 1, LANES),
            jnp.cos(offs), jnp.sin(offs), sign)


def _in_proj(x2, seq, norm_g, w_in, gate_b):
    n = x2.shape[0]
    tm = TOKEN_TILE
    tiles_per_seq = seq // tm
    bsz = n // seq
    cos_a, sin_a, cos_b, sin_b, sign = _rope_tables(seq, tm)

    k = WEIGHT_CAST_STEPS
    tile = lambda i: jnp.maximum(i - k, 0)
    row = lambda w: pl.BlockSpec((tm, w), lambda i: (tile(i), 0))
    tile_tab = pl.BlockSpec((None, 1, LANES), lambda i: (tile(i) % tiles_per_seq, 0, 0))
    res = lambda d: pl.BlockSpec((None, d, tm // d, ATTN_WIDTH),
                                 lambda i: (tile(i) // tiles_per_seq, 0, tile(i) % tiles_per_seq, 0))
    res_shape = lambda d: jax.ShapeDtypeStruct((bsz, d, seq // d, ATTN_WIDTH), BF16)
    d1, d2 = DILATIONS[1], DILATIONS[2]
    outs = pl.pallas_call(
        _in_proj_kernel,
        grid=(k + n // tm,),
        in_specs=[row(D_MODEL), _const_spec((1, D_MODEL)), tile_tab, tile_tab,
                  _const_spec((tm, LANES)), _const_spec((tm, LANES)), _const_spec((SUBLANES, LANES)),
                  pl.BlockSpec((D_MODEL // k, IN_WIDTH), lambda i: (jnp.minimum(i, k - 1), 0)),
                  _const_spec((1, 2 * D_MODEL))],
        out_specs=[pl.BlockSpec((tm // SSM_CHUNK, SSM_ROW), lambda i: (tile(i), 0))]
                  + [row(ATTN_WIDTH), res(d1), res(d2), row(D_MODEL), row(D_MODEL)],
        out_shape=[jax.ShapeDtypeStruct((n // SSM_CHUNK, SSM_ROW), BF16),
                   jax.ShapeDtypeStruct((n, ATTN_WIDTH), BF16), res_shape(d1), res_shape(d2),
                   jax.ShapeDtypeStruct((n, D_MODEL), BF16), jax.ShapeDtypeStruct((n, D_MODEL), BF16)],
        scratch_shapes=[pltpu.VMEM((D_MODEL, IN_WIDTH), BF16), pltpu.VMEM((tm, LANES), F32)],
        compiler_params=_params("arbitrary"),
        name="in_proj",
    )(x2, norm_g.reshape(1, D_MODEL), cos_a, sin_a, cos_b, sin_b, sign, w_in, gate_b.reshape(1, 2 * D_MODEL))
    u, qkv0, qkv1, qkv2, ga, gb = outs
    return u, (qkv0.reshape(bsz, 1, seq, ATTN_WIDTH), qkv1, qkv2), ga, gb


def _cmul(ar, ai, br, bi):
    return ar * br - ai * bi, ar * bi + ai * br


def _discretise(lr, li, dt):
    mag = jnp.exp(lr * dt)
    ar = mag * jnp.cos(li * dt)
    ai = mag * jnp.sin(li * dt)
    den = lr * lr + li * li
    cr = ((ar - 1.0) * lr + ai * li) / den
    ci = (ai * lr - (ar - 1.0) * li) / den
    return ar, ai, cr, ci


def _ssm_prep_kernel(lr_ref, li_ref, ldt_ref, brt_ref, bit_ref, cre_ref, cim_ref, lrf_ref, lif_ref, ldtf_ref,
                     w2_ref, e2_ref, apsr_ref, apsi_ref, aplr_ref, apli_ref, apbr_ref, apbi_ref, er_ref, ei_ref):
    g_n, t_n, h_n, sw = SSM_GROUPS, SSM_CHUNK, SSM_GROUP, 2 * SSM_STATE
    ar, ai, cr, ci = _discretise(lr_ref[...], li_ref[...], jnp.exp(ldt_ref[...]))
    brt, bit = brt_ref[...], bit_ref[...]
    bbr = cr * brt - ci * bit
    bbi = cr * bit + ci * brt
    cre, cim = cre_ref[...], cim_ref[...]
    by_group = lambda a: a.reshape(g_n, h_n, sw)
    pr, pi = jnp.ones_like(ar), jnp.zeros_like(ai)
    for j in range(t_n):
        rows = slice((t_n - 1 - j) * h_n, (t_n - j) * h_n)
        rr, ri = _cmul(pr, pi, bbr, bbi)
        er_ref[:, rows, :] = by_group(rr)
        ei_ref[:, rows, :] = by_group(ri)
        e2_ref[:, rows, :sw] = by_group(rr).astype(BF16)
        e2_ref[:, rows, sw:] = by_group(ri).astype(BF16)
        pr, pi = _cmul(pr, pi, ar, ai)
        rows = slice(j * h_n, (j + 1) * h_n)
        w2_ref[:, rows, SSM_CK:SSM_CK + sw] = by_group(cre * pr - cim * pi).astype(BF16)
        w2_ref[:, rows, SSM_CK + sw:] = by_group(-cre * pi - cim * pr).astype(BF16)

    nt = (((1,), (1,)), ((), ()))
    hi = lax.Precision.HIGHEST

    def toeplitz(g, _):
        rows = pl.ds(pl.multiple_of(g * h_n, h_n), h_n)
        krev = (lax.dot_general(cre_ref[rows, :], er_ref[g], nt, precision=hi, preferred_element_type=F32)
                - lax.dot_general(cim_ref[rows, :], ei_ref[g], nt, precision=hi, preferred_element_type=F32))
        kext = jnp.concatenate([krev, jnp.zeros_like(krev)], axis=1)
        for t in range(t_n):
            off = (t_n - 1 - t) * h_n
            win = kext if off == 0 else pltpu.roll(kext, 2 * SSM_CK - off, 1)
            w2_ref[g, t * h_n:(t + 1) * h_n, :SSM_CK] = win[:, :SSM_CK].astype(BF16)
        return 0
    lax.fori_loop(0, g_n, toeplitz, 0, unroll=8)

    acr, aci, _, _ = _discretise(lrf_ref[...], lif_ref[...], jnp.exp(ldtf_ref[...]))
    for _ in range(4):
        acr, aci = _cmul(acr, aci, acr, aci)
    shape = (SUBLANES, acr.shape[1])
    row = lax.broadcasted_iota(jnp.int32, shape, 0)
    qr, qi = jnp.ones(shape, F32), jnp.zeros(shape, F32)
    apsr_ref[...] = jnp.zeros_like(apsr_ref)
    apsi_ref[...] = jnp.zeros_like(apsi_ref)
    for k in range(SSM_LOG_STEPS + 1):
        for p2 in range(SSM_PAIRS):
            apsr_ref[p2, k:k + 1, :] = acr[:, p2 * sw:(p2 + 1) * sw]
            apsi_ref[p2, k:k + 1, :] = aci[:, p2 * sw:(p2 + 1) * sw]
        if (1 << k) < SUBLANES:
            nr, ni = _cmul(qr, qi, acr, aci)
            bit_set = (row & (1 << k)) != 0
            qr, qi = jnp.where(bit_set, nr, qr), jnp.where(bit_set, ni, qi)
        if (1 << k) == SUBLANES:
            a8r, a8i = acr, aci
        acr, aci = _cmul(acr, aci, acr, aci)
    br, bi = jnp.ones_like(a8r), jnp.zeros_like(a8i)
    blk_r, blk_i = [], []
    for _ in range(SSM_CHUNKS_PER_TILE // SUBLANES):
        blk_r.append(br)
        blk_i.append(bi)
        br, bi = _cmul(br, bi, a8r, a8i)
    blk_r, blk_i = jnp.concatenate(blk_r, axis=0), jnp.concatenate(blk_i, axis=0)
    for p2 in range(SSM_PAIRS):
        lanes = slice(p2 * sw, (p2 + 1) * sw)
        aplr_ref[p2], apli_ref[p2] = qr[:, lanes], qi[:, lanes]
        apbr_ref[p2], apbi_ref[p2] = blk_r[:, lanes], blk_i[:, lanes]


def _pad_pair_lanes(a):
    z = jnp.zeros_like(a)
    even = (jnp.arange(a.shape[0]) % 2 == 0)[:, None, None]
    padded = jnp.where(even, jnp.concatenate([a, z], -1), jnp.concatenate([z, a], -1))
    return padded.reshape(a.shape[0] * a.shape[1], 2 * a.shape[2])


def _ssm_operators(lam_re, lam_im, log_dt, b_re, b_im, c_re, c_im):
    g, p, h = SSM_GROUPS, SSM_STATE, SSM_GROUP
    gp, sw = g * p, 2 * p
    rep = lambda a: jnp.repeat(jnp.tile(a, (1, 2)), h, axis=0)
    ldt2 = jnp.broadcast_to(log_dt[:, None], (g, p))
    full = lambda shape: pl.BlockSpec(shape, lambda: (0,) * len(shape))
    in_arrays = (rep(lam_re), rep(lam_im), rep(ldt2),
                 _pad_pair_lanes(b_re.transpose(0, 2, 1)), _pad_pair_lanes(b_im.transpose(0, 2, 1)),
                 _pad_pair_lanes(c_re), _pad_pair_lanes(c_im),
                 lam_re.reshape(1, gp), lam_im.reshape(1, gp), ldt2.reshape(1, gp))
    out_shapes = ([((g, SSM_CK, SSM_CK + 2 * sw), BF16), ((g, SSM_CK, 2 * sw), BF16)]
                  + [((SSM_PAIRS, 2 * SUBLANES, sw), F32)] * 2
                  + [((SSM_PAIRS, SUBLANES, sw), F32)] * 2
                  + [((SSM_PAIRS, SSM_CHUNKS_PER_TILE // SUBLANES, sw), F32)] * 2)
    w2, e2, *powers = pl.pallas_call(
        _ssm_prep_kernel,
        in_specs=[full(a.shape) for a in in_arrays],
        out_specs=[full(s) for s, _ in out_shapes],
        out_shape=[jax.ShapeDtypeStruct(s, dt) for s, dt in out_shapes],
        scratch_shapes=[pltpu.VMEM((g, SSM_CK, sw), F32), pltpu.VMEM((g, SSM_CK, sw), F32)],
        compiler_params=pltpu.CompilerParams(vmem_limit_bytes=VMEM_LIMIT_BYTES),
        name="ssm_prep",
    )(*in_arrays)
    return (w2, e2.reshape(SSM_PAIRS, 2 * SSM_CK, 2 * sw), *powers)


def _shift_rows(z, s, row):
    if s % SUBLANES == 0:
        return jnp.concatenate([jnp.zeros((s, z.shape[1]), z.dtype), z[:-s]], axis=0)
    return jnp.where(row >= s, pltpu.roll(z, s, 0), 0.0)


def _ssm_scan_kernel(u_ref, d_ref, w2_ref, e2_ref, apsr_ref, apsi_ref, aplr_ref, apli_ref, apbr_ref, apbi_ref,
                     y_ref, xs_ref, sc_ref, yt_ref, carry_ref, loc_ref, sin_ref):
    t_n, h_n, c_n = SSM_CHUNK, SSM_GROUP, SSM_CHUNKS_PER_TILE
    n_slab = SSM_WIDTH // LANES
    pairs_per_slab = LANES // (2 * h_n)
    sw = 2 * SSM_STATE
    blk_n = SUBLANES
    n_blk = c_n // blk_n
    log_blk = blk_n.bit_length() - 1

    @pl.when(pl.program_id(1) == 0)
    def _():
        carry_ref[...] = jnp.zeros_like(carry_ref)

    for t in range(t_n):
        for j in range(n_slab):
            col = t * SSM_WIDTH + j * LANES
            blk = u_ref[:, col:col + LANES].T
            xs_ref[j * pairs_per_slab:(j + 1) * pairs_per_slab, :, t * h_n:(t + 1) * h_n, :] = (
                blk.reshape(pairs_per_slab, 2, h_n, c_n))

    row = lax.broadcasted_iota(jnp.int32, (n_blk, sw), 0)
    nt = (((1,), (1,)), ((), ()))
    tn = (((0,), (0,)), ((), ()))

    def local_states(pr, _):
        xp = xs_ref[pr].reshape(2 * SSM_CK, c_n)
        loc = lax.dot_general(xp, e2_ref[pr], tn, preferred_element_type=F32)
        loc_ref[pr, 0] = loc[:, :sw]
        loc_ref[pr, 1] = loc[:, sw:]
        return 0
    lax.fori_loop(0, SSM_PAIRS, local_states, 0, unroll=8)

    def chunk_scan(pr, _):
        slot = pr % sin_ref.shape[0]
        power = lambda k: (apsr_ref[pr, k:k + 1, :], apsi_ref[pr, k:k + 1, :])
        zr, zi = [], []
        for lo in range(blk_n):
            rows = pl.ds(lo, n_blk, stride=blk_n)
            xr, xi = loc_ref[pr, 0, rows, :], loc_ref[pr, 1, rows, :]
            if lo:
                dr, di = _cmul(zr[-1], zi[-1], *power(0))
                xr, xi = xr + dr, xi + di
            zr.append(xr)
            zi.append(xi)
        er, ei = zr[-1], zi[-1]
        s = 1
        while s < n_blk:
            dr, di = _cmul(_shift_rows(er, s, row), _shift_rows(ei, s, row),
                           *power(log_blk + s.bit_length() - 1))
            er, ei = er + dr, ei + di
            s *= 2
        cr, ci = carry_ref[pr, 0:1, :], carry_ref[pr, 1:2, :]
        hr, hi = _cmul(apbr_ref[pr], apbi_ref[pr], cr, ci)
        br, bi = _shift_rows(er, 1, row) + hr, _shift_rows(ei, 1, row) + hi
        for lo in range(blk_n):
            sr, si = _cmul(br, bi, aplr_ref[pr, lo:lo + 1, :], apli_ref[pr, lo:lo + 1, :])
            if lo:
                sr, si = sr + zr[lo - 1], si + zi[lo - 1]
            rows = pl.ds(lo, n_blk, stride=blk_n)
            sin_ref[slot, 0, rows, :] = sr
            sin_ref[slot, 1, rows, :] = si
        sc_ref[pr, :, :sw] = sin_ref[slot, 0].astype(BF16)
        sc_ref[pr, :, sw:] = sin_ref[slot, 1].astype(BF16)
        nr, ni = _cmul(cr, ci, *power(SSM_LOG_STEPS))
        carry_ref[pr, 0:1, :] = er[n_blk - 1:n_blk, :] + nr
        carry_ref[pr, 1:2, :] = ei[n_blk - 1:n_blk, :] + ni
        return 0
    lax.fori_loop(0, SSM_PAIRS, chunk_scan, 0, unroll=4)

    def outputs(g, _):
        pr = g // 2
        yg = (_dot(w2_ref[g, :, :SSM_CK], xs_ref[pr, g % 2])
              + lax.dot_general(w2_ref[g, :, SSM_CK:], sc_ref[pr], nt, preferred_element_type=F32))
        yt_ref[:, pl.ds(pl.multiple_of(g * h_n, h_n), h_n), :] = yg.reshape(t_n, h_n, c_n)
        return 0
    lax.fori_loop(0, SSM_GROUPS, outputs, 0, unroll=8)

    for t in range(t_n):
        for j in range(n_slab):
            sl = slice(j * LANES, (j + 1) * LANES)
            col = t * SSM_WIDTH + j * LANES
            y_ref[:, col:col + LANES] = (
                yt_ref[t, sl, :].T + d_ref[:, sl] * u_ref[:, col:col + LANES].astype(F32)).astype(BF16)


def _ssm_scan(u_rows, bsz, d_skip, ops):
    g, p, c_n = SSM_GROUPS, SSM_STATE, SSM_CHUNKS_PER_TILE
    tiles = u_rows.shape[0] // (bsz * c_n)
    tile = pl.BlockSpec((c_n, SSM_ROW), lambda b, i: (b * tiles + i, 0))
    return pl.pallas_call(
        _ssm_scan_kernel,
        grid=(bsz, tiles),
        in_specs=[tile, _const_spec((1, SSM_WIDTH))] + [_const_spec(op.shape) for op in ops],
        out_specs=tile,
        out_shape=jax.ShapeDtypeStruct(u_rows.shape, BF16),
        scratch_shapes=[pltpu.VMEM((SSM_PAIRS, 2, SSM_CK, c_n), BF16),
                        pltpu.VMEM((SSM_PAIRS, c_n, 4 * p), BF16),
                        pltpu.VMEM((SSM_CHUNK, SSM_WIDTH, c_n), F32),
                        pltpu.VMEM((SSM_PAIRS, SUBLANES, 2 * p), F32),
                        pltpu.VMEM((SSM_PAIRS, 2, c_n, 2 * p), F32),
                        pltpu.VMEM((4, 2, c_n, 2 * p), F32)],
        compiler_params=_params("parallel", "arbitrary"),
        name="ssm_scan",
    )(u_rows, d_skip.reshape(1, SSM_WIDTH), *ops)


def _attn_kernel(q_ref, kc_ref, kp_ref, vc_ref, vp_ref, o_ref, lse_ref):
    qb, nk = ATTN_QB, ATTN_QB + WINDOW_KEYS
    row = lax.broadcasted_iota(jnp.int32, (qb, nk), 0)
    col = lax.broadcasted_iota(jnp.int32, (qb, nk), 1)
    dist = row + WINDOW_KEYS - col
    in_band = (dist >= 0) & (dist <= WINDOW_KEYS)
    in_band_first = in_band & ((col >= WINDOW_KEYS) | (pl.program_id(2) > 0))
    lane = lax.broadcasted_iota(jnp.int32, (qb, LANES), 1)
    first_head = lane < HEAD_DIM
    nt = (((1,), (1,)), ((), ()))
    n_res, n_rows = o_ref.shape[0], o_ref.shape[1]
    for res in range(n_res):
        for sb in range(n_rows // qb):
            rows = slice(sb * qb, (sb + 1) * qb)
            valid = in_band if sb else in_band_first
            for pair in range(GROUP_WIDTH // LANES):
                cols = slice(pair * LANES, (pair + 1) * LANES)
                qp = q_ref[res, rows, cols]
                if sb:
                    window = slice(sb * qb - WINDOW_KEYS, (sb + 1) * qb)
                    kp, vp = kc_ref[res, window, cols], vc_ref[res, window, cols]
                else:
                    kp = jnp.concatenate([kp_ref[res, :, cols], kc_ref[res, :qb, cols]], axis=0)
                    vp = jnp.concatenate([vp_ref[res, :, cols], vc_ref[res, :qb, cols]], axis=0)
                outs, lses = [], []
                for sel in (first_head, ~first_head):
                    qm = jnp.where(sel, qp, jnp.zeros_like(qp))
                    s = lax.dot_general(qm, kp, nt, preferred_element_type=F32)
                    s = jnp.where(valid, s, NEG_BIG)
                    m = jnp.max(s, axis=-1, keepdims=True)
                    e = jnp.exp(s - m)
                    den = jnp.sum(e, axis=-1, keepdims=True)
                    outs.append(_dot(e.astype(BF16), vp) / den)
                    lses.append(m + jnp.log(den))
                o_ref[res, rows, cols] = jnp.where(first_head, outs[0], outs[1]).astype(BF16)
                lse_ref[res, rows, cols] = jnp.where(first_head, lses[0], lses[1])


def _attn_group(qkv4):
    bsz, dil, lr, _ = qkv4.shape
    rows = min(ATTN_STEP_ROWS, lr)
    n_res = ATTN_STEP_ROWS // rows
    back = rows // WINDOW_KEYS
    cur = lambda part: pl.BlockSpec((None, n_res, rows, GROUP_WIDTH), lambda b, r, i: (b, r, i, part))
    prev = lambda part: pl.BlockSpec((None, n_res, WINDOW_KEYS, GROUP_WIDTH),
                                     lambda b, r, i: (b, r, jnp.maximum(i * back - 1, 0), part))
    out_shape = (bsz, dil, lr, GROUP_WIDTH)
    return pl.pallas_call(
        _attn_kernel,
        grid=(bsz, dil // n_res, lr // rows),
        in_specs=[cur(0), cur(1), prev(1), cur(2), prev(2)],
        out_specs=[cur(0), cur(0)],
        out_shape=[jax.ShapeDtypeStruct(out_shape, BF16), jax.ShapeDtypeStruct(out_shape, F32)],
        compiler_params=_params("parallel", "parallel", "arbitrary"),
        name=f"attn_d{dil}",
    )(qkv4, qkv4, qkv4, qkv4, qkv4)


def _load_token_major(stage_ref, in_ref, tok0, ntok, lanes):
    dil = in_ref.shape[0]
    first, rows = tok0 // dil, ntok // dil
    if dil == 1:
        return in_ref[0, first:first + rows, lanes].astype(F32)
    for r in range(dil):
        stage_ref[pl.ds(r, rows, stride=dil), :] = in_ref[r, first:first + rows, lanes].astype(F32)
    return stage_ref[...]


def _merge_kernel(x_ref, ys_ref, gluw_ref, glub_ref, wa_ref,
                  o0_ref, o1_ref, o2_ref, l0_ref, l1_ref, l2_ref, wb_ref,
                  ga_ref, gb_ref, wout_ref, h_ref, stage_ref, y_ref, ya_ref, attn_ref, mix_ref):
    pieces = lambda width: [slice(c * COL_TILE, (c + 1) * COL_TILE) for c in range(width // COL_TILE)]
    for s in range(x_ref.shape[0] // SUB_TILE):
        tok0 = s * SUB_TILE
        rows = slice(tok0, tok0 + SUB_TILE)
        stage = stage_ref.at[s]
        y_ref[s] = jax.nn.gelu(_load_chunk_rows(stage, ys_ref, SSM_WIDTH, tok0 // SSM_CHUNK, SUB_TILE // SSM_CHUNK))
        y_bf = y_ref[s].astype(BF16)
        for cols in pieces(SSM_WIDTH):
            gate = jax.nn.sigmoid(_dot(y_bf, gluw_ref[:, cols]) + glub_ref[:, cols])
            ya_ref[s, :, cols] = (y_ref[s, :, cols] * gate).astype(BF16)

        for j in range(GROUP_WIDTH // LANES):
            lanes = slice(j * LANES, (j + 1) * LANES)
            o, l = ([_load_token_major(stage, ref, tok0, SUB_TILE, lanes) for ref in refs]
                    for refs in ((o0_ref, o1_ref, o2_ref), (l0_ref, l1_ref, l2_ref)))
            top = jnp.maximum(jnp.maximum(l[0], l[1]), l[2])
            w = [jnp.exp(lg - top) for lg in l]
            attn_ref[s, :, lanes] = ((w[0] * o[0] + w[1] * o[1] + w[2] * o[2])
                                     / (w[0] + w[1] + w[2])).astype(BF16)

        for cols in pieces(D_MODEL):
            mix = (ga_ref[rows, cols].astype(F32) * _dot(ya_ref[s], wa_ref[:, cols])
                   + gb_ref[rows, cols].astype(F32) * _dot(attn_ref[s], wb_ref[:, cols]))
            mix_ref[s, :, cols] = mix.astype(BF16)
        for cols in pieces(D_MODEL):
            h_ref[rows, cols] = x_ref[rows, cols] + _dot(mix_ref[s], wout_ref[:, cols])


def _merge(x2, seq, ys_rows, glu_w, glu_b, w_a, attn_outs, w_b, ga, gb, w_out):
    n = x2.shape[0]
    tm = TOKEN_TILE
    tiles_per_seq = seq // tm
    row = lambda w: pl.BlockSpec((tm, w), lambda i: (i, 0))
    res = lambda d: pl.BlockSpec((None, d, tm // d, GROUP_WIDTH),
                                 lambda i: (i // tiles_per_seq, 0, i % tiles_per_seq, 0))
    (o0, l0), (o1, l1), (o2, l2) = attn_outs
    d0, d1, d2 = DILATIONS
    return pl.pallas_call(
        _merge_kernel,
        grid=(n // tm,),
        in_specs=[row(D_MODEL), pl.BlockSpec((tm // SSM_CHUNK, SSM_ROW), lambda i: (i, 0)),
                  _const_spec((SSM_WIDTH, SSM_WIDTH)), _const_spec((1, SSM_WIDTH)),
                  _const_spec((SSM_WIDTH, D_MODEL)),
                  res(d0), res(d1), res(d2), res(d0), res(d1), res(d2),
                  _const_spec((GROUP_WIDTH, D_MODEL)), row(D_MODEL), row(D_MODEL),
                  _const_spec((D_MODEL, D_MODEL))],
        out_specs=row(D_MODEL),
        out_shape=jax.ShapeDtypeStruct((n, D_MODEL), F32),
        scratch_shapes=[pltpu.VMEM((tm // SUB_TILE, SUB_TILE, LANES), F32),
                        pltpu.VMEM((tm // SUB_TILE, SUB_TILE, SSM_WIDTH), F32),
                        pltpu.VMEM((tm // SUB_TILE, SUB_TILE, SSM_WIDTH), BF16),
                        pltpu.VMEM((tm // SUB_TILE, SUB_TILE, GROUP_WIDTH), BF16),
                        pltpu.VMEM((tm // SUB_TILE, SUB_TILE, D_MODEL), BF16)],
        compiler_params=_params("parallel"),
        name="merge",
    )(x2, ys_rows, glu_w.astype(BF16), glu_b.reshape(1, SSM_WIDTH),
      w_a.astype(BF16), o0, o1, o2, l0, l1, l2, w_b.astype(BF16), ga, gb, w_out.astype(BF16))


def _cast_weight_rows(step, src_refs, dst_refs):
    for src, dst in zip(src_refs, dst_refs):
        rb = src.shape[0]
        dst[pl.ds(pl.multiple_of(step * rb, rb), rb), :] = src[...].astype(BF16)


def _ffn_kernel(h_ref, g2_ref, wg32_ref, wu32_ref, cw_ref, cb_ref, wd32_ref, g3_ref, wpg32_ref,
                p_ref, wpp32_ref, gf_ref, out_ref, wg_ref, wu_ref, wd_ref, wpg_ref, wpp_ref,
                act_ref, carry_ref, *, tiles_per_seq):
    step = pl.program_id(0)

    @pl.when(step < WEIGHT_CAST_STEPS)
    def _():
        _cast_weight_rows(step, (wg32_ref, wu32_ref, wd32_ref, wpg32_ref, wpp32_ref),
                          (wg_ref, wu_ref, wd_ref, wpg_ref, wpp_ref))

    @pl.when(step >= WEIGHT_CAST_STEPS)
    def _():
        _ffn_tile(step - WEIGHT_CAST_STEPS, h_ref, g2_ref, wg_ref, wu_ref, cw_ref, cb_ref, wd_ref, g3_ref, wpg_ref,
                  p_ref, wpp_ref, gf_ref, out_ref, act_ref, carry_ref, tiles_per_seq)


def _ffn_tile(tile, h_ref, g2_ref, wg_ref, wu_ref, cw_ref, cb_ref, wd_ref, g3_ref, wpg_ref,
              p_ref, wpp_ref, gf_ref, out_ref, act_ref, carry_ref, tiles_per_seq):
    tm = h_ref.shape[0]

    @pl.when(tile % tiles_per_seq == 0)
    def _():
        carry_ref[...] = jnp.zeros_like(carry_ref)

    subs = [slice(s * SUB_TILE, (s + 1) * SUB_TILE) for s in range(tm // SUB_TILE)]
    hs = [h_ref[rows, :] for rows in subs]
    u2s = [_rms(h, g2_ref[...]).astype(BF16) for h in hs]
    row = lax.broadcasted_iota(jnp.int32, (SUBLANES, FFN_CHUNK), 0)
    for c in range(D_FF // FFN_CHUNK):
        sl = slice(c * FFN_CHUNK, (c + 1) * FFN_CHUNK)
        prev = carry_ref[:, sl]
        for rows, u2 in zip(subs, u2s):
            gp = _dot(u2, wg_ref[:, sl])
            up = _dot(u2, wu_ref[:, sl])
            r1 = pltpu.roll(gp, 1, 0)
            r2 = pltpu.roll(gp, 2, 0)
            r1 = jnp.concatenate([jnp.where(row < 1, pltpu.roll(prev, 1, 0), r1[:SUBLANES]), r1[SUBLANES:]], axis=0)
            r2 = jnp.concatenate([jnp.where(row < 2, pltpu.roll(prev, 2, 0), r2[:SUBLANES]), r2[SUBLANES:]], axis=0)
            gate = cw_ref[0:1, sl] * r2 + cw_ref[1:2, sl] * r1 + cw_ref[2:3, sl] * gp + cb_ref[:, sl]
            act_ref[rows, sl] = (jax.nn.gelu(gate) * up).astype(BF16)
            prev = gp[SUB_TILE - SUBLANES:, :]
        carry_ref[:, sl] = prev
    for rows, h in zip(subs, hs):
        h = h + _dot(act_ref[rows, :], wd_ref[...])
        u3 = _rms(h, g3_ref[...]).astype(BF16)
        h = h + jax.nn.sigmoid(_dot(u3, wpg_ref[...])) * _dot(p_ref[rows, :].astype(BF16), wpp_ref[...])
        out_ref[rows, :] = _rms(h, gf_ref[...])


def _ffn(h1, seq, p2, norm_g, w_gate, w_up, conv_w, conv_b, w_down, ple_g, ple_w_gate, ple_w_proj, final_g):
    n = h1.shape[0]
    tm = TOKEN_TILE
    k = WEIGHT_CAST_STEPS
    row = lambda w: pl.BlockSpec((tm, w), lambda i: (jnp.maximum(i - k, 0), 0))
    wrows = lambda a: pl.BlockSpec((a.shape[0] // k, a.shape[1]), lambda i: (jnp.minimum(i, k - 1), 0))
    resident = lambda a: pltpu.VMEM(a.shape, BF16)
    vec = lambda a: a.reshape(1, -1)
    weights = (w_gate, w_up, w_down, ple_w_gate, ple_w_proj)
    return pl.pallas_call(
        functools.partial(_ffn_kernel, tiles_per_seq=seq // tm),
        grid=(k + n // tm,),
        in_specs=[row(D_MODEL), _const_spec((1, D_MODEL)), wrows(w_gate),
                  wrows(w_up), _const_spec((CONV_WIDTH, D_FF)), _const_spec((1, D_FF)),
                  wrows(w_down), _const_spec((1, D_MODEL)), wrows(ple_w_gate),
                  row(PLE_DIM), wrows(ple_w_proj), _const_spec((1, D_MODEL))],
        out_specs=row(D_MODEL),
        out_shape=jax.ShapeDtypeStruct((n, D_MODEL), F32),
        scratch_shapes=[resident(w) for w in weights]
                       + [pltpu.VMEM((tm, D_FF), BF16), pltpu.VMEM((SUBLANES, D_FF), F32)],
        compiler_params=_params("arbitrary"),
        name="ffn",
    )(h1, vec(norm_g), w_gate, w_up, conv_w, vec(conv_b),
      w_down, vec(ple_g), ple_w_gate, p2, ple_w_proj, vec(final_g))


def _layer(h2, bsz, seq, p2, mix_norm_g, w_in, gate_b, ssm_lam_re, ssm_lam_im, ssm_log_dt, ssm_b_re,
           ssm_b_im, ssm_c_re, ssm_c_im, ssm_d, ssm_glu_w, ssm_glu_b, w_branch_a, w_branch_b, w_out,
           ffn_norm_g, ffn_w_gate, ffn_w_up, ffn_conv_w, ffn_conv_b, ffn_w_down,
           ple_norm_g, ple_w_gate, ple_w_proj, out_norm_g):
    u_rows, qkv, ga, gb = _in_proj(h2, seq, mix_norm_g, w_in, gate_b)
    ops = _ssm_operators(ssm_lam_re, ssm_lam_im, ssm_log_dt, ssm_b_re, ssm_b_im, ssm_c_re, ssm_c_im)
    ys_rows = _ssm_scan(u_rows, bsz, ssm_d.reshape(-1), ops)
    attn_outs = [_attn_group(group) for group in qkv]
    h1 = _merge(h2, seq, ys_rows, ssm_glu_w, ssm_glu_b, w_branch_a, attn_outs, w_branch_b, ga, gb, w_out)
    return _ffn(h1, seq, p2, ffn_norm_g, ffn_w_gate, ffn_w_up, ffn_conv_w, ffn_conv_b, ffn_w_down,
                ple_norm_g, ple_w_gate, ple_w_proj, out_norm_g)


def kernel(x, p, mix_norm_g, w_in, gate_b, ssm_lam_re, ssm_lam_im, ssm_log_dt, ssm_b_re, ssm_b_im, ssm_c_re, ssm_c_im, ssm_d, ssm_glu_w, ssm_glu_b, w_branch_a, w_branch_b, w_out, ffn_norm_g, ffn_w_gate, ffn_w_up, ffn_conv_w, ffn_conv_b, ffn_w_down, ple_norm_g, ple_w_gate, ple_w_proj, final_norm_g):
    bsz, seq, _ = x.shape
    depth = p.shape[0]
    assert depth == 1, "the final norm is fused into the layer's last kernel"
    h2 = x.reshape(bsz * seq, D_MODEL)
    out = _layer(h2, bsz, seq, p[0].reshape(bsz * seq, PLE_DIM), mix_norm_g[0], w_in[0], gate_b[0],
                 ssm_lam_re[0], ssm_lam_im[0], ssm_log_dt[0], ssm_b_re[0], ssm_b_im[0], ssm_c_re[0],
                 ssm_c_im[0], ssm_d[0], ssm_glu_w[0], ssm_glu_b[0], w_branch_a[0], w_branch_b[0],
                 w_out[0], ffn_norm_g[0], ffn_w_gate[0], ffn_w_up[0], ffn_conv_w[0], ffn_conv_b[0],
                 ffn_w_down[0], ple_norm_g[0], ple_w_gate[0], ple_w_proj[0], final_norm_g)
    return out.reshape(bsz, seq, D_MODEL)
```

```python
import functools

import jax
import jax.numpy as jnp
from jax import lax
from jax.experimental import pallas as pl
from jax.experimental.pallas import tpu as pltpu

F32 = jnp.float32
BF16 = jnp.bfloat16

D_MODEL = 1024
EPS = 1e-6
PLE_DIM = 256
SSM_GROUP = 16
SSM_STATE = 64
SSM_WIDTH = 512
SSM_GROUPS = SSM_WIDTH // SSM_GROUP
HEAD_DIM = 64
DILATIONS = (1, 4, 16)
WINDOW_KEYS = 128
HEADS_PER_GROUP = 4
GROUP_WIDTH = HEADS_PER_GROUP * HEAD_DIM
ATTN_WIDTH = len(DILATIONS) * GROUP_WIDTH
ROT_DIM = HEAD_DIM // 4
ROPE_THETA = 500000.0
NEG_BIG = -1e30
D_FF = 2816
CONV_WIDTH = 3
OFF_Q = SSM_WIDTH
OFF_K = OFF_Q + ATTN_WIDTH
OFF_V = OFF_K + ATTN_WIDTH
OFF_GA = OFF_V + ATTN_WIDTH
OFF_GB = OFF_GA + D_MODEL
IN_WIDTH = OFF_GB + D_MODEL

LANES = 128
SUBLANES = 8
VMEM_LIMIT_BYTES = 56 * 1024 * 1024

TOKEN_TILE = 1024
SUB_TILE = 256
COL_TILE = 256
SSM_CHUNK = 16
SSM_CHUNKS_PER_TILE = 256
SSM_TILE = SSM_CHUNK * SSM_CHUNKS_PER_TILE
SSM_CK = SSM_CHUNK * SSM_GROUP
SSM_ROW = SSM_CHUNK * SSM_WIDTH
SSM_PAIRS = SSM_GROUPS // 2
SSM_LOG_STEPS = 8
ATTN_QB = 128
ATTN_STEP_ROWS = 2048
FFN_CHUNK = 256
WEIGHT_CAST_STEPS = 8


def _dot(a, b):
    return jnp.dot(a, b, preferred_element_type=F32)


def _rms(x, g):
    var = jnp.mean(x * x, axis=-1, keepdims=True)
    return x * lax.rsqrt(var + EPS) * g


def _const_spec(shape):
    nd = len(shape)
    return pl.BlockSpec(shape, lambda *_: (0,) * nd, pipeline_mode=pl.Buffered(1))


def _params(*sem):
    return pltpu.CompilerParams(dimension_semantics=sem, vmem_limit_bytes=VMEM_LIMIT_BYTES)


def _rope(z, cos, sin_lo, sin_hi):
    up = pltpu.roll(z, LANES - ROT_DIM // 2, 1)
    dn = pltpu.roll(z, ROT_DIM // 2, 1)
    return z * cos + up * sin_lo + dn * sin_hi


def _store_residue_major(stage_ref, out_ref, z, dil, tok0, col0=0):
    rows, first = z.shape[0] // dil, tok0 // dil
    for j in range(z.shape[1] // LANES):
        stage_ref[...] = z[:, j * LANES:(j + 1) * LANES]
        sl = slice(col0 + j * LANES, col0 + (j + 1) * LANES)
        for r in range(dil):
            out_ref[r, first:first + rows, sl] = stage_ref[pl.ds(r, rows, stride=dil), :].astype(out_ref.dtype)


def _store_chunk_rows(stage_ref, out_ref, z, c0):
    rows, width = z.shape[0] // SSM_CHUNK, z.shape[1]
    for j in range(width // LANES):
        stage_ref[...] = z[:, j * LANES:(j + 1) * LANES]
        for t in range(SSM_CHUNK):
            col = t * width + j * LANES
            out_ref[c0:c0 + rows, col:col + LANES] = (
                stage_ref[pl.ds(t, rows, stride=SSM_CHUNK), :].astype(out_ref.dtype))


def _interleave_rows(stage_ref, blocks):
    n, rows = len(blocks), blocks[0].shape[0]
    out = stage_ref.at[1]
    if n % 4 or n == 4:
        for t, blk in enumerate(blocks):
            out[pl.ds(t, rows, stride=n), :] = blk
        return out[...]
    mid, quarter = stage_ref.at[0], n // 4
    for t, blk in enumerate(blocks):
        t_hi, t_lo = divmod(t, 4)
        mid[pl.ds(t_lo * quarter * rows + t_hi, rows, stride=quarter), :] = blk
    for t_lo in range(4):
        out[pl.ds(t_lo, quarter * rows, stride=4), :] = mid[t_lo * quarter * rows:(t_lo + 1) * quarter * rows, :]
    return out[...]


def _load_chunk_rows(stage_ref, in_ref, width, c0, rows):
    slabs = []
    for j in range(width // LANES):
        cols = [t * width + j * LANES for t in range(SSM_CHUNK)]
        slabs.append(_interleave_rows(
            stage_ref, [in_ref[c0:c0 + rows, col:col + LANES].astype(F32) for col in cols]))
    return jnp.concatenate(slabs, axis=1)


def _in_proj_kernel(x_ref, g_ref, ca_ref, sa_ref, cb_ref, sb_ref, sign_ref, w32_ref, bg_ref,
                    s_ref, qkv0_ref, qkv1_ref, qkv2_ref, ga_ref, gb_ref, w_ref, stage_ref):
    step = pl.program_id(0)

    @pl.when(step < WEIGHT_CAST_STEPS)
    def _():
        _cast_weight_rows(step, (w32_ref,), (w_ref,))

    @pl.when(step >= WEIGHT_CAST_STEPS)
    def _():
        _in_proj_tile(x_ref, g_ref, ca_ref, sa_ref, cb_ref, sb_ref, sign_ref, w_ref, bg_ref,
                      s_ref, qkv0_ref, qkv1_ref, qkv2_ref, ga_ref, gb_ref, stage_ref)


def _in_proj_tile(x_ref, g_ref, ca_ref, sa_ref, cb_ref, sb_ref, sign_ref, w_ref, bg_ref,
                  s_ref, qkv0_ref, qkv1_ref, qkv2_ref, ga_ref, gb_ref, stage_ref):
    u = _rms(x_ref[...], g_ref[...]).astype(BF16)
    _store_chunk_rows(stage_ref, s_ref, _dot(u, w_ref[:, :OFF_Q]), 0)
    ca, sa, cb, sb = ca_ref[...], sa_ref[...], cb_ref[...], sb_ref[...]
    cos = ca * cb - sa * sb
    sin = sa * cb + ca * sb
    slo, shi = sin * sign_ref[0:1, :], sin * sign_ref[1:2, :]
    scale = HEAD_DIM ** -0.5

    def rope(z):
        return jnp.concatenate([_rope(z[:, j * LANES:(j + 1) * LANES], cos, slo, shi)
                                for j in range(z.shape[1] // LANES)], axis=1)

    q = rope(_dot(u, w_ref[:, OFF_Q:OFF_K])) * scale
    k = rope(_dot(u, w_ref[:, OFF_K:OFF_V]))
    v = _dot(u, w_ref[:, OFF_V:OFF_GA])
    for part, z in enumerate((q, k, v)):
        col0 = part * GROUP_WIDTH
        qkv0_ref[:, col0:col0 + GROUP_WIDTH] = z[:, :GROUP_WIDTH].astype(BF16)
        for grp, out_ref in ((1, qkv1_ref), (2, qkv2_ref)):
            _store_residue_major(stage_ref, out_ref, z[:, grp * GROUP_WIDTH:(grp + 1) * GROUP_WIDTH],
                                 DILATIONS[grp], 0, col0)
    ga_ref[...] = jax.nn.sigmoid(_dot(u, w_ref[:, OFF_GA:OFF_GB]) + bg_ref[:, :D_MODEL]).astype(BF16)
    gb_ref[...] = jax.nn.sigmoid(_dot(u, w_ref[:, OFF_GB:]) + bg_ref[:, D_MODEL:]).astype(BF16)


def _rope_tables(seq, tm):
    half = ROT_DIM // 2
    freqs = ROPE_THETA ** (-jnp.arange(half, dtype=F32) * (2.0 / ROT_DIM))
    head = jnp.concatenate([freqs, freqs, jnp.zeros((HEAD_DIM - ROT_DIM,), F32)])
    lane_freq = jnp.tile(head, LANES // HEAD_DIM)[None, :]
    base = jnp.arange(0, seq, tm, dtype=F32)[:, None] * lane_freq
    offs = jnp.arange(tm, dtype=F32)[:, None] * lane_freq
    in_head = jnp.arange(LANES) % HEAD_DIM
    sign = jnp.zeros((SUBLANES, LANES), F32)
    sign = sign.at[0].set(jnp.where(in_head < half, -1.0, 0.0))
    sign = sign.at[1].set(jnp.where((in_head >= half) & (in_head < ROT_DIM), 1.0, 0.0))
    n_tiles = seq // tm
    return (jnp.cos(base).reshape(n_tiles, 1, LANES), jnp.sin(base).reshape(n_tiles, 1, LANES),
            jnp.cos(offs), jnp.sin(offs), sign)


def _in_proj(x2, seq, norm_g, w_in, gate_b):
    n = x2.shape[0]
    tm = TOKEN_TILE
    tiles_per_seq = seq // tm
    bsz = n // seq
    cos_a, sin_a, cos_b, sin_b, sign = _rope_tables(seq, tm)

    k = WEIGHT_CAST_STEPS
    tile = lambda i: jnp.maximum(i - k, 0)
    row = lambda w: pl.BlockSpec((tm, w), lambda i: (tile(i), 0))
    tile_tab = pl.BlockSpec((None, 1, LANES), lambda i: (tile(i) % tiles_per_seq, 0, 0))
    res = lambda d: pl.BlockSpec((None, d, tm // d, ATTN_WIDTH),
                                 lambda i: (tile(i) // tiles_per_seq, 0, tile(i) % tiles_per_seq, 0))
    res_shape = lambda d: jax.ShapeDtypeStruct((bsz, d, seq // d, ATTN_WIDTH), BF16)
    d1, d2 = DILATIONS[1], DILATIONS[2]
    outs = pl.pallas_call(
        _in_proj_kernel,
        grid=(k + n // tm,),
        in_specs=[row(D_MODEL), _const_spec((1, D_MODEL)), tile_tab, tile_tab,
                  _const_spec((tm, LANES)), _const_spec((tm, LANES)), _const_spec((SUBLANES, LANES)),
                  pl.BlockSpec((D_MODEL // k, IN_WIDTH), lambda i: (jnp.minimum(i, k - 1), 0)),
                  _const_spec((1, 2 * D_MODEL))],
        out_specs=[pl.BlockSpec((tm // SSM_CHUNK, SSM_ROW), lambda i: (tile(i), 0))]
                  + [row(ATTN_WIDTH), res(d1), res(d2), row(D_MODEL), row(D_MODEL)],
        out_shape=[jax.ShapeDtypeStruct((n // SSM_CHUNK, SSM_ROW), BF16),
                   jax.ShapeDtypeStruct((n, ATTN_WIDTH), BF16), res_shape(d1), res_shape(d2),
                   jax.ShapeDtypeStruct((n, D_MODEL), BF16), jax.ShapeDtypeStruct((n, D_MODEL), BF16)],
        scratch_shapes=[pltpu.VMEM((D_MODEL, IN_WIDTH), BF16), pltpu.VMEM((tm, LANES), F32)],
        compiler_params=_params("arbitrary"),
        name="in_proj",
    )(x2, norm_g.reshape(1, D_MODEL), cos_a, sin_a, cos_b, sin_b, sign, w_in, gate_b.reshape(1, 2 * D_MODEL))
    u, qkv0, qkv1, qkv2, ga, gb = outs
    return u, (qkv0.reshape(bsz, 1, seq, ATTN_WIDTH), qkv1, qkv2), ga, gb


def _cmul(ar, ai, br, bi):
    return ar * br - ai * bi, ar * bi + ai * br


def _discretise(lr, li, dt):
    mag = jnp.exp(lr * dt)
    ar = mag * jnp.cos(li * dt)
    ai = mag * jnp.sin(li * dt)
    den = lr * lr + li * li
    cr = ((ar - 1.0) * lr + ai * li) / den
    ci = (ai * lr - (ar - 1.0) * li) / den
    return ar, ai, cr, ci


def _ssm_prep_kernel(lr_ref, li_ref, ldt_ref, brt_ref, bit_ref, cre_ref, cim_ref, lrf_ref, lif_ref, ldtf_ref,
                     w2_ref, e2_ref, apsr_ref, apsi_ref, aplr_ref, apli_ref, apbr_ref, apbi_ref, er_ref, ei_ref):
    g_n, t_n, h_n, sw = SSM_GROUPS, SSM_CHUNK, SSM_GROUP, 2 * SSM_STATE
    ar, ai, cr, ci = _discretise(lr_ref[...], li_ref[...], jnp.exp(ldt_ref[...]))
    brt, bit = brt_ref[...], bit_ref[...]
    bbr = cr * brt - ci * bit
    bbi = cr * bit + ci * brt
    cre, cim = cre_ref[...], cim_ref[...]
    by_group = lambda a: a.reshape(g_n, h_n, sw)
    pr, pi = jnp.ones_like(ar), jnp.zeros_like(ai)
    for j in range(t_n):
        rows = slice((t_n - 1 - j) * h_n, (t_n - j) * h_n)
        rr, ri = _cmul(pr, pi, bbr, bbi)
        er_ref[:, rows, :] = by_group(rr)
        ei_ref[:, rows, :] = by_group(ri)
        e2_ref[:, rows, :sw] = by_group(rr).astype(BF16)
        e2_ref[:, rows, sw:] = by_group(ri).astype(BF16)
        pr, pi = _cmul(pr, pi, ar, ai)
        rows = slice(j * h_n, (j + 1) * h_n)
        w2_ref[:, rows, SSM_CK:SSM_CK + sw] = by_group(cre * pr - cim * pi).astype(BF16)
        w2_ref[:, rows, SSM_CK + sw:] = by_group(-cre * pi - cim * pr).astype(BF16)

    nt = (((1,), (1,)), ((), ()))
    hi = lax.Precision.HIGHEST

    def toeplitz(g, _):
        rows = pl.ds(pl.multiple_of(g * h_n, h_n), h_n)
        krev = (lax.dot_general(cre_ref[rows, :], er_ref[g], nt, precision=hi, preferred_element_type=F32)
                - lax.dot_general(cim_ref[rows, :], ei_ref[g], nt, precision=hi, preferred_element_type=F32))
        kext = jnp.concatenate([krev, jnp.zeros_like(krev)], axis=1)
        for t in range(t_n):
            off = (t_n - 1 - t) * h_n
            win = kext if off == 0 else pltpu.roll(kext, 2 * SSM_CK - off, 1)
            w2_ref[g, t * h_n:(t + 1) * h_n, :SSM_CK] = win[:, :SSM_CK].astype(BF16)
        return 0
    lax.fori_loop(0, g_n, toeplitz, 0, unroll=8)

    acr, aci, _, _ = _discretise(lrf_ref[...], lif_ref[...], jnp.exp(ldtf_ref[...]))
    for _ in range(4):
        acr, aci = _cmul(acr, aci, acr, aci)
    shape = (SUBLANES, acr.shape[1])
    row = lax.broadcasted_iota(jnp.int32, shape, 0)
    qr, qi = jnp.ones(shape, F32), jnp.zeros(shape, F32)
    apsr_ref[...] = jnp.zeros_like(apsr_ref)
    apsi_ref[...] = jnp.zeros_like(apsi_ref)
    for k in range(SSM_LOG_STEPS + 1):
        for p2 in range(SSM_PAIRS):
            apsr_ref[p2, k:k + 1, :] = acr[:, p2 * sw:(p2 + 1) * sw]
            apsi_ref[p2, k:k + 1, :] = aci[:, p2 * sw:(p2 + 1) * sw]
        if (1 << k) < SUBLANES:
            nr, ni = _cmul(qr, qi, acr, aci)
            bit_set = (row & (1 << k)) != 0
            qr, qi = jnp.where(bit_set, nr, qr), jnp.where(bit_set, ni, qi)
        if (1 << k) == SUBLANES:
            a8r, a8i = acr, aci
        acr, aci = _cmul(acr, aci, acr, aci)
    br, bi = jnp.ones_like(a8r), jnp.zeros_like(a8i)
    blk_r, blk_i = [], []
    for _ in range(SSM_CHUNKS_PER_TILE // SUBLANES):
        blk_r.append(br)
        blk_i.append(bi)
        br, bi = _cmul(br, bi, a8r, a8i)
    blk_r, blk_i = jnp.concatenate(blk_r, axis=0), jnp.concatenate(blk_i, axis=0)
    for p2 in range(SSM_PAIRS):
        lanes = slice(p2 * sw, (p2 + 1) * sw)
        aplr_ref[p2], apli_ref[p2] = qr[:, lanes], qi[:, lanes]
        apbr_ref[p2], apbi_ref[p2] = blk_r[:, lanes], blk_i[:, lanes]


def _pad_pair_lanes(a):
    z = jnp.zeros_like(a)
    even = (jnp.arange(a.shape[0]) % 2 == 0)[:, None, None]
    padded = jnp.where(even, jnp.concatenate([a, z], -1), jnp.concatenate([z, a], -1))
    return padded.reshape(a.shape[0] * a.shape[1], 2 * a.shape[2])


def _ssm_operators(lam_re, lam_im, log_dt, b_re, b_im, c_re, c_im):
    g, p, h = SSM_GROUPS, SSM_STATE, SSM_GROUP
    gp, sw = g * p, 2 * p
    rep = lambda a: jnp.repeat(jnp.tile(a, (1, 2)), h, axis=0)
    ldt2 = jnp.broadcast_to(log_dt[:, None], (g, p))
    full = lambda shape: pl.BlockSpec(shape, lambda: (0,) * len(shape))
    in_arrays = (rep(lam_re), rep(lam_im), rep(ldt2),
                 _pad_pair_lanes(b_re.transpose(0, 2, 1)), _pad_pair_lanes(b_im.transpose(0, 2, 1)),
                 _pad_pair_lanes(c_re), _pad_pair_lanes(c_im),
                 lam_re.reshape(1, gp), lam_im.reshape(1, gp), ldt2.reshape(1, gp))
    out_shapes = ([((g, SSM_CK, SSM_CK + 2 * sw), BF16), ((g, SSM_CK, 2 * sw), BF16)]
                  + [((SSM_PAIRS, 2 * SUBLANES, sw), F32)] * 2
                  + [((SSM_PAIRS, SUBLANES, sw), F32)] * 2
                  + [((SSM_PAIRS, SSM_CHUNKS_PER_TILE // SUBLANES, sw), F32)] * 2)
    w2, e2, *powers = pl.pallas_call(
        _ssm_prep_kernel,
        in_specs=[full(a.shape) for a in in_arrays],
        out_specs=[full(s) for s, _ in out_shapes],
        out_shape=[jax.ShapeDtypeStruct(s, dt) for s, dt in out_shapes],
        scratch_shapes=[pltpu.VMEM((g, SSM_CK, sw), F32), pltpu.VMEM((g, SSM_CK, sw), F32)],
        compiler_params=pltpu.CompilerParams(vmem_limit_bytes=VMEM_LIMIT_BYTES),
        name="ssm_prep",
    )(*in_arrays)
    return (w2, e2.reshape(SSM_PAIRS, 2 * SSM_CK, 2 * sw), *powers)


def _shift_rows(z, s, row):
    if s % SUBLANES == 0:
        return jnp.concatenate([jnp.zeros((s, z.shape[1]), z.dtype), z[:-s]], axis=0)
    return jnp.where(row >= s, pltpu.roll(z, s, 0), 0.0)


def _ssm_scan_kernel(u_ref, d_ref, w2_ref, e2_ref, apsr_ref, apsi_ref, aplr_ref, apli_ref, apbr_ref, apbi_ref,
                     y_ref, xs_ref, sc_ref, yt_ref, carry_ref, loc_ref, sin_ref):
    t_n, h_n, c_n = SSM_CHUNK, SSM_GROUP, SSM_CHUNKS_PER_TILE
    n_slab = SSM_WIDTH // LANES
    pairs_per_slab = LANES // (2 * h_n)
    sw = 2 * SSM_STATE
    blk_n = SUBLANES
    n_blk = c_n // blk_n
    log_blk = blk_n.bit_length() - 1

    @pl.when(pl.program_id(1) == 0)
    def _():
        carry_ref[...] = jnp.zeros_like(carry_ref)

    for t in range(t_n):
        for j in range(n_slab):
            col = t * SSM_WIDTH + j * LANES
            blk = u_ref[:, col:col + LANES].T
            xs_ref[j * pairs_per_slab:(j + 1) * pairs_per_slab, :, t * h_n:(t + 1) * h_n, :] = (
                blk.reshape(pairs_per_slab, 2, h_n, c_n))

    row = lax.broadcasted_iota(jnp.int32, (n_blk, sw), 0)
    nt = (((1,), (1,)), ((), ()))
    tn = (((0,), (0,)), ((), ()))

    def local_states(pr, _):
        xp = xs_ref[pr].reshape(2 * SSM_CK, c_n)
        loc = lax.dot_general(xp, e2_ref[pr], tn, preferred_element_type=F32)
        loc_ref[pr, 0] = loc[:, :sw]
        loc_ref[pr, 1] = loc[:, sw:]
        return 0
    lax.fori_loop(0, SSM_PAIRS, local_states, 0, unroll=8)

    def chunk_scan(pr, _):
        slot = pr % sin_ref.shape[0]
        power = lambda k: (apsr_ref[pr, k:k + 1, :], apsi_ref[pr, k:k + 1, :])
        zr, zi = [], []
        for lo in range(blk_n):
            rows = pl.ds(lo, n_blk, stride=blk_n)
            xr, xi = loc_ref[pr, 0, rows, :], loc_ref[pr, 1, rows, :]
            if lo:
                dr, di = _cmul(zr[-1], zi[-1], *power(0))
                xr, xi = xr + dr, xi + di
            zr.append(xr)
            zi.append(xi)
        er, ei = zr[-1], zi[-1]
        s = 1
        while s < n_blk:
            dr, di = _cmul(_shift_rows(er, s, row), _shift_rows(ei, s, row),
                           *power(log_blk + s.bit_length() - 1))
            er, ei = er + dr, ei + di
            s *= 2
        cr, ci = carry_ref[pr, 0:1, :], carry_ref[pr, 1:2, :]
        hr, hi = _cmul(apbr_ref[pr], apbi_ref[pr], cr, ci)
        br, bi = _shift_rows(er, 1, row) + hr, _shift_rows(ei, 1, row) + hi
        for lo in range(blk_n):
            sr, si = _cmul(br, bi, aplr_ref[pr, lo:lo + 1, :], apli_ref[pr, lo:lo + 1, :])
            if lo:
                sr, si = sr + zr[lo - 1], si + zi[lo - 1]
            rows = pl.ds(lo, n_blk, stride=blk_n)
            sin_ref[slot, 0, rows, :] = sr
            sin_ref[slot, 1, rows, :] = si
        sc_ref[pr, :, :sw] = sin_ref[slot, 0].astype(BF16)
        sc_ref[pr, :, sw:] = sin_ref[slot, 1].astype(BF16)
        nr, ni = _cmul(cr, ci, *power(SSM_LOG_STEPS))
        carry_ref[pr, 0:1, :] = er[n_blk - 1:n_blk, :] + nr
        carry_ref[pr, 1:2, :] = ei[n_blk - 1:n_blk, :] + ni
        return 0
    lax.fori_loop(0, SSM_PAIRS, chunk_scan, 0, unroll=4)

    def outputs(g, _):
        pr = g // 2
        yg = (_dot(w2_ref[g, :, :SSM_CK], xs_ref[pr, g % 2])
              + lax.dot_general(w2_ref[g, :, SSM_CK:], sc_ref[pr], nt, preferred_element_type=F32))
        yt_ref[:, pl.ds(pl.multiple_of(g * h_n, h_n), h_n), :] = yg.reshape(t_n, h_n, c_n)
        return 0
    lax.fori_loop(0, SSM_GROUPS, outputs, 0, unroll=8)

    for t in range(t_n):
        for j in range(n_slab):
            sl = slice(j * LANES, (j + 1) * LANES)
            col = t * SSM_WIDTH + j * LANES
            y_ref[:, col:col + LANES] = (
                yt_ref[t, sl, :].T + d_ref[:, sl] * u_ref[:, col:col + LANES].astype(F32)).astype(BF16)


def _ssm_scan(u_rows, bsz, d_skip, ops):
    g, p, c_n = SSM_GROUPS, SSM_STATE, SSM_CHUNKS_PER_TILE
    tiles = u_rows.shape[0] // (bsz * c_n)
    tile = pl.BlockSpec((c_n, SSM_ROW), lambda b, i: (b * tiles + i, 0))
    return pl.pallas_call(
        _ssm_scan_kernel,
        grid=(bsz, tiles),
        in_specs=[tile, _const_spec((1, SSM_WIDTH))] + [_const_spec(op.shape) for op in ops],
        out_specs=tile,
        out_shape=jax.ShapeDtypeStruct(u_rows.shape, BF16),
        scratch_shapes=[pltpu.VMEM((SSM_PAIRS, 2, SSM_CK, c_n), BF16),
                        pltpu.VMEM((SSM_PAIRS, c_n, 4 * p), BF16),
                        pltpu.VMEM((SSM_CHUNK, SSM_WIDTH, c_n), F32),
                        pltpu.VMEM((SSM_PAIRS, SUBLANES, 2 * p), F32),
                        pltpu.VMEM((SSM_PAIRS, 2, c_n, 2 * p), F32),
                        pltpu.VMEM((4, 2, c_n, 2 * p), F32)],
        compiler_params=_params("parallel", "arbitrary"),
        name="ssm_scan",
    )(u_rows, d_skip.reshape(1, SSM_WIDTH), *ops)


def _attn_kernel(q_ref, kc_ref, kp_ref, vc_ref, vp_ref, o_ref, lse_ref):
    qb, nk = ATTN_QB, ATTN_QB + WINDOW_KEYS
    row = lax.broadcasted_iota(jnp.int32, (qb, nk), 0)
    col = lax.broadcasted_iota(jnp.int32, (qb, nk), 1)
    dist = row + WINDOW_KEYS - col
    in_band = (dist >= 0) & (dist <= WINDOW_KEYS)
    in_band_first = in_band & ((col >= WINDOW_KEYS) | (pl.program_id(2) > 0))
    lane = lax.broadcasted_iota(jnp.int32, (qb, LANES), 1)
    first_head = lane < HEAD_DIM
    nt = (((1,), (1,)), ((), ()))
    n_res, n_rows = o_ref.shape[0], o_ref.shape[1]
    for res in range(n_res):
        for sb in range(n_rows // qb):
            rows = slice(sb * qb, (sb + 1) * qb)
            valid = in_band if sb else in_band_first
            for pair in range(GROUP_WIDTH // LANES):
                cols = slice(pair * LANES, (pair + 1) * LANES)
                qp = q_ref[res, rows, cols]
                if sb:
                    window = slice(sb * qb - WINDOW_KEYS, (sb + 1) * qb)
                    kp, vp = kc_ref[res, window, cols], vc_ref[res, window, cols]
                else:
                    kp = jnp.concatenate([kp_ref[res, :, cols], kc_ref[res, :qb, cols]], axis=0)
                    vp = jnp.concatenate([vp_ref[res, :, cols], vc_ref[res, :qb, cols]], axis=0)
                outs, lses = [], []
                for sel in (first_head, ~first_head):
                    qm = jnp.where(sel, qp, jnp.zeros_like(qp))
                    s = lax.dot_general(qm, kp, nt, preferred_element_type=F32)
                    s = jnp.where(valid, s, NEG_BIG)
                    m = jnp.max(s, axis=-1, keepdims=True)
                    e = jnp.exp(s - m)
                    den = jnp.sum(e, axis=-1, keepdims=True)
                    outs.append(_dot(e.astype(BF16), vp) / den)
                    lses.append(m + jnp.log(den))
                o_ref[res, rows, cols] = jnp.where(first_head, outs[0], outs[1]).astype(BF16)
                lse_ref[res, rows, cols] = jnp.where(first_head, lses[0], lses[1])


def _attn_group(qkv4):
    bsz, dil, lr, _ = qkv4.shape
    rows = min(ATTN_STEP_ROWS, lr)
    n_res = ATTN_STEP_ROWS // rows
    back = rows // WINDOW_KEYS
    cur = lambda part: pl.BlockSpec((None, n_res, rows, GROUP_WIDTH), lambda b, r, i: (b, r, i, part))
    prev = lambda part: pl.BlockSpec((None, n_res, WINDOW_KEYS, GROUP_WIDTH),
                                     lambda b, r, i: (b, r, jnp.maximum(i * back - 1, 0), part))
    out_shape = (bsz, dil, lr, GROUP_WIDTH)
    return pl.pallas_call(
        _attn_kernel,
        grid=(bsz, dil // n_res, lr // rows),
        in_specs=[cur(0), cur(1), prev(1), cur(2), prev(2)],
        out_specs=[cur(0), cur(0)],
        out_shape=[jax.ShapeDtypeStruct(out_shape, BF16), jax.ShapeDtypeStruct(out_shape, F32)],
        compiler_params=_params("parallel", "parallel", "arbitrary"),
        name=f"attn_d{dil}",
    )(qkv4, qkv4, qkv4, qkv4, qkv4)


def _load_token_major(stage_ref, in_ref, tok0, ntok, lanes):
    dil = in_ref.shape[0]
    first, rows = tok0 // dil, ntok // dil
    if dil == 1:
        return in_ref[0, first:first + rows, lanes].astype(F32)
    return _interleave_rows(stage_ref, [in_ref[r, first:first + rows, lanes].astype(F32) for r in range(dil)])


def _merge_kernel(x_ref, ys_ref, gluw_ref, glub_ref, wa_ref,
                  o0_ref, o1_ref, o2_ref, l0_ref, l1_ref, l2_ref, wb_ref,
                  ga_ref, gb_ref, wout_ref, h_ref, stage_ref, y_ref, ya_ref, attn_ref, mix_ref):
    pieces = lambda width: [slice(c * COL_TILE, (c + 1) * COL_TILE) for c in range(width // COL_TILE)]
    for s in range(x_ref.shape[0] // SUB_TILE):
        tok0 = s * SUB_TILE
        rows = slice(tok0, tok0 + SUB_TILE)
        stage = stage_ref.at[s]
        y_ref[s] = jax.nn.gelu(_load_chunk_rows(stage, ys_ref, SSM_WIDTH, tok0 // SSM_CHUNK, SUB_TILE // SSM_CHUNK))
        y_bf = y_ref[s].astype(BF16)
        for cols in pieces(SSM_WIDTH):
            gate = jax.nn.sigmoid(_dot(y_bf, gluw_ref[:, cols]) + glub_ref[:, cols])
            ya_ref[s, :, cols] = (y_ref[s, :, cols] * gate).astype(BF16)

        for j in range(GROUP_WIDTH // LANES):
            lanes = slice(j * LANES, (j + 1) * LANES)
            o, l = ([_load_token_major(stage, ref, tok0, SUB_TILE, lanes) for ref in refs]
                    for refs in ((o0_ref, o1_ref, o2_ref), (l0_ref, l1_ref, l2_ref)))
            top = jnp.maximum(jnp.maximum(l[0], l[1]), l[2])
            w = [jnp.exp(lg - top) for lg in l]
            attn_ref[s, :, lanes] = ((w[0] * o[0] + w[1] * o[1] + w[2] * o[2])
                                     / (w[0] + w[1] + w[2])).astype(BF16)

        for cols in pieces(D_MODEL):
            mix = (ga_ref[rows, cols].astype(F32) * _dot(ya_ref[s], wa_ref[:, cols])
                   + gb_ref[rows, cols].astype(F32) * _dot(attn_ref[s], wb_ref[:, cols]))
            mix_ref[s, :, cols] = mix.astype(BF16)
        for cols in pieces(D_MODEL):
            h_ref[rows, cols] = x_ref[rows, cols] + _dot(mix_ref[s], wout_ref[:, cols])


def _merge(x2, seq, ys_rows, glu_w, glu_b, w_a, attn_outs, w_b, ga, gb, w_out):
    n = x2.shape[0]
    tm = TOKEN_TILE
    tiles_per_seq = seq // tm
    row = lambda w: pl.BlockSpec((tm, w), lambda i: (i, 0))
    res = lambda d: pl.BlockSpec((None, d, tm // d, GROUP_WIDTH),
                                 lambda i: (i // tiles_per_seq, 0, i % tiles_per_seq, 0))
    (o0, l0), (o1, l1), (o2, l2) = attn_outs
    d0, d1, d2 = DILATIONS
    return pl.pallas_call(
        _merge_kernel,
        grid=(n // tm,),
        in_specs=[row(D_MODEL), pl.BlockSpec((tm // SSM_CHUNK, SSM_ROW), lambda i: (i, 0)),
                  _const_spec((SSM_WIDTH, SSM_WIDTH)), _const_spec((1, SSM_WIDTH)),
                  _const_spec((SSM_WIDTH, D_MODEL)),
                  res(d0), res(d1), res(d2), res(d0), res(d1), res(d2),
                  _const_spec((GROUP_WIDTH, D_MODEL)), row(D_MODEL), row(D_MODEL),
                  _const_spec((D_MODEL, D_MODEL))],
        out_specs=row(D_MODEL),
        out_shape=jax.ShapeDtypeStruct((n, D_MODEL), F32),
        scratch_shapes=[pltpu.VMEM((tm // SUB_TILE, 2, SUB_TILE, LANES), F32),
                        pltpu.VMEM((tm // SUB_TILE, SUB_TILE, SSM_WIDTH), F32),
                        pltpu.VMEM((tm // SUB_TILE, SUB_TILE, SSM_WIDTH), BF16),
                        pltpu.VMEM((tm // SUB_TILE, SUB_TILE, GROUP_WIDTH), BF16),
                        pltpu.VMEM((tm // SUB_TILE, SUB_TILE, D_MODEL), BF16)],
        compiler_params=_params("parallel"),
        name="merge",
    )(x2, ys_rows, glu_w.astype(BF16), glu_b.reshape(1, SSM_WIDTH),
      w_a.astype(BF16), o0, o1, o2, l0, l1, l2, w_b.astype(BF16), ga, gb, w_out.astype(BF16))


def _cast_weight_rows(step, src_refs, dst_refs):
    for src, dst in zip(src_refs, dst_refs):
        rb = src.shape[0]
        dst[pl.ds(pl.multiple_of(step * rb, rb), rb), :] = src[...].astype(BF16)


def _ffn_kernel(h_ref, g2_ref, wg32_ref, wu32_ref, cw_ref, cb_ref, wd32_ref, g3_ref, wpg32_ref,
                p_ref, wpp32_ref, gf_ref, out_ref, wg_ref, wu_ref, wd_ref, wpg_ref, wpp_ref,
                act_ref, carry_ref, *, tiles_per_seq):
    step = pl.program_id(0)

    @pl.when(step < WEIGHT_CAST_STEPS)
    def _():
        _cast_weight_rows(step, (wg32_ref, wu32_ref, wd32_ref, wpg32_ref, wpp32_ref),
                          (wg_ref, wu_ref, wd_ref, wpg_ref, wpp_ref))

    @pl.when(step >= WEIGHT_CAST_STEPS)
    def _():
        _ffn_tile(step - WEIGHT_CAST_STEPS, h_ref, g2_ref, wg_ref, wu_ref, cw_ref, cb_ref, wd_ref, g3_ref, wpg_ref,
                  p_ref, wpp_ref, gf_ref, out_ref, act_ref, carry_ref, tiles_per_seq)


def _ffn_tile(tile, h_ref, g2_ref, wg_ref, wu_ref, cw_ref, cb_ref, wd_ref, g3_ref, wpg_ref,
              p_ref, wpp_ref, gf_ref, out_ref, act_ref, carry_ref, tiles_per_seq):
    tm = h_ref.shape[0]

    @pl.when(tile % tiles_per_seq == 0)
    def _():
        carry_ref[...] = jnp.zeros_like(carry_ref)

    subs = [slice(s * SUB_TILE, (s + 1) * SUB_TILE) for s in range(tm // SUB_TILE)]
    hs = [h_ref[rows, :] for rows in subs]
    u2s = [_rms(h, g2_ref[...]).astype(BF16) for h in hs]
    row = lax.broadcasted_iota(jnp.int32, (SUBLANES, FFN_CHUNK), 0)
    for c in range(D_FF // FFN_CHUNK):
        sl = slice(c * FFN_CHUNK, (c + 1) * FFN_CHUNK)
        prev = carry_ref[:, sl]
        for rows, u2 in zip(subs, u2s):
            gp = _dot(u2, wg_ref[:, sl])
            up = _dot(u2, wu_ref[:, sl])
            r1 = pltpu.roll(gp, 1, 0)
            r2 = pltpu.roll(gp, 2, 0)
            r1 = jnp.concatenate([jnp.where(row < 1, pltpu.roll(prev, 1, 0), r1[:SUBLANES]), r1[SUBLANES:]], axis=0)
            r2 = jnp.concatenate([jnp.where(row < 2, pltpu.roll(prev, 2, 0), r2[:SUBLANES]), r2[SUBLANES:]], axis=0)
            gate = cw_ref[0:1, sl] * r2 + cw_ref[1:2, sl] * r1 + cw_ref[2:3, sl] * gp + cb_ref[:, sl]
            act_ref[rows, sl] = (jax.nn.gelu(gate) * up).astype(BF16)
            prev = gp[SUB_TILE - SUBLANES:, :]
        carry_ref[:, sl] = prev
    for rows, h in zip(subs, hs):
        h = h + _dot(act_ref[rows, :], wd_ref[...])
        u3 = _rms(h, g3_ref[...]).astype(BF16)
        h = h + jax.nn.sigmoid(_dot(u3, wpg_ref[...])) * _dot(p_ref[rows, :].astype(BF16), wpp_ref[...])
        out_ref[rows, :] = _rms(h, gf_ref[...])


def _ffn(h1, seq, p2, norm_g, w_gate, w_up, conv_w, conv_b, w_down, ple_g, ple_w_gate, ple_w_proj, final_g):
    n = h1.shape[0]
    tm = TOKEN_TILE
    k = WEIGHT_CAST_STEPS
    row = lambda w: pl.BlockSpec((tm, w), lambda i: (jnp.maximum(i - k, 0), 0))
    wrows = lambda a: pl.BlockSpec((a.shape[0] // k, a.shape[1]), lambda i: (jnp.minimum(i, k - 1), 0))
    resident = lambda a: pltpu.VMEM(a.shape, BF16)
    vec = lambda a: a.reshape(1, -1)
    weights = (w_gate, w_up, w_down, ple_w_gate, ple_w_proj)
    return pl.pallas_call(
        functools.partial(_ffn_kernel, tiles_per_seq=seq // tm),
        grid=(k + n // tm,),
        in_specs=[row(D_MODEL), _const_spec((1, D_MODEL)), wrows(w_gate),
                  wrows(w_up), _const_spec((CONV_WIDTH, D_FF)), _const_spec((1, D_FF)),
                  wrows(w_down), _const_spec((1, D_MODEL)), wrows(ple_w_gate),
                  row(PLE_DIM), wrows(ple_w_proj), _const_spec((1, D_MODEL))],
        out_specs=row(D_MODEL),
        out_shape=jax.ShapeDtypeStruct((n, D_MODEL), F32),
        scratch_shapes=[resident(w) for w in weights]
                       + [pltpu.VMEM((tm, D_FF), BF16), pltpu.VMEM((SUBLANES, D_FF), F32)],
        compiler_params=_params("arbitrary"),
        name="ffn",
    )(h1, vec(norm_g), w_gate, w_up, conv_w, vec(conv_b),
      w_down, vec(ple_g), ple_w_gate, p2, ple_w_proj, vec(final_g))


def _layer(h2, bsz, seq, p2, mix_norm_g, w_in, gate_b, ssm_lam_re, ssm_lam_im, ssm_log_dt, ssm_b_re,
           ssm_b_im, ssm_c_re, ssm_c_im, ssm_d, ssm_glu_w, ssm_glu_b, w_branch_a, w_branch_b, w_out,
           ffn_norm_g, ffn_w_gate, ffn_w_up, ffn_conv_w, ffn_conv_b, ffn_w_down,
           ple_norm_g, ple_w_gate, ple_w_proj, out_norm_g):
    u_rows, qkv, ga, gb = _in_proj(h2, seq, mix_norm_g, w_in, gate_b)
    ops = _ssm_operators(ssm_lam_re, ssm_lam_im, ssm_log_dt, ssm_b_re, ssm_b_im, ssm_c_re, ssm_c_im)
    ys_rows = _ssm_scan(u_rows, bsz, ssm_d.reshape(-1), ops)
    attn_outs = [_attn_group(group) for group in qkv]
    h1 = _merge(h2, seq, ys_rows, ssm_glu_w, ssm_glu_b, w_branch_a, attn_outs, w_branch_b, ga, gb, w_out)
    return _ffn(h1, seq, p2, ffn_norm_g, ffn_w_gate, ffn_w_up, ffn_conv_w, ffn_conv_b, ffn_w_down,
                ple_norm_g, ple_w_gate, ple_w_proj, out_norm_g)


def kernel(x, p, mix_norm_g, w_in, gate_b, ssm_lam_re, ssm_lam_im, ssm_log_dt, ssm_b_re, ssm_b_im, ssm_c_re, ssm_c_im, ssm_d, ssm_glu_w, ssm_glu_b, w_branch_a, w_branch_b, w_out, ffn_norm_g, ffn_w_gate, ffn_w_up, ffn_conv_w, ffn_conv_b, ffn_w_down, ple_norm_g, ple_w_gate, ple_w_proj, final_norm_g):
    bsz, seq, _ = x.shape
    depth = p.shape[0]
    assert depth == 1, "the final norm is fused into the layer's last kernel"
    h2 = x.reshape(bsz * seq, D_MODEL)
    out = _layer(h2, bsz, seq, p[0].reshape(bsz * seq, PLE_DIM), mix_norm_g[0], w_in[0], gate_b[0],
                 ssm_lam_re[0], ssm_lam_im[0], ssm_log_dt[0], ssm_b_re[0], ssm_b_im[0], ssm_c_re[0],
                 ssm_c_im[0], ssm_d[0], ssm_glu_w[0], ssm_glu_b[0], w_branch_a[0], w_branch_b[0],
                 w_out[0], ffn_norm_g[0], ffn_w_gate[0], ffn_w_up[0], ffn_conv_w[0], ffn_conv_b[0],
                 ffn_w_down[0], ple_norm_g[0], ple_w_gate[0], ple_w_proj[0], final_norm_g)
    return out.reshape(bsz, seq, D_MODEL)
```

```python
import functools

import jax
import jax.numpy as jnp
from jax import lax
from jax.experimental import pallas as pl
from jax.experimental.pallas import tpu as pltpu

F32 = jnp.float32
BF16 = jnp.bfloat16

D_MODEL = 1024
EPS = 1e-6
PLE_DIM = 256
SSM_GROUP = 16
SSM_STATE = 64
SSM_WIDTH = 512
SSM_GROUPS = SSM_WIDTH // SSM_GROUP
HEAD_DIM = 64
DILATIONS = (1, 4, 16)
WINDOW_KEYS = 128
HEADS_PER_GROUP = 4
GROUP_WIDTH = HEADS_PER_GROUP * HEAD_DIM
ATTN_WIDTH = len(DILATIONS) * GROUP_WIDTH
ROT_DIM = HEAD_DIM // 4
ROPE_THETA = 500000.0
NEG_BIG = -1e30
D_FF = 2816
CONV_WIDTH = 3
OFF_Q = SSM_WIDTH
OFF_K = OFF_Q + ATTN_WIDTH
OFF_V = OFF_K + ATTN_WIDTH
OFF_GA = OFF_V + ATTN_WIDTH
OFF_GB = OFF_GA + D_MODEL
IN_WIDTH = OFF_GB + D_MODEL

LANES = 128
SUBLANES = 8
VMEM_LIMIT_BYTES = 56 * 1024 * 1024

TOKEN_TILE = 1024
SUB_TILE = 256
COL_TILE = 256
SSM_CHUNK = 16
SSM_CHUNKS_PER_TILE = 256
SSM_TILE = SSM_CHUNK * SSM_CHUNKS_PER_TILE
SSM_CK = SSM_CHUNK * SSM_GROUP
SSM_ROW = SSM_CHUNK * SSM_WIDTH
SSM_PAIRS = SSM_GROUPS // 2
SSM_LOG_STEPS = 8
ATTN_QB = 128
ATTN_STEP_ROWS = 2048
FFN_CHUNK = 256
WEIGHT_CAST_STEPS = 8


def _dot(a, b):
    return jnp.dot(a, b, preferred_element_type=F32)


def _rms(x, g):
    var = jnp.mean(x * x, axis=-1, keepdims=True)
    return x * lax.rsqrt(var + EPS) * g


def _const_spec(shape):
    nd = len(shape)
    return pl.BlockSpec(shape, lambda *_: (0,) * nd, pipeline_mode=pl.Buffered(1))


def _params(*sem):
    return pltpu.CompilerParams(dimension_semantics=sem, vmem_limit_bytes=VMEM_LIMIT_BYTES)


def _rope(z, cos, sin_lo, sin_hi):
    up = pltpu.roll(z, LANES - ROT_DIM // 2, 1)
    dn = pltpu.roll(z, ROT_DIM // 2, 1)
    return z * cos + up * sin_lo + dn * sin_hi


def _deinterleave_rows(stage_ref, z, n):
    rows = z.shape[0] // n
    src = stage_ref.at[0]
    src[...] = z
    if n % 4 or n == 4:
        return lambda t: src[pl.ds(t, rows, stride=n), :]
    mid, quarter = stage_ref.at[1], n // 4
    for t_lo in range(4):
        mid[t_lo * quarter * rows:(t_lo + 1) * quarter * rows, :] = src[pl.ds(t_lo, quarter * rows, stride=4), :]
    return lambda t: mid[pl.ds((t % 4) * quarter * rows + t // 4, rows, stride=quarter), :]


def _store_residue_major(stage_ref, out_ref, z, dil, tok0, col0=0):
    rows, first = z.shape[0] // dil, tok0 // dil
    for j in range(z.shape[1] // LANES):
        get = _deinterleave_rows(stage_ref, z[:, j * LANES:(j + 1) * LANES], dil)
        sl = slice(col0 + j * LANES, col0 + (j + 1) * LANES)
        for r in range(dil):
            out_ref[r, first:first + rows, sl] = get(r).astype(out_ref.dtype)


def _store_chunk_rows(stage_ref, out_ref, z, c0):
    rows, width = z.shape[0] // SSM_CHUNK, z.shape[1]
    for j in range(width // LANES):
        get = _deinterleave_rows(stage_ref, z[:, j * LANES:(j + 1) * LANES], SSM_CHUNK)
        for t in range(SSM_CHUNK):
            col = t * width + j * LANES
            out_ref[c0:c0 + rows, col:col + LANES] = get(t).astype(out_ref.dtype)


def _interleave_rows(stage_ref, blocks):
    n, rows = len(blocks), blocks[0].shape[0]
    out = stage_ref.at[1]
    if n % 4 or n == 4:
        for t, blk in enumerate(blocks):
            out[pl.ds(t, rows, stride=n), :] = blk
        return out[...]
    mid, quarter = stage_ref.at[0], n // 4
    for t, blk in enumerate(blocks):
        t_hi, t_lo = divmod(t, 4)
        mid[pl.ds(t_lo * quarter * rows + t_hi, rows, stride=quarter), :] = blk
    for t_lo in range(4):
        out[pl.ds(t_lo, quarter * rows, stride=4), :] = mid[t_lo * quarter * rows:(t_lo + 1) * quarter * rows, :]
    return out[...]


def _load_chunk_rows(stage_ref, in_ref, width, c0, rows):
    slabs = []
    for j in range(width // LANES):
        cols = [t * width + j * LANES for t in range(SSM_CHUNK)]
        slabs.append(_interleave_rows(
            stage_ref, [in_ref[c0:c0 + rows, col:col + LANES].astype(F32) for col in cols]))
    return jnp.concatenate(slabs, axis=1)


def _in_proj_kernel(x_ref, g_ref, ca_ref, sa_ref, cb_ref, sb_ref, sign_ref, w32_ref, bg_ref,
                    s_ref, qkv0_ref, qkv1_ref, qkv2_ref, ga_ref, gb_ref, w_ref, stage_ref):
    step = pl.program_id(0)

    @pl.when(step < WEIGHT_CAST_STEPS)
    def _():
        _cast_weight_rows(step, (w32_ref,), (w_ref,))

    @pl.when(step >= WEIGHT_CAST_STEPS)
    def _():
        _in_proj_tile(x_ref, g_ref, ca_ref, sa_ref, cb_ref, sb_ref, sign_ref, w_ref, bg_ref,
                      s_ref, qkv0_ref, qkv1_ref, qkv2_ref, ga_ref, gb_ref, stage_ref)


def _in_proj_tile(x_ref, g_ref, ca_ref, sa_ref, cb_ref, sb_ref, sign_ref, w_ref, bg_ref,
                  s_ref, qkv0_ref, qkv1_ref, qkv2_ref, ga_ref, gb_ref, stage_ref):
    u = _rms(x_ref[...], g_ref[...]).astype(BF16)
    _store_chunk_rows(stage_ref, s_ref, _dot(u, w_ref[:, :OFF_Q]), 0)
    ca, sa, cb, sb = ca_ref[...], sa_ref[...], cb_ref[...], sb_ref[...]
    cos = ca * cb - sa * sb
    sin = sa * cb + ca * sb
    slo, shi = sin * sign_ref[0:1, :], sin * sign_ref[1:2, :]
    scale = HEAD_DIM ** -0.5

    def rope(z):
        return jnp.concatenate([_rope(z[:, j * LANES:(j + 1) * LANES], cos, slo, shi)
                                for j in range(z.shape[1] // LANES)], axis=1)

    q = rope(_dot(u, w_ref[:, OFF_Q:OFF_K])) * scale
    k = rope(_dot(u, w_ref[:, OFF_K:OFF_V]))
    v = _dot(u, w_ref[:, OFF_V:OFF_GA])
    for part, z in enumerate((q, k, v)):
        col0 = part * GROUP_WIDTH
        qkv0_ref[:, col0:col0 + GROUP_WIDTH] = z[:, :GROUP_WIDTH].astype(BF16)
        for grp, out_ref in ((1, qkv1_ref), (2, qkv2_ref)):
            _store_residue_major(stage_ref, out_ref, z[:, grp * GROUP_WIDTH:(grp + 1) * GROUP_WIDTH],
                                 DILATIONS[grp], 0, col0)
    ga_ref[...] = jax.nn.sigmoid(_dot(u, w_ref[:, OFF_GA:OFF_GB]) + bg_ref[:, :D_MODEL]).astype(BF16)
    gb_ref[...] = jax.nn.sigmoid(_dot(u, w_ref[:, OFF_GB:]) + bg_ref[:, D_MODEL:]).astype(BF16)


def _rope_tables(seq, tm):
    half = ROT_DIM // 2
    freqs = ROPE_THETA ** (-jnp.arange(half, dtype=F32) * (2.0 / ROT_DIM))
    head = jnp.concatenate([freqs, freqs, jnp.zeros((HEAD_DIM - ROT_DIM,), F32)])
    lane_freq = jnp.tile(head, LANES // HEAD_DIM)[None, :]
    base = jnp.arange(0, seq, tm, dtype=F32)[:, None] * lane_freq
    offs = jnp.arange(tm, dtype=F32)[:, None] * lane_freq
    in_head = jnp.arange(LANES) % HEAD_DIM
    sign = jnp.zeros((SUBLANES, LANES), F32)
    sign = sign.at[0].set(jnp.where(in_head < half, -1.0, 0.0))
    sign = sign.at[1].set(jnp.where((in_head >= half) & (in_head < ROT_DIM), 1.0, 0.0))
    n_tiles = seq // tm
    return (jnp.cos(base).reshape(n_tiles, 1, LANES), jnp.sin(base).reshape(n_tiles, 1, LANES),
            jnp.cos(offs), jnp.sin(offs), sign)


def _in_proj(x2, seq, norm_g, w_in, gate_b):
    n = x2.shape[0]
    tm = TOKEN_TILE
    tiles_per_seq = seq // tm
    bsz = n // seq
    cos_a, sin_a, cos_b, sin_b, sign = _rope_tables(seq, tm)

    k = WEIGHT_CAST_STEPS
    tile = lambda i: jnp.maximum(i - k, 0)
    row = lambda w: pl.BlockSpec((tm, w), lambda i: (tile(i), 0))
    tile_tab = pl.BlockSpec((None, 1, LANES), lambda i: (tile(i) % tiles_per_seq, 0, 0))
    res = lambda d: pl.BlockSpec((None, d, tm // d, ATTN_WIDTH),
                                 lambda i: (tile(i) // tiles_per_seq, 0, tile(i) % tiles_per_seq, 0))
    res_shape = lambda d: jax.ShapeDtypeStruct((bsz, d, seq // d, ATTN_WIDTH), BF16)
    d1, d2 = DILATIONS[1], DILATIONS[2]
    outs = pl.pallas_call(
        _in_proj_kernel,
        grid=(k + n // tm,),
        in_specs=[row(D_MODEL), _const_spec((1, D_MODEL)), tile_tab, tile_tab,
                  _const_spec((tm, LANES)), _const_spec((tm, LANES)), _const_spec((SUBLANES, LANES)),
                  pl.BlockSpec((D_MODEL // k, IN_WIDTH), lambda i: (jnp.minimum(i, k - 1), 0)),
                  _const_spec((1, 2 * D_MODEL))],
        out_specs=[pl.BlockSpec((tm // SSM_CHUNK, SSM_ROW), lambda i: (tile(i), 0))]
                  + [row(ATTN_WIDTH), res(d1), res(d2), row(D_MODEL), row(D_MODEL)],
        out_shape=[jax.ShapeDtypeStruct((n // SSM_CHUNK, SSM_ROW), BF16),
                   jax.ShapeDtypeStruct((n, ATTN_WIDTH), BF16), res_shape(d1), res_shape(d2),
                   jax.ShapeDtypeStruct((n, D_MODEL), BF16), jax.ShapeDtypeStruct((n, D_MODEL), BF16)],
        scratch_shapes=[pltpu.VMEM((D_MODEL, IN_WIDTH), BF16), pltpu.VMEM((2, tm, LANES), F32)],
        compiler_params=_params("arbitrary"),
        name="in_proj",
    )(x2, norm_g.reshape(1, D_MODEL), cos_a, sin_a, cos_b, sin_b, sign, w_in, gate_b.reshape(1, 2 * D_MODEL))
    u, qkv0, qkv1, qkv2, ga, gb = outs
    return u, (qkv0.reshape(bsz, 1, seq, ATTN_WIDTH), qkv1, qkv2), ga, gb


def _cmul(ar, ai, br, bi):
    return ar * br - ai * bi, ar * bi + ai * br


def _discretise(lr, li, dt):
    mag = jnp.exp(lr * dt)
    ar = mag * jnp.cos(li * dt)
    ai = mag * jnp.sin(li * dt)
    den = lr * lr + li * li
    cr = ((ar - 1.0) * lr + ai * li) / den
    ci = (ai * lr - (ar - 1.0) * li) / den
    return ar, ai, cr, ci


def _ssm_prep_kernel(lr_ref, li_ref, ldt_ref, brt_ref, bit_ref, cre_ref, cim_ref, lrf_ref, lif_ref, ldtf_ref,
                     w2_ref, e2_ref, apsr_ref, apsi_ref, aplr_ref, apli_ref, apbr_ref, apbi_ref, er_ref, ei_ref):
    g_n, t_n, h_n, sw = SSM_GROUPS, SSM_CHUNK, SSM_GROUP, 2 * SSM_STATE
    ar, ai, cr, ci = _discretise(lr_ref[...], li_ref[...], jnp.exp(ldt_ref[...]))
    brt, bit = brt_ref[...], bit_ref[...]
    bbr = cr * brt - ci * bit
    bbi = cr * bit + ci * brt
    cre, cim = cre_ref[...], cim_ref[...]
    by_group = lambda a: a.reshape(g_n, h_n, sw)
    pr, pi = jnp.ones_like(ar), jnp.zeros_like(ai)
    for j in range(t_n):
        rows = slice((t_n - 1 - j) * h_n, (t_n - j) * h_n)
        rr, ri = _cmul(pr, pi, bbr, bbi)
        er_ref[:, rows, :] = by_group(rr)
        ei_ref[:, rows, :] = by_group(ri)
        e2_ref[:, rows, :sw] = by_group(rr).astype(BF16)
        e2_ref[:, rows, sw:] = by_group(ri).astype(BF16)
        pr, pi = _cmul(pr, pi, ar, ai)
        rows = slice(j * h_n, (j + 1) * h_n)
        w2_ref[:, rows, SSM_CK:SSM_CK + sw] = by_group(cre * pr - cim * pi).astype(BF16)
        w2_ref[:, rows, SSM_CK + sw:] = by_group(-cre * pi - cim * pr).astype(BF16)

    nt = (((1,), (1,)), ((), ()))
    hi = lax.Precision.HIGHEST

    def toeplitz(g, _):
        rows = pl.ds(pl.multiple_of(g * h_n, h_n), h_n)
        krev = (lax.dot_general(cre_ref[rows, :], er_ref[g], nt, precision=hi, preferred_element_type=F32)
                - lax.dot_general(cim_ref[rows, :], ei_ref[g], nt, precision=hi, preferred_element_type=F32))
        kext = jnp.concatenate([krev, jnp.zeros_like(krev)], axis=1)
        for t in range(t_n):
            off = (t_n - 1 - t) * h_n
            win = kext if off == 0 else pltpu.roll(kext, 2 * SSM_CK - off, 1)
            w2_ref[g, t * h_n:(t + 1) * h_n, :SSM_CK] = win[:, :SSM_CK].astype(BF16)
        return 0
    lax.fori_loop(0, g_n, toeplitz, 0, unroll=8)

    acr, aci, _, _ = _discretise(lrf_ref[...], lif_ref[...], jnp.exp(ldtf_ref[...]))
    for _ in range(4):
        acr, aci = _cmul(acr, aci, acr, aci)
    shape = (SUBLANES, acr.shape[1])
    row = lax.broadcasted_iota(jnp.int32, shape, 0)
    qr, qi = jnp.ones(shape, F32), jnp.zeros(shape, F32)
    apsr_ref[...] = jnp.zeros_like(apsr_ref)
    apsi_ref[...] = jnp.zeros_like(apsi_ref)
    for k in range(SSM_LOG_STEPS + 1):
        for p2 in range(SSM_PAIRS):
            apsr_ref[p2, k:k + 1, :] = acr[:, p2 * sw:(p2 + 1) * sw]
            apsi_ref[p2, k:k + 1, :] = aci[:, p2 * sw:(p2 + 1) * sw]
        if (1 << k) < SUBLANES:
            nr, ni = _cmul(qr, qi, acr, aci)
            bit_set = (row & (1 << k)) != 0
            qr, qi = jnp.where(bit_set, nr, qr), jnp.where(bit_set, ni, qi)
        if (1 << k) == SUBLANES:
            a8r, a8i = acr, aci
        acr, aci = _cmul(acr, aci, acr, aci)
    br, bi = jnp.ones_like(a8r), jnp.zeros_like(a8i)
    blk_r, blk_i = [], []
    for _ in range(SSM_CHUNKS_PER_TILE // SUBLANES):
        blk_r.append(br)
        blk_i.append(bi)
        br, bi = _cmul(br, bi, a8r, a8i)
    blk_r, blk_i = jnp.concatenate(blk_r, axis=0), jnp.concatenate(blk_i, axis=0)
    for p2 in range(SSM_PAIRS):
        lanes = slice(p2 * sw, (p2 + 1) * sw)
        aplr_ref[p2], apli_ref[p2] = qr[:, lanes], qi[:, lanes]
        apbr_ref[p2], apbi_ref[p2] = blk_r[:, lanes], blk_i[:, lanes]


def _pad_pair_lanes(a):
    z = jnp.zeros_like(a)
    even = (jnp.arange(a.shape[0]) % 2 == 0)[:, None, None]
    padded = jnp.where(even, jnp.concatenate([a, z], -1), jnp.concatenate([z, a], -1))
    return padded.reshape(a.shape[0] * a.shape[1], 2 * a.shape[2])


def _ssm_operators(lam_re, lam_im, log_dt, b_re, b_im, c_re, c_im):
    g, p, h = SSM_GROUPS, SSM_STATE, SSM_GROUP
    gp, sw = g * p, 2 * p
    rep = lambda a: jnp.repeat(jnp.tile(a, (1, 2)), h, axis=0)
    ldt2 = jnp.broadcast_to(log_dt[:, None], (g, p))
    full = lambda shape: pl.BlockSpec(shape, lambda: (0,) * len(shape))
    in_arrays = (rep(lam_re), rep(lam_im), rep(ldt2),
                 _pad_pair_lanes(b_re.transpose(0, 2, 1)), _pad_pair_lanes(b_im.transpose(0, 2, 1)),
                 _pad_pair_lanes(c_re), _pad_pair_lanes(c_im),
                 lam_re.reshape(1, gp), lam_im.reshape(1, gp), ldt2.reshape(1, gp))
    out_shapes = ([((g, SSM_CK, SSM_CK + 2 * sw), BF16), ((g, SSM_CK, 2 * sw), BF16)]
                  + [((SSM_PAIRS, 2 * SUBLANES, sw), F32)] * 2
                  + [((SSM_PAIRS, SUBLANES, sw), F32)] * 2
                  + [((SSM_PAIRS, SSM_CHUNKS_PER_TILE // SUBLANES, sw), F32)] * 2)
    w2, e2, *powers = pl.pallas_call(
        _ssm_prep_kernel,
        in_specs=[full(a.shape) for a in in_arrays],
        out_specs=[full(s) for s, _ in out_shapes],
        out_shape=[jax.ShapeDtypeStruct(s, dt) for s, dt in out_shapes],
        scratch_shapes=[pltpu.VMEM((g, SSM_CK, sw), F32), pltpu.VMEM((g, SSM_CK, sw), F32)],
        compiler_params=pltpu.CompilerParams(vmem_limit_bytes=VMEM_LIMIT_BYTES),
        name="ssm_prep",
    )(*in_arrays)
    return (w2, e2.reshape(SSM_PAIRS, 2 * SSM_CK, 2 * sw), *powers)


def _shift_rows(z, s, row):
    if s % SUBLANES == 0:
        return jnp.concatenate([jnp.zeros((s, z.shape[1]), z.dtype), z[:-s]], axis=0)
    return jnp.where(row >= s, pltpu.roll(z, s, 0), 0.0)


def _ssm_scan_kernel(u_ref, d_ref, w2_ref, e2_ref, apsr_ref, apsi_ref, aplr_ref, apli_ref, apbr_ref, apbi_ref,
                     y_ref, xs_ref, sc_ref, yt_ref, carry_ref, loc_ref, sin_ref):
    t_n, h_n, c_n = SSM_CHUNK, SSM_GROUP, SSM_CHUNKS_PER_TILE
    n_slab = SSM_WIDTH // LANES
    pairs_per_slab = LANES // (2 * h_n)
    sw = 2 * SSM_STATE
    blk_n = SUBLANES
    n_blk = c_n // blk_n
    log_blk = blk_n.bit_length() - 1

    @pl.when(pl.program_id(1) == 0)
    def _():
        carry_ref[...] = jnp.zeros_like(carry_ref)

    for t in range(t_n):
        for j in range(n_slab):
            col = t * SSM_WIDTH + j * LANES
            blk = u_ref[:, col:col + LANES].T
            xs_ref[j * pairs_per_slab:(j + 1) * pairs_per_slab, :, t * h_n:(t + 1) * h_n, :] = (
                blk.reshape(pairs_per_slab, 2, h_n, c_n))

    row = lax.broadcasted_iota(jnp.int32, (n_blk, sw), 0)
    nt = (((1,), (1,)), ((), ()))
    tn = (((0,), (0,)), ((), ()))

    def local_states(pr, _):
        xp = xs_ref[pr].reshape(2 * SSM_CK, c_n)
        loc = lax.dot_general(xp, e2_ref[pr], tn, preferred_element_type=F32)
        loc_ref[pr, 0] = loc[:, :sw]
        loc_ref[pr, 1] = loc[:, sw:]
        return 0
    lax.fori_loop(0, SSM_PAIRS, local_states, 0, unroll=8)

    def chunk_scan(pr, _):
        slot = pr % sin_ref.shape[0]
        power = lambda k: (apsr_ref[pr, k:k + 1, :], apsi_ref[pr, k:k + 1, :])
        zr, zi = [], []
        for lo in range(blk_n):
            rows = pl.ds(lo, n_blk, stride=blk_n)
            xr, xi = loc_ref[pr, 0, rows, :], loc_ref[pr, 1, rows, :]
            if lo:
                dr, di = _cmul(zr[-1], zi[-1], *power(0))
                xr, xi = xr + dr, xi + di
            zr.append(xr)
            zi.append(xi)
        er, ei = zr[-1], zi[-1]
        s = 1
        while s < n_blk:
            dr, di = _cmul(_shift_rows(er, s, row), _shift_rows(ei, s, row),
                           *power(log_blk + s.bit_length() - 1))
            er, ei = er + dr, ei + di
            s *= 2
        cr, ci = carry_ref[pr, 0:1, :], carry_ref[pr, 1:2, :]
        hr, hi = _cmul(apbr_ref[pr], apbi_ref[pr], cr, ci)
        br, bi = _shift_rows(er, 1, row) + hr, _shift_rows(ei, 1, row) + hi
        for lo in range(blk_n):
            sr, si = _cmul(br, bi, aplr_ref[pr, lo:lo + 1, :], apli_ref[pr, lo:lo + 1, :])
            if lo:
                sr, si = sr + zr[lo - 1], si + zi[lo - 1]
            rows = pl.ds(lo, n_blk, stride=blk_n)
            sin_ref[slot, 0, rows, :] = sr
            sin_ref[slot, 1, rows, :] = si
        sc_ref[pr, :, :sw] = sin_ref[slot, 0].astype(BF16)
        sc_ref[pr, :, sw:] = sin_ref[slot, 1].astype(BF16)
        nr, ni = _cmul(cr, ci, *power(SSM_LOG_STEPS))
        carry_ref[pr, 0:1, :] = er[n_blk - 1:n_blk, :] + nr
        carry_ref[pr, 1:2, :] = ei[n_blk - 1:n_blk, :] + ni
        return 0
    lax.fori_loop(0, SSM_PAIRS, chunk_scan, 0, unroll=4)

    def outputs(g, _):
        pr = g // 2
        yg = (_dot(w2_ref[g, :, :SSM_CK], xs_ref[pr, g % 2])
              + lax.dot_general(w2_ref[g, :, SSM_CK:], sc_ref[pr], nt, preferred_element_type=F32))
        yt_ref[:, pl.ds(pl.multiple_of(g * h_n, h_n), h_n), :] = yg.reshape(t_n, h_n, c_n)
        return 0
    lax.fori_loop(0, SSM_GROUPS, outputs, 0, unroll=8)

    for t in range(t_n):
        for j in range(n_slab):
            sl = slice(j * LANES, (j + 1) * LANES)
            col = t * SSM_WIDTH + j * LANES
            y_ref[:, col:col + LANES] = (
                yt_ref[t, sl, :].T + d_ref[:, sl] * u_ref[:, col:col + LANES].astype(F32)).astype(BF16)


def _ssm_scan(u_rows, bsz, d_skip, ops):
    g, p, c_n = SSM_GROUPS, SSM_STATE, SSM_CHUNKS_PER_TILE
    tiles = u_rows.shape[0] // (bsz * c_n)
    tile = pl.BlockSpec((c_n, SSM_ROW), lambda b, i: (b * tiles + i, 0))
    return pl.pallas_call(
        _ssm_scan_kernel,
        grid=(bsz, tiles),
        in_specs=[tile, _const_spec((1, SSM_WIDTH))] + [_const_spec(op.shape) for op in ops],
        out_specs=tile,
        out_shape=jax.ShapeDtypeStruct(u_rows.shape, BF16),
        scratch_shapes=[pltpu.VMEM((SSM_PAIRS, 2, SSM_CK, c_n), BF16),
                        pltpu.VMEM((SSM_PAIRS, c_n, 4 * p), BF16),
                        pltpu.VMEM((SSM_CHUNK, SSM_WIDTH, c_n), F32),
                        pltpu.VMEM((SSM_PAIRS, SUBLANES, 2 * p), F32),
                        pltpu.VMEM((SSM_PAIRS, 2, c_n, 2 * p), F32),
                        pltpu.VMEM((4, 2, c_n, 2 * p), F32)],
        compiler_params=_params("parallel", "arbitrary"),
        name="ssm_scan",
    )(u_rows, d_skip.reshape(1, SSM_WIDTH), *ops)


def _attn_kernel(q_ref, kc_ref, kp_ref, vc_ref, vp_ref, o_ref, lse_ref):
    qb, nk = ATTN_QB, ATTN_QB + WINDOW_KEYS
    row = lax.broadcasted_iota(jnp.int32, (qb, nk), 0)
    col = lax.broadcasted_iota(jnp.int32, (qb, nk), 1)
    dist = row + WINDOW_KEYS - col
    in_band = (dist >= 0) & (dist <= WINDOW_KEYS)
    in_band_first = in_band & ((col >= WINDOW_KEYS) | (pl.program_id(2) > 0))
    lane = lax.broadcasted_iota(jnp.int32, (qb, LANES), 1)
    first_head = lane < HEAD_DIM
    nt = (((1,), (1,)), ((), ()))
    n_res, n_rows = o_ref.shape[0], o_ref.shape[1]
    for res in range(n_res):
        for sb in range(n_rows // qb):
            rows = slice(sb * qb, (sb + 1) * qb)
            valid = in_band if sb else in_band_first
            for pair in range(GROUP_WIDTH // LANES):
                cols = slice(pair * LANES, (pair + 1) * LANES)
                qp = q_ref[res, rows, cols]
                if sb:
                    window = slice(sb * qb - WINDOW_KEYS, (sb + 1) * qb)
                    kp, vp = kc_ref[res, window, cols], vc_ref[res, window, cols]
                else:
                    kp = jnp.concatenate([kp_ref[res, :, cols], kc_ref[res, :qb, cols]], axis=0)
                    vp = jnp.concatenate([vp_ref[res, :, cols], vc_ref[res, :qb, cols]], axis=0)
                outs, lses = [], []
                for sel in (first_head, ~first_head):
                    qm = jnp.where(sel, qp, jnp.zeros_like(qp))
                    s = lax.dot_general(qm, kp, nt, preferred_element_type=F32)
                    s = jnp.where(valid, s, NEG_BIG)
                    m = jnp.max(s, axis=-1, keepdims=True)
                    e = jnp.exp(s - m)
                    den = jnp.sum(e, axis=-1, keepdims=True)
                    outs.append(_dot(e.astype(BF16), vp) / den)
                    lses.append(m + jnp.log(den))
                o_ref[res, rows, cols] = jnp.where(first_head, outs[0], outs[1]).astype(BF16)
                lse_ref[res, rows, cols] = jnp.where(first_head, lses[0], lses[1])


def _attn_group(qkv4):
    bsz, dil, lr, _ = qkv4.shape
    rows = min(ATTN_STEP_ROWS, lr)
    n_res = ATTN_STEP_ROWS // rows
    back = rows // WINDOW_KEYS
    cur = lambda part: pl.BlockSpec((None, n_res, rows, GROUP_WIDTH), lambda b, r, i: (b, r, i, part))
    prev = lambda part: pl.BlockSpec((None, n_res, WINDOW_KEYS, GROUP_WIDTH),
                                     lambda b, r, i: (b, r, jnp.maximum(i * back - 1, 0), part))
    out_shape = (bsz, dil, lr, GROUP_WIDTH)
    return pl.pallas_call(
        _attn_kernel,
        grid=(bsz, dil // n_res, lr // rows),
        in_specs=[cur(0), cur(1), prev(1), cur(2), prev(2)],
        out_specs=[cur(0), cur(0)],
        out_shape=[jax.ShapeDtypeStruct(out_shape, BF16), jax.ShapeDtypeStruct(out_shape, F32)],
        compiler_params=_params("parallel", "parallel", "arbitrary"),
        name=f"attn_d{dil}",
    )(qkv4, qkv4, qkv4, qkv4, qkv4)


def _load_token_major(stage_ref, in_ref, tok0, ntok, lanes):
    dil = in_ref.shape[0]
    first, rows = tok0 // dil, ntok // dil
    if dil == 1:
        return in_ref[0, first:first + rows, lanes].astype(F32)
    return _interleave_rows(stage_ref, [in_ref[r, first:first + rows, lanes].astype(F32) for r in range(dil)])


def _merge_kernel(x_ref, ys_ref, gluw_ref, glub_ref, wa_ref,
                  o0_ref, o1_ref, o2_ref, l0_ref, l1_ref, l2_ref, wb_ref,
                  ga_ref, gb_ref, wout_ref, h_ref, stage_ref, y_ref, ya_ref, attn_ref, mix_ref):
    pieces = lambda width: [slice(c * COL_TILE, (c + 1) * COL_TILE) for c in range(width // COL_TILE)]
    for s in range(x_ref.shape[0] // SUB_TILE):
        tok0 = s * SUB_TILE
        rows = slice(tok0, tok0 + SUB_TILE)
        stage = stage_ref.at[s]
        y_ref[s] = jax.nn.gelu(_load_chunk_rows(stage, ys_ref, SSM_WIDTH, tok0 // SSM_CHUNK, SUB_TILE // SSM_CHUNK))
        y_bf = y_ref[s].astype(BF16)
        for cols in pieces(SSM_WIDTH):
            gate = jax.nn.sigmoid(_dot(y_bf, gluw_ref[:, cols]) + glub_ref[:, cols])
            ya_ref[s, :, cols] = (y_ref[s, :, cols] * gate).astype(BF16)

        for j in range(GROUP_WIDTH // LANES):
            lanes = slice(j * LANES, (j + 1) * LANES)
            o, l = ([_load_token_major(stage, ref, tok0, SUB_TILE, lanes) for ref in refs]
                    for refs in ((o0_ref, o1_ref, o2_ref), (l0_ref, l1_ref, l2_ref)))
            top = jnp.maximum(jnp.maximum(l[0], l[1]), l[2])
            w = [jnp.exp(lg - top) for lg in l]
            attn_ref[s, :, lanes] = ((w[0] * o[0] + w[1] * o[1] + w[2] * o[2])
                                     / (w[0] + w[1] + w[2])).astype(BF16)

        for cols in pieces(D_MODEL):
            mix = (ga_ref[rows, cols].astype(F32) * _dot(ya_ref[s], wa_ref[:, cols])
                   + gb_ref[rows, cols].astype(F32) * _dot(attn_ref[s], wb_ref[:, cols]))
            mix_ref[s, :, cols] = mix.astype(BF16)
        for cols in pieces(D_MODEL):
            h_ref[rows, cols] = x_ref[rows, cols] + _dot(mix_ref[s], wout_ref[:, cols])


def _merge(x2, seq, ys_rows, glu_w, glu_b, w_a, attn_outs, w_b, ga, gb, w_out):
    n = x2.shape[0]
    tm = TOKEN_TILE
    tiles_per_seq = seq // tm
    row = lambda w: pl.BlockSpec((tm, w), lambda i: (i, 0))
    res = lambda d: pl.BlockSpec((None, d, tm // d, GROUP_WIDTH),
                                 lambda i: (i // tiles_per_seq, 0, i % tiles_per_seq, 0))
    (o0, l0), (o1, l1), (o2, l2) = attn_outs
    d0, d1, d2 = DILATIONS
    return pl.pallas_call(
        _merge_kernel,
        grid=(n // tm,),
        in_specs=[row(D_MODEL), pl.BlockSpec((tm // SSM_CHUNK, SSM_ROW), lambda i: (i, 0)),
                  _const_spec((SSM_WIDTH, SSM_WIDTH)), _const_spec((1, SSM_WIDTH)),
                  _const_spec((SSM_WIDTH, D_MODEL)),
                  res(d0), res(d1), res(d2), res(d0), res(d1), res(d2),
                  _const_spec((GROUP_WIDTH, D_MODEL)), row(D_MODEL), row(D_MODEL),
                  _const_spec((D_MODEL, D_MODEL))],
        out_specs=row(D_MODEL),
        out_shape=jax.ShapeDtypeStruct((n, D_MODEL), F32),
        scratch_shapes=[pltpu.VMEM((tm // SUB_TILE, 2, SUB_TILE, LANES), F32),
                        pltpu.VMEM((tm // SUB_TILE, SUB_TILE, SSM_WIDTH), F32),
                        pltpu.VMEM((tm // SUB_TILE, SUB_TILE, SSM_WIDTH), BF16),
                        pltpu.VMEM((tm // SUB_TILE, SUB_TILE, GROUP_WIDTH), BF16),
                        pltpu.VMEM((tm // SUB_TILE, SUB_TILE, D_MODEL), BF16)],
        compiler_params=_params("parallel"),
        name="merge",
    )(x2, ys_rows, glu_w.astype(BF16), glu_b.reshape(1, SSM_WIDTH),
      w_a.astype(BF16), o0, o1, o2, l0, l1, l2, w_b.astype(BF16), ga, gb, w_out.astype(BF16))


def _cast_weight_rows(step, src_refs, dst_refs):
    for src, dst in zip(src_refs, dst_refs):
        rb = src.shape[0]
        dst[pl.ds(pl.multiple_of(step * rb, rb), rb), :] = src[...].astype(BF16)


def _ffn_kernel(h_ref, g2_ref, wg32_ref, wu32_ref, cw_ref, cb_ref, wd32_ref, g3_ref, wpg32_ref,
                p_ref, wpp32_ref, gf_ref, out_ref, wg_ref, wu_ref, wd_ref, wpg_ref, wpp_ref,
                act_ref, carry_ref, *, tiles_per_seq):
    step = pl.program_id(0)

    @pl.when(step < WEIGHT_CAST_STEPS)
    def _():
        _cast_weight_rows(step, (wg32_ref, wu32_ref, wd32_ref, wpg32_ref, wpp32_ref),
                          (wg_ref, wu_ref, wd_ref, wpg_ref, wpp_ref))

    @pl.when(step >= WEIGHT_CAST_STEPS)
    def _():
        _ffn_tile(step - WEIGHT_CAST_STEPS, h_ref, g2_ref, wg_ref, wu_ref, cw_ref, cb_ref, wd_ref, g3_ref, wpg_ref,
                  p_ref, wpp_ref, gf_ref, out_ref, act_ref, carry_ref, tiles_per_seq)


def _ffn_tile(tile, h_ref, g2_ref, wg_ref, wu_ref, cw_ref, cb_ref, wd_ref, g3_ref, wpg_ref,
              p_ref, wpp_ref, gf_ref, out_ref, act_ref, carry_ref, tiles_per_seq):
    tm = h_ref.shape[0]

    @pl.when(tile % tiles_per_seq == 0)
    def _():
        carry_ref[...] = jnp.zeros_like(carry_ref)

    subs = [slice(s * SUB_TILE, (s + 1) * SUB_TILE) for s in range(tm // SUB_TILE)]
    hs = [h_ref[rows, :] for rows in subs]
    u2s = [_rms(h, g2_ref[...]).astype(BF16) for h in hs]
    row = lax.broadcasted_iota(jnp.int32, (SUBLANES, FFN_CHUNK), 0)
    for c in range(D_FF // FFN_CHUNK):
        sl = slice(c * FFN_CHUNK, (c + 1) * FFN_CHUNK)
        prev = carry_ref[:, sl]
        for rows, u2 in zip(subs, u2s):
            gp = _dot(u2, wg_ref[:, sl])
            up = _dot(u2, wu_ref[:, sl])
            r1 = pltpu.roll(gp, 1, 0)
            r2 = pltpu.roll(gp, 2, 0)
            r1 = jnp.concatenate([jnp.where(row < 1, pltpu.roll(prev, 1, 0), r1[:SUBLANES]), r1[SUBLANES:]], axis=0)
            r2 = jnp.concatenate([jnp.where(row < 2, pltpu.roll(prev, 2, 0), r2[:SUBLANES]), r2[SUBLANES:]], axis=0)
            gate = cw_ref[0:1, sl] * r2 + cw_ref[1:2, sl] * r1 + cw_ref[2:3, sl] * gp + cb_ref[:, sl]
            act_ref[rows, sl] = (jax.nn.gelu(gate) * up).astype(BF16)
            prev = gp[SUB_TILE - SUBLANES:, :]
        carry_ref[:, sl] = prev
    for rows, h in zip(subs, hs):
        h = h + _dot(act_ref[rows, :], wd_ref[...])
        u3 = _rms(h, g3_ref[...]).astype(BF16)
        h = h + jax.nn.sigmoid(_dot(u3, wpg_ref[...])) * _dot(p_ref[rows, :].astype(BF16), wpp_ref[...])
        out_ref[rows, :] = _rms(h, gf_ref[...])


def _ffn(h1, seq, p2, norm_g, w_gate, w_up, conv_w, conv_b, w_down, ple_g, ple_w_gate, ple_w_proj, final_g):
    n = h1.shape[0]
    tm = TOKEN_TILE
    k = WEIGHT_CAST_STEPS
    row = lambda w: pl.BlockSpec((tm, w), lambda i: (jnp.maximum(i - k, 0), 0))
    wrows = lambda a: pl.BlockSpec((a.shape[0] // k, a.shape[1]), lambda i: (jnp.minimum(i, k - 1), 0))
    resident = lambda a: pltpu.VMEM(a.shape, BF16)
    vec = lambda a: a.reshape(1, -1)
    weights = (w_gate, w_up, w_down, ple_w_gate, ple_w_proj)
    return pl.pallas_call(
        functools.partial(_ffn_kernel, tiles_per_seq=seq // tm),
        grid=(k + n // tm,),
        in_specs=[row(D_MODEL), _const_spec((1, D_MODEL)), wrows(w_gate),
                  wrows(w_up), _const_spec((CONV_WIDTH, D_FF)), _const_spec((1, D_FF)),
                  wrows(w_down), _const_spec((1, D_MODEL)), wrows(ple_w_gate),
                  row(PLE_DIM), wrows(ple_w_proj), _const_spec((1, D_MODEL))],
        out_specs=row(D_MODEL),
        out_shape=jax.ShapeDtypeStruct((n, D_MODEL), F32),
        scratch_shapes=[resident(w) for w in weights]
                       + [pltpu.VMEM((tm, D_FF), BF16), pltpu.VMEM((SUBLANES, D_FF), F32)],
        compiler_params=_params("arbitrary"),
        name="ffn",
    )(h1, vec(norm_g), w_gate, w_up, conv_w, vec(conv_b),
      w_down, vec(ple_g), ple_w_gate, p2, ple_w_proj, vec(final_g))


def _layer(h2, bsz, seq, p2, mix_norm_g, w_in, gate_b, ssm_lam_re, ssm_lam_im, ssm_log_dt, ssm_b_re,
           ssm_b_im, ssm_c_re, ssm_c_im, ssm_d, ssm_glu_w, ssm_glu_b, w_branch_a, w_branch_b, w_out,
           ffn_norm_g, ffn_w_gate, ffn_w_up, ffn_conv_w, ffn_conv_b, ffn_w_down,
           ple_norm_g, ple_w_gate, ple_w_proj, out_norm_g):
    u_rows, qkv, ga, gb = _in_proj(h2, seq, mix_norm_g, w_in, gate_b)
    ops = _ssm_operators(ssm_lam_re, ssm_lam_im, ssm_log_dt, ssm_b_re, ssm_b_im, ssm_c_re, ssm_c_im)
    ys_rows = _ssm_scan(u_rows, bsz, ssm_d.reshape(-1), ops)
    attn_outs = [_attn_group(group) for group in qkv]
    h1 = _merge(h2, seq, ys_rows, ssm_glu_w, ssm_glu_b, w_branch_a, attn_outs, w_branch_b, ga, gb, w_out)
    return _ffn(h1, seq, p2, ffn_norm_g, ffn_w_gate, ffn_w_up, ffn_conv_w, ffn_conv_b, ffn_w_down,
                ple_norm_g, ple_w_gate, ple_w_proj, out_norm_g)


def kernel(x, p, mix_norm_g, w_in, gate_b, ssm_lam_re, ssm_lam_im, ssm_log_dt, ssm_b_re, ssm_b_im, ssm_c_re, ssm_c_im, ssm_d, ssm_glu_w, ssm_glu_b, w_branch_a, w_branch_b, w_out, ffn_norm_g, ffn_w_gate, ffn_w_up, ffn_conv_w, ffn_conv_b, ffn_w_down, ple_norm_g, ple_w_gate, ple_w_proj, final_norm_g):
    bsz, seq, _ = x.shape
    depth = p.shape[0]
    assert depth == 1, "the final norm is fused into the layer's last kernel"
    h2 = x.reshape(bsz * seq, D_MODEL)
    out = _layer(h2, bsz, seq, p[0].reshape(bsz * seq, PLE_DIM), mix_norm_g[0], w_in[0], gate_b[0],
                 ssm_lam_re[0], ssm_lam_im[0], ssm_log_dt[0], ssm_b_re[0], ssm_b_im[0], ssm_c_re[0],
                 ssm_c_im[0], ssm_d[0], ssm_glu_w[0], ssm_glu_b[0], w_branch_a[0], w_branch_b[0],
                 w_out[0], ffn_norm_g[0], ffn_w_gate[0], ffn_w_up[0], ffn_conv_w[0], ffn_conv_b[0],
                 ffn_w_down[0], ple_norm_g[0], ple_w_gate[0], ple_w_proj[0], final_norm_g)
    return out.reshape(bsz, seq, D_MODEL)
```
